```python
import math
import jax, jax.numpy as jnp
from jax import lax
import numpy as np

D_MODEL = 2048
BATCH = 4
SEQ = 2048
DEPTH = 1

N_MEM = 256
HEAD_DIM = D_MODEL // 16
BRANCH_W = D_MODEL // 2
N_BRANCH = 3
DIFF_HEADS = 8
DIFF_QK = HEAD_DIM // 2
DIFF_V = HEAD_DIM
WIN_HEADS = 8
WIN_KV_HEADS = 2
WIN_DIM = HEAD_DIM
WINDOW = 128
BLOCK = 128
MEM_HEADS = 4
MEM_DIM = BRANCH_W // MEM_HEADS
SEG_DIFF_Q = DIFF_HEADS * 2 * DIFF_QK
SEG_DIFF_K = DIFF_HEADS * 2 * DIFF_QK
SEG_DIFF_V = DIFF_HEADS * DIFF_V
SEG_WIN_Q = WIN_HEADS * WIN_DIM
SEG_WIN_K = WIN_KV_HEADS * WIN_DIM
SEG_WIN_V = WIN_KV_HEADS * WIN_DIM
SEG_MEM_Q = MEM_HEADS * MEM_DIM
SEGMENTS = (SEG_DIFF_Q, SEG_DIFF_K, SEG_DIFF_V, SEG_WIN_Q, SEG_WIN_K, SEG_WIN_V, SEG_MEM_Q)
IN_W = SEG_DIFF_Q + SEG_DIFF_K + SEG_DIFF_V + SEG_WIN_Q + SEG_WIN_K + SEG_WIN_V + SEG_MEM_Q
D_FF = 5504
CONV_W = 3
REL_BUCKETS = 32
REL_MAX_DIST = 128
REL_HEADS = DIFF_HEADS + WIN_HEADS
ALPHA = (2 * DEPTH) ** 0.25
BETA = (8 * DEPTH) ** -0.25
LN_EPS = 1e-5
NEG = -1e30

kernel_name = "hybrid_diffattn_wingqa_mem_convffn_deepnorm"


def layer_norm(x, g, b):
    xf = x.astype(jnp.float32)
    mu = jnp.mean(xf, axis=-1, keepdims=True)
    var = jnp.mean(jnp.square(xf - mu), axis=-1, keepdims=True)
    return ((xf - mu) * lax.rsqrt(var + LN_EPS) * g.astype(jnp.float32) + b.astype(jnp.float32)).astype(x.dtype)


def rms_norm(x, g):
    xf = x.astype(jnp.float32)
    ms = jnp.mean(jnp.square(xf), axis=-1, keepdims=True)
    return (xf * lax.rsqrt(ms + LN_EPS) * g.astype(jnp.float32)).astype(x.dtype)


def t5_bucket(rel):
    half = REL_BUCKETS // 2
    max_exact = half // 2
    ret = jnp.where(rel > 0, half, 0)
    n = jnp.abs(rel)
    nf = jnp.maximum(n, 1).astype(jnp.float32)
    large = max_exact + (jnp.log(nf / max_exact) / math.log(REL_MAX_DIST / max_exact)
                         * (half - max_exact)).astype(jnp.int32)
    large = jnp.minimum(large, half - 1)
    return ret + jnp.where(n < max_exact, n, large)


def diff_attention(q, k, v, lq1, lk1, lq2, lk2, subln_g, rel_table, lambda_init):
    B, S = q.shape[0], q.shape[1]
    nblk = S // BLOCK
    lam = (jnp.exp(jnp.sum(lq1.astype(jnp.float32) * lk1.astype(jnp.float32)))
           - jnp.exp(jnp.sum(lq2.astype(jnp.float32) * lk2.astype(jnp.float32))) + lambda_init)
    scale = DIFF_QK ** -0.5
    table = rel_table[:, :DIFF_HEADS].astype(jnp.float32)
    kpos = jnp.arange(S)

    def one_block(i):
        qb = lax.dynamic_slice_in_dim(q, i * BLOCK, BLOCK, axis=1)
        qpos = i * BLOCK + jnp.arange(BLOCK)
        bias = jnp.transpose(table[t5_bucket(kpos[None, :] - qpos[:, None])], (2, 0, 1))
        s = jnp.einsum('bqhcd,bkhcd->bhcqk', qb, k).astype(jnp.float32) * scale + bias[None, :, None]
        p = jax.nn.softmax(s, axis=-1)
        a = p[:, :, 0] - lam * p[:, :, 1]
        return jnp.einsum('bhqk,bkhd->bqhd', a.astype(v.dtype), v)

    o = lax.map(one_block, jnp.arange(nblk))
    o = jnp.transpose(o, (1, 0, 2, 3, 4)).reshape(B, S, DIFF_HEADS, DIFF_V)
    o = rms_norm(o, subln_g) * (1.0 - lambda_init)
    return o.reshape(B, S, DIFF_HEADS * DIFF_V)


def windowed_gqa(q, k, v, sink, rel_table):
    B, S = q.shape[0], q.shape[1]
    nblk = S // BLOCK
    G = WIN_HEADS // WIN_KV_HEADS
    scale = WIN_DIM ** -0.5
    qb = q.reshape(B, nblk, BLOCK, WIN_KV_HEADS, G, WIN_DIM)

    def band(t):
        tp = jnp.pad(t, ((0, 0), (WINDOW, WINDOW), (0, 0), (0, 0)))
        tb = tp.reshape(B, nblk + 2, BLOCK, WIN_KV_HEADS, WIN_DIM)
        return jnp.concatenate([tb[:, :-2], tb[:, 1:-1], tb[:, 2:]], axis=2)

    kb, vb = band(k), band(v)
    blk = jnp.arange(nblk)[:, None] * BLOCK
    qpos = blk + jnp.arange(BLOCK)[None, :]
    kpos = blk - WINDOW + jnp.arange(3 * BLOCK)[None, :]
    rel = kpos[:, None, :] - qpos[:, :, None]
    valid = (jnp.abs(rel) <= WINDOW) & (kpos[:, None, :] >= 0) & (kpos[:, None, :] < S)
    bias = rel_table[:, DIFF_HEADS:].astype(jnp.float32)[t5_bucket(rel)]
    bias = jnp.transpose(bias.reshape(nblk, BLOCK, 3 * BLOCK, WIN_KV_HEADS, G), (3, 4, 0, 1, 2))
    s = jnp.einsum('bnqhgd,bnkhd->bhgnqk', qb, kb).astype(jnp.float32) * scale + bias[None]
    s = jnp.where(valid, s, NEG)
    sk = sink.astype(jnp.float32).reshape(WIN_KV_HEADS, G)[None, :, :, None, None, None]
    m = jnp.maximum(jnp.max(s, axis=-1, keepdims=True), sk)
    p = jnp.exp(s - m)
    p = p / (jnp.sum(p, axis=-1, keepdims=True) + jnp.exp(sk - m))
    o = jnp.einsum('bhgnqk,bnkhd->bnqhgd', p.astype(v.dtype), vb)
    return o.reshape(B, S, WIN_HEADS * WIN_DIM)


def memory_attention(q, mk, mv):
    B, S = q.shape[0], q.shape[1]
    s = jnp.einsum('bqhd,bkhd->bhqk', q, mk).astype(jnp.float32) * (MEM_DIM ** -0.5)
    p = jax.nn.softmax(s, axis=-1)
    o = jnp.einsum('bhqk,bkhd->bqhd', p.astype(mv.dtype), mv)
    return o.reshape(B, S, MEM_HEADS * MEM_DIM)


def conv_ffn(h, w_up, conv_w, conv_b, w_down):
    u = h @ w_up
    up = jnp.pad(u, ((0, 0), (1, 1), (0, 0)))
    u = up[:, :-2] * conv_w[0] + up[:, 1:-1] * conv_w[1] + up[:, 2:] * conv_w[2] + conv_b
    val, gate = jnp.split(u, 2, axis=-1)
    return (jax.nn.gelu(gate) * val) @ w_down


def setup_inputs(seed: int = 0) -> dict:
    key = jax.random.key(seed)
    ks = jax.random.split(key, 26)
    f32 = jnp.float32
    nrm = lambda k, shape: jax.random.normal(k, shape, f32)
    D = D_MODEL
    col_scale = jnp.concatenate([
        jnp.full((SEG_DIFF_Q + SEG_DIFF_K,), 1.0, f32),
        jnp.full((SEG_DIFF_V,), BETA, f32),
        jnp.full((SEG_WIN_Q + SEG_WIN_K,), 1.0, f32),
        jnp.full((SEG_WIN_V,), BETA, f32),
        jnp.full((SEG_MEM_Q,), 1.0, f32)]) * (D ** -0.5)
    mem_scale = jnp.concatenate([jnp.full((BRANCH_W,), 1.0, f32), jnp.full((BRANCH_W,), BETA, f32)]) * (D ** -0.5)
    return {
        "x": nrm(ks[0], (BATCH, SEQ, D)),
        "mem": nrm(ks[1], (BATCH, N_MEM, D)),
        "ln_in_g": 1.0 + 0.02 * nrm(ks[2], (D,)),
        "ln_in_b": 0.02 * nrm(ks[3], (D,)),
        "rel_table": 0.2 * nrm(ks[4], (REL_BUCKETS, REL_HEADS)),
        "w_in": nrm(ks[5], (DEPTH, D, IN_W)) * col_scale,
        "w_mem_kv": nrm(ks[6], (DEPTH, D, 2 * BRANCH_W)) * mem_scale,
        "diff_lq1": 0.1 * nrm(ks[7], (DEPTH, DIFF_QK)),
        "diff_lk1": 0.1 * nrm(ks[8], (DEPTH, DIFF_QK)),
        "diff_lq2": 0.1 * nrm(ks[9], (DEPTH, DIFF_QK)),
        "diff_lk2": 0.1 * nrm(ks[10], (DEPTH, DIFF_QK)),
        "diff_subln_g": 1.0 + 0.02 * nrm(ks[11], (DEPTH, DIFF_V)),
        "win_sink": 0.5 * nrm(ks[12], (DEPTH, WIN_HEADS)),
        "w_gate": nrm(ks[13], (DEPTH, D, N_BRANCH * D)) * (D ** -0.5),
        "b_gate": 0.02 * nrm(ks[14], (DEPTH, N_BRANCH * D)),
        "w_branch": nrm(ks[15], (DEPTH, N_BRANCH, BRANCH_W, D)) * (BRANCH_W ** -0.5),
        "w_o": nrm(ks[16], (DEPTH, D, D)) * (BETA * D ** -0.5),
        "ln1_g": 1.0 + 0.02 * nrm(ks[17], (DEPTH, D)),
        "ln1_b": 0.02 * nrm(ks[18], (DEPTH, D)),
        "w_up": nrm(ks[19], (DEPTH, D, 2 * D_FF)) * (D ** -0.5),
        "conv_w": nrm(ks[20], (DEPTH, CONV_W, 2 * D_FF)) * (CONV_W ** -0.5),
        "conv_b": 0.02 * nrm(ks[21], (DEPTH, 2 * D_FF)),
        "w_down": nrm(ks[22], (DEPTH, D_FF, D)) * (BETA * D_FF ** -0.5),
        "ln2_g": 1.0 + 0.02 * nrm(ks[23], (DEPTH, D)),
        "ln2_b": 0.02 * nrm(ks[24], (DEPTH, D)),
    }


def reference(x, mem, ln_in_g, ln_in_b, rel_table, w_in, w_mem_kv, diff_lq1, diff_lk1, diff_lq2,
              diff_lk2, diff_subln_g, win_sink, w_gate, b_gate, w_branch, w_o, ln1_g, ln1_b,
              w_up, conv_w, conv_b, w_down, ln2_g, ln2_b):
    B, S, _ = x.shape
    NM = mem.shape[1]
    split_points = np.cumsum(SEGMENTS)[:-1].tolist()
    h = layer_norm(x, ln_in_g, ln_in_b)
    for l in range(DEPTH):
        lambda_init = 0.8 - 0.6 * math.exp(-0.3 * l)
        proj = h @ w_in[l]
        dq, dk, dv, wq, wk, wv, mq = jnp.split(proj, split_points, axis=-1)
        a = diff_attention(dq.reshape(B, S, DIFF_HEADS, 2, DIFF_QK),
                           dk.reshape(B, S, DIFF_HEADS, 2, DIFF_QK),
                           dv.reshape(B, S, DIFF_HEADS, DIFF_V),
                           diff_lq1[l], diff_lk1[l], diff_lq2[l], diff_lk2[l], diff_subln_g[l],
                           rel_table, lambda_init)
        b = windowed_gqa(wq.reshape(B, S, WIN_HEADS, WIN_DIM),
                         wk.reshape(B, S, WIN_KV_HEADS, WIN_DIM),
                         wv.reshape(B, S, WIN_KV_HEADS, WIN_DIM),
                         win_sink[l], rel_table)
        mk, mv = jnp.split(mem @ w_mem_kv[l], 2, axis=-1)
        c = memory_attention(mq.reshape(B, S, MEM_HEADS, MEM_DIM),
                             mk.reshape(B, NM, MEM_HEADS, MEM_DIM),
                             mv.reshape(B, NM, MEM_HEADS, MEM_DIM))
        branches = jnp.stack([a, b, c], axis=2)
        widened = jnp.einsum('bsnc,ncd->bsnd', branches, w_branch[l])
        gates = jax.nn.sigmoid(h @ w_gate[l] + b_gate[l]).reshape(B, S, N_BRANCH, D_MODEL)
        mix = jnp.sum(gates * widened, axis=2) @ w_o[l]
        h = layer_norm(ALPHA * h + mix, ln1_g[l], ln1_b[l])
        ffn = conv_ffn(h, w_up[l], conv_w[l], conv_b[l], w_down[l])
        h = layer_norm(ALPHA * h + ffn, ln2_g[l], ln2_b[l])
    return h
```

```python
import functools
import math

import jax
import jax.numpy as jnp
from jax import lax
from jax.experimental import pallas as pl
from jax.experimental.pallas import tpu as pltpu

F32 = jnp.float32
BF16 = jnp.bfloat16

D_MODEL = 2048
SEQ = 2048
N_MEM = 256
HEAD_DIM = 128
BRANCH_W = 1024
DIFF_HEADS = 8
DIFF_QK = 64
WIN_HEADS = 8
WIN_KV_HEADS = 2
WIN_GROUP = WIN_HEADS // WIN_KV_HEADS
WINDOW = 128
MEM_HEADS = 4
MEM_DIM = 256
OFF_DQ, OFF_DK, OFF_DV, OFF_WQ, OFF_WK, OFF_WV, OFF_MQ = 0, 1024, 2048, 3072, 4096, 4352, 4608
IN_W = 5632
D_FF = 5504
D_FF_PAD = 5632
REL_BUCKETS = 32
REL_MAX_DIST = 128
DEPTH = 1
ALPHA = (2 * DEPTH) ** 0.25
LN_EPS = 1e-5
NEG = -1e30
LAMBDA_INIT = 0.8 - 0.6 * math.exp(-0.3 * 0)

VMEM_LIMIT = 56 * 1024 * 1024
LN_ROWS = 128


def _ln_rows(x, g, b):
    mu = jnp.mean(x, axis=-1, keepdims=True)
    xc = x - mu
    var = jnp.mean(xc * xc, axis=-1, keepdims=True)
    return xc * lax.rsqrt(var + LN_EPS) * g + b


def _dot(a, b):
    return jnp.dot(a, b, preferred_element_type=F32)


def _dot_nt(a, b):
    return lax.dot_general(a, b, (((1,), (1,)), ((), ())), preferred_element_type=F32)


def _params(*sem):
    return pltpu.CompilerParams(dimension_semantics=sem, vmem_limit_bytes=VMEM_LIMIT)


def _ln_proj_kernel(x_ref, g_ref, b_ref, w_ref, o_ref, h_scr):
    @pl.when(pl.program_id(1) == 0)
    def _():
        def body(r, c):
            rows = pl.ds(pl.multiple_of(r * LN_ROWS, LN_ROWS), LN_ROWS)
            h_scr[rows, :] = _ln_rows(x_ref[rows, :], g_ref[...], b_ref[...]).astype(BF16)
            return c
        lax.fori_loop(0, x_ref.shape[0] // LN_ROWS, body, 0)

    o_ref[...] = _dot(h_scr[...], w_ref[...]).astype(BF16)


def _ln_proj(x2, g, b, w, tm=1024, tn=512):
    m, d = x2.shape
    n = w.shape[1]
    return pl.pallas_call(
        _ln_proj_kernel,
        grid=(m // tm, n // tn),
        in_specs=[pl.BlockSpec((tm, d), lambda i, j: (i, 0)),
                  pl.BlockSpec((1, d), lambda i, j: (0, 0)),
                  pl.BlockSpec((1, d), lambda i, j: (0, 0)),
                  pl.BlockSpec((d, tn), lambda i, j: (0, j))],
        out_specs=pl.BlockSpec((tm, tn), lambda i, j: (i, j)),
        out_shape=jax.ShapeDtypeStruct((m, n), BF16),
        scratch_shapes=[pltpu.VMEM((tm, d), BF16)],
        compiler_params=_params("parallel", "arbitrary"),
        name="ln_proj",
    )(x2, g, b, w)


def _mem_kv_kernel(m_ref, w_ref, o_ref):
    o_ref[...] = _dot(m_ref[...].astype(BF16), w_ref[...]).astype(BF16)


def _mem_kv(mem2, w, tn=512):
    m, d = mem2.shape
    n = w.shape[1]
    return pl.pallas_call(
        _mem_kv_kernel,
        grid=(n // tn,),
        in_specs=[pl.BlockSpec((m, d), lambda j: (0, 0)),
                  pl.BlockSpec((d, tn), lambda j: (0, j))],
        out_specs=pl.BlockSpec((m, tn), lambda j: (0, j)),
        out_shape=jax.ShapeDtypeStruct((m, n), BF16),
        compiler_params=_params("parallel"),
        name="mem_kv",
    )(mem2, w)


def _t5_bucket(rel):
    half = REL_BUCKETS // 2
    max_exact = half // 2
    ret = jnp.where(rel > 0, half, 0)
    n = jnp.abs(rel)
    nf = jnp.maximum(n, 1).astype(F32)
    large = max_exact + (jnp.log(nf / max_exact) / math.log(REL_MAX_DIST / max_exact)
                         * (half - max_exact)).astype(jnp.int32)
    large = jnp.minimum(large, half - 1)
    return ret + jnp.where(n < max_exact, n, large)


def _diff_bias_tiles(table, tq):
    d = jnp.arange(5)[:, None, None] - 2
    q = jnp.arange(tq)[None, :, None]
    k = jnp.arange(tq)[None, None, :]
    bucket = _t5_bucket(d * tq + k - q)
    return jnp.transpose(table.astype(F32)[bucket], (3, 0, 1, 2))


def _win_bias_tiles(table, tq):
    q = jnp.arange(tq)[:, None]
    c = jnp.arange(tq + 2 * WINDOW)[None, :]
    rel = c - WINDOW - q
    bias = jnp.transpose(table.astype(F32)[_t5_bucket(rel)], (2, 0, 1))
    return jnp.where((jnp.abs(rel) <= WINDOW)[None], bias, NEG)


def _diff_kernel(q_ref, k_ref, v_ref, band_ref, lq1_ref, lk1_ref, lq2_ref, lk2_ref, g_ref, o_ref,
                 s_scr, a_scr, *, tq, rc):
    i = pl.program_id(2)
    nk = k_ref.shape[1] // tq
    q = q_ref[0] * jnp.asarray(DIFF_QK ** -0.5, BF16)
    lane = lax.broadcasted_iota(jnp.int32, q.shape, 1)
    zero = jnp.zeros_like(q)
    q0 = jnp.where(lane < DIFF_QK, q, zero)
    q1 = jnp.where(lane >= DIFF_QK, q, zero)
    for j in range(nk):
        cols = slice(j * tq, (j + 1) * tq)
        kj = k_ref[0, cols, :]
        bias = band_ref[0, jnp.clip(j - i, -2, 2) + 2]
        s_scr[0, :, cols] = _dot_nt(q0, kj) + bias
        s_scr[1, :, cols] = _dot_nt(q1, kj) + bias

    lam = (jnp.exp(jnp.sum(lq1_ref[...] * lk1_ref[...], axis=-1, keepdims=True))
           - jnp.exp(jnp.sum(lq2_ref[...] * lk2_ref[...], axis=-1, keepdims=True)) + LAMBDA_INIT)

    def body(r, c):
        rows = pl.ds(pl.multiple_of(r * rc, rc), rc)
        s0 = s_scr[0, rows, :]
        s1 = s_scr[1, rows, :]
        p0 = jnp.exp(s0 - jnp.max(s0, axis=-1, keepdims=True))
        p1 = jnp.exp(s1 - jnp.max(s1, axis=-1, keepdims=True))
        r0 = 1.0 / jnp.sum(p0, axis=-1, keepdims=True)
        r1 = lam / jnp.sum(p1, axis=-1, keepdims=True)
        a_scr[rows, :] = (p0 * r0 - p1 * r1).astype(BF16)
        return c
    lax.fori_loop(0, tq // rc, body, 0)

    o = _dot(a_scr[...], v_ref[0])
    ms = jnp.mean(o * o, axis=-1, keepdims=True)
    o_ref[0] = (o * lax.rsqrt(ms + LN_EPS) * g_ref[...] * (1.0 - LAMBDA_INIT)).astype(BF16)


def _diff_attn(proj, band, lq1, lk1, lq2, lk2, g, tq=256, rc=32):
    b, s, _ = proj.shape
    cb = HEAD_DIM
    vec = lambda n: pl.BlockSpec((1, n), lambda bi, h, i: (0, 0))
    return pl.pallas_call(
        functools.partial(_diff_kernel, tq=tq, rc=rc),
        grid=(b, DIFF_HEADS, s // tq),
        in_specs=[pl.BlockSpec((1, tq, cb), lambda bi, h, i: (bi, i, OFF_DQ // cb + h)),
                  pl.BlockSpec((1, s, cb), lambda bi, h, i: (bi, 0, OFF_DK // cb + h)),
                  pl.BlockSpec((1, s, cb), lambda bi, h, i: (bi, 0, OFF_DV // cb + h)),
                  pl.BlockSpec((1, 5, tq, tq), lambda bi, h, i: (h, 0, 0, 0)),
                  vec(DIFF_QK), vec(DIFF_QK), vec(DIFF_QK), vec(DIFF_QK), vec(HEAD_DIM)],
        out_specs=pl.BlockSpec((1, tq, cb), lambda bi, h, i: (bi, i, h)),
        out_shape=jax.ShapeDtypeStruct((b, s, BRANCH_W), BF16),
        scratch_shapes=[pltpu.VMEM((2, tq, s), F32), pltpu.VMEM((tq, s), BF16)],
        compiler_params=_params("parallel", "parallel", "arbitrary"),
        name="diff_attn",
    )(proj, proj, proj, band, lq1, lk1, lq2, lk2, g)


def _win_kernel(sink_ref, q_ref, kp_ref, km_ref, kn_ref, vp_ref, vm_ref, vn_ref, bias_ref, o_ref, *, tq):
    g = pl.program_id(1)
    i = pl.program_id(2)
    first = i == 0
    last = i == pl.num_programs(2) - 1
    kp, km, kn = kp_ref[0], km_ref[0], kn_ref[0]
    vp, vm, vn = vp_ref[0], vm_ref[0], vn_ref[0]
    scale = jnp.asarray(HEAD_DIM ** -0.5, F32)
    w = WINDOW
    for j in range(WIN_GROUP):
        cols = slice(j * HEAD_DIM, (j + 1) * HEAD_DIM)
        qh = q_ref[0, :, cols]
        sp = _dot_nt(qh, kp) * scale + bias_ref[j, :, 0:w]
        sm = _dot_nt(qh, km) * scale + bias_ref[j, :, w:w + tq]
        sn = _dot_nt(qh, kn) * scale + bias_ref[j, :, w + tq:]
        sp = jnp.where(first, NEG, sp)
        sn = jnp.where(last, NEG, sn)
        sink = sink_ref[g * WIN_GROUP + j]
        m = jnp.maximum(jnp.maximum(jnp.max(sp, axis=-1, keepdims=True), jnp.max(sm, axis=-1, keepdims=True)),
                        jnp.maximum(jnp.max(sn, axis=-1, keepdims=True), sink))
        pp, pm, pn = jnp.exp(sp - m), jnp.exp(sm - m), jnp.exp(sn - m)
        den = (jnp.sum(pp, axis=-1, keepdims=True) + jnp.sum(pm, axis=-1, keepdims=True)
               + jnp.sum(pn, axis=-1, keepdims=True) + jnp.exp(sink - m))
        r = 1.0 / den
        o = (_dot((pp * r).astype(BF16), vp) + _dot((pm * r).astype(BF16), vm)
             + _dot((pn * r).astype(BF16), vn))
        o_ref[0, :, cols] = o.astype(BF16)


def _win_attn(proj, bias, sink, tq=256):
    b, s, _ = proj.shape
    hd = HEAD_DIM
    nb = s // WINDOW
    r = tq // WINDOW
    prev = lambda bi, g, i: jnp.maximum(i * r - 1, 0)
    nxt = lambda bi, g, i: jnp.minimum((i + 1) * r, nb - 1)
    kcol, vcol = OFF_WK // hd, OFF_WV // hd
    return pl.pallas_call(
        functools.partial(_win_kernel, tq=tq),
        grid=(b, WIN_KV_HEADS, s // tq),
        in_specs=[pl.BlockSpec(memory_space=pltpu.SMEM),
                  pl.BlockSpec((1, tq, WIN_GROUP * hd), lambda bi, g, i: (bi, i, OFF_WQ // (WIN_GROUP * hd) + g)),
                  pl.BlockSpec((1, WINDOW, hd), lambda bi, g, i: (bi, prev(bi, g, i), kcol + g)),
                  pl.BlockSpec((1, tq, hd), lambda bi, g, i: (bi, i, kcol + g)),
                  pl.BlockSpec((1, WINDOW, hd), lambda bi, g, i: (bi, nxt(bi, g, i), kcol + g)),
                  pl.BlockSpec((1, WINDOW, hd), lambda bi, g, i: (bi, prev(bi, g, i), vcol + g)),
                  pl.BlockSpec((1, tq, hd), lambda bi, g, i: (bi, i, vcol + g)),
                  pl.BlockSpec((1, WINDOW, hd), lambda bi, g, i: (bi, nxt(bi, g, i), vcol + g)),
                  pl.BlockSpec((WIN_GROUP, tq, tq + 2 * WINDOW), lambda bi, g, i: (g, 0, 0))],
        out_specs=pl.BlockSpec((1, tq, WIN_GROUP * hd), lambda bi, g, i: (bi, i, g)),
        out_shape=jax.ShapeDtypeStruct((b, s, BRANCH_W), BF16),
        compiler_params=_params("parallel", "parallel", "arbitrary"),
        name="win_attn",
    )(sink, proj, proj, proj, proj, proj, proj, proj, bias)


def _mem_attn_kernel(q_ref, k_ref, v_ref, o_ref):
    s = _dot_nt(q_ref[0], k_ref[0]) * jnp.asarray(MEM_DIM ** -0.5, F32)
    p = jnp.exp(s - jnp.max(s, axis=-1, keepdims=True))
    p = p * (1.0 / jnp.sum(p, axis=-1, keepdims=True))
    o_ref[0] = _dot(p.astype(BF16), v_ref[0]).astype(BF16)


def _mem_attn(proj, memkv, tq=512):
    b, s, _ = proj.shape
    md = MEM_DIM
    return pl.pallas_call(
        _mem_attn_kernel,
        grid=(b, MEM_HEADS, s // tq),
        in_specs=[pl.BlockSpec((1, tq, md), lambda bi, h, i: (bi, i, OFF_MQ // md + h)),
                  pl.BlockSpec((1, N_MEM, md), lambda bi, h, i: (bi, 0, h)),
                  pl.BlockSpec((1, N_MEM, md), lambda bi, h, i: (bi, 0, MEM_HEADS + h))],
        out_specs=pl.BlockSpec((1, tq, md), lambda bi, h, i: (bi, i, h)),
        out_shape=jax.ShapeDtypeStruct((b, s, BRANCH_W), BF16),
        compiler_params=_params("parallel", "parallel", "arbitrary"),
        name="mem_attn",
    )(proj, memkv, memkv)


def _mix_kernel(x_ref, lg_ref, lb_ref, a_ref, b_ref, c_ref, wg0_ref, wg1_ref, wg2_ref,
                bg0_ref, bg1_ref, bg2_ref, wb_ref, wo_ref, g1_ref, b1_ref, o_ref, h_scr, acc_scr):
    n = pl.program_id(1)
    nrow = x_ref.shape[0] // LN_ROWS

    @pl.when(n == 0)
    def _():
        def body(r, c):
            rows = pl.ds(pl.multiple_of(r * LN_ROWS, LN_ROWS), LN_ROWS)
            h_scr[rows, :] = _ln_rows(x_ref[rows, :], lg_ref[...], lb_ref[...]).astype(BF16)
            return c
        lax.fori_loop(0, nrow, body, 0)
        acc_scr[...] = jnp.zeros_like(acc_scr)

    h = h_scr[...]
    mixed = None
    for br_ref, wg_ref, bg_ref, k in ((a_ref, wg0_ref, bg0_ref, 0), (b_ref, wg1_ref, bg1_ref, 1),
                                      (c_ref, wg2_ref, bg2_ref, 2)):
        gate = jax.nn.sigmoid(_dot(h, wg_ref[...]) + bg_ref[...])
        term = gate * _dot(br_ref[...], wb_ref[k])
        mixed = term if mixed is None else mixed + term
    acc_scr[...] += _dot(mixed.astype(BF16), wo_ref[...])

    @pl.when(n == pl.num_programs(1) - 1)
    def _():
        def body(r, c):
            rows = pl.ds(pl.multiple_of(r * LN_ROWS, LN_ROWS), LN_ROWS)
            hh = _ln_rows(x_ref[rows, :], lg_ref[...], lb_ref[...])
            o_ref[rows, :] = _ln_rows(ALPHA * hh + acc_scr[rows, :], g1_ref[...], b1_ref[...])
            return c
        lax.fori_loop(0, nrow, body, 0)


def _mix(x2, lg, lb, a, b, c, wg, bg, wb, wo, g1, b1, tm=512, tn=256):
    m, d = x2.shape
    bw = a.shape[1]
    nn = d // tn
    row = lambda w: pl.BlockSpec((tm, w), lambda i, n: (i, 0))
    vec = pl.BlockSpec((1, d), lambda i, n: (0, 0))
    wgs = [pl.BlockSpec((d, tn), functools.partial(lambda i, n, k: (0, k * nn + n), k=k)) for k in range(3)]
    bgs = [pl.BlockSpec((1, tn), functools.partial(lambda i, n, k: (0, k * nn + n), k=k)) for k in range(3)]
    return pl.pallas_call(
        _mix_kernel,
        grid=(m // tm, nn),
        in_specs=[row(d), vec, vec, row(bw), row(bw), row(bw), *wgs, *bgs,
                  pl.BlockSpec((3, bw, tn), lambda i, n: (0, 0, n)),
                  pl.BlockSpec((tn, d), lambda i, n: (n, 0)), vec, vec],
        out_specs=pl.BlockSpec((tm, d), lambda i, n: (i, 0)),
        out_shape=jax.ShapeDtypeStruct((m, d), F32),
        scratch_shapes=[pltpu.VMEM((tm, d), BF16), pltpu.VMEM((tm, d), F32)],
        compiler_params=_params("parallel", "arbitrary"),
        name="mix",
    )(x2, lg, lb, a, b, c, wg, wg, wg, bg, bg, bg, wb, wo, g1, b1)


HALO = 16


def _gelu_tanh(x):
    return 0.5 * x * (1.0 + jnp.tanh(math.sqrt(2.0 / math.pi) * (x + 0.044715 * (x * x * x))))


def _ffn_kernel(h_ref, hp_ref, hn_ref, wv_ref, wg_ref, cwv_ref, cwg_ref, cbv_ref, cbg_ref, wd_ref,
                g2_ref, b2_ref, o_ref, hx_scr, acc_scr, *, tiles_per_seq):
    i = pl.program_id(0)
    f = pl.program_id(1)
    tm = h_ref.shape[0]
    nrow = tm // LN_ROWS

    @pl.when(f == 0)
    def _():
        seq_first = (i % tiles_per_seq) == 0
        seq_last = (i % tiles_per_seq) == tiles_per_seq - 1
        hx_scr[0:HALO, :] = jnp.where(seq_first, 0.0, hp_ref[...]).astype(BF16)
        hx_scr[HALO:HALO + tm, :] = h_ref[...].astype(BF16)
        hx_scr[HALO + tm:, :] = jnp.where(seq_last, 0.0, hn_ref[...]).astype(BF16)
        acc_scr[...] = jnp.zeros_like(acc_scr)

    hx = hx_scr[...]
    ext = tm + 2 * HALO

    def conv(u, cw_ref, cb_ref):
        prev = pltpu.roll(u, 1, 0)[HALO:HALO + tm]
        nxt = pltpu.roll(u, ext - 1, 0)[HALO:HALO + tm]
        return prev * cw_ref[0:1, :] + u[HALO:HALO + tm] * cw_ref[1:2, :] + nxt * cw_ref[2:3, :] + cb_ref[...]

    val = conv(_dot(hx, wv_ref[...]), cwv_ref, cbv_ref)
    gate = conv(_dot(hx, wg_ref[...]), cwg_ref, cbg_ref)
    acc_scr[...] += _dot((_gelu_tanh(gate) * val).astype(BF16), wd_ref[...])

    @pl.when(f == pl.num_programs(1) - 1)
    def _():
        def body(r, c):
            rows = pl.ds(pl.multiple_of(r * LN_ROWS, LN_ROWS), LN_ROWS)
            o_ref[rows, :] = _ln_rows(ALPHA * h_ref[rows, :] + acc_scr[rows, :], g2_ref[...], b2_ref[...])
            return c
        lax.fori_loop(0, nrow, body, 0)


def _ffn(h1, w_up, cw, cb, w_down, g2, b2, seq, tm=512, tf=512):
    m, d = h1.shape
    nf = D_FF_PAD // tf
    hb = tm // HALO
    nhb = m // HALO
    vec = pl.BlockSpec((1, d), lambda i, f: (0, 0))
    return pl.pallas_call(
        functools.partial(_ffn_kernel, tiles_per_seq=seq // tm),
        grid=(m // tm, nf),
        in_specs=[pl.BlockSpec((tm, d), lambda i, f: (i, 0)),
                  pl.BlockSpec((HALO, d), lambda i, f: (jnp.maximum(i * hb - 1, 0), 0)),
                  pl.BlockSpec((HALO, d), lambda i, f: (jnp.minimum((i + 1) * hb, nhb - 1), 0)),
                  pl.BlockSpec((d, tf), lambda i, f: (0, f)),
                  pl.BlockSpec((d, tf), lambda i, f: (0, nf + f)),
                  pl.BlockSpec((3, tf), lambda i, f: (0, f)),
                  pl.BlockSpec((3, tf), lambda i, f: (0, nf + f)),
                  pl.BlockSpec((1, tf), lambda i, f: (0, f)),
                  pl.BlockSpec((1, tf), lambda i, f: (0, nf + f)),
                  pl.BlockSpec((tf, d), lambda i, f: (f, 0)), vec, vec],
        out_specs=pl.BlockSpec((tm, d), lambda i, f: (i, 0)),
        out_shape=jax.ShapeDtypeStruct((m, d), F32),
        scratch_shapes=[pltpu.VMEM((tm + 2 * HALO, d), BF16), pltpu.VMEM((tm, d), F32)],
        compiler_params=_params("parallel", "arbitrary"),
        name="ffn",
    )(h1, h1, h1, w_up, w_up, cw, cw, cb, cb, w_down, g2, b2)


def _pad_ff(t, axis):
    val, gate = jnp.split(t, 2, axis=axis)
    pad = [(0, 0)] * t.ndim
    pad[axis] = (0, D_FF_PAD - D_FF)
    return jnp.concatenate([jnp.pad(val, pad), jnp.pad(gate, pad)], axis=axis)


def kernel(x, mem, ln_in_g, ln_in_b, rel_table, w_in, w_mem_kv, diff_lq1, diff_lk1, diff_lq2, diff_lk2,
           diff_subln_g, win_sink, w_gate, b_gate, w_branch, w_o, ln1_g, ln1_b, w_up, conv_w, conv_b,
           w_down, ln2_g, ln2_b):
    assert w_in.shape[0] == DEPTH == 1
    bsz, seq, d = x.shape
    x2 = x.reshape(bsz * seq, d)
    row = lambda v: v.reshape(1, -1).astype(F32)
    l = 0
    proj = _ln_proj(x2, row(ln_in_g), row(ln_in_b), w_in[l].astype(BF16)).reshape(bsz, seq, IN_W)
    memkv = _mem_kv(mem.reshape(bsz * N_MEM, d), w_mem_kv[l].astype(BF16)).reshape(bsz, N_MEM, 2 * BRANCH_W)

    tq_d, tq_w = 256, 256
    a = _diff_attn(proj, _diff_bias_tiles(rel_table[:, :DIFF_HEADS], tq_d), row(diff_lq1[l]), row(diff_lk1[l]),
                   row(diff_lq2[l]), row(diff_lk2[l]), row(diff_subln_g[l]), tq=tq_d)
    b = _win_attn(proj, _win_bias_tiles(rel_table[:, DIFF_HEADS:], tq_w), win_sink[l].astype(F32), tq=tq_w)
    c = _mem_attn(proj, memkv)

    m = bsz * seq
    h1 = _mix(x2, row(ln_in_g), row(ln_in_b), a.reshape(m, -1), b.reshape(m, -1), c.reshape(m, -1),
              w_gate[l].astype(BF16), row(b_gate[l]), w_branch[l].astype(BF16), w_o[l].astype(BF16),
              row(ln1_g[l]), row(ln1_b[l]))

    w_up_p = _pad_ff(w_up[l], 1).astype(BF16)
    cw_p = _pad_ff(conv_w[l].astype(F32), 1)
    cb_p = _pad_ff(row(conv_b[l]), 1)
    w_down_p = jnp.pad(w_down[l], ((0, D_FF_PAD - D_FF), (0, 0))).astype(BF16)
    out = _ffn(h1, w_up_p, cw_p, cb_p, w_down_p, row(ln2_g[l]), row(ln2_b[l]), seq)
    return out.reshape(bsz, seq, d)
```

```python
import functools
import math

import jax
import jax.numpy as jnp
from jax import lax
from jax.experimental import pallas as pl
from jax.experimental.pallas import tpu as pltpu

F32 = jnp.float32
BF16 = jnp.bfloat16

D_MODEL = 2048
SEQ = 2048
N_MEM = 256
HEAD_DIM = 128
BRANCH_W = 1024
DIFF_HEADS = 8
DIFF_QK = 64
WIN_HEADS = 8
WIN_KV_HEADS = 2
WIN_GROUP = WIN_HEADS // WIN_KV_HEADS
WINDOW = 128
MEM_HEADS = 4
MEM_DIM = 256
OFF_DQ, OFF_DK, OFF_DV, OFF_WQ, OFF_WK, OFF_WV, OFF_MQ = 0, 1024, 2048, 3072, 4096, 4352, 4608
IN_W = 5632
D_FF = 5504
D_FF_PAD = 5632
REL_BUCKETS = 32
REL_MAX_DIST = 128
DEPTH = 1
ALPHA = (2 * DEPTH) ** 0.25
LN_EPS = 1e-5
NEG = -1e30
LAMBDA_INIT = 0.8 - 0.6 * math.exp(-0.3 * 0)

VMEM_LIMIT = 56 * 1024 * 1024
LN_ROWS = 128


def _ln_rows(x, g, b):
    mu = jnp.mean(x, axis=-1, keepdims=True)
    xc = x - mu
    var = jnp.mean(xc * xc, axis=-1, keepdims=True)
    return xc * lax.rsqrt(var + LN_EPS) * g + b


def _dot(a, b):
    return jnp.dot(a, b, preferred_element_type=F32)


def _dot_nt(a, b):
    return lax.dot_general(a, b, (((1,), (1,)), ((), ())), preferred_element_type=F32)


def _params(*sem):
    return pltpu.CompilerParams(dimension_semantics=sem, vmem_limit_bytes=VMEM_LIMIT)


def _ln_proj_kernel(x_ref, g_ref, b_ref, w_ref, o_ref, h_scr):
    @pl.when(pl.program_id(1) == 0)
    def _():
        def body(r, c):
            rows = pl.ds(pl.multiple_of(r * LN_ROWS, LN_ROWS), LN_ROWS)
            h_scr[rows, :] = _ln_rows(x_ref[rows, :], g_ref[...], b_ref[...]).astype(BF16)
            return c
        lax.fori_loop(0, x_ref.shape[0] // LN_ROWS, body, 0)

    o_ref[...] = _dot(h_scr[...], w_ref[...]).astype(BF16)


def _ln_proj(x2, g, b, w, tm=1024, tn=512):
    m, d = x2.shape
    n = w.shape[1]
    return pl.pallas_call(
        _ln_proj_kernel,
        grid=(m // tm, n // tn),
        in_specs=[pl.BlockSpec((tm, d), lambda i, j: (i, 0)),
                  pl.BlockSpec((1, d), lambda i, j: (0, 0)),
                  pl.BlockSpec((1, d), lambda i, j: (0, 0)),
                  pl.BlockSpec((d, tn), lambda i, j: (0, j))],
        out_specs=pl.BlockSpec((tm, tn), lambda i, j: (i, j)),
        out_shape=jax.ShapeDtypeStruct((m, n), BF16),
        scratch_shapes=[pltpu.VMEM((tm, d), BF16)],
        compiler_params=_params("parallel", "arbitrary"),
        name="ln_proj",
    )(x2, g, b, w)


def _mem_kv_kernel(m_ref, w_ref, o_ref):
    o_ref[...] = _dot(m_ref[...].astype(BF16), w_ref[...]).astype(BF16)


def _mem_kv(mem2, w, tn=512):
    m, d = mem2.shape
    n = w.shape[1]
    return pl.pallas_call(
        _mem_kv_kernel,
        grid=(n // tn,),
        in_specs=[pl.BlockSpec((m, d), lambda j: (0, 0)),
                  pl.BlockSpec((d, tn), lambda j: (0, j))],
        out_specs=pl.BlockSpec((m, tn), lambda j: (0, j)),
        out_shape=jax.ShapeDtypeStruct((m, n), BF16),
        compiler_params=_params("parallel"),
        name="mem_kv",
    )(mem2, w)


def _t5_bucket(rel):
    half = REL_BUCKETS // 2
    max_exact = half // 2
    ret = jnp.where(rel > 0, half, 0)
    n = jnp.abs(rel)
    nf = jnp.maximum(n, 1).astype(F32)
    large = max_exact + (jnp.log(nf / max_exact) / math.log(REL_MAX_DIST / max_exact)
                         * (half - max_exact)).astype(jnp.int32)
    large = jnp.minimum(large, half - 1)
    return ret + jnp.where(n < max_exact, n, large)


BIAS_ROWS = 64


def _bias_lookup_kernel(tab_ref, bucket_ref, o_ref):
    h = pl.program_id(0)

    def body(r, carry):
        rows = pl.ds(pl.multiple_of(r * BIAS_ROWS, BIAS_ROWS), BIAS_ROWS)
        bk = bucket_ref[rows, :]
        out = jnp.full(bk.shape, NEG, F32)
        for b in range(REL_BUCKETS):
            out = jnp.where(bk == b, tab_ref[b, h], out)
        o_ref[0, rows, :] = out
        return carry
    lax.fori_loop(0, bucket_ref.shape[0] // BIAS_ROWS, body, 0)


def _bias_lookup(table, bucket):
    nh = table.shape[1]
    r, c = bucket.shape
    return pl.pallas_call(
        _bias_lookup_kernel,
        grid=(nh,),
        in_specs=[pl.BlockSpec(memory_space=pltpu.SMEM),
                  pl.BlockSpec((r, c), lambda h: (0, 0))],
        out_specs=pl.BlockSpec((1, r, c), lambda h: (h, 0, 0)),
        out_shape=jax.ShapeDtypeStruct((nh, r, c), F32),
        compiler_params=_params("parallel"),
        name="bias_lookup",
    )(table.astype(F32), bucket)


def _diff_bias_tiles(table, tq):
    d = jnp.arange(5)[:, None, None] - 2
    q = jnp.arange(tq)[None, :, None]
    k = jnp.arange(tq)[None, None, :]
    bucket = _t5_bucket(d * tq + k - q).astype(jnp.int32).reshape(5 * tq, tq)
    return _bias_lookup(table, bucket).reshape(table.shape[1], 5, tq, tq)


def _win_bias_tiles(table, tq):
    q = jnp.arange(tq)[:, None]
    c = jnp.arange(tq + 2 * WINDOW)[None, :]
    rel = c - WINDOW - q
    bucket = jnp.where(jnp.abs(rel) <= WINDOW, _t5_bucket(rel), -1).astype(jnp.int32)
    return _bias_lookup(table, bucket)


def _diff_kernel(q_ref, k_ref, v_ref, band_ref, lq1_ref, lk1_ref, lq2_ref, lk2_ref, g_ref, o_ref,
                 s_scr, a_scr, *, tq, rc):
    i = pl.program_id(2)
    nk = k_ref.shape[1] // tq
    q = q_ref[0] * jnp.asarray(DIFF_QK ** -0.5, BF16)
    lane = lax.broadcasted_iota(jnp.int32, q.shape, 1)
    zero = jnp.zeros_like(q)
    q0 = jnp.where(lane < DIFF_QK, q, zero)
    q1 = jnp.where(lane >= DIFF_QK, q, zero)
    for j in range(nk):
        cols = slice(j * tq, (j + 1) * tq)
        kj = k_ref[0, cols, :]
        bias = band_ref[0, jnp.clip(j - i, -2, 2) + 2]
        s_scr[0, :, cols] = _dot_nt(q0, kj) + bias
        s_scr[1, :, cols] = _dot_nt(q1, kj) + bias

    lam = (jnp.exp(jnp.sum(lq1_ref[...] * lk1_ref[...], axis=-1, keepdims=True))
           - jnp.exp(jnp.sum(lq2_ref[...] * lk2_ref[...], axis=-1, keepdims=True)) + LAMBDA_INIT)

    def body(r, c):
        rows = pl.ds(pl.multiple_of(r * rc, rc), rc)
        s0 = s_scr[0, rows, :]
        s1 = s_scr[1, rows, :]
        p0 = jnp.exp(s0 - jnp.max(s0, axis=-1, keepdims=True))
        p1 = jnp.exp(s1 - jnp.max(s1, axis=-1, keepdims=True))
        r0 = 1.0 / jnp.sum(p0, axis=-1, keepdims=True)
        r1 = lam / jnp.sum(p1, axis=-1, keepdims=True)
        a_scr[rows, :] = (p0 * r0 - p1 * r1).astype(BF16)
        return c
    lax.fori_loop(0, tq // rc, body, 0)

    o = _dot(a_scr[...], v_ref[0])
    ms = jnp.mean(o * o, axis=-1, keepdims=True)
    o_ref[0] = (o * lax.rsqrt(ms + LN_EPS) * g_ref[...] * (1.0 - LAMBDA_INIT)).astype(BF16)


def _diff_attn(proj, band, lq1, lk1, lq2, lk2, g, tq=256, rc=32):
    b, s, _ = proj.shape
    cb = HEAD_DIM
    vec = lambda n: pl.BlockSpec((1, n), lambda bi, h, i: (0, 0))
    return pl.pallas_call(
        functools.partial(_diff_kernel, tq=tq, rc=rc),
        grid=(b, DIFF_HEADS, s // tq),
        in_specs=[pl.BlockSpec((1, tq, cb), lambda bi, h, i: (bi, i, OFF_DQ // cb + h)),
                  pl.BlockSpec((1, s, cb), lambda bi, h, i: (bi, 0, OFF_DK // cb + h)),
                  pl.BlockSpec((1, s, cb), lambda bi, h, i: (bi, 0, OFF_DV // cb + h)),
                  pl.BlockSpec((1, 5, tq, tq), lambda bi, h, i: (h, 0, 0, 0)),
                  vec(DIFF_QK), vec(DIFF_QK), vec(DIFF_QK), vec(DIFF_QK), vec(HEAD_DIM)],
        out_specs=pl.BlockSpec((1, tq, cb), lambda bi, h, i: (bi, i, h)),
        out_shape=jax.ShapeDtypeStruct((b, s, BRANCH_W), BF16),
        scratch_shapes=[pltpu.VMEM((2, tq, s), F32), pltpu.VMEM((tq, s), BF16)],
        compiler_params=_params("parallel", "parallel", "arbitrary"),
        name="diff_attn",
    )(proj, proj, proj, band, lq1, lk1, lq2, lk2, g)


def _win_kernel(sink_ref, q_ref, kp_ref, km_ref, kn_ref, vp_ref, vm_ref, vn_ref, bias_ref, o_ref, *, tq):
    g = pl.program_id(1)
    i = pl.program_id(2)
    first = i == 0
    last = i == pl.num_programs(2) - 1
    kp, km, kn = kp_ref[0], km_ref[0], kn_ref[0]
    vp, vm, vn = vp_ref[0], vm_ref[0], vn_ref[0]
    scale = jnp.asarray(HEAD_DIM ** -0.5, F32)
    w = WINDOW
    for j in range(WIN_GROUP):
        cols = slice(j * HEAD_DIM, (j + 1) * HEAD_DIM)
        qh = q_ref[0, :, cols]
        sp = _dot_nt(qh, kp) * scale + bias_ref[j, :, 0:w]
        sm = _dot_nt(qh, km) * scale + bias_ref[j, :, w:w + tq]
        sn = _dot_nt(qh, kn) * scale + bias_ref[j, :, w + tq:]
        sp = jnp.where(first, NEG, sp)
        sn = jnp.where(last, NEG, sn)
        sink = sink_ref[g * WIN_GROUP + j]
        m = jnp.maximum(jnp.maximum(jnp.max(sp, axis=-1, keepdims=True), jnp.max(sm, axis=-1, keepdims=True)),
                        jnp.maximum(jnp.max(sn, axis=-1, keepdims=True), sink))
        pp, pm, pn = jnp.exp(sp - m), jnp.exp(sm - m), jnp.exp(sn - m)
        den = (jnp.sum(pp, axis=-1, keepdims=True) + jnp.sum(pm, axis=-1, keepdims=True)
               + jnp.sum(pn, axis=-1, keepdims=True) + jnp.exp(sink - m))
        r = 1.0 / den
        o = (_dot((pp * r).astype(BF16), vp) + _dot((pm * r).astype(BF16), vm)
             + _dot((pn * r).astype(BF16), vn))
        o_ref[0, :, cols] = o.astype(BF16)


def _win_attn(proj, bias, sink, tq=256):
    b, s, _ = proj.shape
    hd = HEAD_DIM
    nb = s // WINDOW
    r = tq // WINDOW
    prev = lambda bi, g, i: jnp.maximum(i * r - 1, 0)
    nxt = lambda bi, g, i: jnp.minimum((i + 1) * r, nb - 1)
    kcol, vcol = OFF_WK // hd, OFF_WV // hd
    return pl.pallas_call(
        functools.partial(_win_kernel, tq=tq),
        grid=(b, WIN_KV_HEADS, s // tq),
        in_specs=[pl.BlockSpec(memory_space=pltpu.SMEM),
                  pl.BlockSpec((1, tq, WIN_GROUP * hd), lambda bi, g, i: (bi, i, OFF_WQ // (WIN_GROUP * hd) + g)),
                  pl.BlockSpec((1, WINDOW, hd), lambda bi, g, i: (bi, prev(bi, g, i), kcol + g)),
                  pl.BlockSpec((1, tq, hd), lambda bi, g, i: (bi, i, kcol + g)),
                  pl.BlockSpec((1, WINDOW, hd), lambda bi, g, i: (bi, nxt(bi, g, i), kcol + g)),
                  pl.BlockSpec((1, WINDOW, hd), lambda bi, g, i: (bi, prev(bi, g, i), vcol + g)),
                  pl.BlockSpec((1, tq, hd), lambda bi, g, i: (bi, i, vcol + g)),
                  pl.BlockSpec((1, WINDOW, hd), lambda bi, g, i: (bi, nxt(bi, g, i), vcol + g)),
                  pl.BlockSpec((WIN_GROUP, tq, tq + 2 * WINDOW), lambda bi, g, i: (g, 0, 0))],
        out_specs=pl.BlockSpec((1, tq, WIN_GROUP * hd), lambda bi, g, i: (bi, i, g)),
        out_shape=jax.ShapeDtypeStruct((b, s, BRANCH_W), BF16),
        compiler_params=_params("parallel", "parallel", "arbitrary"),
        name="win_attn",
    )(sink, proj, proj, proj, proj, proj, proj, proj, bias)


def _mem_attn_kernel(q_ref, k_ref, v_ref, o_ref):
    s = _dot_nt(q_ref[0], k_ref[0]) * jnp.asarray(MEM_DIM ** -0.5, F32)
    p = jnp.exp(s - jnp.max(s, axis=-1, keepdims=True))
    p = p * (1.0 / jnp.sum(p, axis=-1, keepdims=True))
    o_ref[0] = _dot(p.astype(BF16), v_ref[0]).astype(BF16)


def _mem_attn(proj, memkv, tq=512):
    b, s, _ = proj.shape
    md = MEM_DIM
    return pl.pallas_call(
        _mem_attn_kernel,
        grid=(b, MEM_HEADS, s // tq),
        in_specs=[pl.BlockSpec((1, tq, md), lambda bi, h, i: (bi, i, OFF_MQ // md + h)),
                  pl.BlockSpec((1, N_MEM, md), lambda bi, h, i: (bi, 0, h)),
                  pl.BlockSpec((1, N_MEM, md), lambda bi, h, i: (bi, 0, MEM_HEADS + h))],
        out_specs=pl.BlockSpec((1, tq, md), lambda bi, h, i: (bi, i, h)),
        out_shape=jax.ShapeDtypeStruct((b, s, BRANCH_W), BF16),
        compiler_params=_params("parallel", "parallel", "arbitrary"),
        name="mem_attn",
    )(proj, memkv, memkv)


def _mix_kernel(x_ref, lg_ref, lb_ref, a_ref, b_ref, c_ref, wg0_ref, wg1_ref, wg2_ref,
                bg0_ref, bg1_ref, bg2_ref, wb_ref, wo_ref, g1_ref, b1_ref, o_ref, h_scr, acc_scr):
    n = pl.program_id(1)
    nrow = x_ref.shape[0] // LN_ROWS

    @pl.when(n == 0)
    def _():
        def body(r, c):
            rows = pl.ds(pl.multiple_of(r * LN_ROWS, LN_ROWS), LN_ROWS)
            h_scr[rows, :] = _ln_rows(x_ref[rows, :], lg_ref[...], lb_ref[...]).astype(BF16)
            return c
        lax.fori_loop(0, nrow, body, 0)
        acc_scr[...] = jnp.zeros_like(acc_scr)

    h = h_scr[...]
    mixed = None
    for br_ref, wg_ref, bg_ref, k in ((a_ref, wg0_ref, bg0_ref, 0), (b_ref, wg1_ref, bg1_ref, 1),
                                      (c_ref, wg2_ref, bg2_ref, 2)):
        gate = jax.nn.sigmoid(_dot(h, wg_ref[...]) + bg_ref[...])
        term = gate * _dot(br_ref[...], wb_ref[k])
        mixed = term if mixed is None else mixed + term
    acc_scr[...] += _dot(mixed.astype(BF16), wo_ref[...])

    @pl.when(n == pl.num_programs(1) - 1)
    def _():
        def body(r, c):
            rows = pl.ds(pl.multiple_of(r * LN_ROWS, LN_ROWS), LN_ROWS)
            hh = _ln_rows(x_ref[rows, :], lg_ref[...], lb_ref[...])
            o_ref[rows, :] = _ln_rows(ALPHA * hh + acc_scr[rows, :], g1_ref[...], b1_ref[...])
            return c
        lax.fori_loop(0, nrow, body, 0)


def _mix(x2, lg, lb, a, b, c, wg, bg, wb, wo, g1, b1, tm=512, tn=256):
    m, d = x2.shape
    bw = a.shape[1]
    nn = d // tn
    row = lambda w: pl.BlockSpec((tm, w), lambda i, n: (i, 0))
    vec = pl.BlockSpec((1, d), lambda i, n: (0, 0))
    wgs = [pl.BlockSpec((d, tn), functools.partial(lambda i, n, k: (0, k * nn + n), k=k)) for k in range(3)]
    bgs = [pl.BlockSpec((1, tn), functools.partial(lambda i, n, k: (0, k * nn + n), k=k)) for k in range(3)]
    return pl.pallas_call(
        _mix_kernel,
        grid=(m // tm, nn),
        in_specs=[row(d), vec, vec, row(bw), row(bw), row(bw), *wgs, *bgs,
                  pl.BlockSpec((3, bw, tn), lambda i, n: (0, 0, n)),
                  pl.BlockSpec((tn, d), lambda i, n: (n, 0)), vec, vec],
        out_specs=pl.BlockSpec((tm, d), lambda i, n: (i, 0)),
        out_shape=jax.ShapeDtypeStruct((m, d), F32),
        scratch_shapes=[pltpu.VMEM((tm, d), BF16), pltpu.VMEM((tm, d), F32)],
        compiler_params=_params("parallel", "arbitrary"),
        name="mix",
    )(x2, lg, lb, a, b, c, wg, wg, wg, bg, bg, bg, wb, wo, g1, b1)


HALO = 16


def _gelu_tanh(x):
    return 0.5 * x * (1.0 + jnp.tanh(math.sqrt(2.0 / math.pi) * (x + 0.044715 * (x * x * x))))


def _ffn_kernel(h_ref, hp_ref, hn_ref, wv_ref, wg_ref, cwv_ref, cwg_ref, cbv_ref, cbg_ref, wd_ref,
                g2_ref, b2_ref, o_ref, hx_scr, acc_scr, *, tiles_per_seq):
    i = pl.program_id(0)
    f = pl.program_id(1)
    tm = h_ref.shape[0]
    nrow = tm // LN_ROWS

    @pl.when(f == 0)
    def _():
        seq_first = (i % tiles_per_seq) == 0
        seq_last = (i % tiles_per_seq) == tiles_per_seq - 1
        hx_scr[0:HALO, :] = jnp.where(seq_first, 0.0, hp_ref[...]).astype(BF16)
        hx_scr[HALO:HALO + tm, :] = h_ref[...].astype(BF16)
        hx_scr[HALO + tm:, :] = jnp.where(seq_last, 0.0, hn_ref[...]).astype(BF16)
        acc_scr[...] = jnp.zeros_like(acc_scr)

    hx = hx_scr[...]
    ext = tm + 2 * HALO

    def conv(u, cw_ref, cb_ref):
        prev = pltpu.roll(u, 1, 0)[HALO:HALO + tm]
        nxt = pltpu.roll(u, ext - 1, 0)[HALO:HALO + tm]
        return prev * cw_ref[0:1, :] + u[HALO:HALO + tm] * cw_ref[1:2, :] + nxt * cw_ref[2:3, :] + cb_ref[...]

    val = conv(_dot(hx, wv_ref[...]), cwv_ref, cbv_ref)
    gate = conv(_dot(hx, wg_ref[...]), cwg_ref, cbg_ref)
    acc_scr[...] += _dot((_gelu_tanh(gate) * val).astype(BF16), wd_ref[...])

    @pl.when(f == pl.num_programs(1) - 1)
    def _():
        def body(r, c):
            rows = pl.ds(pl.multiple_of(r * LN_ROWS, LN_ROWS), LN_ROWS)
            o_ref[rows, :] = _ln_rows(ALPHA * h_ref[rows, :] + acc_scr[rows, :], g2_ref[...], b2_ref[...])
            return c
        lax.fori_loop(0, nrow, body, 0)


def _ffn(h1, w_up, cw, cb, w_down, g2, b2, seq, tm=512, tf=512):
    m, d = h1.shape
    nf = D_FF_PAD // tf
    hb = tm // HALO
    nhb = m // HALO
    vec = pl.BlockSpec((1, d), lambda i, f: (0, 0))
    return pl.pallas_call(
        functools.partial(_ffn_kernel, tiles_per_seq=seq // tm),
        grid=(m // tm, nf),
        in_specs=[pl.BlockSpec((tm, d), lambda i, f: (i, 0)),
                  pl.BlockSpec((HALO, d), lambda i, f: (jnp.maximum(i * hb - 1, 0), 0)),
                  pl.BlockSpec((HALO, d), lambda i, f: (jnp.minimum((i + 1) * hb, nhb - 1), 0)),
                  pl.BlockSpec((d, tf), lambda i, f: (0, f)),
                  pl.BlockSpec((d, tf), lambda i, f: (0, nf + f)),
                  pl.BlockSpec((3, tf), lambda i, f: (0, f)),
                  pl.BlockSpec((3, tf), lambda i, f: (0, nf + f)),
                  pl.BlockSpec((1, tf), lambda i, f: (0, f)),
                  pl.BlockSpec((1, tf), lambda i, f: (0, nf + f)),
                  pl.BlockSpec((tf, d), lambda i, f: (f, 0)), vec, vec],
        out_specs=pl.BlockSpec((tm, d), lambda i, f: (i, 0)),
        out_shape=jax.ShapeDtypeStruct((m, d), F32),
        scratch_shapes=[pltpu.VMEM((tm + 2 * HALO, d), BF16), pltpu.VMEM((tm, d), F32)],
        compiler_params=_params("parallel", "arbitrary"),
        name="ffn",
    )(h1, h1, h1, w_up, w_up, cw, cw, cb, cb, w_down, g2, b2)


def _pad_ff(t, axis):
    val, gate = jnp.split(t, 2, axis=axis)
    pad = [(0, 0)] * t.ndim
    pad[axis] = (0, D_FF_PAD - D_FF)
    return jnp.concatenate([jnp.pad(val, pad), jnp.pad(gate, pad)], axis=axis)


def kernel(x, mem, ln_in_g, ln_in_b, rel_table, w_in, w_mem_kv, diff_lq1, diff_lk1, diff_lq2, diff_lk2,
           diff_subln_g, win_sink, w_gate, b_gate, w_branch, w_o, ln1_g, ln1_b, w_up, conv_w, conv_b,
           w_down, ln2_g, ln2_b):
    assert w_in.shape[0] == DEPTH == 1
    bsz, seq, d = x.shape
    x2 = x.reshape(bsz * seq, d)
    row = lambda v: v.reshape(1, -1).astype(F32)
    l = 0
    proj = _ln_proj(x2, row(ln_in_g), row(ln_in_b), w_in[l].astype(BF16)).reshape(bsz, seq, IN_W)
    memkv = _mem_kv(mem.reshape(bsz * N_MEM, d), w_mem_kv[l].astype(BF16)).reshape(bsz, N_MEM, 2 * BRANCH_W)

    tq_d, tq_w = 256, 256
    a = _diff_attn(proj, _diff_bias_tiles(rel_table[:, :DIFF_HEADS], tq_d), row(diff_lq1[l]), row(diff_lk1[l]),
                   row(diff_lq2[l]), row(diff_lk2[l]), row(diff_subln_g[l]), tq=tq_d)
    b = _win_attn(proj, _win_bias_tiles(rel_table[:, DIFF_HEADS:], tq_w), win_sink[l].astype(F32), tq=tq_w)
    c = _mem_attn(proj, memkv)

    m = bsz * seq
    h1 = _mix(x2, row(ln_in_g), row(ln_in_b), a.reshape(m, -1), b.reshape(m, -1), c.reshape(m, -1),
              w_gate[l].astype(BF16), row(b_gate[l]), w_branch[l].astype(BF16), w_o[l].astype(BF16),
              row(ln1_g[l]), row(ln1_b[l]))

    w_up_p = _pad_ff(w_up[l], 1).astype(BF16)
    cw_p = _pad_ff(conv_w[l].astype(F32), 1)
    cb_p = _pad_ff(row(conv_b[l]), 1)
    w_down_p = jnp.pad(w_down[l], ((0, D_FF_PAD - D_FF), (0, 0))).astype(BF16)
    out = _ffn(h1, w_up_p, cw_p, cb_p, w_down_p, row(ln2_g[l]), row(ln2_b[l]), seq)
    return out.reshape(bsz, seq, d)
```

```python
import functools
import math

import jax
import jax.numpy as jnp
from jax import lax
from jax.experimental import pallas as pl
from jax.experimental.pallas import tpu as pltpu

F32 = jnp.float32
BF16 = jnp.bfloat16

D_MODEL = 2048
SEQ = 2048
N_MEM = 256
HEAD_DIM = 128
BRANCH_W = 1024
DIFF_HEADS = 8
DIFF_QK = 64
WIN_HEADS = 8
WIN_KV_HEADS = 2
WIN_GROUP = WIN_HEADS // WIN_KV_HEADS
WINDOW = 128
MEM_HEADS = 4
MEM_DIM = 256
OFF_DQ, OFF_DK, OFF_DV, OFF_WQ, OFF_WK, OFF_WV, OFF_MQ = 0, 1024, 2048, 3072, 4096, 4352, 4608
IN_W = 5632
D_FF = 5504
D_FF_PAD = 5632
REL_BUCKETS = 32
REL_MAX_DIST = 128
DEPTH = 1
ALPHA = (2 * DEPTH) ** 0.25
LN_EPS = 1e-5
NEG = -1e30
LOG2E = math.log2(math.e)
LAMBDA_INIT =0.8 - 0.6 * math.exp(-0.3 * 0)

VMEM_LIMIT = 56 * 1024 * 1024
LN_ROWS = 128


def _ln_rows(x, g, b):
    mu = jnp.mean(x, axis=-1, keepdims=True)
    xc = x - mu
    var = jnp.mean(xc * xc, axis=-1, keepdims=True)
    return xc * lax.rsqrt(var + LN_EPS) * g + b


def _dot(a, b):
    return jnp.dot(a, b, preferred_element_type=F32)


def _dot_nt(a, b):
    return lax.dot_general(a, b, (((1,), (1,)), ((), ())), preferred_element_type=F32)


def _params(*sem):
    return pltpu.CompilerParams(dimension_semantics=sem, vmem_limit_bytes=VMEM_LIMIT)


def _ln_proj_kernel(x_ref, g_ref, b_ref, w_ref, o_ref, h_scr):
    @pl.when(pl.program_id(1) == 0)
    def _():
        def body(r, c):
            rows = pl.ds(pl.multiple_of(r * LN_ROWS, LN_ROWS), LN_ROWS)
            h_scr[rows, :] = _ln_rows(x_ref[rows, :], g_ref[...], b_ref[...]).astype(BF16)
            return c
        lax.fori_loop(0, x_ref.shape[0] // LN_ROWS, body, 0)

    o_ref[...] = _dot(h_scr[...], w_ref[...]).astype(BF16)


def _ln_proj(x2, g, b, w, tm=1024, tn=512):
    m, d = x2.shape
    n = w.shape[1]
    return pl.pallas_call(
        _ln_proj_kernel,
        grid=(m // tm, n // tn),
        in_specs=[pl.BlockSpec((tm, d), lambda i, j: (i, 0)),
                  pl.BlockSpec((1, d), lambda i, j: (0, 0)),
                  pl.BlockSpec((1, d), lambda i, j: (0, 0)),
                  pl.BlockSpec((d, tn), lambda i, j: (0, j))],
        out_specs=pl.BlockSpec((tm, tn), lambda i, j: (i, j)),
        out_shape=jax.ShapeDtypeStruct((m, n), BF16),
        scratch_shapes=[pltpu.VMEM((tm, d), BF16)],
        compiler_params=_params("parallel", "arbitrary"),
        name="ln_proj",
    )(x2, g, b, w)


def _mem_kv_kernel(m_ref, w_ref, o_ref):
    o_ref[...] = _dot(m_ref[...].astype(BF16), w_ref[...]).astype(BF16)


def _mem_kv(mem2, w, tn=512):
    m, d = mem2.shape
    n = w.shape[1]
    return pl.pallas_call(
        _mem_kv_kernel,
        grid=(n // tn,),
        in_specs=[pl.BlockSpec((m, d), lambda j: (0, 0)),
                  pl.BlockSpec((d, tn), lambda j: (0, j))],
        out_specs=pl.BlockSpec((m, tn), lambda j: (0, j)),
        out_shape=jax.ShapeDtypeStruct((m, n), BF16),
        compiler_params=_params("parallel"),
        name="mem_kv",
    )(mem2, w)


def _t5_bucket(rel):
    half = REL_BUCKETS // 2
    max_exact = half // 2
    ret = jnp.where(rel > 0, half, 0)
    n = jnp.abs(rel)
    nf = jnp.maximum(n, 1).astype(F32)
    large = max_exact + (jnp.log(nf / max_exact) / math.log(REL_MAX_DIST / max_exact)
                         * (half - max_exact)).astype(jnp.int32)
    large = jnp.minimum(large, half - 1)
    return ret + jnp.where(n < max_exact, n, large)


BIAS_ROWS = 64


def _bias_lookup_kernel(tab_ref, bucket_ref, o_ref):
    h = pl.program_id(0)

    def body(r, carry):
        rows = pl.ds(pl.multiple_of(r * BIAS_ROWS, BIAS_ROWS), BIAS_ROWS)
        bk = bucket_ref[rows, :]
        out = jnp.full(bk.shape, NEG, F32)
        for b in range(REL_BUCKETS):
            out = jnp.where(bk == b, tab_ref[b, h], out)
        o_ref[0, rows, :] = out
        return carry
    lax.fori_loop(0, bucket_ref.shape[0] // BIAS_ROWS, body, 0)


def _bias_lookup(table, bucket):
    nh = table.shape[1]
    r, c = bucket.shape
    return pl.pallas_call(
        _bias_lookup_kernel,
        grid=(nh,),
        in_specs=[pl.BlockSpec(memory_space=pltpu.SMEM),
                  pl.BlockSpec((r, c), lambda h: (0, 0))],
        out_specs=pl.BlockSpec((1, r, c), lambda h: (h, 0, 0)),
        out_shape=jax.ShapeDtypeStruct((nh, r, c), F32),
        compiler_params=_params("parallel"),
        name="bias_lookup",
    )(table.astype(F32), bucket)


def _diff_bias_tiles(table, tq):
    d = jnp.arange(5)[:, None, None] - 2
    q = jnp.arange(tq)[None, :, None]
    k = jnp.arange(tq)[None, None, :]
    bucket = _t5_bucket(d * tq + k - q).astype(jnp.int32).reshape(5 * tq, tq)
    return _bias_lookup(table, bucket).reshape(table.shape[1], 5, tq, tq)


def _win_bias_tiles(table, tq):
    q = jnp.arange(tq)[:, None]
    c = jnp.arange(tq + 2 * WINDOW)[None, :]
    rel = c - WINDOW - q
    bucket = jnp.where(jnp.abs(rel) <= WINDOW, _t5_bucket(rel), -1).astype(jnp.int32)
    return _bias_lookup(table, bucket)


def _diff_kernel(q_ref, k_ref, v_ref, band_ref, lq1_ref, lk1_ref, lq2_ref, lk2_ref, g_ref, o_ref,
                 s_scr, p_scr, den_scr, *, tq, rc, n_tiles, tiles_per_head):
    t = pl.program_id(0)

    @pl.when(t == 0)
    def _():
        s_scr[...] = jnp.zeros_like(s_scr)
        p_scr[...] = jnp.zeros_like(p_scr)
        den_scr[...] = jnp.ones_like(den_scr)

    def tick(slot):
        other = 1 - slot
        i = jnp.minimum(t, n_tiles - 1) % tiles_per_head
        nk = k_ref.shape[1] // tq
        q = (q_ref[0].astype(F32) * (DIFF_QK ** -0.5 * LOG2E)).astype(BF16)
        lane = lax.broadcasted_iota(jnp.int32, q.shape, 1)
        zero = jnp.zeros_like(q)
        qs = jnp.concatenate([jnp.where(lane < DIFF_QK, q, zero), jnp.where(lane >= DIFF_QK, q, zero)], axis=0)
        for j in range(nk):
            cols = slice(j * tq, (j + 1) * tq)
            bias = band_ref[0, jnp.clip(j - i, -2, 2) + 2]
            s = _dot_nt(qs, k_ref[0, cols, :])
            s_scr[slot, 0:tq, cols] = s[0:tq] + bias
            s_scr[slot, tq:, cols] = s[tq:] + bias

        nchunk = 2 * tq // rc
        chunk = lambda r: slice(r * rc, (r + 1) * rc)
        row_max = [jnp.max(s_scr[other, chunk(r), :], axis=-1, keepdims=True) for r in range(nchunk)]
        for r in range(nchunk):
            p = jnp.exp2(s_scr[other, chunk(r), :] - row_max[r])
            den_scr[other, chunk(r), :] = jnp.sum(p, axis=-1, keepdims=True)
            p_scr[other, chunk(r), :] = p.astype(BF16)

        lam = (jnp.exp(jnp.sum(lq1_ref[...] * lk1_ref[...], axis=-1, keepdims=True))
               - jnp.exp(jnp.sum(lq2_ref[...] * lk2_ref[...], axis=-1, keepdims=True)) + LAMBDA_INIT)
        pv = _dot(p_scr[slot], v_ref[0])
        den = den_scr[slot]
        o = pv[0:tq] * (1.0 / den[0:tq]) - pv[tq:] * (lam / den[tq:])
        ms = jnp.mean(o * o, axis=-1, keepdims=True)
        o_ref[0] = (o * lax.rsqrt(ms + LN_EPS) * g_ref[...] * (1.0 - LAMBDA_INIT)).astype(BF16)

    pl.when(t % 2 == 0)(functools.partial(tick, 0))
    pl.when(t % 2 == 1)(functools.partial(tick, 1))


def _diff_attn(proj, band, lq1, lk1, lq2, lk2, g, tq=256, rc=32):
    b, s, _ = proj.shape
    cb = HEAD_DIM
    ni = s // tq
    n_tiles = b * DIFF_HEADS * ni
    depth = 2

    def tile(t, lag):
        tc = jnp.clip(t - lag, 0, n_tiles - 1)
        return tc // (DIFF_HEADS * ni), (tc // ni) % DIFF_HEADS, tc % ni

    def q_map(t):
        bi, h, i = tile(t, 0)
        return bi, i, OFF_DQ // cb + h

    def k_map(t):
        bi, h, _ = tile(t, 0)
        return bi, 0, OFF_DK // cb + h

    def v_map(t):
        bi, h, _ = tile(t, depth)
        return bi, 0, OFF_DV // cb + h

    def o_map(t):
        bi, h, i = tile(t, depth)
        return bi, i, h

    vec = lambda n: pl.BlockSpec((1, n), lambda t: (0, 0))
    return pl.pallas_call(
        functools.partial(_diff_kernel, tq=tq, rc=rc, n_tiles=n_tiles, tiles_per_head=ni),
        grid=(n_tiles + depth,),
        in_specs=[pl.BlockSpec((1, tq, cb), q_map),
                  pl.BlockSpec((1, s, cb), k_map),
                  pl.BlockSpec((1, s, cb), v_map),
                  pl.BlockSpec((1, 5, tq, tq), lambda t: (tile(t, 0)[1], 0, 0, 0)),
                  vec(DIFF_QK), vec(DIFF_QK), vec(DIFF_QK), vec(DIFF_QK), vec(HEAD_DIM)],
        out_specs=pl.BlockSpec((1, tq, cb), o_map),
        out_shape=jax.ShapeDtypeStruct((b, s, BRANCH_W), BF16),
        scratch_shapes=[pltpu.VMEM((2, 2 * tq, s), F32), pltpu.VMEM((2, 2 * tq, s), BF16),
                        pltpu.VMEM((2, 2 * tq, 1), F32)],
        compiler_params=_params("arbitrary"),
        name="diff_attn",
    )(proj, proj, proj, band, lq1, lk1, lq2, lk2, g)


def _win_kernel(sink_ref, q_ref, kp_ref, km_ref, kn_ref, vp_ref, vm_ref, vn_ref, bias_ref, o_ref, *, tq):
    g = pl.program_id(1)
    i = pl.program_id(2)
    first = i == 0
    last = i == pl.num_programs(2) - 1
    kp, km, kn = kp_ref[0], km_ref[0], kn_ref[0]
    vp, vm, vn = vp_ref[0], vm_ref[0], vn_ref[0]
    scale = jnp.asarray(HEAD_DIM ** -0.5, F32)
    w = WINDOW
    for j in range(WIN_GROUP):
        cols = slice(j * HEAD_DIM, (j + 1) * HEAD_DIM)
        qh = q_ref[0, :, cols]
        sp = _dot_nt(qh, kp) * scale + bias_ref[j, :, 0:w]
        sm = _dot_nt(qh, km) * scale + bias_ref[j, :, w:w + tq]
        sn = _dot_nt(qh, kn) * scale + bias_ref[j, :, w + tq:]
        sp = jnp.where(first, NEG, sp)
        sn = jnp.where(last, NEG, sn)
        sink = sink_ref[g * WIN_GROUP + j]
        m = jnp.maximum(jnp.maximum(jnp.max(sp, axis=-1, keepdims=True), jnp.max(sm, axis=-1, keepdims=True)),
                        jnp.maximum(jnp.max(sn, axis=-1, keepdims=True), sink))
        pp, pm, pn = jnp.exp(sp - m), jnp.exp(sm - m), jnp.exp(sn - m)
        den = (jnp.sum(pp, axis=-1, keepdims=True) + jnp.sum(pm, axis=-1, keepdims=True)
               + jnp.sum(pn, axis=-1, keepdims=True) + jnp.exp(sink - m))
        r = 1.0 / den
        o = (_dot((pp * r).astype(BF16), vp) + _dot((pm * r).astype(BF16), vm)
             + _dot((pn * r).astype(BF16), vn))
        o_ref[0, :, cols] = o.astype(BF16)


def _win_attn(proj, bias, sink, tq=256):
    b, s, _ = proj.shape
    hd = HEAD_DIM
    nb = s // WINDOW
    r = tq // WINDOW
    prev = lambda bi, g, i: jnp.maximum(i * r - 1, 0)
    nxt = lambda bi, g, i: jnp.minimum((i + 1) * r, nb - 1)
    kcol, vcol = OFF_WK // hd, OFF_WV // hd
    return pl.pallas_call(
        functools.partial(_win_kernel, tq=tq),
        grid=(b, WIN_KV_HEADS, s // tq),
        in_specs=[pl.BlockSpec(memory_space=pltpu.SMEM),
                  pl.BlockSpec((1, tq, WIN_GROUP * hd), lambda bi, g, i: (bi, i, OFF_WQ // (WIN_GROUP * hd) + g)),
                  pl.BlockSpec((1, WINDOW, hd), lambda bi, g, i: (bi, prev(bi, g, i), kcol + g)),
                  pl.BlockSpec((1, tq, hd), lambda bi, g, i: (bi, i, kcol + g)),
                  pl.BlockSpec((1, WINDOW, hd), lambda bi, g, i: (bi, nxt(bi, g, i), kcol + g)),
                  pl.BlockSpec((1, WINDOW, hd), lambda bi, g, i: (bi, prev(bi, g, i), vcol + g)),
                  pl.BlockSpec((1, tq, hd), lambda bi, g, i: (bi, i, vcol + g)),
                  pl.BlockSpec((1, WINDOW, hd), lambda bi, g, i: (bi, nxt(bi, g, i), vcol + g)),
                  pl.BlockSpec((WIN_GROUP, tq, tq + 2 * WINDOW), lambda bi, g, i: (g, 0, 0))],
        out_specs=pl.BlockSpec((1, tq, WIN_GROUP * hd), lambda bi, g, i: (bi, i, g)),
        out_shape=jax.ShapeDtypeStruct((b, s, BRANCH_W), BF16),
        compiler_params=_params("parallel", "parallel", "arbitrary"),
        name="win_attn",
    )(sink, proj, proj, proj, proj, proj, proj, proj, bias)


def _mem_attn_kernel(q_ref, k_ref, v_ref, o_ref):
    s = _dot_nt(q_ref[0], k_ref[0]) * jnp.asarray(MEM_DIM ** -0.5, F32)
    p = jnp.exp(s - jnp.max(s, axis=-1, keepdims=True))
    p = p * (1.0 / jnp.sum(p, axis=-1, keepdims=True))
    o_ref[0] = _dot(p.astype(BF16), v_ref[0]).astype(BF16)


def _mem_attn(proj, memkv, tq=512):
    b, s, _ = proj.shape
    md = MEM_DIM
    return pl.pallas_call(
        _mem_attn_kernel,
        grid=(b, MEM_HEADS, s // tq),
        in_specs=[pl.BlockSpec((1, tq, md), lambda bi, h, i: (bi, i, OFF_MQ // md + h)),
                  pl.BlockSpec((1, N_MEM, md), lambda bi, h, i: (bi, 0, h)),
                  pl.BlockSpec((1, N_MEM, md), lambda bi, h, i: (bi, 0, MEM_HEADS + h))],
        out_specs=pl.BlockSpec((1, tq, md), lambda bi, h, i: (bi, i, h)),
        out_shape=jax.ShapeDtypeStruct((b, s, BRANCH_W), BF16),
        compiler_params=_params("parallel", "parallel", "arbitrary"),
        name="mem_attn",
    )(proj, memkv, memkv)


def _mix_kernel(x_ref, lg_ref, lb_ref, a_ref, b_ref, c_ref, wg0_ref, wg1_ref, wg2_ref,
                bg0_ref, bg1_ref, bg2_ref, wb_ref, wo_ref, g1_ref, b1_ref, o_ref, h_scr, acc_scr):
    n = pl.program_id(1)
    nrow = x_ref.shape[0] // LN_ROWS

    @pl.when(n == 0)
    def _():
        def body(r, c):
            rows = pl.ds(pl.multiple_of(r * LN_ROWS, LN_ROWS), LN_ROWS)
            h_scr[rows, :] = _ln_rows(x_ref[rows, :], lg_ref[...], lb_ref[...]).astype(BF16)
            return c
        lax.fori_loop(0, nrow, body, 0)
        acc_scr[...] = jnp.zeros_like(acc_scr)

    h = h_scr[...]
    mixed = None
    for br_ref, wg_ref, bg_ref, k in ((a_ref, wg0_ref, bg0_ref, 0), (b_ref, wg1_ref, bg1_ref, 1),
                                      (c_ref, wg2_ref, bg2_ref, 2)):
        gate = jax.nn.sigmoid(_dot(h, wg_ref[...]) + bg_ref[...])
        term = gate * _dot(br_ref[...], wb_ref[k])
        mixed = term if mixed is None else mixed + term
    acc_scr[...] += _dot(mixed.astype(BF16), wo_ref[...])

    @pl.when(n == pl.num_programs(1) - 1)
    def _():
        def body(r, c):
            rows = pl.ds(pl.multiple_of(r * LN_ROWS, LN_ROWS), LN_ROWS)
            hh = _ln_rows(x_ref[rows, :], lg_ref[...], lb_ref[...])
            o_ref[rows, :] = _ln_rows(ALPHA * hh + acc_scr[rows, :], g1_ref[...], b1_ref[...])
            return c
        lax.fori_loop(0, nrow, body, 0)


def _mix(x2, lg, lb, a, b, c, wg, bg, wb, wo, g1, b1, tm=512, tn=256):
    m, d = x2.shape
    bw = a.shape[1]
    nn = d // tn
    row = lambda w: pl.BlockSpec((tm, w), lambda i, n: (i, 0))
    vec = pl.BlockSpec((1, d), lambda i, n: (0, 0))
    wgs = [pl.BlockSpec((d, tn), functools.partial(lambda i, n, k: (0, k * nn + n), k=k)) for k in range(3)]
    bgs = [pl.BlockSpec((1, tn), functools.partial(lambda i, n, k: (0, k * nn + n), k=k)) for k in range(3)]
    return pl.pallas_call(
        _mix_kernel,
        grid=(m // tm, nn),
        in_specs=[row(d), vec, vec, row(bw), row(bw), row(bw), *wgs, *bgs,
                  pl.BlockSpec((3, bw, tn), lambda i, n: (0, 0, n)),
                  pl.BlockSpec((tn, d), lambda i, n: (n, 0)), vec, vec],
        out_specs=pl.BlockSpec((tm, d), lambda i, n: (i, 0)),
        out_shape=jax.ShapeDtypeStruct((m, d), F32),
        scratch_shapes=[pltpu.VMEM((tm, d), BF16), pltpu.VMEM((tm, d), F32)],
        compiler_params=_params("parallel", "arbitrary"),
        name="mix",
    )(x2, lg, lb, a, b, c, wg, wg, wg, bg, bg, bg, wb, wo, g1, b1)


HALO = 16


def _gelu_tanh(x):
    return 0.5 * x * (1.0 + jnp.tanh(math.sqrt(2.0 / math.pi) * (x + 0.044715 * (x * x * x))))


def _ffn_kernel(h_ref, hp_ref, hn_ref, wv_ref, wg_ref, cwv_ref, cwg_ref, cbv_ref, cbg_ref, wd_ref,
                g2_ref, b2_ref, o_ref, hx_scr, acc_scr, *, tiles_per_seq):
    i = pl.program_id(0)
    f = pl.program_id(1)
    tm = h_ref.shape[0]
    nrow = tm // LN_ROWS

    @pl.when(f == 0)
    def _():
        seq_first = (i % tiles_per_seq) == 0
        seq_last = (i % tiles_per_seq) == tiles_per_seq - 1
        hx_scr[0:HALO, :] = jnp.where(seq_first, 0.0, hp_ref[...]).astype(BF16)
        hx_scr[HALO:HALO + tm, :] = h_ref[...].astype(BF16)
        hx_scr[HALO + tm:, :] = jnp.where(seq_last, 0.0, hn_ref[...]).astype(BF16)
        acc_scr[...] = jnp.zeros_like(acc_scr)

    hx = hx_scr[...]
    ext = tm + 2 * HALO

    def conv(u, cw_ref, cb_ref):
        prev = pltpu.roll(u, 1, 0)[HALO:HALO + tm]
        nxt = pltpu.roll(u, ext - 1, 0)[HALO:HALO + tm]
        return prev * cw_ref[0:1, :] + u[HALO:HALO + tm] * cw_ref[1:2, :] + nxt * cw_ref[2:3, :] + cb_ref[...]

    val = conv(_dot(hx, wv_ref[...]), cwv_ref, cbv_ref)
    gate = conv(_dot(hx, wg_ref[...]), cwg_ref, cbg_ref)
    acc_scr[...] += _dot((_gelu_tanh(gate) * val).astype(BF16), wd_ref[...])

    @pl.when(f == pl.num_programs(1) - 1)
    def _():
        def body(r, c):
            rows = pl.ds(pl.multiple_of(r * LN_ROWS, LN_ROWS), LN_ROWS)
            o_ref[rows, :] = _ln_rows(ALPHA * h_ref[rows, :] + acc_scr[rows, :], g2_ref[...], b2_ref[...])
            return c
        lax.fori_loop(0, nrow, body, 0)


def _ffn(h1, w_up, cw, cb, w_down, g2, b2, seq, tm=512, tf=512):
    m, d = h1.shape
    nf = D_FF_PAD // tf
    hb = tm // HALO
    nhb = m // HALO
    vec = pl.BlockSpec((1, d), lambda i, f: (0, 0))
    return pl.pallas_call(
        functools.partial(_ffn_kernel, tiles_per_seq=seq // tm),
        grid=(m // tm, nf),
        in_specs=[pl.BlockSpec((tm, d), lambda i, f: (i, 0)),
                  pl.BlockSpec((HALO, d), lambda i, f: (jnp.maximum(i * hb - 1, 0), 0)),
                  pl.BlockSpec((HALO, d), lambda i, f: (jnp.minimum((i + 1) * hb, nhb - 1), 0)),
                  pl.BlockSpec((d, tf), lambda i, f: (0, f)),
                  pl.BlockSpec((d, tf), lambda i, f: (0, nf + f)),
                  pl.BlockSpec((3, tf), lambda i, f: (0, f)),
                  pl.BlockSpec((3, tf), lambda i, f: (0, nf + f)),
                  pl.BlockSpec((1, tf), lambda i, f: (0, f)),
                  pl.BlockSpec((1, tf), lambda i, f: (0, nf + f)),
                  pl.BlockSpec((tf, d), lambda i, f: (f, 0)), vec, vec],
        out_specs=pl.BlockSpec((tm, d), lambda i, f: (i, 0)),
        out_shape=jax.ShapeDtypeStruct((m, d), F32),
        scratch_shapes=[pltpu.VMEM((tm + 2 * HALO, d), BF16), pltpu.VMEM((tm, d), F32)],
        compiler_params=_params("parallel", "arbitrary"),
        name="ffn",
    )(h1, h1, h1, w_up, w_up, cw, cw, cb, cb, w_down, g2, b2)


def _pad_ff(t, axis):
    val, gate = jnp.split(t, 2, axis=axis)
    pad = [(0, 0)] * t.ndim
    pad[axis] = (0, D_FF_PAD - D_FF)
    return jnp.concatenate([jnp.pad(val, pad), jnp.pad(gate, pad)], axis=axis)


def kernel(x, mem, ln_in_g, ln_in_b, rel_table, w_in, w_mem_kv, diff_lq1, diff_lk1, diff_lq2, diff_lk2,
           diff_subln_g, win_sink, w_gate, b_gate, w_branch, w_o, ln1_g, ln1_b, w_up, conv_w, conv_b,
           w_down, ln2_g, ln2_b):
    assert w_in.shape[0] == DEPTH == 1
    bsz, seq, d = x.shape
    x2 = x.reshape(bsz * seq, d)
    row = lambda v: v.reshape(1, -1).astype(F32)
    l = 0
    proj = _ln_proj(x2, row(ln_in_g), row(ln_in_b), w_in[l].astype(BF16)).reshape(bsz, seq, IN_W)
    memkv = _mem_kv(mem.reshape(bsz * N_MEM, d), w_mem_kv[l].astype(BF16)).reshape(bsz, N_MEM, 2 * BRANCH_W)

    tq_d, tq_w = 256, 256
    a = _diff_attn(proj, _diff_bias_tiles(rel_table[:, :DIFF_HEADS] * LOG2E, tq_d), row(diff_lq1[l]), row(diff_lk1[l]),
                   row(diff_lq2[l]), row(diff_lk2[l]), row(diff_subln_g[l]), tq=tq_d)
    b = _win_attn(proj, _win_bias_tiles(rel_table[:, DIFF_HEADS:], tq_w), win_sink[l].astype(F32), tq=tq_w)
    c = _mem_attn(proj, memkv)

    m = bsz * seq
    h1 = _mix(x2, row(ln_in_g), row(ln_in_b), a.reshape(m, -1), b.reshape(m, -1), c.reshape(m, -1),
              w_gate[l].astype(BF16), row(b_gate[l]), w_branch[l].astype(BF16), w_o[l].astype(BF16),
              row(ln1_g[l]), row(ln1_b[l]))

    w_up_p = _pad_ff(w_up[l], 1).astype(BF16)
    cw_p = _pad_ff(conv_w[l].astype(F32), 1)
    cb_p = _pad_ff(row(conv_b[l]), 1)
    w_down_p = jnp.pad(w_down[l], ((0, D_FF_PAD - D_FF), (0, 0))).astype(BF16)
    out = _ffn(h1, w_up_p, cw_p, cb_p, w_down_p, row(ln2_g[l]), row(ln2_b[l]), seq)
    return out.reshape(bsz, seq, d)
```

```python
import functools
import math

import jax
import jax.numpy as jnp
from jax import lax
from jax.experimental import pallas as pl
from jax.experimental.pallas import tpu as pltpu

F32 = jnp.float32
BF16 = jnp.bfloat16

D_MODEL = 2048
SEQ = 2048
N_MEM = 256
HEAD_DIM = 128
BRANCH_W = 1024
DIFF_HEADS = 8
DIFF_QK = 64
WIN_HEADS = 8
WIN_KV_HEADS = 2
WIN_GROUP = WIN_HEADS // WIN_KV_HEADS
WINDOW = 128
MEM_HEADS = 4
MEM_DIM = 256
OFF_DQ, OFF_DK, OFF_DV, OFF_WQ, OFF_WK, OFF_WV, OFF_MQ = 0, 1024, 2048, 3072, 4096, 4352, 4608
IN_W = 5632
D_FF = 5504
D_FF_PAD = 5632
REL_BUCKETS = 32
REL_MAX_DIST = 128
DEPTH = 1
ALPHA = (2 * DEPTH) ** 0.25
LN_EPS = 1e-5
NEG = -1e30
LOG2E = math.log2(math.e)
LAMBDA_INIT =0.8 - 0.6 * math.exp(-0.3 * 0)

VMEM_LIMIT = 56 * 1024 * 1024
LN_ROWS = 128


def _ln_rows(x, g, b):
    mu = jnp.mean(x, axis=-1, keepdims=True)
    xc = x - mu
    var = jnp.mean(xc * xc, axis=-1, keepdims=True)
    return xc * lax.rsqrt(var + LN_EPS) * g + b


def _dot(a, b):
    return jnp.dot(a, b, preferred_element_type=F32)


def _dot_nt(a, b):
    return lax.dot_general(a, b, (((1,), (1,)), ((), ())), preferred_element_type=F32)


def _params(*sem, flags=None):
    return pltpu.CompilerParams(dimension_semantics=sem, vmem_limit_bytes=VMEM_LIMIT, flags=flags)


def _ln_proj_kernel(x_ref, g_ref, b_ref, w_ref, o_ref, h_scr):
    @pl.when(pl.program_id(1) == 0)
    def _():
        def body(r, c):
            rows = pl.ds(pl.multiple_of(r * LN_ROWS, LN_ROWS), LN_ROWS)
            h_scr[rows, :] = _ln_rows(x_ref[rows, :], g_ref[...], b_ref[...]).astype(BF16)
            return c
        lax.fori_loop(0, x_ref.shape[0] // LN_ROWS, body, 0)

    o_ref[...] = _dot(h_scr[...], w_ref[...]).astype(BF16)


def _ln_proj(x2, g, b, w, tm=1024, tn=512):
    m, d = x2.shape
    n = w.shape[1]
    return pl.pallas_call(
        _ln_proj_kernel,
        grid=(m // tm, n // tn),
        in_specs=[pl.BlockSpec((tm, d), lambda i, j: (i, 0)),
                  pl.BlockSpec((1, d), lambda i, j: (0, 0)),
                  pl.BlockSpec((1, d), lambda i, j: (0, 0)),
                  pl.BlockSpec((d, tn), lambda i, j: (0, j))],
        out_specs=pl.BlockSpec((tm, tn), lambda i, j: (i, j)),
        out_shape=jax.ShapeDtypeStruct((m, n), BF16),
        scratch_shapes=[pltpu.VMEM((tm, d), BF16)],
        compiler_params=_params("parallel", "arbitrary"),
        name="ln_proj",
    )(x2, g, b, w)


def _mem_kv_kernel(m_ref, w_ref, o_ref):
    o_ref[...] = _dot(m_ref[...].astype(BF16), w_ref[...]).astype(BF16)


def _mem_kv(mem2, w, tn=512):
    m, d = mem2.shape
    n = w.shape[1]
    return pl.pallas_call(
        _mem_kv_kernel,
        grid=(n // tn,),
        in_specs=[pl.BlockSpec((m, d), lambda j: (0, 0)),
                  pl.BlockSpec((d, tn), lambda j: (0, j))],
        out_specs=pl.BlockSpec((m, tn), lambda j: (0, j)),
        out_shape=jax.ShapeDtypeStruct((m, n), BF16),
        compiler_params=_params("parallel"),
        name="mem_kv",
    )(mem2, w)


def _t5_bucket(rel):
    half = REL_BUCKETS // 2
    max_exact = half // 2
    ret = jnp.where(rel > 0, half, 0)
    n = jnp.abs(rel)
    nf = jnp.maximum(n, 1).astype(F32)
    large = max_exact + (jnp.log(nf / max_exact) / math.log(REL_MAX_DIST / max_exact)
                         * (half - max_exact)).astype(jnp.int32)
    large = jnp.minimum(large, half - 1)
    return ret + jnp.where(n < max_exact, n, large)


BIAS_ROWS = 64


def _bias_lookup_kernel(tab_ref, bucket_ref, o_ref):
    h = pl.program_id(0)

    def body(r, carry):
        rows = pl.ds(pl.multiple_of(r * BIAS_ROWS, BIAS_ROWS), BIAS_ROWS)
        bk = bucket_ref[rows, :]
        out = jnp.full(bk.shape, NEG, F32)
        for b in range(REL_BUCKETS):
            out = jnp.where(bk == b, tab_ref[b, h], out)
        o_ref[0, rows, :] = out
        return carry
    lax.fori_loop(0, bucket_ref.shape[0] // BIAS_ROWS, body, 0)


def _bias_lookup(table, bucket):
    nh = table.shape[1]
    r, c = bucket.shape
    return pl.pallas_call(
        _bias_lookup_kernel,
        grid=(nh,),
        in_specs=[pl.BlockSpec(memory_space=pltpu.SMEM),
                  pl.BlockSpec((r, c), lambda h: (0, 0))],
        out_specs=pl.BlockSpec((1, r, c), lambda h: (h, 0, 0)),
        out_shape=jax.ShapeDtypeStruct((nh, r, c), F32),
        compiler_params=_params("parallel"),
        name="bias_lookup",
    )(table.astype(F32), bucket)


def _diff_bias_tiles(table, tq):
    d = jnp.arange(5)[:, None, None] - 2
    q = jnp.arange(tq)[None, :, None]
    k = jnp.arange(tq)[None, None, :]
    bucket = _t5_bucket(d * tq + k - q).astype(jnp.int32).reshape(5 * tq, tq)
    return _bias_lookup(table, bucket).reshape(table.shape[1], 5, tq, tq)


def _win_bias_tiles(table, tq):
    q = jnp.arange(tq)[:, None]
    c = jnp.arange(tq + 2 * WINDOW)[None, :]
    rel = c - WINDOW - q
    bucket = jnp.where(jnp.abs(rel) <= WINDOW, _t5_bucket(rel), -1).astype(jnp.int32)
    return _bias_lookup(table, bucket)


def _diff_kernel(q_ref, k_ref, v_ref, band_ref, lq1_ref, lk1_ref, lq2_ref, lk2_ref, g_ref, o_ref,
                 s_scr, p_scr, den_scr, *, tq, rc, n_tiles, tiles_per_head):
    t = pl.program_id(0)

    @pl.when(t == 0)
    def _():
        s_scr[...] = jnp.zeros_like(s_scr)
        p_scr[...] = jnp.zeros_like(p_scr)
        den_scr[...] = jnp.ones_like(den_scr)

    def tick(slot):
        other = 1 - slot
        i = jnp.minimum(t, n_tiles - 1) % tiles_per_head
        nk = k_ref.shape[1] // tq
        q = (q_ref[0].astype(F32) * (DIFF_QK ** -0.5 * LOG2E)).astype(BF16)
        lane = lax.broadcasted_iota(jnp.int32, q.shape, 1)
        zero = jnp.zeros_like(q)
        qs = jnp.concatenate([jnp.where(lane < DIFF_QK, q, zero), jnp.where(lane >= DIFF_QK, q, zero)], axis=0)
        for j in range(nk):
            cols = slice(j * tq, (j + 1) * tq)
            bias = band_ref[0, jnp.clip(j - i, -2, 2) + 2]
            s = _dot_nt(qs, k_ref[0, cols, :])
            s_scr[slot, 0:tq, cols] = s[0:tq] + bias
            s_scr[slot, tq:, cols] = s[tq:] + bias

        nchunk = 2 * tq // rc
        chunk = lambda r: slice(r * rc, (r + 1) * rc)
        row_max = [jnp.max(s_scr[other, chunk(r), :], axis=-1, keepdims=True) for r in range(nchunk)]
        for r in range(nchunk):
            p = jnp.exp2(s_scr[other, chunk(r), :] - row_max[r])
            den_scr[other, chunk(r), :] = jnp.sum(p, axis=-1, keepdims=True)
            p_scr[other, chunk(r), :] = p.astype(BF16)

        lam = (jnp.exp(jnp.sum(lq1_ref[...] * lk1_ref[...], axis=-1, keepdims=True))
               - jnp.exp(jnp.sum(lq2_ref[...] * lk2_ref[...], axis=-1, keepdims=True)) + LAMBDA_INIT)
        pv = _dot(p_scr[slot], v_ref[0])
        den = den_scr[slot]
        o = pv[0:tq] * (1.0 / den[0:tq]) - pv[tq:] * (lam / den[tq:])
        ms = jnp.mean(o * o, axis=-1, keepdims=True)
        o_ref[0] = (o * lax.rsqrt(ms + LN_EPS) * g_ref[...] * (1.0 - LAMBDA_INIT)).astype(BF16)

    pl.when(t % 2 == 0)(functools.partial(tick, 0))
    pl.when(t % 2 == 1)(functools.partial(tick, 1))


def _diff_attn(proj, band, lq1, lk1, lq2, lk2, g, tq=256, rc=32):
    b, s, _ = proj.shape
    cb = HEAD_DIM
    ni = s // tq
    n_tiles = b * DIFF_HEADS * ni
    depth = 2

    def tile(t, lag):
        tc = jnp.clip(t - lag, 0, n_tiles - 1)
        return tc // (DIFF_HEADS * ni), (tc // ni) % DIFF_HEADS, tc % ni

    def q_map(t):
        bi, h, i = tile(t, 0)
        return bi, i, OFF_DQ // cb + h

    def k_map(t):
        bi, h, _ = tile(t, 0)
        return bi, 0, OFF_DK // cb + h

    def v_map(t):
        bi, h, _ = tile(t, depth)
        return bi, 0, OFF_DV // cb + h

    def o_map(t):
        bi, h, i = tile(t, depth)
        return bi, i, h

    vec = lambda n: pl.BlockSpec((1, n), lambda t: (0, 0))
    return pl.pallas_call(
        functools.partial(_diff_kernel, tq=tq, rc=rc, n_tiles=n_tiles, tiles_per_head=ni),
        grid=(n_tiles + depth,),
        in_specs=[pl.BlockSpec((1, tq, cb), q_map),
                  pl.BlockSpec((1, s, cb), k_map),
                  pl.BlockSpec((1, s, cb), v_map),
                  pl.BlockSpec((1, 5, tq, tq), lambda t: (tile(t, 0)[1], 0, 0, 0)),
                  vec(DIFF_QK), vec(DIFF_QK), vec(DIFF_QK), vec(DIFF_QK), vec(HEAD_DIM)],
        out_specs=pl.BlockSpec((1, tq, cb), o_map),
        out_shape=jax.ShapeDtypeStruct((b, s, BRANCH_W), BF16),
        scratch_shapes=[pltpu.VMEM((2, 2 * tq, s), F32), pltpu.VMEM((2, 2 * tq, s), BF16),
                        pltpu.VMEM((2, 2 * tq, 1), F32)],
        compiler_params=_params("arbitrary"),
        name="diff_attn",
    )(proj, proj, proj, band, lq1, lk1, lq2, lk2, g)


def _win_kernel(sink_ref, q_ref, kp_ref, km_ref, kn_ref, vp_ref, vm_ref, vn_ref, bias_ref, o_ref, *, tq):
    g = pl.program_id(1)
    i = pl.program_id(2)
    first = i == 0
    last = i == pl.num_programs(2) - 1
    kp, km, kn = kp_ref[0], km_ref[0], kn_ref[0]
    vp, vm, vn = vp_ref[0], vm_ref[0], vn_ref[0]
    scale = jnp.asarray(HEAD_DIM ** -0.5, F32)
    w = WINDOW
    for j in range(WIN_GROUP):
        cols = slice(j * HEAD_DIM, (j + 1) * HEAD_DIM)
        qh = q_ref[0, :, cols]
        sp = _dot_nt(qh, kp) * scale + bias_ref[j, :, 0:w]
        sm = _dot_nt(qh, km) * scale + bias_ref[j, :, w:w + tq]
        sn = _dot_nt(qh, kn) * scale + bias_ref[j, :, w + tq:]
        sp = jnp.where(first, NEG, sp)
        sn = jnp.where(last, NEG, sn)
        sink = sink_ref[g * WIN_GROUP + j]
        m = jnp.maximum(jnp.maximum(jnp.max(sp, axis=-1, keepdims=True), jnp.max(sm, axis=-1, keepdims=True)),
                        jnp.maximum(jnp.max(sn, axis=-1, keepdims=True), sink))
        pp, pm, pn = jnp.exp(sp - m), jnp.exp(sm - m), jnp.exp(sn - m)
        den = (jnp.sum(pp, axis=-1, keepdims=True) + jnp.sum(pm, axis=-1, keepdims=True)
               + jnp.sum(pn, axis=-1, keepdims=True) + jnp.exp(sink - m))
        r = 1.0 / den
        o = (_dot((pp * r).astype(BF16), vp) + _dot((pm * r).astype(BF16), vm)
             + _dot((pn * r).astype(BF16), vn))
        o_ref[0, :, cols] = o.astype(BF16)


def _win_attn(proj, bias, sink, tq=256):
    b, s, _ = proj.shape
    hd = HEAD_DIM
    nb = s // WINDOW
    r = tq // WINDOW
    prev = lambda bi, g, i: jnp.maximum(i * r - 1, 0)
    nxt = lambda bi, g, i: jnp.minimum((i + 1) * r, nb - 1)
    kcol, vcol = OFF_WK // hd, OFF_WV // hd
    return pl.pallas_call(
        functools.partial(_win_kernel, tq=tq),
        grid=(b, WIN_KV_HEADS, s // tq),
        in_specs=[pl.BlockSpec(memory_space=pltpu.SMEM),
                  pl.BlockSpec((1, tq, WIN_GROUP * hd), lambda bi, g, i: (bi, i, OFF_WQ // (WIN_GROUP * hd) + g)),
                  pl.BlockSpec((1, WINDOW, hd), lambda bi, g, i: (bi, prev(bi, g, i), kcol + g)),
                  pl.BlockSpec((1, tq, hd), lambda bi, g, i: (bi, i, kcol + g)),
                  pl.BlockSpec((1, WINDOW, hd), lambda bi, g, i: (bi, nxt(bi, g, i), kcol + g)),
                  pl.BlockSpec((1, WINDOW, hd), lambda bi, g, i: (bi, prev(bi, g, i), vcol + g)),
                  pl.BlockSpec((1, tq, hd), lambda bi, g, i: (bi, i, vcol + g)),
                  pl.BlockSpec((1, WINDOW, hd), lambda bi, g, i: (bi, nxt(bi, g, i), vcol + g)),
                  pl.BlockSpec((WIN_GROUP, tq, tq + 2 * WINDOW), lambda bi, g, i: (g, 0, 0))],
        out_specs=pl.BlockSpec((1, tq, WIN_GROUP * hd), lambda bi, g, i: (bi, i, g)),
        out_shape=jax.ShapeDtypeStruct((b, s, BRANCH_W), BF16),
        compiler_params=_params("parallel", "parallel", "arbitrary"),
        name="win_attn",
    )(sink, proj, proj, proj, proj, proj, proj, proj, bias)


def _mem_attn_kernel(q_ref, k_ref, v_ref, o_ref):
    s = _dot_nt(q_ref[0], k_ref[0]) * jnp.asarray(MEM_DIM ** -0.5, F32)
    p = jnp.exp(s - jnp.max(s, axis=-1, keepdims=True))
    p = p * (1.0 / jnp.sum(p, axis=-1, keepdims=True))
    o_ref[0] = _dot(p.astype(BF16), v_ref[0]).astype(BF16)


def _mem_attn(proj, memkv, tq=512):
    b, s, _ = proj.shape
    md = MEM_DIM
    return pl.pallas_call(
        _mem_attn_kernel,
        grid=(b, MEM_HEADS, s // tq),
        in_specs=[pl.BlockSpec((1, tq, md), lambda bi, h, i: (bi, i, OFF_MQ // md + h)),
                  pl.BlockSpec((1, N_MEM, md), lambda bi, h, i: (bi, 0, h)),
                  pl.BlockSpec((1, N_MEM, md), lambda bi, h, i: (bi, 0, MEM_HEADS + h))],
        out_specs=pl.BlockSpec((1, tq, md), lambda bi, h, i: (bi, i, h)),
        out_shape=jax.ShapeDtypeStruct((b, s, BRANCH_W), BF16),
        compiler_params=_params("parallel", "parallel", "arbitrary"),
        name="mem_attn",
    )(proj, memkv, memkv)


def _mix_kernel(x_ref, lg_ref, lb_ref, a_ref, b_ref, c_ref, wg0_ref, wg1_ref, wg2_ref,
                bg0_ref, bg1_ref, bg2_ref, wb_ref, wo_ref, g1_ref, b1_ref, o_ref, h_scr, acc_scr):
    n = pl.program_id(1)
    nrow = x_ref.shape[0] // LN_ROWS

    @pl.when(n == 0)
    def _():
        def body(r, c):
            rows = pl.ds(pl.multiple_of(r * LN_ROWS, LN_ROWS), LN_ROWS)
            h_scr[rows, :] = _ln_rows(x_ref[rows, :], lg_ref[...], lb_ref[...]).astype(BF16)
            return c
        lax.fori_loop(0, nrow, body, 0)
        acc_scr[...] = jnp.zeros_like(acc_scr)

    h = h_scr[...]
    mixed = None
    for br_ref, wg_ref, bg_ref, k in ((a_ref, wg0_ref, bg0_ref, 0), (b_ref, wg1_ref, bg1_ref, 1),
                                      (c_ref, wg2_ref, bg2_ref, 2)):
        gate = jax.nn.sigmoid(_dot(h, wg_ref[...]) + bg_ref[...])
        term = gate * _dot(br_ref[...], wb_ref[k])
        mixed = term if mixed is None else mixed + term
    acc_scr[...] += _dot(mixed.astype(BF16), wo_ref[...])

    @pl.when(n == pl.num_programs(1) - 1)
    def _():
        def body(r, c):
            rows = pl.ds(pl.multiple_of(r * LN_ROWS, LN_ROWS), LN_ROWS)
            hh = _ln_rows(x_ref[rows, :], lg_ref[...], lb_ref[...])
            o_ref[rows, :] = _ln_rows(ALPHA * hh + acc_scr[rows, :], g1_ref[...], b1_ref[...])
            return c
        lax.fori_loop(0, nrow, body, 0)


def _mix(x2, lg, lb, a, b, c, wg, bg, wb, wo, g1, b1, tm=512, tn=256):
    m, d = x2.shape
    bw = a.shape[1]
    nn = d // tn
    row = lambda w: pl.BlockSpec((tm, w), lambda i, n: (i, 0))
    vec = pl.BlockSpec((1, d), lambda i, n: (0, 0))
    wgs = [pl.BlockSpec((d, tn), functools.partial(lambda i, n, k: (0, k * nn + n), k=k)) for k in range(3)]
    bgs = [pl.BlockSpec((1, tn), functools.partial(lambda i, n, k: (0, k * nn + n), k=k)) for k in range(3)]
    return pl.pallas_call(
        _mix_kernel,
        grid=(m // tm, nn),
        in_specs=[row(d), vec, vec, row(bw), row(bw), row(bw), *wgs, *bgs,
                  pl.BlockSpec((3, bw, tn), lambda i, n: (0, 0, n)),
                  pl.BlockSpec((tn, d), lambda i, n: (n, 0)), vec, vec],
        out_specs=pl.BlockSpec((tm, d), lambda i, n: (i, 0)),
        out_shape=jax.ShapeDtypeStruct((m, d), F32),
        scratch_shapes=[pltpu.VMEM((tm, d), BF16), pltpu.VMEM((tm, d), F32)],
        compiler_params=_params("parallel", "arbitrary"),
        name="mix",
    )(x2, lg, lb, a, b, c, wg, wg, wg, bg, bg, bg, wb, wo, g1, b1)


HALO = 16
CONV_ROWS = 32
MXU_N = 256


def _gelu_tanh(x):
    return 0.5 * x * (1.0 + jnp.tanh(math.sqrt(2.0 / math.pi) * (x + 0.044715 * (x * x * x))))


def _ffn_kernel(ha_ref, hp_ref, hn_ref, hc_ref, wv_ref, wg_ref, cwv_ref, cwg_ref, cbv_ref, cbg_ref, wd_ref,
                g2_ref, b2_ref, o_ref, hx_scr, u0_scr, u1_scr, act0_scr, act1_scr, acc_scr,
                *, nf, n_chunks, tiles_per_seq):
    t = pl.program_id(0)
    ta = jnp.minimum(t, n_chunks - 1)
    tc = jnp.clip(t - 2, 0, n_chunks - 1)
    fa, ia, fc = ta % nf, ta // nf, tc % nf
    tm = ha_ref.shape[0]
    tf = wv_ref.shape[1]
    ext = tm + 2 * HALO

    @pl.when(t == 0)
    def _():
        for ref in (u0_scr, u1_scr, act0_scr, act1_scr):
            ref[...] = jnp.zeros_like(ref)

    @pl.when(fa == 0)
    def _():
        seq_first = (ia % tiles_per_seq) == 0
        seq_last = (ia % tiles_per_seq) == tiles_per_seq - 1
        hx_scr[0:HALO, :] = jnp.where(seq_first, 0.0, hp_ref[...]).astype(BF16)
        hx_scr[HALO:HALO + tm, :] = ha_ref[...].astype(BF16)
        hx_scr[HALO + tm:, :] = jnp.where(seq_last, 0.0, hn_ref[...]).astype(BF16)

    @pl.when(fc == 0)
    def _():
        acc_scr[...] = jnp.zeros_like(acc_scr)

    def conv_rows(u_ref, r, cols, cw_ref, cb_ref):
        taps = [u_ref[HALO + r + k - 1:HALO + r + k - 1 + CONV_ROWS, cols] for k in range(3)]
        return (taps[0] * cw_ref[0:1, :] + taps[1] * cw_ref[1:2, :] + taps[2] * cw_ref[2:3, :]) + cb_ref[...]

    def tick(u_new, u_old, act_new, act_old):
        hx = hx_scr[...]
        d = acc_scr.shape[1]

        def up(w_ref, j, base):
            c = slice(j * MXU_N, (j + 1) * MXU_N)
            u_new[:, base + j * MXU_N:base + (j + 1) * MXU_N] = _dot(hx, w_ref[:, c])

        def down(j):
            c = slice(j * MXU_N, (j + 1) * MXU_N)
            acc_scr[:, c] += _dot(act_old[...], wd_ref[:, c])

        mxu_pieces = ([functools.partial(up, wv_ref, j, 0) for j in range(tf // MXU_N)]
                      + [functools.partial(up, wg_ref, j, tf) for j in range(tf // MXU_N)]
                      + [functools.partial(down, j) for j in range(d // MXU_N)])
        weights = [4] * (2 * tf // MXU_N) + [1] * (d // MXU_N)
        n_conv = tm // CONV_ROWS
        done, spent = 0, 0
        for piece, w in zip(mxu_pieces, weights):
            piece()
            spent += w
            target = n_conv * spent // sum(weights)
            for k in range(done, target):
                r = k * CONV_ROWS
                val = conv_rows(u_old, r, slice(0, tf), cwv_ref, cbv_ref)
                gate = conv_rows(u_old, r, slice(tf, 2 * tf), cwg_ref, cbg_ref)
                act_new[r:r + CONV_ROWS, :] = (_gelu_tanh(gate) * val).astype(BF16)
            done = target

    pl.when(t % 2 == 0)(functools.partial(tick, u0_scr, u1_scr, act0_scr, act1_scr))
    pl.when(t % 2 == 1)(functools.partial(tick, u1_scr, u0_scr, act1_scr, act0_scr))

    @pl.when(jnp.logical_and(t >= 2, fc == nf - 1))
    def _():
        def body(r, c):
            rows = pl.ds(pl.multiple_of(r * LN_ROWS, LN_ROWS), LN_ROWS)
            o_ref[rows, :] = _ln_rows(ALPHA * hc_ref[rows, :] + acc_scr[rows, :], g2_ref[...], b2_ref[...])
            return c
        lax.fori_loop(0, tm // LN_ROWS, body, 0)


def _ffn(h1, w_up, cw, cb, w_down, g2, b2, seq, tm=512, tf=512):
    m, d = h1.shape
    nf = D_FF_PAD // tf
    n_chunks = (m // tm) * nf
    hb = tm // HALO
    nhb = m // HALO
    chunk = lambda t, lag: jnp.clip(t - lag, 0, n_chunks - 1)
    ia = lambda t: chunk(t, 0) // nf
    fa = lambda t: chunk(t, 0) % nf
    fb = lambda t: chunk(t, 1) % nf
    ic = lambda t: chunk(t, 2) // nf
    fc = lambda t: chunk(t, 2) % nf
    vec = pl.BlockSpec((1, d), lambda t: (0, 0))
    return pl.pallas_call(
        functools.partial(_ffn_kernel, nf=nf, n_chunks=n_chunks, tiles_per_seq=seq // tm),
        grid=(n_chunks + 2,),
        in_specs=[pl.BlockSpec((tm, d), lambda t: (ia(t), 0)),
                  pl.BlockSpec((HALO, d), lambda t: (jnp.maximum(ia(t) * hb - 1, 0), 0)),
                  pl.BlockSpec((HALO, d), lambda t: (jnp.minimum((ia(t) + 1) * hb, nhb - 1), 0)),
                  pl.BlockSpec((tm, d), lambda t: (ic(t), 0)),
                  pl.BlockSpec((d, tf), lambda t: (0, fa(t))),
                  pl.BlockSpec((d, tf), lambda t: (0, nf + fa(t))),
                  pl.BlockSpec((3, tf), lambda t: (0, fb(t))),
                  pl.BlockSpec((3, tf), lambda t: (0, nf + fb(t))),
                  pl.BlockSpec((1, tf), lambda t: (0, fb(t))),
                  pl.BlockSpec((1, tf), lambda t: (0, nf + fb(t))),
                  pl.BlockSpec((tf, d), lambda t: (fc(t), 0)), vec, vec],
        out_specs=pl.BlockSpec((tm, d), lambda t: (ic(t), 0)),
        out_shape=jax.ShapeDtypeStruct((m, d), F32),
        scratch_shapes=[pltpu.VMEM((tm + 2 * HALO, d), BF16),
                        pltpu.VMEM((tm + 2 * HALO, 2 * tf), F32), pltpu.VMEM((tm + 2 * HALO, 2 * tf), F32),
                        pltpu.VMEM((tm, tf), BF16), pltpu.VMEM((tm, tf), BF16), pltpu.VMEM((tm, d), F32)],
        compiler_params=_params("arbitrary"),
        name="ffn",
    )(h1, h1, h1, h1, w_up, w_up, cw, cw, cb, cb, w_down, g2, b2)


def _pad_ff(t, axis):
    val, gate = jnp.split(t, 2, axis=axis)
    pad = [(0, 0)] * t.ndim
    pad[axis] = (0, D_FF_PAD - D_FF)
    return jnp.concatenate([jnp.pad(val, pad), jnp.pad(gate, pad)], axis=axis)


def kernel(x, mem, ln_in_g, ln_in_b, rel_table, w_in, w_mem_kv, diff_lq1, diff_lk1, diff_lq2, diff_lk2,
           diff_subln_g, win_sink, w_gate, b_gate, w_branch, w_o, ln1_g, ln1_b, w_up, conv_w, conv_b,
           w_down, ln2_g, ln2_b):
    assert w_in.shape[0] == DEPTH == 1
    bsz, seq, d = x.shape
    x2 = x.reshape(bsz * seq, d)
    row = lambda v: v.reshape(1, -1).astype(F32)
    l = 0
    proj = _ln_proj(x2, row(ln_in_g), row(ln_in_b), w_in[l].astype(BF16)).reshape(bsz, seq, IN_W)
    memkv = _mem_kv(mem.reshape(bsz * N_MEM, d), w_mem_kv[l].astype(BF16)).reshape(bsz, N_MEM, 2 * BRANCH_W)

    tq_d, tq_w = 256, 256
    a = _diff_attn(proj, _diff_bias_tiles(rel_table[:, :DIFF_HEADS] * LOG2E, tq_d), row(diff_lq1[l]), row(diff_lk1[l]),
                   row(diff_lq2[l]), row(diff_lk2[l]), row(diff_subln_g[l]), tq=tq_d)
    b = _win_attn(proj, _win_bias_tiles(rel_table[:, DIFF_HEADS:], tq_w), win_sink[l].astype(F32), tq=tq_w)
    c = _mem_attn(proj, memkv)

    m = bsz * seq
    h1 = _mix(x2, row(ln_in_g), row(ln_in_b), a.reshape(m, -1), b.reshape(m, -1), c.reshape(m, -1),
              w_gate[l].astype(BF16), row(b_gate[l]), w_branch[l].astype(BF16), w_o[l].astype(BF16),
              row(ln1_g[l]), row(ln1_b[l]))

    w_up_p = _pad_ff(w_up[l], 1).astype(BF16)
    cw_p = _pad_ff(conv_w[l].astype(F32), 1)
    cb_p = _pad_ff(row(conv_b[l]), 1)
    w_down_p = jnp.pad(w_down[l], ((0, D_FF_PAD - D_FF), (0, 0))).astype(BF16)
    out = _ffn(h1, w_up_p, cw_p, cb_p, w_down_p, row(ln2_g[l]), row(ln2_b[l]), seq)
    return out.reshape(bsz, seq, d)
```

```python
import functools
import math

import jax
import jax.numpy as jnp
from jax import lax
from jax.experimental import pallas as pl
from jax.experimental.pallas import tpu as pltpu

F32 = jnp.float32
BF16 = jnp.bfloat16

D_MODEL = 2048
SEQ = 2048
N_MEM = 256
HEAD_DIM = 128
BRANCH_W = 1024
DIFF_HEADS = 8
DIFF_QK = 64
WIN_HEADS = 8
WIN_KV_HEADS = 2
WIN_GROUP = WIN_HEADS // WIN_KV_HEADS
WINDOW = 128
MEM_HEADS = 4
MEM_DIM = 256
OFF_DQ, OFF_DK, OFF_DV, OFF_WQ, OFF_WK, OFF_WV, OFF_MQ = 0, 1024, 2048, 3072, 4096, 4352, 4608
IN_W = 5632
D_FF = 5504
D_FF_PAD = 5632
REL_BUCKETS = 32
REL_MAX_DIST = 128
DEPTH = 1
ALPHA = (2 * DEPTH) ** 0.25
LN_EPS = 1e-5
NEG = -1e30
LOG2E = math.log2(math.e)
LAMBDA_INIT =0.8 - 0.6 * math.exp(-0.3 * 0)

VMEM_LIMIT = 56 * 1024 * 1024
LN_ROWS = 128


def _ln_rows(x, g, b):
    mu = jnp.mean(x, axis=-1, keepdims=True)
    xc = x - mu
    var = jnp.mean(xc * xc, axis=-1, keepdims=True)
    return xc * lax.rsqrt(var + LN_EPS) * g + b


def _dot(a, b):
    return jnp.dot(a, b, preferred_element_type=F32)


def _dot_nt(a, b):
    return lax.dot_general(a, b, (((1,), (1,)), ((), ())), preferred_element_type=F32)


def _params(*sem, flags=None):
    return pltpu.CompilerParams(dimension_semantics=sem, vmem_limit_bytes=VMEM_LIMIT, flags=flags)


def _ln_proj_kernel(x_ref, g_ref, b_ref, w_ref, o_ref, h_scr):
    @pl.when(pl.program_id(1) == 0)
    def _():
        def body(r, c):
            rows = pl.ds(pl.multiple_of(r * LN_ROWS, LN_ROWS), LN_ROWS)
            h_scr[rows, :] = _ln_rows(x_ref[rows, :], g_ref[...], b_ref[...]).astype(BF16)
            return c
        lax.fori_loop(0, x_ref.shape[0] // LN_ROWS, body, 0)

    o_ref[...] = _dot(h_scr[...], w_ref[...]).astype(BF16)


def _ln_proj(x2, g, b, w, tm=1024, tn=512):
    m, d = x2.shape
    n = w.shape[1]
    return pl.pallas_call(
        _ln_proj_kernel,
        grid=(m // tm, n // tn),
        in_specs=[pl.BlockSpec((tm, d), lambda i, j: (i, 0)),
                  pl.BlockSpec((1, d), lambda i, j: (0, 0)),
                  pl.BlockSpec((1, d), lambda i, j: (0, 0)),
                  pl.BlockSpec((d, tn), lambda i, j: (0, j))],
        out_specs=pl.BlockSpec((tm, tn), lambda i, j: (i, j)),
        out_shape=jax.ShapeDtypeStruct((m, n), BF16),
        scratch_shapes=[pltpu.VMEM((tm, d), BF16)],
        compiler_params=_params("parallel", "arbitrary"),
        name="ln_proj",
    )(x2, g, b, w)


def _mem_kv_kernel(m_ref, w_ref, o_ref):
    o_ref[...] = _dot(m_ref[...].astype(BF16), w_ref[...]).astype(BF16)


def _mem_kv(mem2, w, tn=512):
    m, d = mem2.shape
    n = w.shape[1]
    return pl.pallas_call(
        _mem_kv_kernel,
        grid=(n // tn,),
        in_specs=[pl.BlockSpec((m, d), lambda j: (0, 0)),
                  pl.BlockSpec((d, tn), lambda j: (0, j))],
        out_specs=pl.BlockSpec((m, tn), lambda j: (0, j)),
        out_shape=jax.ShapeDtypeStruct((m, n), BF16),
        compiler_params=_params("parallel"),
        name="mem_kv",
    )(mem2, w)


def _t5_bucket(rel):
    half = REL_BUCKETS // 2
    max_exact = half // 2
    ret = jnp.where(rel > 0, half, 0)
    n = jnp.abs(rel)
    nf = jnp.maximum(n, 1).astype(F32)
    large = max_exact + (jnp.log(nf / max_exact) / math.log(REL_MAX_DIST / max_exact)
                         * (half - max_exact)).astype(jnp.int32)
    large = jnp.minimum(large, half - 1)
    return ret + jnp.where(n < max_exact, n, large)


BIAS_ROWS = 64


def _bias_lookup_kernel(tab_ref, bucket_ref, o_ref):
    h = pl.program_id(0)

    def body(r, carry):
        rows = pl.ds(pl.multiple_of(r * BIAS_ROWS, BIAS_ROWS), BIAS_ROWS)
        bk = bucket_ref[rows, :]
        out = jnp.full(bk.shape, NEG, F32)
        for b in range(REL_BUCKETS):
            out = jnp.where(bk == b, tab_ref[b, h], out)
        o_ref[0, rows, :] = out
        return carry
    lax.fori_loop(0, bucket_ref.shape[0] // BIAS_ROWS, body, 0)


def _bias_lookup(table, bucket):
    nh = table.shape[1]
    r, c = bucket.shape
    return pl.pallas_call(
        _bias_lookup_kernel,
        grid=(nh,),
        in_specs=[pl.BlockSpec(memory_space=pltpu.SMEM),
                  pl.BlockSpec((r, c), lambda h: (0, 0))],
        out_specs=pl.BlockSpec((1, r, c), lambda h: (h, 0, 0)),
        out_shape=jax.ShapeDtypeStruct((nh, r, c), F32),
        compiler_params=_params("parallel"),
        name="bias_lookup",
    )(table.astype(F32), bucket)


def _diff_bias_tiles(table, tq):
    d = jnp.arange(5)[:, None, None] - 2
    q = jnp.arange(tq)[None, :, None]
    k = jnp.arange(tq)[None, None, :]
    bucket = _t5_bucket(d * tq + k - q).astype(jnp.int32).reshape(5 * tq, tq)
    return _bias_lookup(table, bucket).reshape(table.shape[1], 5, tq, tq)


def _win_bias_tiles(table, tq):
    q = jnp.arange(tq)[:, None]
    c = jnp.arange(tq + 2 * WINDOW)[None, :]
    rel = c - WINDOW - q
    bucket = jnp.where(jnp.abs(rel) <= WINDOW, _t5_bucket(rel), -1).astype(jnp.int32)
    return _bias_lookup(table, bucket)


def _diff_kernel(q_ref, k_ref, v_ref, band_ref, lq1_ref, lk1_ref, lq2_ref, lk2_ref, g_ref, o_ref,
                 s_scr, p_scr, den_scr, *, tq, rc, n_tiles, tiles_per_head):
    t = pl.program_id(0)

    @pl.when(t == 0)
    def _():
        s_scr[...] = jnp.zeros_like(s_scr)
        p_scr[...] = jnp.zeros_like(p_scr)
        den_scr[...] = jnp.ones_like(den_scr)

    def tick(slot):
        other = 1 - slot
        i = jnp.minimum(t, n_tiles - 1) % tiles_per_head
        nk = k_ref.shape[1] // tq
        q = (q_ref[0].astype(F32) * (DIFF_QK ** -0.5 * LOG2E)).astype(BF16)
        lane = lax.broadcasted_iota(jnp.int32, q.shape, 1)
        zero = jnp.zeros_like(q)
        qs = jnp.concatenate([jnp.where(lane < DIFF_QK, q, zero), jnp.where(lane >= DIFF_QK, q, zero)], axis=0)
        for j in range(nk):
            cols = slice(j * tq, (j + 1) * tq)
            bias = band_ref[0, jnp.clip(j - i, -2, 2) + 2]
            s = _dot_nt(qs, k_ref[0, cols, :])
            s_scr[slot, 0:tq, cols] = s[0:tq] + bias
            s_scr[slot, tq:, cols] = s[tq:] + bias

        nchunk = 2 * tq // rc
        chunk = lambda r: slice(r * rc, (r + 1) * rc)
        row_max = [jnp.max(s_scr[other, chunk(r), :], axis=-1, keepdims=True) for r in range(nchunk)]
        for r in range(nchunk):
            p = jnp.exp2(s_scr[other, chunk(r), :] - row_max[r])
            den_scr[other, chunk(r), :] = jnp.sum(p, axis=-1, keepdims=True)
            p_scr[other, chunk(r), :] = p.astype(BF16)

        lam = (jnp.exp(jnp.sum(lq1_ref[...] * lk1_ref[...], axis=-1, keepdims=True))
               - jnp.exp(jnp.sum(lq2_ref[...] * lk2_ref[...], axis=-1, keepdims=True)) + LAMBDA_INIT)
        pv = _dot(p_scr[slot], v_ref[0])
        den = den_scr[slot]
        o = pv[0:tq] * (1.0 / den[0:tq]) - pv[tq:] * (lam / den[tq:])
        ms = jnp.mean(o * o, axis=-1, keepdims=True)
        o_ref[0] = (o * lax.rsqrt(ms + LN_EPS) * g_ref[...] * (1.0 - LAMBDA_INIT)).astype(BF16)

    pl.when(t % 2 == 0)(functools.partial(tick, 0))
    pl.when(t % 2 == 1)(functools.partial(tick, 1))


def _diff_attn(proj, band, lq1, lk1, lq2, lk2, g, tq=256, rc=32):
    b, s, _ = proj.shape
    cb = HEAD_DIM
    ni = s // tq
    n_tiles = b * DIFF_HEADS * ni
    depth = 2

    def tile(t, lag):
        tc = jnp.clip(t - lag, 0, n_tiles - 1)
        return tc // (DIFF_HEADS * ni), (tc // ni) % DIFF_HEADS, tc % ni

    def q_map(t):
        bi, h, i = tile(t, 0)
        return bi, i, OFF_DQ // cb + h

    def k_map(t):
        bi, h, _ = tile(t, 0)
        return bi, 0, OFF_DK // cb + h

    def v_map(t):
        bi, h, _ = tile(t, depth)
        return bi, 0, OFF_DV // cb + h

    def o_map(t):
        bi, h, i = tile(t, depth)
        return bi, i, h

    vec = lambda n: pl.BlockSpec((1, n), lambda t: (0, 0))
    return pl.pallas_call(
        functools.partial(_diff_kernel, tq=tq, rc=rc, n_tiles=n_tiles, tiles_per_head=ni),
        grid=(n_tiles + depth,),
        in_specs=[pl.BlockSpec((1, tq, cb), q_map),
                  pl.BlockSpec((1, s, cb), k_map),
                  pl.BlockSpec((1, s, cb), v_map),
                  pl.BlockSpec((1, 5, tq, tq), lambda t: (tile(t, 0)[1], 0, 0, 0)),
                  vec(DIFF_QK), vec(DIFF_QK), vec(DIFF_QK), vec(DIFF_QK), vec(HEAD_DIM)],
        out_specs=pl.BlockSpec((1, tq, cb), o_map),
        out_shape=jax.ShapeDtypeStruct((b, s, BRANCH_W), BF16),
        scratch_shapes=[pltpu.VMEM((2, 2 * tq, s), F32), pltpu.VMEM((2, 2 * tq, s), BF16),
                        pltpu.VMEM((2, 2 * tq, 1), F32)],
        compiler_params=_params("arbitrary"),
        name="diff_attn",
    )(proj, proj, proj, band, lq1, lk1, lq2, lk2, g)


def _win_kernel(sink_ref, q_ref, kp_ref, km_ref, kn_ref, vp_ref, vm_ref, vn_ref, bias_ref, o_ref, *, tq):
    g = pl.program_id(1)
    i = pl.program_id(2)
    first = i == 0
    last = i == pl.num_programs(2) - 1
    kp, km, kn = kp_ref[0], km_ref[0], kn_ref[0]
    vp, vm, vn = vp_ref[0], vm_ref[0], vn_ref[0]
    scale = jnp.asarray(HEAD_DIM ** -0.5, F32)
    w = WINDOW
    for j in range(WIN_GROUP):
        cols = slice(j * HEAD_DIM, (j + 1) * HEAD_DIM)
        qh = q_ref[0, :, cols]
        sp = _dot_nt(qh, kp) * scale + bias_ref[j, :, 0:w]
        sm = _dot_nt(qh, km) * scale + bias_ref[j, :, w:w + tq]
        sn = _dot_nt(qh, kn) * scale + bias_ref[j, :, w + tq:]
        sp = jnp.where(first, NEG, sp)
        sn = jnp.where(last, NEG, sn)
        sink = sink_ref[g * WIN_GROUP + j]
        m = jnp.maximum(jnp.maximum(jnp.max(sp, axis=-1, keepdims=True), jnp.max(sm, axis=-1, keepdims=True)),
                        jnp.maximum(jnp.max(sn, axis=-1, keepdims=True), sink))
        pp, pm, pn = jnp.exp(sp - m), jnp.exp(sm - m), jnp.exp(sn - m)
        den = (jnp.sum(pp, axis=-1, keepdims=True) + jnp.sum(pm, axis=-1, keepdims=True)
               + jnp.sum(pn, axis=-1, keepdims=True) + jnp.exp(sink - m))
        r = 1.0 / den
        o = (_dot((pp * r).astype(BF16), vp) + _dot((pm * r).astype(BF16), vm)
             + _dot((pn * r).astype(BF16), vn))
        o_ref[0, :, cols] = o.astype(BF16)


def _win_attn(proj, bias, sink, tq=256):
    b, s, _ = proj.shape
    hd = HEAD_DIM
    nb = s // WINDOW
    r = tq // WINDOW
    prev = lambda bi, g, i: jnp.maximum(i * r - 1, 0)
    nxt = lambda bi, g, i: jnp.minimum((i + 1) * r, nb - 1)
    kcol, vcol = OFF_WK // hd, OFF_WV // hd
    return pl.pallas_call(
        functools.partial(_win_kernel, tq=tq),
        grid=(b, WIN_KV_HEADS, s // tq),
        in_specs=[pl.BlockSpec(memory_space=pltpu.SMEM),
                  pl.BlockSpec((1, tq, WIN_GROUP * hd), lambda bi, g, i: (bi, i, OFF_WQ // (WIN_GROUP * hd) + g)),
                  pl.BlockSpec((1, WINDOW, hd), lambda bi, g, i: (bi, prev(bi, g, i), kcol + g)),
                  pl.BlockSpec((1, tq, hd), lambda bi, g, i: (bi, i, kcol + g)),
                  pl.BlockSpec((1, WINDOW, hd), lambda bi, g, i: (bi, nxt(bi, g, i), kcol + g)),
                  pl.BlockSpec((1, WINDOW, hd), lambda bi, g, i: (bi, prev(bi, g, i), vcol + g)),
                  pl.BlockSpec((1, tq, hd), lambda bi, g, i: (bi, i, vcol + g)),
                  pl.BlockSpec((1, WINDOW, hd), lambda bi, g, i: (bi, nxt(bi, g, i), vcol + g)),
                  pl.BlockSpec((WIN_GROUP, tq, tq + 2 * WINDOW), lambda bi, g, i: (g, 0, 0))],
        out_specs=pl.BlockSpec((1, tq, WIN_GROUP * hd), lambda bi, g, i: (bi, i, g)),
        out_shape=jax.ShapeDtypeStruct((b, s, BRANCH_W), BF16),
        compiler_params=_params("parallel", "parallel", "arbitrary"),
        name="win_attn",
    )(sink, proj, proj, proj, proj, proj, proj, proj, bias)


def _mem_attn_kernel(q_ref, k_ref, v_ref, o_ref):
    s = _dot_nt(q_ref[0], k_ref[0]) * jnp.asarray(MEM_DIM ** -0.5, F32)
    p = jnp.exp(s - jnp.max(s, axis=-1, keepdims=True))
    p = p * (1.0 / jnp.sum(p, axis=-1, keepdims=True))
    o_ref[0] = _dot(p.astype(BF16), v_ref[0]).astype(BF16)


def _mem_attn(proj, memkv, tq=512):
    b, s, _ = proj.shape
    md = MEM_DIM
    return pl.pallas_call(
        _mem_attn_kernel,
        grid=(b, MEM_HEADS, s // tq),
        in_specs=[pl.BlockSpec((1, tq, md), lambda bi, h, i: (bi, i, OFF_MQ // md + h)),
                  pl.BlockSpec((1, N_MEM, md), lambda bi, h, i: (bi, 0, h)),
                  pl.BlockSpec((1, N_MEM, md), lambda bi, h, i: (bi, 0, MEM_HEADS + h))],
        out_specs=pl.BlockSpec((1, tq, md), lambda bi, h, i: (bi, i, h)),
        out_shape=jax.ShapeDtypeStruct((b, s, BRANCH_W), BF16),
        compiler_params=_params("parallel", "parallel", "arbitrary"),
        name="mem_attn",
    )(proj, memkv, memkv)


def _mix_kernel(x_ref, lg_ref, lb_ref, a_ref, b_ref, c_ref, wg0_ref, wg1_ref, wg2_ref,
                bg0_ref, bg1_ref, bg2_ref, wb_ref, wo_ref, g1_ref, b1_ref, o_ref, h_scr, acc_scr):
    n = pl.program_id(1)
    nrow = x_ref.shape[0] // LN_ROWS

    @pl.when(n == 0)
    def _():
        def body(r, c):
            rows = pl.ds(pl.multiple_of(r * LN_ROWS, LN_ROWS), LN_ROWS)
            h_scr[rows, :] = _ln_rows(x_ref[rows, :], lg_ref[...], lb_ref[...]).astype(BF16)
            return c
        lax.fori_loop(0, nrow, body, 0)
        acc_scr[...] = jnp.zeros_like(acc_scr)

    h = h_scr[...]
    mixed = None
    for br_ref, wg_ref, bg_ref, k in ((a_ref, wg0_ref, bg0_ref, 0), (b_ref, wg1_ref, bg1_ref, 1),
                                      (c_ref, wg2_ref, bg2_ref, 2)):
        gate = jax.nn.sigmoid(_dot(h, wg_ref[...]) + bg_ref[...])
        term = gate * _dot(br_ref[...], wb_ref[k])
        mixed = term if mixed is None else mixed + term
    acc_scr[...] += _dot(mixed.astype(BF16), wo_ref[...])

    @pl.when(n == pl.num_programs(1) - 1)
    def _():
        def body(r, c):
            rows = pl.ds(pl.multiple_of(r * LN_ROWS, LN_ROWS), LN_ROWS)
            hh = _ln_rows(x_ref[rows, :], lg_ref[...], lb_ref[...])
            o_ref[rows, :] = _ln_rows(ALPHA * hh + acc_scr[rows, :], g1_ref[...], b1_ref[...])
            return c
        lax.fori_loop(0, nrow, body, 0)


def _mix(x2, lg, lb, a, b, c, wg, bg, wb, wo, g1, b1, tm=512, tn=256):
    m, d = x2.shape
    bw = a.shape[1]
    nn = d // tn
    row = lambda w: pl.BlockSpec((tm, w), lambda i, n: (i, 0))
    vec = pl.BlockSpec((1, d), lambda i, n: (0, 0))
    wgs = [pl.BlockSpec((d, tn), functools.partial(lambda i, n, k: (0, k * nn + n), k=k)) for k in range(3)]
    bgs = [pl.BlockSpec((1, tn), functools.partial(lambda i, n, k: (0, k * nn + n), k=k)) for k in range(3)]
    return pl.pallas_call(
        _mix_kernel,
        grid=(m // tm, nn),
        in_specs=[row(d), vec, vec, row(bw), row(bw), row(bw), *wgs, *bgs,
                  pl.BlockSpec((3, bw, tn), lambda i, n: (0, 0, n)),
                  pl.BlockSpec((tn, d), lambda i, n: (n, 0)), vec, vec],
        out_specs=pl.BlockSpec((tm, d), lambda i, n: (i, 0)),
        out_shape=jax.ShapeDtypeStruct((m, d), F32),
        scratch_shapes=[pltpu.VMEM((tm, d), BF16), pltpu.VMEM((tm, d), F32)],
        compiler_params=_params("parallel", "arbitrary"),
        name="mix",
    )(x2, lg, lb, a, b, c, wg, wg, wg, bg, bg, bg, wb, wo, g1, b1)


HALO = 16
CONV_ROWS = 32
MXU_N = 256


def _gelu_tanh(x):
    return 0.5 * x * (1.0 + jnp.tanh(math.sqrt(2.0 / math.pi) * (x + 0.044715 * (x * x * x))))


def _ffn_kernel(ha_ref, hp_ref, hn_ref, hc_ref, wv_ref, wg_ref, cwv_ref, cwg_ref, cbv_ref, cbg_ref, wd_ref,
                g2_ref, b2_ref, o_ref, hx_scr, u0_scr, u1_scr, act0_scr, act1_scr, acc_scr,
                *, nf, n_chunks, tiles_per_seq):
    t = pl.program_id(0)
    ta = jnp.minimum(t, n_chunks - 1)
    tc = jnp.clip(t - 2, 0, n_chunks - 1)
    fa, ia, fc = ta % nf, ta // nf, tc % nf
    tm = ha_ref.shape[0]
    tf = wv_ref.shape[1]
    ext = tm + 2 * HALO

    @pl.when(t == 0)
    def _():
        for ref in (u0_scr, u1_scr, act0_scr, act1_scr):
            ref[...] = jnp.zeros_like(ref)

    @pl.when(fa == 0)
    def _():
        seq_first = (ia % tiles_per_seq) == 0
        seq_last = (ia % tiles_per_seq) == tiles_per_seq - 1
        hx_scr[0:HALO, :] = jnp.where(seq_first, 0.0, hp_ref[...]).astype(BF16)
        hx_scr[HALO:HALO + tm, :] = ha_ref[...].astype(BF16)
        hx_scr[HALO + tm:, :] = jnp.where(seq_last, 0.0, hn_ref[...]).astype(BF16)

    @pl.when(fc == 0)
    def _():
        acc_scr[...] = jnp.zeros_like(acc_scr)

    def conv_rows(u_ref, r, cols, cw_ref, cb_ref):
        taps = [u_ref[HALO + r + k - 1:HALO + r + k - 1 + CONV_ROWS, cols] for k in range(3)]
        return (taps[0] * cw_ref[0:1, :] + taps[1] * cw_ref[1:2, :] + taps[2] * cw_ref[2:3, :]) + cb_ref[...]

    def tick(u_new, u_old, act_new, act_old):
        hx = hx_scr[...]
        d = acc_scr.shape[1]

        def up(w_ref, base):
            u_new[:, base:base + tf] = _dot(hx, w_ref[...])

        def down():
            acc_scr[...] += _dot(act_old[...], wd_ref[...])

        mxu_pieces = [functools.partial(up, wv_ref, 0), functools.partial(up, wg_ref, tf), down]
        weights = [1, 0, 0]
        n_conv = tm // CONV_ROWS
        done, spent = 0, 0
        for piece, w in zip(mxu_pieces, weights):
            piece()
            spent += w
            target = n_conv * spent // sum(weights)
            for k in range(done, target):
                r = k * CONV_ROWS
                val = conv_rows(u_old, r, slice(0, tf), cwv_ref, cbv_ref)
                gate = conv_rows(u_old, r, slice(tf, 2 * tf), cwg_ref, cbg_ref)
                act_new[r:r + CONV_ROWS, :] = (_gelu_tanh(gate) * val).astype(BF16)
            done = target

    pl.when(t % 2 == 0)(functools.partial(tick, u0_scr, u1_scr, act0_scr, act1_scr))
    pl.when(t % 2 == 1)(functools.partial(tick, u1_scr, u0_scr, act1_scr, act0_scr))

    @pl.when(jnp.logical_and(t >= 2, fc == nf - 1))
    def _():
        def body(r, c):
            rows = pl.ds(pl.multiple_of(r * LN_ROWS, LN_ROWS), LN_ROWS)
            o_ref[rows, :] = _ln_rows(ALPHA * hc_ref[rows, :] + acc_scr[rows, :], g2_ref[...], b2_ref[...])
            return c
        lax.fori_loop(0, tm // LN_ROWS, body, 0)


def _ffn(h1, w_up, cw, cb, w_down, g2, b2, seq, tm=512, tf=512):
    m, d = h1.shape
    nf = D_FF_PAD // tf
    n_chunks = (m // tm) * nf
    hb = tm // HALO
    nhb = m // HALO
    chunk = lambda t, lag: jnp.clip(t - lag, 0, n_chunks - 1)
    ia = lambda t: chunk(t, 0) // nf
    fa = lambda t: chunk(t, 0) % nf
    fb = lambda t: chunk(t, 1) % nf
    ic = lambda t: chunk(t, 2) // nf
    fc = lambda t: chunk(t, 2) % nf
    vec = pl.BlockSpec((1, d), lambda t: (0, 0))
    return pl.pallas_call(
        functools.partial(_ffn_kernel, nf=nf, n_chunks=n_chunks, tiles_per_seq=seq // tm),
        grid=(n_chunks + 2,),
        in_specs=[pl.BlockSpec((tm, d), lambda t: (ia(t), 0)),
                  pl.BlockSpec((HALO, d), lambda t: (jnp.maximum(ia(t) * hb - 1, 0), 0)),
                  pl.BlockSpec((HALO, d), lambda t: (jnp.minimum((ia(t) + 1) * hb, nhb - 1), 0)),
                  pl.BlockSpec((tm, d), lambda t: (ic(t), 0)),
                  pl.BlockSpec((d, tf), lambda t: (0, fa(t))),
                  pl.BlockSpec((d, tf), lambda t: (0, nf + fa(t))),
                  pl.BlockSpec((3, tf), lambda t: (0, fb(t))),
                  pl.BlockSpec((3, tf), lambda t: (0, nf + fb(t))),
                  pl.BlockSpec((1, tf), lambda t: (0, fb(t))),
                  pl.BlockSpec((1, tf), lambda t: (0, nf + fb(t))),
                  pl.BlockSpec((tf, d), lambda t: (fc(t), 0)), vec, vec],
        out_specs=pl.BlockSpec((tm, d), lambda t: (ic(t), 0)),
        out_shape=jax.ShapeDtypeStruct((m, d), F32),
        scratch_shapes=[pltpu.VMEM((tm + 2 * HALO, d), BF16),
                        pltpu.VMEM((tm + 2 * HALO, 2 * tf), F32), pltpu.VMEM((tm + 2 * HALO, 2 * tf), F32),
                        pltpu.VMEM((tm, tf), BF16), pltpu.VMEM((tm, tf), BF16), pltpu.VMEM((tm, d), F32)],
        compiler_params=_params("arbitrary"),
        name="ffn",
    )(h1, h1, h1, h1, w_up, w_up, cw, cw, cb, cb, w_down, g2, b2)


def _pad_ff(t, axis):
    val, gate = jnp.split(t, 2, axis=axis)
    pad = [(0, 0)] * t.ndim
    pad[axis] = (0, D_FF_PAD - D_FF)
    return jnp.concatenate([jnp.pad(val, pad), jnp.pad(gate, pad)], axis=axis)


def kernel(x, mem, ln_in_g, ln_in_b, rel_table, w_in, w_mem_kv, diff_lq1, diff_lk1, diff_lq2, diff_lk2,
           diff_subln_g, win_sink, w_gate, b_gate, w_branch, w_o, ln1_g, ln1_b, w_up, conv_w, conv_b,
           w_down, ln2_g, ln2_b):
    assert w_in.shape[0] == DEPTH == 1
    bsz, seq, d = x.shape
    x2 = x.reshape(bsz * seq, d)
    row = lambda v: v.reshape(1, -1).astype(F32)
    l = 0
    proj = _ln_proj(x2, row(ln_in_g), row(ln_in_b), w_in[l].astype(BF16)).reshape(bsz, seq, IN_W)
    memkv = _mem_kv(mem.reshape(bsz * N_MEM, d), w_mem_kv[l].astype(BF16)).reshape(bsz, N_MEM, 2 * BRANCH_W)

    tq_d, tq_w = 256, 256
    a = _diff_attn(proj, _diff_bias_tiles(rel_table[:, :DIFF_HEADS] * LOG2E, tq_d), row(diff_lq1[l]), row(diff_lk1[l]),
                   row(diff_lq2[l]), row(diff_lk2[l]), row(diff_subln_g[l]), tq=tq_d)
    b = _win_attn(proj, _win_bias_tiles(rel_table[:, DIFF_HEADS:], tq_w), win_sink[l].astype(F32), tq=tq_w)
    c = _mem_attn(proj, memkv)

    m = bsz * seq
    h1 = _mix(x2, row(ln_in_g), row(ln_in_b), a.reshape(m, -1), b.reshape(m, -1), c.reshape(m, -1),
              w_gate[l].astype(BF16), row(b_gate[l]), w_branch[l].astype(BF16), w_o[l].astype(BF16),
              row(ln1_g[l]), row(ln1_b[l]))

    w_up_p = _pad_ff(w_up[l], 1).astype(BF16)
    cw_p = _pad_ff(conv_w[l].astype(F32), 1)
    cb_p = _pad_ff(row(conv_b[l]), 1)
    w_down_p = jnp.pad(w_down[l], ((0, D_FF_PAD - D_FF), (0, 0))).astype(BF16)
    out = _ffn(h1, w_up_p, cw_p, cb_p, w_down_p, row(ln2_g[l]), row(ln2_b[l]), seq)
    return out.reshape(bsz, seq, d)
```

```python
import functools
import math

import jax
import jax.numpy as jnp
from jax import lax
from jax.experimental import pallas as pl
from jax.experimental.pallas import tpu as pltpu

F32 = jnp.float32
BF16 = jnp.bfloat16

D_MODEL = 2048
SEQ = 2048
N_MEM = 256
HEAD_DIM = 128
BRANCH_W = 1024
DIFF_HEADS = 8
DIFF_QK = 64
WIN_HEADS = 8
WIN_KV_HEADS = 2
WIN_GROUP = WIN_HEADS // WIN_KV_HEADS
WINDOW = 128
MEM_HEADS = 4
MEM_DIM = 256
OFF_DQ, OFF_DK, OFF_DV, OFF_WQ, OFF_WK, OFF_WV, OFF_MQ = 0, 1024, 2048, 3072, 4096, 4352, 4608
IN_W = 5632
D_FF = 5504
D_FF_PAD = 5632
REL_BUCKETS = 32
REL_MAX_DIST = 128
DEPTH = 1
ALPHA = (2 * DEPTH) ** 0.25
LN_EPS = 1e-5
NEG = -1e30
LOG2E = math.log2(math.e)
LAMBDA_INIT =0.8 - 0.6 * math.exp(-0.3 * 0)

VMEM_LIMIT = 56 * 1024 * 1024
LN_ROWS = 128


def _ln_rows(x, g, b):
    mu = jnp.mean(x, axis=-1, keepdims=True)
    xc = x - mu
    var = jnp.mean(xc * xc, axis=-1, keepdims=True)
    return xc * lax.rsqrt(var + LN_EPS) * g + b


def _dot(a, b):
    return jnp.dot(a, b, preferred_element_type=F32)


def _dot_nt(a, b):
    return lax.dot_general(a, b, (((1,), (1,)), ((), ())), preferred_element_type=F32)


def _params(*sem, flags=None):
    return pltpu.CompilerParams(dimension_semantics=sem, vmem_limit_bytes=VMEM_LIMIT, flags=flags)


def _ln_proj_kernel(x_ref, g_ref, b_ref, w_ref, o_ref, h_scr):
    @pl.when(pl.program_id(1) == 0)
    def _():
        def body(r, c):
            rows = pl.ds(pl.multiple_of(r * LN_ROWS, LN_ROWS), LN_ROWS)
            h_scr[rows, :] = _ln_rows(x_ref[rows, :], g_ref[...], b_ref[...]).astype(BF16)
            return c
        lax.fori_loop(0, x_ref.shape[0] // LN_ROWS, body, 0)

    o_ref[...] = _dot(h_scr[...], w_ref[...]).astype(BF16)


def _ln_proj(x2, g, b, w, tm=1024, tn=512):
    m, d = x2.shape
    n = w.shape[1]
    return pl.pallas_call(
        _ln_proj_kernel,
        grid=(m // tm, n // tn),
        in_specs=[pl.BlockSpec((tm, d), lambda i, j: (i, 0)),
                  pl.BlockSpec((1, d), lambda i, j: (0, 0)),
                  pl.BlockSpec((1, d), lambda i, j: (0, 0)),
                  pl.BlockSpec((d, tn), lambda i, j: (0, j))],
        out_specs=pl.BlockSpec((tm, tn), lambda i, j: (i, j)),
        out_shape=jax.ShapeDtypeStruct((m, n), BF16),
        scratch_shapes=[pltpu.VMEM((tm, d), BF16)],
        compiler_params=_params("parallel", "arbitrary"),
        name="ln_proj",
    )(x2, g, b, w)


def _mem_kv_kernel(m_ref, w_ref, o_ref):
    o_ref[...] = _dot(m_ref[...].astype(BF16), w_ref[...]).astype(BF16)


def _mem_kv(mem2, w, tn=512):
    m, d = mem2.shape
    n = w.shape[1]
    return pl.pallas_call(
        _mem_kv_kernel,
        grid=(n // tn,),
        in_specs=[pl.BlockSpec((m, d), lambda j: (0, 0)),
                  pl.BlockSpec((d, tn), lambda j: (0, j))],
        out_specs=pl.BlockSpec((m, tn), lambda j: (0, j)),
        out_shape=jax.ShapeDtypeStruct((m, n), BF16),
        compiler_params=_params("parallel"),
        name="mem_kv",
    )(mem2, w)


def _t5_bucket(rel):
    half = REL_BUCKETS // 2
    max_exact = half // 2
    ret = jnp.where(rel > 0, half, 0)
    n = jnp.abs(rel)
    nf = jnp.maximum(n, 1).astype(F32)
    large = max_exact + (jnp.log(nf / max_exact) / math.log(REL_MAX_DIST / max_exact)
                         * (half - max_exact)).astype(jnp.int32)
    large = jnp.minimum(large, half - 1)
    return ret + jnp.where(n < max_exact, n, large)


BIAS_ROWS = 64


def _bias_lookup_kernel(tab_ref, bucket_ref, o_ref):
    h = pl.program_id(0)

    def body(r, carry):
        rows = pl.ds(pl.multiple_of(r * BIAS_ROWS, BIAS_ROWS), BIAS_ROWS)
        bk = bucket_ref[rows, :]
        out = jnp.full(bk.shape, NEG, F32)
        for b in range(REL_BUCKETS):
            out = jnp.where(bk == b, tab_ref[b, h], out)
        o_ref[0, rows, :] = out
        return carry
    lax.fori_loop(0, bucket_ref.shape[0] // BIAS_ROWS, body, 0)


def _bias_lookup(table, bucket):
    nh = table.shape[1]
    r, c = bucket.shape
    return pl.pallas_call(
        _bias_lookup_kernel,
        grid=(nh,),
        in_specs=[pl.BlockSpec(memory_space=pltpu.SMEM),
                  pl.BlockSpec((r, c), lambda h: (0, 0))],
        out_specs=pl.BlockSpec((1, r, c), lambda h: (h, 0, 0)),
        out_shape=jax.ShapeDtypeStruct((nh, r, c), F32),
        compiler_params=_params("parallel"),
        name="bias_lookup",
    )(table.astype(F32), bucket)


def _diff_bias_tiles(table, tq):
    d = jnp.arange(5)[:, None, None] - 2
    q = jnp.arange(tq)[None, :, None]
    k = jnp.arange(tq)[None, None, :]
    bucket = _t5_bucket(d * tq + k - q).astype(jnp.int32).reshape(5 * tq, tq)
    return _bias_lookup(table, bucket).reshape(table.shape[1], 5, tq, tq)


def _win_bias_tiles(table, tq):
    q = jnp.arange(tq)[:, None]
    c = jnp.arange(tq + 2 * WINDOW)[None, :]
    rel = c - WINDOW - q
    bucket = jnp.where(jnp.abs(rel) <= WINDOW, _t5_bucket(rel), -1).astype(jnp.int32)
    return _bias_lookup(table, bucket)


def _diff_kernel(q_ref, k_ref, v_ref, band_ref, lq1_ref, lk1_ref, lq2_ref, lk2_ref, g_ref, o_ref,
                 s_scr, p_scr, den_scr, *, tq, rc, n_tiles, tiles_per_head):
    t = pl.program_id(0)

    @pl.when(t == 0)
    def _():
        s_scr[...] = jnp.zeros_like(s_scr)
        p_scr[...] = jnp.zeros_like(p_scr)
        den_scr[...] = jnp.ones_like(den_scr)

    def tick(slot):
        other = 1 - slot
        i = jnp.minimum(t, n_tiles - 1) % tiles_per_head
        nk = k_ref.shape[1] // tq
        q = (q_ref[0].astype(F32) * (DIFF_QK ** -0.5 * LOG2E)).astype(BF16)
        lane = lax.broadcasted_iota(jnp.int32, q.shape, 1)
        zero = jnp.zeros_like(q)
        qs = jnp.concatenate([jnp.where(lane < DIFF_QK, q, zero), jnp.where(lane >= DIFF_QK, q, zero)], axis=0)
        for j in range(nk):
            cols = slice(j * tq, (j + 1) * tq)
            bias = band_ref[0, jnp.clip(j - i, -2, 2) + 2]
            s = _dot_nt(qs, k_ref[0, cols, :])
            s_scr[slot, 0:tq, cols] = s[0:tq] + bias
            s_scr[slot, tq:, cols] = s[tq:] + bias

        nchunk = 2 * tq // rc
        chunk = lambda r: slice(r * rc, (r + 1) * rc)
        row_max = [jnp.max(s_scr[other, chunk(r), :], axis=-1, keepdims=True) for r in range(nchunk)]
        for r in range(nchunk):
            p = jnp.exp2(s_scr[other, chunk(r), :] - row_max[r])
            den_scr[other, chunk(r), :] = jnp.sum(p, axis=-1, keepdims=True)
            p_scr[other, chunk(r), :] = p.astype(BF16)

        lam = (jnp.exp(jnp.sum(lq1_ref[...] * lk1_ref[...], axis=-1, keepdims=True))
               - jnp.exp(jnp.sum(lq2_ref[...] * lk2_ref[...], axis=-1, keepdims=True)) + LAMBDA_INIT)
        pv = _dot(p_scr[slot], v_ref[0])
        den = den_scr[slot]
        o = pv[0:tq] * (1.0 / den[0:tq]) - pv[tq:] * (lam / den[tq:])
        ms = jnp.mean(o * o, axis=-1, keepdims=True)
        o_ref[0] = (o * lax.rsqrt(ms + LN_EPS) * g_ref[...] * (1.0 - LAMBDA_INIT)).astype(BF16)

    pl.when(t % 2 == 0)(functools.partial(tick, 0))
    pl.when(t % 2 == 1)(functools.partial(tick, 1))


def _diff_attn(proj, band, lq1, lk1, lq2, lk2, g, tq=256, rc=32):
    b, s, _ = proj.shape
    cb = HEAD_DIM
    ni = s // tq
    n_tiles = b * DIFF_HEADS * ni
    depth = 2

    def tile(t, lag):
        tc = jnp.clip(t - lag, 0, n_tiles - 1)
        return tc // (DIFF_HEADS * ni), (tc // ni) % DIFF_HEADS, tc % ni

    def q_map(t):
        bi, h, i = tile(t, 0)
        return bi, i, OFF_DQ // cb + h

    def k_map(t):
        bi, h, _ = tile(t, 0)
        return bi, 0, OFF_DK // cb + h

    def v_map(t):
        bi, h, _ = tile(t, depth)
        return bi, 0, OFF_DV // cb + h

    def o_map(t):
        bi, h, i = tile(t, depth)
        return bi, i, h

    vec = lambda n: pl.BlockSpec((1, n), lambda t: (0, 0))
    return pl.pallas_call(
        functools.partial(_diff_kernel, tq=tq, rc=rc, n_tiles=n_tiles, tiles_per_head=ni),
        grid=(n_tiles + depth,),
        in_specs=[pl.BlockSpec((1, tq, cb), q_map),
                  pl.BlockSpec((1, s, cb), k_map),
                  pl.BlockSpec((1, s, cb), v_map),
                  pl.BlockSpec((1, 5, tq, tq), lambda t: (tile(t, 0)[1], 0, 0, 0)),
                  vec(DIFF_QK), vec(DIFF_QK), vec(DIFF_QK), vec(DIFF_QK), vec(HEAD_DIM)],
        out_specs=pl.BlockSpec((1, tq, cb), o_map),
        out_shape=jax.ShapeDtypeStruct((b, s, BRANCH_W), BF16),
        scratch_shapes=[pltpu.VMEM((2, 2 * tq, s), F32), pltpu.VMEM((2, 2 * tq, s), BF16),
                        pltpu.VMEM((2, 2 * tq, 1), F32)],
        compiler_params=_params("arbitrary"),
        name="diff_attn",
    )(proj, proj, proj, band, lq1, lk1, lq2, lk2, g)


def _win_kernel(sink_ref, q_ref, kp_ref, km_ref, kn_ref, vp_ref, vm_ref, vn_ref, bias_ref, o_ref, *, tq):
    g = pl.program_id(1)
    i = pl.program_id(2)
    first = i == 0
    last = i == pl.num_programs(2) - 1
    kp, km, kn = kp_ref[0], km_ref[0], kn_ref[0]
    vp, vm, vn = vp_ref[0], vm_ref[0], vn_ref[0]
    scale = jnp.asarray(HEAD_DIM ** -0.5, F32)
    w = WINDOW
    for j in range(WIN_GROUP):
        cols = slice(j * HEAD_DIM, (j + 1) * HEAD_DIM)
        qh = q_ref[0, :, cols]
        sp = _dot_nt(qh, kp) * scale + bias_ref[j, :, 0:w]
        sm = _dot_nt(qh, km) * scale + bias_ref[j, :, w:w + tq]
        sn = _dot_nt(qh, kn) * scale + bias_ref[j, :, w + tq:]
        sp = jnp.where(first, NEG, sp)
        sn = jnp.where(last, NEG, sn)
        sink = sink_ref[g * WIN_GROUP + j]
        m = jnp.maximum(jnp.maximum(jnp.max(sp, axis=-1, keepdims=True), jnp.max(sm, axis=-1, keepdims=True)),
                        jnp.maximum(jnp.max(sn, axis=-1, keepdims=True), sink))
        pp, pm, pn = jnp.exp(sp - m), jnp.exp(sm - m), jnp.exp(sn - m)
        den = (jnp.sum(pp, axis=-1, keepdims=True) + jnp.sum(pm, axis=-1, keepdims=True)
               + jnp.sum(pn, axis=-1, keepdims=True) + jnp.exp(sink - m))
        r = 1.0 / den
        o = (_dot((pp * r).astype(BF16), vp) + _dot((pm * r).astype(BF16), vm)
             + _dot((pn * r).astype(BF16), vn))
        o_ref[0, :, cols] = o.astype(BF16)


def _win_attn(proj, bias, sink, tq=256):
    b, s, _ = proj.shape
    hd = HEAD_DIM
    nb = s // WINDOW
    r = tq // WINDOW
    prev = lambda bi, g, i: jnp.maximum(i * r - 1, 0)
    nxt = lambda bi, g, i: jnp.minimum((i + 1) * r, nb - 1)
    kcol, vcol = OFF_WK // hd, OFF_WV // hd
    return pl.pallas_call(
        functools.partial(_win_kernel, tq=tq),
        grid=(b, WIN_KV_HEADS, s // tq),
        in_specs=[pl.BlockSpec(memory_space=pltpu.SMEM),
                  pl.BlockSpec((1, tq, WIN_GROUP * hd), lambda bi, g, i: (bi, i, OFF_WQ // (WIN_GROUP * hd) + g)),
                  pl.BlockSpec((1, WINDOW, hd), lambda bi, g, i: (bi, prev(bi, g, i), kcol + g)),
                  pl.BlockSpec((1, tq, hd), lambda bi, g, i: (bi, i, kcol + g)),
                  pl.BlockSpec((1, WINDOW, hd), lambda bi, g, i: (bi, nxt(bi, g, i), kcol + g)),
                  pl.BlockSpec((1, WINDOW, hd), lambda bi, g, i: (bi, prev(bi, g, i), vcol + g)),
                  pl.BlockSpec((1, tq, hd), lambda bi, g, i: (bi, i, vcol + g)),
                  pl.BlockSpec((1, WINDOW, hd), lambda bi, g, i: (bi, nxt(bi, g, i), vcol + g)),
                  pl.BlockSpec((WIN_GROUP, tq, tq + 2 * WINDOW), lambda bi, g, i: (g, 0, 0))],
        out_specs=pl.BlockSpec((1, tq, WIN_GROUP * hd), lambda bi, g, i: (bi, i, g)),
        out_shape=jax.ShapeDtypeStruct((b, s, BRANCH_W), BF16),
        compiler_params=_params("parallel", "parallel", "arbitrary"),
        name="win_attn",
    )(sink, proj, proj, proj, proj, proj, proj, proj, bias)


def _mem_attn_kernel(q_ref, k_ref, v_ref, o_ref):
    s = _dot_nt(q_ref[0], k_ref[0]) * jnp.asarray(MEM_DIM ** -0.5, F32)
    p = jnp.exp(s - jnp.max(s, axis=-1, keepdims=True))
    p = p * (1.0 / jnp.sum(p, axis=-1, keepdims=True))
    o_ref[0] = _dot(p.astype(BF16), v_ref[0]).astype(BF16)


def _mem_attn(proj, memkv, tq=512):
    b, s, _ = proj.shape
    md = MEM_DIM
    return pl.pallas_call(
        _mem_attn_kernel,
        grid=(b, MEM_HEADS, s // tq),
        in_specs=[pl.BlockSpec((1, tq, md), lambda bi, h, i: (bi, i, OFF_MQ // md + h)),
                  pl.BlockSpec((1, N_MEM, md), lambda bi, h, i: (bi, 0, h)),
                  pl.BlockSpec((1, N_MEM, md), lambda bi, h, i: (bi, 0, MEM_HEADS + h))],
        out_specs=pl.BlockSpec((1, tq, md), lambda bi, h, i: (bi, i, h)),
        out_shape=jax.ShapeDtypeStruct((b, s, BRANCH_W), BF16),
        compiler_params=_params("parallel", "parallel", "arbitrary"),
        name="mem_attn",
    )(proj, memkv, memkv)


def _mix_kernel(x_ref, lg_ref, lb_ref, a_ref, b_ref, c_ref, wg0_ref, wg1_ref, wg2_ref,
                bg0_ref, bg1_ref, bg2_ref, wb_ref, wo_ref, g1_ref, b1_ref, o_ref, h_scr, acc_scr):
    n = pl.program_id(1)
    nrow = x_ref.shape[0] // LN_ROWS

    @pl.when(n == 0)
    def _():
        def body(r, c):
            rows = pl.ds(pl.multiple_of(r * LN_ROWS, LN_ROWS), LN_ROWS)
            h_scr[rows, :] = _ln_rows(x_ref[rows, :], lg_ref[...], lb_ref[...]).astype(BF16)
            return c
        lax.fori_loop(0, nrow, body, 0)
        acc_scr[...] = jnp.zeros_like(acc_scr)

    h = h_scr[...]
    mixed = None
    for br_ref, wg_ref, bg_ref, k in ((a_ref, wg0_ref, bg0_ref, 0), (b_ref, wg1_ref, bg1_ref, 1),
                                      (c_ref, wg2_ref, bg2_ref, 2)):
        gate = jax.nn.sigmoid(_dot(h, wg_ref[...]) + bg_ref[...])
        term = gate * _dot(br_ref[...], wb_ref[k])
        mixed = term if mixed is None else mixed + term
    acc_scr[...] += _dot(mixed.astype(BF16), wo_ref[...])

    @pl.when(n == pl.num_programs(1) - 1)
    def _():
        def body(r, c):
            rows = pl.ds(pl.multiple_of(r * LN_ROWS, LN_ROWS), LN_ROWS)
            hh = _ln_rows(x_ref[rows, :], lg_ref[...], lb_ref[...])
            o_ref[rows, :] = _ln_rows(ALPHA * hh + acc_scr[rows, :], g1_ref[...], b1_ref[...])
            return c
        lax.fori_loop(0, nrow, body, 0)


def _mix(x2, lg, lb, a, b, c, wg, bg, wb, wo, g1, b1, tm=512, tn=256):
    m, d = x2.shape
    bw = a.shape[1]
    nn = d // tn
    row = lambda w: pl.BlockSpec((tm, w), lambda i, n: (i, 0))
    vec = pl.BlockSpec((1, d), lambda i, n: (0, 0))
    wgs = [pl.BlockSpec((d, tn), functools.partial(lambda i, n, k: (0, k * nn + n), k=k)) for k in range(3)]
    bgs = [pl.BlockSpec((1, tn), functools.partial(lambda i, n, k: (0, k * nn + n), k=k)) for k in range(3)]
    return pl.pallas_call(
        _mix_kernel,
        grid=(m // tm, nn),
        in_specs=[row(d), vec, vec, row(bw), row(bw), row(bw), *wgs, *bgs,
                  pl.BlockSpec((3, bw, tn), lambda i, n: (0, 0, n)),
                  pl.BlockSpec((tn, d), lambda i, n: (n, 0)), vec, vec],
        out_specs=pl.BlockSpec((tm, d), lambda i, n: (i, 0)),
        out_shape=jax.ShapeDtypeStruct((m, d), F32),
        scratch_shapes=[pltpu.VMEM((tm, d), BF16), pltpu.VMEM((tm, d), F32)],
        compiler_params=_params("parallel", "arbitrary"),
        name="mix",
    )(x2, lg, lb, a, b, c, wg, wg, wg, bg, bg, bg, wb, wo, g1, b1)


HALO = 16
CONV_ROWS = 32
MXU_N = 256


def _gelu_tanh(x):
    return 0.5 * x * (1.0 + jnp.tanh(math.sqrt(2.0 / math.pi) * (x + 0.044715 * (x * x * x))))


def _ffn_kernel(ha_ref, hp_ref, hn_ref, hc_ref, wv_ref, wg_ref, cwv_ref, cwg_ref, cbv_ref, cbg_ref, wd_ref,
                g2_ref, b2_ref, o_ref, hx_scr, u_scr, act_scr, acc_scr, *, nf, n_chunks, tiles_per_seq):
    t = pl.program_id(0)
    ta = jnp.minimum(t, n_chunks - 1)
    tc = jnp.clip(t - 2, 0, n_chunks - 1)
    fa, ia, fc = ta % nf, ta // nf, tc % nf
    tm = ha_ref.shape[0]
    tf = wv_ref.shape[1]
    ext = tm + 2 * HALO

    @pl.when(t == 0)
    def _():
        u_scr[...] = jnp.zeros_like(u_scr)
        act_scr[...] = jnp.zeros_like(act_scr)

    @pl.when(fa == 0)
    def _():
        seq_first = (ia % tiles_per_seq) == 0
        seq_last = (ia % tiles_per_seq) == tiles_per_seq - 1
        hx_scr[0:HALO, :] = jnp.where(seq_first, 0.0, hp_ref[...]).astype(BF16)
        hx_scr[HALO:HALO + tm, :] = ha_ref[...].astype(BF16)
        hx_scr[HALO + tm:, :] = jnp.where(seq_last, 0.0, hn_ref[...]).astype(BF16)

    @pl.when(fc == 0)
    def _():
        acc_scr[...] = jnp.zeros_like(acc_scr)

    def conv_rows(u_ref, r, cols, cw_ref, cb_ref):
        taps = [u_ref[HALO + r + k - 1:HALO + r + k - 1 + CONV_ROWS, cols] for k in range(3)]
        return (taps[0] * cw_ref[0:1, :] + taps[1] * cw_ref[1:2, :] + taps[2] * cw_ref[2:3, :]) + cb_ref[...]

    def tick(u_new, u_old, act_new, act_old):
        hx = hx_scr[...]
        d = acc_scr.shape[1]

        def up(w_ref, base):
            u_new[:, base:base + tf] = _dot(hx, w_ref[...])

        def down():
            acc_scr[...] += _dot(act_old[...], wd_ref[...])

        mxu_pieces = [functools.partial(up, wv_ref, 0), functools.partial(up, wg_ref, tf), down]
        weights = [1, 0, 0]
        n_conv = tm // CONV_ROWS
        done, spent = 0, 0
        for piece, w in zip(mxu_pieces, weights):
            piece()
            spent += w
            target = n_conv * spent // sum(weights)
            for k in range(done, target):
                r = k * CONV_ROWS
                val = conv_rows(u_old, r, slice(0, tf), cwv_ref, cbv_ref)
                gate = conv_rows(u_old, r, slice(tf, 2 * tf), cwg_ref, cbg_ref)
                act_new[r:r + CONV_ROWS, :] = (_gelu_tanh(gate) * val).astype(BF16)
            done = target

    slot = t % 2
    tick(u_scr.at[slot], u_scr.at[1 - slot], act_scr.at[slot], act_scr.at[1 - slot])

    @pl.when(jnp.logical_and(t >= 2, fc == nf - 1))
    def _():
        def body(r, c):
            rows = pl.ds(pl.multiple_of(r * LN_ROWS, LN_ROWS), LN_ROWS)
            o_ref[rows, :] = _ln_rows(ALPHA * hc_ref[rows, :] + acc_scr[rows, :], g2_ref[...], b2_ref[...])
            return c
        lax.fori_loop(0, tm // LN_ROWS, body, 0)


def _ffn(h1, w_up, cw, cb, w_down, g2, b2, seq, tm=512, tf=512):
    m, d = h1.shape
    nf = D_FF_PAD // tf
    n_chunks = (m // tm) * nf
    hb = tm // HALO
    nhb = m // HALO
    chunk = lambda t, lag: jnp.clip(t - lag, 0, n_chunks - 1)
    ia = lambda t: chunk(t, 0) // nf
    fa = lambda t: chunk(t, 0) % nf
    fb = lambda t: chunk(t, 1) % nf
    ic = lambda t: chunk(t, 2) // nf
    fc = lambda t: chunk(t, 2) % nf
    vec = pl.BlockSpec((1, d), lambda t: (0, 0))
    return pl.pallas_call(
        functools.partial(_ffn_kernel, nf=nf, n_chunks=n_chunks, tiles_per_seq=seq // tm),
        grid=(n_chunks + 2,),
        in_specs=[pl.BlockSpec((tm, d), lambda t: (ia(t), 0)),
                  pl.BlockSpec((HALO, d), lambda t: (jnp.maximum(ia(t) * hb - 1, 0), 0)),
                  pl.BlockSpec((HALO, d), lambda t: (jnp.minimum((ia(t) + 1) * hb, nhb - 1), 0)),
                  pl.BlockSpec((tm, d), lambda t: (ic(t), 0)),
                  pl.BlockSpec((d, tf), lambda t: (0, fa(t))),
                  pl.BlockSpec((d, tf), lambda t: (0, nf + fa(t))),
                  pl.BlockSpec((3, tf), lambda t: (0, fb(t))),
                  pl.BlockSpec((3, tf), lambda t: (0, nf + fb(t))),
                  pl.BlockSpec((1, tf), lambda t: (0, fb(t))),
                  pl.BlockSpec((1, tf), lambda t: (0, nf + fb(t))),
                  pl.BlockSpec((tf, d), lambda t: (fc(t), 0)), vec, vec],
        out_specs=pl.BlockSpec((tm, d), lambda t: (ic(t), 0)),
        out_shape=jax.ShapeDtypeStruct((m, d), F32),
        scratch_shapes=[pltpu.VMEM((tm + 2 * HALO, d), BF16),
                        pltpu.VMEM((2, tm + 2 * HALO, 2 * tf), F32), pltpu.VMEM((2, tm, tf), BF16),
                        pltpu.VMEM((tm, d), F32)],
        compiler_params=_params("arbitrary"),
        name="ffn",
    )(h1, h1, h1, h1, w_up, w_up, cw, cw, cb, cb, w_down, g2, b2)


def _pad_ff(t, axis):
    val, gate = jnp.split(t, 2, axis=axis)
    pad = [(0, 0)] * t.ndim
    pad[axis] = (0, D_FF_PAD - D_FF)
    return jnp.concatenate([jnp.pad(val, pad), jnp.pad(gate, pad)], axis=axis)


def kernel(x, mem, ln_in_g, ln_in_b, rel_table, w_in, w_mem_kv, diff_lq1, diff_lk1, diff_lq2, diff_lk2,
           diff_subln_g, win_sink, w_gate, b_gate, w_branch, w_o, ln1_g, ln1_b, w_up, conv_w, conv_b,
           w_down, ln2_g, ln2_b):
    assert w_in.shape[0] == DEPTH == 1
    bsz, seq, d = x.shape
    x2 = x.reshape(bsz * seq, d)
    row = lambda v: v.reshape(1, -1).astype(F32)
    l = 0
    proj = _ln_proj(x2, row(ln_in_g), row(ln_in_b), w_in[l].astype(BF16)).reshape(bsz, seq, IN_W)
    memkv = _mem_kv(mem.reshape(bsz * N_MEM, d), w_mem_kv[l].astype(BF16)).reshape(bsz, N_MEM, 2 * BRANCH_W)

    tq_d, tq_w = 256, 256
    a = _diff_attn(proj, _diff_bias_tiles(rel_table[:, :DIFF_HEADS] * LOG2E, tq_d), row(diff_lq1[l]), row(diff_lk1[l]),
                   row(diff_lq2[l]), row(diff_lk2[l]), row(diff_subln_g[l]), tq=tq_d)
    b = _win_attn(proj, _win_bias_tiles(rel_table[:, DIFF_HEADS:], tq_w), win_sink[l].astype(F32), tq=tq_w)
    c = _mem_attn(proj, memkv)

    m = bsz * seq
    h1 = _mix(x2, row(ln_in_g), row(ln_in_b), a.reshape(m, -1), b.reshape(m, -1), c.reshape(m, -1),
              w_gate[l].astype(BF16), row(b_gate[l]), w_branch[l].astype(BF16), w_o[l].astype(BF16),
              row(ln1_g[l]), row(ln1_b[l]))

    w_up_p = _pad_ff(w_up[l], 1).astype(BF16)
    cw_p = _pad_ff(conv_w[l].astype(F32), 1)
    cb_p = _pad_ff(row(conv_b[l]), 1)
    w_down_p = jnp.pad(w_down[l], ((0, D_FF_PAD - D_FF), (0, 0))).astype(BF16)
    out = _ffn(h1, w_up_p, cw_p, cb_p, w_down_p, row(ln2_g[l]), row(ln2_b[l]), seq)
    return out.reshape(bsz, seq, d)
```

```python
import functools
import math

import jax
import jax.numpy as jnp
from jax import lax
from jax.experimental import pallas as pl
from jax.experimental.pallas import tpu as pltpu

F32 = jnp.float32
BF16 = jnp.bfloat16

D_MODEL = 2048
SEQ = 2048
N_MEM = 256
HEAD_DIM = 128
BRANCH_W = 1024
DIFF_HEADS = 8
DIFF_QK = 64
WIN_HEADS = 8
WIN_KV_HEADS = 2
WIN_GROUP = WIN_HEADS // WIN_KV_HEADS
WINDOW = 128
MEM_HEADS = 4
MEM_DIM = 256
OFF_DQ, OFF_DK, OFF_DV, OFF_WQ, OFF_WK, OFF_WV, OFF_MQ = 0, 1024, 2048, 3072, 4096, 4352, 4608
IN_W = 5632
D_FF = 5504
D_FF_PAD = 5632
REL_BUCKETS = 32
REL_MAX_DIST = 128
DEPTH = 1
ALPHA = (2 * DEPTH) ** 0.25
LN_EPS = 1e-5
NEG = -1e30
LOG2E = math.log2(math.e)
LAMBDA_INIT = 0.8 - 0.6 * math.exp(-0.3 * 0)

VMEM_LIMIT = 56 * 1024 * 1024
LN_ROWS = 128


def _ln_rows(x, g, b):
    mu = jnp.mean(x, axis=-1, keepdims=True)
    xc = x - mu
    var = jnp.mean(xc * xc, axis=-1, keepdims=True)
    return xc * lax.rsqrt(var + LN_EPS) * g + b


def _dot(a, b):
    return jnp.dot(a, b, preferred_element_type=F32)


def _dot_nt(a, b):
    return lax.dot_general(a, b, (((1,), (1,)), ((), ())), preferred_element_type=F32)


def _params(*sem):
    return pltpu.CompilerParams(dimension_semantics=sem, vmem_limit_bytes=VMEM_LIMIT)


def _ln_proj_kernel(x_ref, g_ref, b_ref, w_ref, o_ref, h_scr):
    @pl.when(pl.program_id(1) == 0)
    def _():
        def body(r, c):
            rows = pl.ds(pl.multiple_of(r * LN_ROWS, LN_ROWS), LN_ROWS)
            h_scr[rows, :] = _ln_rows(x_ref[rows, :], g_ref[...], b_ref[...]).astype(BF16)
            return c
        lax.fori_loop(0, x_ref.shape[0] // LN_ROWS, body, 0)

    o_ref[...] = _dot(h_scr[...], w_ref[...]).astype(BF16)


def _ln_proj(x2, g, b, w, tm=1024, tn=512):
    m, d = x2.shape
    n = w.shape[1]
    return pl.pallas_call(
        _ln_proj_kernel,
        grid=(m // tm, n // tn),
        in_specs=[pl.BlockSpec((tm, d), lambda i, j: (i, 0)),
                  pl.BlockSpec((1, d), lambda i, j: (0, 0)),
                  pl.BlockSpec((1, d), lambda i, j: (0, 0)),
                  pl.BlockSpec((d, tn), lambda i, j: (0, j))],
        out_specs=pl.BlockSpec((tm, tn), lambda i, j: (i, j)),
        out_shape=jax.ShapeDtypeStruct((m, n), BF16),
        scratch_shapes=[pltpu.VMEM((tm, d), BF16)],
        compiler_params=_params("parallel", "arbitrary"),
        name="ln_proj",
    )(x2, g, b, w)


def _mem_kv_kernel(m_ref, w_ref, o_ref):
    o_ref[...] = _dot(m_ref[...].astype(BF16), w_ref[...]).astype(BF16)


def _mem_kv(mem2, w, tn=512):
    m, d = mem2.shape
    n = w.shape[1]
    return pl.pallas_call(
        _mem_kv_kernel,
        grid=(n // tn,),
        in_specs=[pl.BlockSpec((m, d), lambda j: (0, 0)),
                  pl.BlockSpec((d, tn), lambda j: (0, j))],
        out_specs=pl.BlockSpec((m, tn), lambda j: (0, j)),
        out_shape=jax.ShapeDtypeStruct((m, n), BF16),
        compiler_params=_params("parallel"),
        name="mem_kv",
    )(mem2, w)


def _t5_bucket(rel):
    half = REL_BUCKETS // 2
    max_exact = half // 2
    ret = jnp.where(rel > 0, half, 0)
    n = jnp.abs(rel)
    nf = jnp.maximum(n, 1).astype(F32)
    large = max_exact + (jnp.log(nf / max_exact) / math.log(REL_MAX_DIST / max_exact)
                         * (half - max_exact)).astype(jnp.int32)
    large = jnp.minimum(large, half - 1)
    return ret + jnp.where(n < max_exact, n, large)


BIAS_ROWS = 64


def _bias_lookup_kernel(tab_ref, bucket_ref, o_ref):
    h = pl.program_id(0)

    def body(r, carry):
        rows = pl.ds(pl.multiple_of(r * BIAS_ROWS, BIAS_ROWS), BIAS_ROWS)
        bk = bucket_ref[rows, :]
        out = jnp.full(bk.shape, NEG, F32)
        for b in range(REL_BUCKETS):
            out = jnp.where(bk == b, tab_ref[b, h], out)
        o_ref[0, rows, :] = out
        return carry
    lax.fori_loop(0, bucket_ref.shape[0] // BIAS_ROWS, body, 0)


def _bias_lookup(table, bucket):
    nh = table.shape[1]
    r, c = bucket.shape
    return pl.pallas_call(
        _bias_lookup_kernel,
        grid=(nh,),
        in_specs=[pl.BlockSpec(memory_space=pltpu.SMEM),
                  pl.BlockSpec((r, c), lambda h: (0, 0))],
        out_specs=pl.BlockSpec((1, r, c), lambda h: (h, 0, 0)),
        out_shape=jax.ShapeDtypeStruct((nh, r, c), F32),
        compiler_params=_params("parallel"),
        name="bias_lookup",
    )(table.astype(F32), bucket)


def _diff_bias_tiles(table, tq):
    d = jnp.arange(5)[:, None, None] - 2
    q = jnp.arange(tq)[None, :, None]
    k = jnp.arange(tq)[None, None, :]
    bucket = _t5_bucket(d * tq + k - q).astype(jnp.int32).reshape(5 * tq, tq)
    return _bias_lookup(table, bucket).reshape(table.shape[1], 5, tq, tq)


def _win_bias_tiles(table, tq):
    q = jnp.arange(tq)[:, None]
    c = jnp.arange(tq + 2 * WINDOW)[None, :]
    rel = c - WINDOW - q
    bucket = jnp.where(jnp.abs(rel) <= WINDOW, _t5_bucket(rel), -1).astype(jnp.int32)
    return _bias_lookup(table, bucket)


def _diff_kernel(q_ref, k_ref, v_ref, band_ref, lq1_ref, lk1_ref, lq2_ref, lk2_ref, g_ref, o_ref,
                 s_scr, p_scr, den_scr, *, tq, rc, n_tiles, tiles_per_head):
    t = pl.program_id(0)

    @pl.when(t == 0)
    def _():
        s_scr[...] = jnp.zeros_like(s_scr)
        p_scr[...] = jnp.zeros_like(p_scr)
        den_scr[...] = jnp.ones_like(den_scr)

    def tick(slot):
        other = 1 - slot
        i = jnp.minimum(t, n_tiles - 1) % tiles_per_head
        nk = k_ref.shape[1] // tq
        q = (q_ref[0].astype(F32) * (DIFF_QK ** -0.5 * LOG2E)).astype(BF16)
        lane = lax.broadcasted_iota(jnp.int32, q.shape, 1)
        zero = jnp.zeros_like(q)
        qs = jnp.concatenate([jnp.where(lane < DIFF_QK, q, zero), jnp.where(lane >= DIFF_QK, q, zero)], axis=0)
        for j in range(nk):
            cols = slice(j * tq, (j + 1) * tq)
            bias = band_ref[0, jnp.clip(j - i, -2, 2) + 2]
            s = _dot_nt(qs, k_ref[0, cols, :])
            s_scr[slot, 0:tq, cols] = s[0:tq] + bias
            s_scr[slot, tq:, cols] = s[tq:] + bias

        nchunk = 2 * tq // rc
        chunk = lambda r: slice(r * rc, (r + 1) * rc)
        row_max = [jnp.max(s_scr[other, chunk(r), :], axis=-1, keepdims=True) for r in range(nchunk)]
        for r in range(nchunk):
            p = jnp.exp2(s_scr[other, chunk(r), :] - row_max[r])
            den_scr[other, chunk(r), :] = jnp.sum(p, axis=-1, keepdims=True)
            p_scr[other, chunk(r), :] = p.astype(BF16)

        lam = (jnp.exp(jnp.sum(lq1_ref[...] * lk1_ref[...], axis=-1, keepdims=True))
               - jnp.exp(jnp.sum(lq2_ref[...] * lk2_ref[...], axis=-1, keepdims=True)) + LAMBDA_INIT)
        pv = _dot(p_scr[slot], v_ref[0])
        den = den_scr[slot]
        o = pv[0:tq] * (1.0 / den[0:tq]) - pv[tq:] * (lam / den[tq:])
        ms = jnp.mean(o * o, axis=-1, keepdims=True)
        o_ref[0] = (o * lax.rsqrt(ms + LN_EPS) * g_ref[...] * (1.0 - LAMBDA_INIT)).astype(BF16)

    pl.when(t % 2 == 0)(functools.partial(tick, 0))
    pl.when(t % 2 == 1)(functools.partial(tick, 1))


def _diff_attn(proj, band, lq1, lk1, lq2, lk2, g, tq=256, rc=32):
    b, s, _ = proj.shape
    cb = HEAD_DIM
    ni = s // tq
    n_tiles = b * DIFF_HEADS * ni
    depth = 2

    def tile(t, lag):
        tc = jnp.clip(t - lag, 0, n_tiles - 1)
        return tc // (DIFF_HEADS * ni), (tc // ni) % DIFF_HEADS, tc % ni

    def q_map(t):
        bi, h, i = tile(t, 0)
        return bi, i, OFF_DQ // cb + h

    def k_map(t):
        bi, h, _ = tile(t, 0)
        return bi, 0, OFF_DK // cb + h

    def v_map(t):
        bi, h, _ = tile(t, depth)
        return bi, 0, OFF_DV // cb + h

    def o_map(t):
        bi, h, i = tile(t, depth)
        return bi, i, h

    vec = lambda n: pl.BlockSpec((1, n), lambda t: (0, 0))
    return pl.pallas_call(
        functools.partial(_diff_kernel, tq=tq, rc=rc, n_tiles=n_tiles, tiles_per_head=ni),
        grid=(n_tiles + depth,),
        in_specs=[pl.BlockSpec((1, tq, cb), q_map),
                  pl.BlockSpec((1, s, cb), k_map),
                  pl.BlockSpec((1, s, cb), v_map),
                  pl.BlockSpec((1, 5, tq, tq), lambda t: (tile(t, 0)[1], 0, 0, 0)),
                  vec(DIFF_QK), vec(DIFF_QK), vec(DIFF_QK), vec(DIFF_QK), vec(HEAD_DIM)],
        out_specs=pl.BlockSpec((1, tq, cb), o_map),
        out_shape=jax.ShapeDtypeStruct((b, s, BRANCH_W), BF16),
        scratch_shapes=[pltpu.VMEM((2, 2 * tq, s), F32), pltpu.VMEM((2, 2 * tq, s), BF16),
                        pltpu.VMEM((2, 2 * tq, 1), F32)],
        compiler_params=_params("arbitrary"),
        name="diff_attn",
    )(proj, proj, proj, band, lq1, lk1, lq2, lk2, g)


def _win_kernel(sink_ref, q_ref, kp_ref, km_ref, kn_ref, vp_ref, vm_ref, vn_ref, bias_ref, o_ref, *, tq):
    g = pl.program_id(1)
    i = pl.program_id(2)
    first = i == 0
    last = i == pl.num_programs(2) - 1
    kp, km, kn = kp_ref[0], km_ref[0], kn_ref[0]
    vp, vm, vn = vp_ref[0], vm_ref[0], vn_ref[0]
    scale = jnp.asarray(HEAD_DIM ** -0.5, F32)
    w = WINDOW
    for j in range(WIN_GROUP):
        cols = slice(j * HEAD_DIM, (j + 1) * HEAD_DIM)
        qh = q_ref[0, :, cols]
        sp = _dot_nt(qh, kp) * scale + bias_ref[j, :, 0:w]
        sm = _dot_nt(qh, km) * scale + bias_ref[j, :, w:w + tq]
        sn = _dot_nt(qh, kn) * scale + bias_ref[j, :, w + tq:]
        sp = jnp.where(first, NEG, sp)
        sn = jnp.where(last, NEG, sn)
        sink = sink_ref[g * WIN_GROUP + j]
        fold = lambda op, parts: functools.reduce(
            op, [x[:, c:c + w] for x in parts for c in range(0, x.shape[1], w)])
        m = jnp.maximum(jnp.max(fold(jnp.maximum, (sp, sm, sn)), axis=-1, keepdims=True), sink)
        pp, pm, pn = jnp.exp(sp - m), jnp.exp(sm - m), jnp.exp(sn - m)
        den = jnp.sum(fold(jnp.add, (pp, pm, pn)), axis=-1, keepdims=True) + jnp.exp(sink - m)
        r = 1.0 / den
        o = (_dot((pp * r).astype(BF16), vp) + _dot((pm * r).astype(BF16), vm)
             + _dot((pn * r).astype(BF16), vn))
        o_ref[0, :, cols] = o.astype(BF16)


def _win_attn(proj, bias, sink, tq=256):
    b, s, _ = proj.shape
    hd = HEAD_DIM
    nb = s // WINDOW
    r = tq // WINDOW
    prev = lambda bi, g, i: jnp.maximum(i * r - 1, 0)
    nxt = lambda bi, g, i: jnp.minimum((i + 1) * r, nb - 1)
    kcol, vcol = OFF_WK // hd, OFF_WV // hd
    return pl.pallas_call(
        functools.partial(_win_kernel, tq=tq),
        grid=(b, WIN_KV_HEADS, s // tq),
        in_specs=[pl.BlockSpec(memory_space=pltpu.SMEM),
                  pl.BlockSpec((1, tq, WIN_GROUP * hd), lambda bi, g, i: (bi, i, OFF_WQ // (WIN_GROUP * hd) + g)),
                  pl.BlockSpec((1, WINDOW, hd), lambda bi, g, i: (bi, prev(bi, g, i), kcol + g)),
                  pl.BlockSpec((1, tq, hd), lambda bi, g, i: (bi, i, kcol + g)),
                  pl.BlockSpec((1, WINDOW, hd), lambda bi, g, i: (bi, nxt(bi, g, i), kcol + g)),
                  pl.BlockSpec((1, WINDOW, hd), lambda bi, g, i: (bi, prev(bi, g, i), vcol + g)),
                  pl.BlockSpec((1, tq, hd), lambda bi, g, i: (bi, i, vcol + g)),
                  pl.BlockSpec((1, WINDOW, hd), lambda bi, g, i: (bi, nxt(bi, g, i), vcol + g)),
                  pl.BlockSpec((WIN_GROUP, tq, tq + 2 * WINDOW), lambda bi, g, i: (g, 0, 0))],
        out_specs=pl.BlockSpec((1, tq, WIN_GROUP * hd), lambda bi, g, i: (bi, i, g)),
        out_shape=jax.ShapeDtypeStruct((b, s, BRANCH_W), BF16),
        compiler_params=_params("parallel", "parallel", "arbitrary"),
        name="win_attn",
    )(sink, proj, proj, proj, proj, proj, proj, proj, bias)


def _mem_attn_kernel(q_ref, k_ref, v_ref, o_ref):
    s = _dot_nt(q_ref[0], k_ref[0]) * jnp.asarray(MEM_DIM ** -0.5, F32)
    p = jnp.exp(s - jnp.max(s, axis=-1, keepdims=True))
    p = p * (1.0 / jnp.sum(p, axis=-1, keepdims=True))
    o_ref[0] = _dot(p.astype(BF16), v_ref[0]).astype(BF16)


def _mem_attn(proj, memkv, tq=1024):
    b, s, _ = proj.shape
    md = MEM_DIM
    return pl.pallas_call(
        _mem_attn_kernel,
        grid=(b, MEM_HEADS, s // tq),
        in_specs=[pl.BlockSpec((1, tq, md), lambda bi, h, i: (bi, i, OFF_MQ // md + h)),
                  pl.BlockSpec((1, N_MEM, md), lambda bi, h, i: (bi, 0, h)),
                  pl.BlockSpec((1, N_MEM, md), lambda bi, h, i: (bi, 0, MEM_HEADS + h))],
        out_specs=pl.BlockSpec((1, tq, md), lambda bi, h, i: (bi, i, h)),
        out_shape=jax.ShapeDtypeStruct((b, s, BRANCH_W), BF16),
        compiler_params=_params("parallel", "parallel", "arbitrary"),
        name="mem_attn",
    )(proj, memkv, memkv)


def _mix_kernel(x_ref, lg_ref, lb_ref, a_ref, b_ref, c_ref, wg0_ref, wg1_ref, wg2_ref,
                bg0_ref, bg1_ref, bg2_ref, wb_ref, wo_ref, g1_ref, b1_ref, o_ref, h_scr, acc_scr):
    n = pl.program_id(1)
    nrow = x_ref.shape[0] // LN_ROWS

    @pl.when(n == 0)
    def _():
        def body(r, c):
            rows = pl.ds(pl.multiple_of(r * LN_ROWS, LN_ROWS), LN_ROWS)
            h_scr[rows, :] = _ln_rows(x_ref[rows, :], lg_ref[...], lb_ref[...]).astype(BF16)
            return c
        lax.fori_loop(0, nrow, body, 0)
        acc_scr[...] = jnp.zeros_like(acc_scr)

    h = h_scr[...]
    mixed = None
    for br_ref, wg_ref, bg_ref, k in ((a_ref, wg0_ref, bg0_ref, 0), (b_ref, wg1_ref, bg1_ref, 1),
                                      (c_ref, wg2_ref, bg2_ref, 2)):
        gate = jax.nn.sigmoid(_dot(h, wg_ref[...]) + bg_ref[...])
        term = gate * _dot(br_ref[...], wb_ref[k])
        mixed = term if mixed is None else mixed + term
    acc_scr[...] += _dot(mixed.astype(BF16), wo_ref[...])

    @pl.when(n == pl.num_programs(1) - 1)
    def _():
        def body(r, c):
            rows = pl.ds(pl.multiple_of(r * LN_ROWS, LN_ROWS), LN_ROWS)
            hh = _ln_rows(x_ref[rows, :], lg_ref[...], lb_ref[...])
            o_ref[rows, :] = _ln_rows(ALPHA * hh + acc_scr[rows, :], g1_ref[...], b1_ref[...])
            return c
        lax.fori_loop(0, nrow, body, 0)


def _mix(x2, lg, lb, a, b, c, wg, bg, wb, wo, g1, b1, tm=512, tn=256):
    m, d = x2.shape
    bw = a.shape[1]
    nn = d // tn
    row = lambda w: pl.BlockSpec((tm, w), lambda i, n: (i, 0))
    vec = pl.BlockSpec((1, d), lambda i, n: (0, 0))
    wgs = [pl.BlockSpec((d, tn), functools.partial(lambda i, n, k: (0, k * nn + n), k=k)) for k in range(3)]
    bgs = [pl.BlockSpec((1, tn), functools.partial(lambda i, n, k: (0, k * nn + n), k=k)) for k in range(3)]
    return pl.pallas_call(
        _mix_kernel,
        grid=(m // tm, nn),
        in_specs=[row(d), vec, vec, row(bw), row(bw), row(bw), *wgs, *bgs,
                  pl.BlockSpec((3, bw, tn), lambda i, n: (0, 0, n)),
                  pl.BlockSpec((tn, d), lambda i, n: (n, 0)), vec, vec],
        out_specs=pl.BlockSpec((tm, d), lambda i, n: (i, 0)),
        out_shape=jax.ShapeDtypeStruct((m, d), F32),
        scratch_shapes=[pltpu.VMEM((tm, d), BF16), pltpu.VMEM((tm, d), F32)],
        compiler_params=_params("parallel", "arbitrary"),
        name="mix",
    )(x2, lg, lb, a, b, c, wg, wg, wg, bg, bg, bg, wb, wo, g1, b1)


HALO = 16


def _gelu_tanh(x):
    return 0.5 * x * (1.0 + jnp.tanh(math.sqrt(2.0 / math.pi) * (x + 0.044715 * (x * x * x))))


def _ffn_kernel(h_ref, hp_ref, hn_ref, wv_ref, wg_ref, cwv_ref, cwg_ref, cbv_ref, cbg_ref, wd_ref,
                g2_ref, b2_ref, o_ref, hx_scr, acc_scr, *, tiles_per_seq):
    i = pl.program_id(0)
    f = pl.program_id(1)
    tm = h_ref.shape[0]
    nrow = tm // LN_ROWS

    @pl.when(f == 0)
    def _():
        seq_first = (i % tiles_per_seq) == 0
        seq_last = (i % tiles_per_seq) == tiles_per_seq - 1
        hx_scr[0:HALO, :] = jnp.where(seq_first, 0.0, hp_ref[...]).astype(BF16)
        hx_scr[HALO:HALO + tm, :] = h_ref[...].astype(BF16)
        hx_scr[HALO + tm:, :] = jnp.where(seq_last, 0.0, hn_ref[...]).astype(BF16)
        acc_scr[...] = jnp.zeros_like(acc_scr)

    hx = hx_scr[...]
    ext = tm + 2 * HALO

    def conv(u, cw_ref, cb_ref):
        prev = pltpu.roll(u, 1, 0)[HALO:HALO + tm]
        nxt = pltpu.roll(u, ext - 1, 0)[HALO:HALO + tm]
        return prev * cw_ref[0:1, :] + u[HALO:HALO + tm] * cw_ref[1:2, :] + nxt * cw_ref[2:3, :] + cb_ref[...]

    val = conv(_dot(hx, wv_ref[...]), cwv_ref, cbv_ref)
    gate = conv(_dot(hx, wg_ref[...]), cwg_ref, cbg_ref)
    acc_scr[...] += _dot((_gelu_tanh(gate) * val).astype(BF16), wd_ref[...])

    @pl.when(f == pl.num_programs(1) - 1)
    def _():
        def body(r, c):
            rows = pl.ds(pl.multiple_of(r * LN_ROWS, LN_ROWS), LN_ROWS)
            o_ref[rows, :] = _ln_rows(ALPHA * h_ref[rows, :] + acc_scr[rows, :], g2_ref[...], b2_ref[...])
            return c
        lax.fori_loop(0, nrow, body, 0)


def _ffn(h1, w_up, cw, cb, w_down, g2, b2, seq, tm=512, tf=512):
    m, d = h1.shape
    nf = D_FF_PAD // tf
    hb = tm // HALO
    nhb = m // HALO
    vec = pl.BlockSpec((1, d), lambda i, f: (0, 0))
    return pl.pallas_call(
        functools.partial(_ffn_kernel, tiles_per_seq=seq // tm),
        grid=(m // tm, nf),
        in_specs=[pl.BlockSpec((tm, d), lambda i, f: (i, 0)),
                  pl.BlockSpec((HALO, d), lambda i, f: (jnp.maximum(i * hb - 1, 0), 0)),
                  pl.BlockSpec((HALO, d), lambda i, f: (jnp.minimum((i + 1) * hb, nhb - 1), 0)),
                  pl.BlockSpec((d, tf), lambda i, f: (0, f)),
                  pl.BlockSpec((d, tf), lambda i, f: (0, nf + f)),
                  pl.BlockSpec((3, tf), lambda i, f: (0, f)),
                  pl.BlockSpec((3, tf), lambda i, f: (0, nf + f)),
                  pl.BlockSpec((1, tf), lambda i, f: (0, f)),
                  pl.BlockSpec((1, tf), lambda i, f: (0, nf + f)),
                  pl.BlockSpec((tf, d), lambda i, f: (f, 0)), vec, vec],
        out_specs=pl.BlockSpec((tm, d), lambda i, f: (i, 0)),
        out_shape=jax.ShapeDtypeStruct((m, d), F32),
        scratch_shapes=[pltpu.VMEM((tm + 2 * HALO, d), BF16), pltpu.VMEM((tm, d), F32)],
        compiler_params=_params("parallel", "arbitrary"),
        name="ffn",
    )(h1, h1, h1, w_up, w_up, cw, cw, cb, cb, w_down, g2, b2)


def _w_up_prep_kernel(w_ref, o_ref):
    o_ref[:, 0:D_FF] = w_ref[...].astype(BF16)
    o_ref[:, D_FF:] = jnp.zeros((o_ref.shape[0], D_FF_PAD - D_FF), BF16)


def _w_up_prep(w_up, tr=256):
    d = w_up.shape[0]
    return pl.pallas_call(
        _w_up_prep_kernel,
        grid=(2, d // tr),
        in_specs=[pl.BlockSpec((tr, D_FF), lambda half, r: (r, half))],
        out_specs=pl.BlockSpec((tr, D_FF_PAD), lambda half, r: (r, half)),
        out_shape=jax.ShapeDtypeStruct((d, 2 * D_FF_PAD), BF16),
        compiler_params=_params("parallel", "parallel"),
        name="w_up_prep",
    )(w_up)


def _pad_ff(t, axis):
    val, gate = jnp.split(t, 2, axis=axis)
    pad = [(0, 0)] * t.ndim
    pad[axis] = (0, D_FF_PAD - D_FF)
    return jnp.concatenate([jnp.pad(val, pad), jnp.pad(gate, pad)], axis=axis)


def kernel(x, mem, ln_in_g, ln_in_b, rel_table, w_in, w_mem_kv, diff_lq1, diff_lk1, diff_lq2, diff_lk2,
           diff_subln_g, win_sink, w_gate, b_gate, w_branch, w_o, ln1_g, ln1_b, w_up, conv_w, conv_b,
           w_down, ln2_g, ln2_b):
    assert w_in.shape[0] == DEPTH == 1
    bsz, seq, d = x.shape
    x2 = x.reshape(bsz * seq, d)
    row = lambda v: v.reshape(1, -1).astype(F32)
    l = 0
    proj = _ln_proj(x2, row(ln_in_g), row(ln_in_b), w_in[l].astype(BF16)).reshape(bsz, seq, IN_W)
    memkv = _mem_kv(mem.reshape(bsz * N_MEM, d), w_mem_kv[l].astype(BF16)).reshape(bsz, N_MEM, 2 * BRANCH_W)

    tq_d, tq_w = 256, 256
    a = _diff_attn(proj, _diff_bias_tiles(rel_table[:, :DIFF_HEADS] * LOG2E, tq_d), row(diff_lq1[l]), row(diff_lk1[l]),
                   row(diff_lq2[l]), row(diff_lk2[l]), row(diff_subln_g[l]), tq=tq_d)
    b = _win_attn(proj, _win_bias_tiles(rel_table[:, DIFF_HEADS:], tq_w), win_sink[l].astype(F32), tq=tq_w)
    c = _mem_attn(proj, memkv)

    m = bsz * seq
    h1 = _mix(x2, row(ln_in_g), row(ln_in_b), a.reshape(m, -1), b.reshape(m, -1), c.reshape(m, -1),
              w_gate[l].astype(BF16), row(b_gate[l]), w_branch[l].astype(BF16), w_o[l].astype(BF16),
              row(ln1_g[l]), row(ln1_b[l]))

    w_up_p = _w_up_prep(w_up[l].astype(F32))
    cw_p = _pad_ff(conv_w[l].astype(F32), 1)
    cb_p = _pad_ff(row(conv_b[l]), 1)
    w_down_p = jnp.pad(w_down[l], ((0, D_FF_PAD - D_FF), (0, 0))).astype(BF16)
    out = _ffn(h1, w_up_p, cw_p, cb_p, w_down_p, row(ln2_g[l]), row(ln2_b[l]), seq)
    return out.reshape(bsz, seq, d)
```

```python
import functools
import math

import jax
import jax.numpy as jnp
from jax import lax
from jax.experimental import pallas as pl
from jax.experimental.pallas import tpu as pltpu

F32 = jnp.float32
BF16 = jnp.bfloat16

D_MODEL = 2048
SEQ = 2048
N_MEM = 256
HEAD_DIM = 128
BRANCH_W = 1024
DIFF_HEADS = 8
DIFF_QK = 64
WIN_HEADS = 8
WIN_KV_HEADS = 2
WIN_GROUP = WIN_HEADS // WIN_KV_HEADS
WINDOW = 128
MEM_HEADS = 4
MEM_DIM = 256
OFF_DQ, OFF_DK, OFF_DV, OFF_WQ, OFF_WK, OFF_WV, OFF_MQ = 0, 1024, 2048, 3072, 4096, 4352, 4608
IN_W = 5632
D_FF = 5504
D_FF_PAD = 5632
REL_BUCKETS = 32
REL_MAX_DIST = 128
DEPTH = 1
ALPHA = (2 * DEPTH) ** 0.25
LN_EPS = 1e-5
NEG = -1e30
LOG2E = math.log2(math.e)
LAMBDA_INIT = 0.8 - 0.6 * math.exp(-0.3 * 0)

VMEM_LIMIT = 56 * 1024 * 1024
LN_ROWS = 128


def _ln_rows(x, g, b):
    mu = jnp.mean(x, axis=-1, keepdims=True)
    xc = x - mu
    var = jnp.mean(xc * xc, axis=-1, keepdims=True)
    return xc * lax.rsqrt(var + LN_EPS) * g + b


def _dot(a, b):
    return jnp.dot(a, b, preferred_element_type=F32)


def _dot_nt(a, b):
    return lax.dot_general(a, b, (((1,), (1,)), ((), ())), preferred_element_type=F32)


def _params(*sem):
    return pltpu.CompilerParams(dimension_semantics=sem, vmem_limit_bytes=VMEM_LIMIT)


def _ln_proj_kernel(x_ref, g_ref, b_ref, w_ref, o_ref, h_ref):
    @pl.when(pl.program_id(1) == 0)
    def _():
        def body(r, c):
            rows = pl.ds(pl.multiple_of(r * LN_ROWS, LN_ROWS), LN_ROWS)
            h_ref[rows, :] = _ln_rows(x_ref[rows, :], g_ref[...], b_ref[...]).astype(BF16)
            return c
        lax.fori_loop(0, x_ref.shape[0] // LN_ROWS, body, 0)

    o_ref[...] = _dot(h_ref[...], w_ref[...]).astype(BF16)


def _ln_proj(x2, g, b, w, tm=1024, tn=512):
    m, d = x2.shape
    n = w.shape[1]
    return pl.pallas_call(
        _ln_proj_kernel,
        grid=(m // tm, n // tn),
        in_specs=[pl.BlockSpec((tm, d), lambda i, j: (i, 0)),
                  pl.BlockSpec((1, d), lambda i, j: (0, 0)),
                  pl.BlockSpec((1, d), lambda i, j: (0, 0)),
                  pl.BlockSpec((d, tn), lambda i, j: (0, j))],
        out_specs=[pl.BlockSpec((tm, tn), lambda i, j: (i, j)),
                   pl.BlockSpec((tm, d), lambda i, j: (i, 0))],
        out_shape=[jax.ShapeDtypeStruct((m, n), BF16), jax.ShapeDtypeStruct((m, d), BF16)],
        compiler_params=_params("parallel", "arbitrary"),
        name="ln_proj",
    )(x2, g, b, w)


def _mem_kv_kernel(m_ref, w_ref, o_ref):
    o_ref[...] = _dot(m_ref[...].astype(BF16), w_ref[...]).astype(BF16)


def _mem_kv(mem2, w, tn=512):
    m, d = mem2.shape
    n = w.shape[1]
    return pl.pallas_call(
        _mem_kv_kernel,
        grid=(n // tn,),
        in_specs=[pl.BlockSpec((m, d), lambda j: (0, 0)),
                  pl.BlockSpec((d, tn), lambda j: (0, j))],
        out_specs=pl.BlockSpec((m, tn), lambda j: (0, j)),
        out_shape=jax.ShapeDtypeStruct((m, n), BF16),
        compiler_params=_params("parallel"),
        name="mem_kv",
    )(mem2, w)


def _t5_bucket(rel):
    half = REL_BUCKETS // 2
    max_exact = half // 2
    ret = jnp.where(rel > 0, half, 0)
    n = jnp.abs(rel)
    nf = jnp.maximum(n, 1).astype(F32)
    large = max_exact + (jnp.log(nf / max_exact) / math.log(REL_MAX_DIST / max_exact)
                         * (half - max_exact)).astype(jnp.int32)
    large = jnp.minimum(large, half - 1)
    return ret + jnp.where(n < max_exact, n, large)


BIAS_ROWS = 64


def _bias_lookup_kernel(tab_ref, bucket_ref, o_ref):
    h = pl.program_id(0)

    def body(r, carry):
        rows = pl.ds(pl.multiple_of(r * BIAS_ROWS, BIAS_ROWS), BIAS_ROWS)
        bk = bucket_ref[rows, :]
        out = jnp.full(bk.shape, NEG, F32)
        for b in range(REL_BUCKETS):
            out = jnp.where(bk == b, tab_ref[b, h], out)
        o_ref[0, rows, :] = out
        return carry
    lax.fori_loop(0, bucket_ref.shape[0] // BIAS_ROWS, body, 0)


def _bias_lookup(table, bucket):
    nh = table.shape[1]
    r, c = bucket.shape
    return pl.pallas_call(
        _bias_lookup_kernel,
        grid=(nh,),
        in_specs=[pl.BlockSpec(memory_space=pltpu.SMEM),
                  pl.BlockSpec((r, c), lambda h: (0, 0))],
        out_specs=pl.BlockSpec((1, r, c), lambda h: (h, 0, 0)),
        out_shape=jax.ShapeDtypeStruct((nh, r, c), F32),
        compiler_params=_params("parallel"),
        name="bias_lookup",
    )(table.astype(F32), bucket)


def _diff_bias_tiles(table, tq):
    d = jnp.arange(5)[:, None, None] - 2
    q = jnp.arange(tq)[None, :, None]
    k = jnp.arange(tq)[None, None, :]
    bucket = _t5_bucket(d * tq + k - q).astype(jnp.int32).reshape(5 * tq, tq)
    return _bias_lookup(table, bucket).reshape(table.shape[1], 5, tq, tq)


def _win_bias_tiles(table, tq):
    q = jnp.arange(tq)[:, None]
    c = jnp.arange(tq + 2 * WINDOW)[None, :]
    rel = c - WINDOW - q
    bucket = jnp.where(jnp.abs(rel) <= WINDOW, _t5_bucket(rel), -1).astype(jnp.int32)
    return _bias_lookup(table, bucket)


def _diff_kernel(q_ref, k_ref, v_ref, band_ref, lq1_ref, lk1_ref, lq2_ref, lk2_ref, g_ref, o_ref,
                 s_scr, mx_scr, p_scr, den_scr, *, tq, rc, n_tiles, tiles_per_head):
    t = pl.program_id(0)

    @pl.when(t == 0)
    def _():
        s_scr[...] = jnp.zeros_like(s_scr)
        mx_scr[...] = jnp.zeros_like(mx_scr)
        p_scr[...] = jnp.zeros_like(p_scr)
        den_scr[...] = jnp.ones_like(den_scr)

    def tick(slot):
        other = 1 - slot
        i = jnp.minimum(t, n_tiles - 1) % tiles_per_head
        nk = k_ref.shape[1] // tq
        q = (q_ref[0].astype(F32) * (DIFF_QK ** -0.5 * LOG2E)).astype(BF16)
        lane = lax.broadcasted_iota(jnp.int32, q.shape, 1)
        zero = jnp.zeros_like(q)
        qs = jnp.concatenate([jnp.where(lane < DIFF_QK, q, zero), jnp.where(lane >= DIFF_QK, q, zero)], axis=0)
        lanes = mx_scr.shape[-1]
        for j in range(nk):
            cols = slice(j * tq, (j + 1) * tq)
            bias = band_ref[0, jnp.clip(j - i, -2, 2) + 2]
            s = _dot_nt(qs, k_ref[0, cols, :])
            for half in (slice(0, tq), slice(tq, 2 * tq)):
                sb = s[half] + bias
                s_scr[slot, half, cols] = sb
                fold = functools.reduce(jnp.maximum, [sb[:, c:c + lanes] for c in range(0, tq, lanes)])
                mx_scr[slot, half, :] = fold if j == 0 else jnp.maximum(mx_scr[slot, half, :], fold)

        nchunk = 2 * tq // rc
        chunk = lambda r: slice(r * rc, (r + 1) * rc)
        row_max = [jnp.max(mx_scr[other, chunk(r), :], axis=-1, keepdims=True) for r in range(nchunk)]
        for r in range(nchunk):
            p = jnp.exp2(s_scr[other, chunk(r), :] - row_max[r])
            den_scr[other, chunk(r), :] = jnp.sum(p, axis=-1, keepdims=True)
            p_scr[other, chunk(r), :] = p.astype(BF16)

        lam = (jnp.exp(jnp.sum(lq1_ref[...] * lk1_ref[...], axis=-1, keepdims=True))
               - jnp.exp(jnp.sum(lq2_ref[...] * lk2_ref[...], axis=-1, keepdims=True)) + LAMBDA_INIT)
        pv = _dot(p_scr[slot], v_ref[0])
        den = den_scr[slot]
        o = pv[0:tq] * (1.0 / den[0:tq]) - pv[tq:] * (lam / den[tq:])
        ms = jnp.mean(o * o, axis=-1, keepdims=True)
        o_ref[0] = (o * lax.rsqrt(ms + LN_EPS) * g_ref[...] * (1.0 - LAMBDA_INIT)).astype(BF16)

    pl.when(t % 2 == 0)(functools.partial(tick, 0))
    pl.when(t % 2 == 1)(functools.partial(tick, 1))


def _diff_attn(proj, band, lq1, lk1, lq2, lk2, g, tq=256, rc=32):
    b, s, _ = proj.shape
    cb = HEAD_DIM
    ni = s // tq
    n_tiles = b * DIFF_HEADS * ni
    depth = 2

    def tile(t, lag):
        tc = jnp.clip(t - lag, 0, n_tiles - 1)
        return tc // (DIFF_HEADS * ni), (tc // ni) % DIFF_HEADS, tc % ni

    def q_map(t):
        bi, h, i = tile(t, 0)
        return bi, i, OFF_DQ // cb + h

    def k_map(t):
        bi, h, _ = tile(t, 0)
        return bi, 0, OFF_DK // cb + h

    def v_map(t):
        bi, h, _ = tile(t, depth)
        return bi, 0, OFF_DV // cb + h

    def o_map(t):
        bi, h, i = tile(t, depth)
        return bi, i, h

    vec = lambda n: pl.BlockSpec((1, n), lambda t: (0, 0))
    return pl.pallas_call(
        functools.partial(_diff_kernel, tq=tq, rc=rc, n_tiles=n_tiles, tiles_per_head=ni),
        grid=(n_tiles + depth,),
        in_specs=[pl.BlockSpec((1, tq, cb), q_map),
                  pl.BlockSpec((1, s, cb), k_map),
                  pl.BlockSpec((1, s, cb), v_map),
                  pl.BlockSpec((1, 5, tq, tq), lambda t: (tile(t, 0)[1], 0, 0, 0)),
                  vec(DIFF_QK), vec(DIFF_QK), vec(DIFF_QK), vec(DIFF_QK), vec(HEAD_DIM)],
        out_specs=pl.BlockSpec((1, tq, cb), o_map),
        out_shape=jax.ShapeDtypeStruct((b, s, BRANCH_W), BF16),
        scratch_shapes=[pltpu.VMEM((2, 2 * tq, s), F32), pltpu.VMEM((2, 2 * tq, cb), F32),
                        pltpu.VMEM((2, 2 * tq, s), BF16),
                        pltpu.VMEM((2, 2 * tq, 1), F32)],
        compiler_params=_params("arbitrary"),
        name="diff_attn",
    )(proj, proj, proj, band, lq1, lk1, lq2, lk2, g)


def _win_kernel(sink_ref, q_ref, kp_ref, km_ref, kn_ref, vp_ref, vm_ref, vn_ref, bias_ref, o_ref, *, tq):
    g = pl.program_id(1)
    i = pl.program_id(2)
    first = i == 0
    last = i == pl.num_programs(2) - 1
    kp, km, kn = kp_ref[0], km_ref[0], kn_ref[0]
    vp, vm, vn = vp_ref[0], vm_ref[0], vn_ref[0]
    scale = jnp.asarray(HEAD_DIM ** -0.5, F32)
    w = WINDOW
    for j in range(WIN_GROUP):
        cols = slice(j * HEAD_DIM, (j + 1) * HEAD_DIM)
        qh = q_ref[0, :, cols]
        sp = _dot_nt(qh, kp) * scale + bias_ref[j, :, 0:w]
        sm = _dot_nt(qh, km) * scale + bias_ref[j, :, w:w + tq]
        sn = _dot_nt(qh, kn) * scale + bias_ref[j, :, w + tq:]
        sp = jnp.where(first, NEG, sp)
        sn = jnp.where(last, NEG, sn)
        sink = sink_ref[g * WIN_GROUP + j]
        fold = lambda op, parts: functools.reduce(
            op, [x[:, c:c + w] for x in parts for c in range(0, x.shape[1], w)])
        m = jnp.maximum(jnp.max(fold(jnp.maximum, (sp, sm, sn)), axis=-1, keepdims=True), sink)
        pp, pm, pn = jnp.exp(sp - m), jnp.exp(sm - m), jnp.exp(sn - m)
        den = jnp.sum(fold(jnp.add, (pp, pm, pn)), axis=-1, keepdims=True) + jnp.exp(sink - m)
        r = 1.0 / den
        o = (_dot((pp * r).astype(BF16), vp) + _dot((pm * r).astype(BF16), vm)
             + _dot((pn * r).astype(BF16), vn))
        o_ref[0, :, cols] = o.astype(BF16)


def _win_attn(proj, bias, sink, tq=256):
    b, s, _ = proj.shape
    hd = HEAD_DIM
    nb = s // WINDOW
    r = tq // WINDOW
    prev = lambda bi, g, i: jnp.maximum(i * r - 1, 0)
    nxt = lambda bi, g, i: jnp.minimum((i + 1) * r, nb - 1)
    kcol, vcol = OFF_WK // hd, OFF_WV // hd
    return pl.pallas_call(
        functools.partial(_win_kernel, tq=tq),
        grid=(b, WIN_KV_HEADS, s // tq),
        in_specs=[pl.BlockSpec(memory_space=pltpu.SMEM),
                  pl.BlockSpec((1, tq, WIN_GROUP * hd), lambda bi, g, i: (bi, i, OFF_WQ // (WIN_GROUP * hd) + g)),
                  pl.BlockSpec((1, WINDOW, hd), lambda bi, g, i: (bi, prev(bi, g, i), kcol + g)),
                  pl.BlockSpec((1, tq, hd), lambda bi, g, i: (bi, i, kcol + g)),
                  pl.BlockSpec((1, WINDOW, hd), lambda bi, g, i: (bi, nxt(bi, g, i), kcol + g)),
                  pl.BlockSpec((1, WINDOW, hd), lambda bi, g, i: (bi, prev(bi, g, i), vcol + g)),
                  pl.BlockSpec((1, tq, hd), lambda bi, g, i: (bi, i, vcol + g)),
                  pl.BlockSpec((1, WINDOW, hd), lambda bi, g, i: (bi, nxt(bi, g, i), vcol + g)),
                  pl.BlockSpec((WIN_GROUP, tq, tq + 2 * WINDOW), lambda bi, g, i: (g, 0, 0))],
        out_specs=pl.BlockSpec((1, tq, WIN_GROUP * hd), lambda bi, g, i: (bi, i, g)),
        out_shape=jax.ShapeDtypeStruct((b, s, BRANCH_W), BF16),
        compiler_params=_params("parallel", "parallel", "arbitrary"),
        name="win_attn",
    )(sink, proj, proj, proj, proj, proj, proj, proj, bias)


def _mem_attn_kernel(q_ref, k_ref, v_ref, o_ref):
    s = _dot_nt(q_ref[0], k_ref[0]) * jnp.asarray(MEM_DIM ** -0.5, F32)
    p = jnp.exp(s - jnp.max(s, axis=-1, keepdims=True))
    p = p * (1.0 / jnp.sum(p, axis=-1, keepdims=True))
    o_ref[0] = _dot(p.astype(BF16), v_ref[0]).astype(BF16)


def _mem_attn(proj, memkv, tq=1024):
    b, s, _ = proj.shape
    md = MEM_DIM
    return pl.pallas_call(
        _mem_attn_kernel,
        grid=(b, MEM_HEADS, s // tq),
        in_specs=[pl.BlockSpec((1, tq, md), lambda bi, h, i: (bi, i, OFF_MQ // md + h)),
                  pl.BlockSpec((1, N_MEM, md), lambda bi, h, i: (bi, 0, h)),
                  pl.BlockSpec((1, N_MEM, md), lambda bi, h, i: (bi, 0, MEM_HEADS + h))],
        out_specs=pl.BlockSpec((1, tq, md), lambda bi, h, i: (bi, i, h)),
        out_shape=jax.ShapeDtypeStruct((b, s, BRANCH_W), BF16),
        compiler_params=_params("parallel", "parallel", "arbitrary"),
        name="mem_attn",
    )(proj, memkv, memkv)


def _mix_kernel(x_ref, h_ref, lg_ref, lb_ref, a_ref, b_ref, c_ref, wg0_ref, wg1_ref, wg2_ref,
                bg0_ref, bg1_ref, bg2_ref, wb_ref, wo_ref, g1_ref, b1_ref, o_ref, acc_scr):
    n = pl.program_id(1)
    nrow = x_ref.shape[0] // LN_ROWS

    @pl.when(n == 0)
    def _():
        acc_scr[...] = jnp.zeros_like(acc_scr)

    h = h_ref[...]
    mixed = None
    for br_ref, wg_ref, bg_ref, k in ((a_ref, wg0_ref, bg0_ref, 0), (b_ref, wg1_ref, bg1_ref, 1),
                                      (c_ref, wg2_ref, bg2_ref, 2)):
        gate = jax.nn.sigmoid(_dot(h, wg_ref[...]) + bg_ref[...])
        term = gate * _dot(br_ref[...], wb_ref[k])
        mixed = term if mixed is None else mixed + term
    acc_scr[...] += _dot(mixed.astype(BF16), wo_ref[...])

    @pl.when(n == pl.num_programs(1) - 1)
    def _():
        def body(r, c):
            rows = pl.ds(pl.multiple_of(r * LN_ROWS, LN_ROWS), LN_ROWS)
            hh = _ln_rows(x_ref[rows, :], lg_ref[...], lb_ref[...])
            o_ref[rows, :] = _ln_rows(ALPHA * hh + acc_scr[rows, :], g1_ref[...], b1_ref[...])
            return c
        lax.fori_loop(0, nrow, body, 0)


def _mix(x2, hb, lg, lb, a, b, c, wg, bg, wb, wo, g1, b1, tm=512, tn=256):
    m, d = x2.shape
    bw = a.shape[1]
    nn = d // tn
    row = lambda w: pl.BlockSpec((tm, w), lambda i, n: (i, 0))
    vec = pl.BlockSpec((1, d), lambda i, n: (0, 0))
    wgs = [pl.BlockSpec((d, tn), functools.partial(lambda i, n, k: (0, k * nn + n), k=k)) for k in range(3)]
    bgs = [pl.BlockSpec((1, tn), functools.partial(lambda i, n, k: (0, k * nn + n), k=k)) for k in range(3)]
    return pl.pallas_call(
        _mix_kernel,
        grid=(m // tm, nn),
        in_specs=[row(d), row(d), vec, vec, row(bw), row(bw), row(bw), *wgs, *bgs,
                  pl.BlockSpec((3, bw, tn), lambda i, n: (0, 0, n)),
                  pl.BlockSpec((tn, d), lambda i, n: (n, 0)), vec, vec],
        out_specs=pl.BlockSpec((tm, d), lambda i, n: (i, 0)),
        out_shape=jax.ShapeDtypeStruct((m, d), F32),
        scratch_shapes=[pltpu.VMEM((tm, d), F32)],
        compiler_params=_params("parallel", "arbitrary"),
        name="mix",
    )(x2, hb, lg, lb, a, b, c, wg, wg, wg, bg, bg, bg, wb, wo, g1, b1)


HALO = 16


def _gelu_tanh(x):
    return 0.5 * x * (1.0 + jnp.tanh(math.sqrt(2.0 / math.pi) * (x + 0.044715 * (x * x * x))))


def _ffn_kernel(h_ref, hp_ref, hn_ref, wv_ref, wg_ref, cwv_ref, cwg_ref, cbv_ref, cbg_ref, wd_ref,
                g2_ref, b2_ref, o_ref, hx_scr, acc_scr, *, tiles_per_seq):
    i = pl.program_id(0)
    f = pl.program_id(1)
    tm = h_ref.shape[0]
    nrow = tm // LN_ROWS

    @pl.when(f == 0)
    def _():
        seq_first = (i % tiles_per_seq) == 0
        seq_last = (i % tiles_per_seq) == tiles_per_seq - 1
        hx_scr[0:HALO, :] = jnp.where(seq_first, 0.0, hp_ref[...]).astype(BF16)
        hx_scr[HALO:HALO + tm, :] = h_ref[...].astype(BF16)
        hx_scr[HALO + tm:, :] = jnp.where(seq_last, 0.0, hn_ref[...]).astype(BF16)
        acc_scr[...] = jnp.zeros_like(acc_scr)

    hx = hx_scr[...]
    ext = tm + 2 * HALO

    def conv(u, cw_ref, cb_ref):
        prev = pltpu.roll(u, 1, 0)[HALO:HALO + tm]
        nxt = pltpu.roll(u, ext - 1, 0)[HALO:HALO + tm]
        return prev * cw_ref[0:1, :] + u[HALO:HALO + tm] * cw_ref[1:2, :] + nxt * cw_ref[2:3, :] + cb_ref[...]

    val = conv(_dot(hx, wv_ref[...]), cwv_ref, cbv_ref)
    gate = conv(_dot(hx, wg_ref[...]), cwg_ref, cbg_ref)
    acc_scr[...] += _dot((_gelu_tanh(gate) * val).astype(BF16), wd_ref[...])

    @pl.when(f == pl.num_programs(1) - 1)
    def _():
        def body(r, c):
            rows = pl.ds(pl.multiple_of(r * LN_ROWS, LN_ROWS), LN_ROWS)
            o_ref[rows, :] = _ln_rows(ALPHA * h_ref[rows, :] + acc_scr[rows, :], g2_ref[...], b2_ref[...])
            return c
        lax.fori_loop(0, nrow, body, 0)


def _ffn(h1, w_up, cw, cb, w_down, g2, b2, seq, tm=512, tf=512):
    m, d = h1.shape
    nf = D_FF_PAD // tf
    hb = tm // HALO
    nhb = m // HALO
    vec = pl.BlockSpec((1, d), lambda i, f: (0, 0))
    return pl.pallas_call(
        functools.partial(_ffn_kernel, tiles_per_seq=seq // tm),
        grid=(m // tm, nf),
        in_specs=[pl.BlockSpec((tm, d), lambda i, f: (i, 0)),
                  pl.BlockSpec((HALO, d), lambda i, f: (jnp.maximum(i * hb - 1, 0), 0)),
                  pl.BlockSpec((HALO, d), lambda i, f: (jnp.minimum((i + 1) * hb, nhb - 1), 0)),
                  pl.BlockSpec((d, tf), lambda i, f: (0, f)),
                  pl.BlockSpec((d, tf), lambda i, f: (0, nf + f)),
                  pl.BlockSpec((3, tf), lambda i, f: (0, f)),
                  pl.BlockSpec((3, tf), lambda i, f: (0, nf + f)),
                  pl.BlockSpec((1, tf), lambda i, f: (0, f)),
                  pl.BlockSpec((1, tf), lambda i, f: (0, nf + f)),
                  pl.BlockSpec((tf, d), lambda i, f: (f, 0)), vec, vec],
        out_specs=pl.BlockSpec((tm, d), lambda i, f: (i, 0)),
        out_shape=jax.ShapeDtypeStruct((m, d), F32),
        scratch_shapes=[pltpu.VMEM((tm + 2 * HALO, d), BF16), pltpu.VMEM((tm, d), F32)],
        compiler_params=_params("parallel", "arbitrary"),
        name="ffn",
    )(h1, h1, h1, w_up, w_up, cw, cw, cb, cb, w_down, g2, b2)


def _w_up_prep_kernel(w_ref, o_ref):
    o_ref[:, 0:D_FF] = w_ref[...].astype(BF16)
    o_ref[:, D_FF:] = jnp.zeros((o_ref.shape[0], D_FF_PAD - D_FF), BF16)


def _w_up_prep(w_up, tr=256):
    d = w_up.shape[0]
    return pl.pallas_call(
        _w_up_prep_kernel,
        grid=(2, d // tr),
        in_specs=[pl.BlockSpec((tr, D_FF), lambda half, r: (r, half))],
        out_specs=pl.BlockSpec((tr, D_FF_PAD), lambda half, r: (r, half)),
        out_shape=jax.ShapeDtypeStruct((d, 2 * D_FF_PAD), BF16),
        compiler_params=_params("parallel", "parallel"),
        name="w_up_prep",
    )(w_up)


def _pad_ff(t, axis):
    val, gate = jnp.split(t, 2, axis=axis)
    pad = [(0, 0)] * t.ndim
    pad[axis] = (0, D_FF_PAD - D_FF)
    return jnp.concatenate([jnp.pad(val, pad), jnp.pad(gate, pad)], axis=axis)


def kernel(x, mem, ln_in_g, ln_in_b, rel_table, w_in, w_mem_kv, diff_lq1, diff_lk1, diff_lq2, diff_lk2,
           diff_subln_g, win_sink, w_gate, b_gate, w_branch, w_o, ln1_g, ln1_b, w_up, conv_w, conv_b,
           w_down, ln2_g, ln2_b):
    assert w_in.shape[0] == DEPTH == 1
    bsz, seq, d = x.shape
    x2 = x.reshape(bsz * seq, d)
    row = lambda v: v.reshape(1, -1).astype(F32)
    l = 0
    proj, hb = _ln_proj(x2, row(ln_in_g), row(ln_in_b), w_in[l].astype(BF16))
    proj = proj.reshape(bsz, seq, IN_W)
    memkv = _mem_kv(mem.reshape(bsz * N_MEM, d), w_mem_kv[l].astype(BF16)).reshape(bsz, N_MEM, 2 * BRANCH_W)

    tq_d, tq_w = 256, 256
    a = _diff_attn(proj, _diff_bias_tiles(rel_table[:, :DIFF_HEADS] * LOG2E, tq_d), row(diff_lq1[l]), row(diff_lk1[l]),
                   row(diff_lq2[l]), row(diff_lk2[l]), row(diff_subln_g[l]), tq=tq_d)
    b = _win_attn(proj, _win_bias_tiles(rel_table[:, DIFF_HEADS:], tq_w), win_sink[l].astype(F32), tq=tq_w)
    c = _mem_attn(proj, memkv)

    m = bsz * seq
    h1 = _mix(x2, hb, row(ln_in_g), row(ln_in_b), a.reshape(m, -1), b.reshape(m, -1), c.reshape(m, -1),
              w_gate[l].astype(BF16), row(b_gate[l]), w_branch[l].astype(BF16), w_o[l].astype(BF16),
              row(ln1_g[l]), row(ln1_b[l]))

    w_up_p = _w_up_prep(w_up[l].astype(F32))
    cw_p = _pad_ff(conv_w[l].astype(F32), 1)
    cb_p = _pad_ff(row(conv_b[l]), 1)
    w_down_p = jnp.pad(w_down[l], ((0, D_FF_PAD - D_FF), (0, 0))).astype(BF16)
    out = _ffn(h1, w_up_p, cw_p, cb_p, w_down_p, row(ln2_g[l]), row(ln2_b[l]), seq)
    return out.reshape(bsz, seq, d)
```

```python
import functools
import math

import jax
import jax.numpy as jnp
from jax import lax
from jax.experimental import pallas as pl
from jax.experimental.pallas import tpu as pltpu

F32 = jnp.float32
BF16 = jnp.bfloat16

D_MODEL = 2048
SEQ = 2048
N_MEM = 256
HEAD_DIM = 128
BRANCH_W = 1024
DIFF_HEADS = 8
DIFF_QK = 64
WIN_HEADS = 8
WIN_KV_HEADS = 2
WIN_GROUP = WIN_HEADS // WIN_KV_HEADS
WINDOW = 128
MEM_HEADS = 4
MEM_DIM = 256
OFF_DQ, OFF_DK, OFF_DV, OFF_WQ, OFF_WK, OFF_WV, OFF_MQ = 0, 1024, 2048, 3072, 4096, 4352, 4608
IN_W = 5632
D_FF = 5504
D_FF_PAD = 5632
REL_BUCKETS = 32
REL_MAX_DIST = 128
DEPTH = 1
ALPHA = (2 * DEPTH) ** 0.25
LN_EPS = 1e-5
NEG = -1e30
LOG2E = math.log2(math.e)
LAMBDA_INIT = 0.8 - 0.6 * math.exp(-0.3 * 0)

VMEM_LIMIT = 56 * 1024 * 1024
LN_ROWS = 128


def _ln_rows(x, g, b):
    mu = jnp.mean(x, axis=-1, keepdims=True)
    xc = x - mu
    var = jnp.mean(xc * xc, axis=-1, keepdims=True)
    return xc * lax.rsqrt(var + LN_EPS) * g + b


def _dot(a, b):
    return jnp.dot(a, b, preferred_element_type=F32)


def _dot_nt(a, b):
    return lax.dot_general(a, b, (((1,), (1,)), ((), ())), preferred_element_type=F32)


def _params(*sem):
    return pltpu.CompilerParams(dimension_semantics=sem, vmem_limit_bytes=VMEM_LIMIT)


def _ln_proj_kernel(x_ref, g_ref, b_ref, w_ref, o_ref, h_ref):
    @pl.when(pl.program_id(1) == 0)
    def _():
        def body(r, c):
            rows = pl.ds(pl.multiple_of(r * LN_ROWS, LN_ROWS), LN_ROWS)
            h_ref[rows, :] = _ln_rows(x_ref[rows, :], g_ref[...], b_ref[...]).astype(BF16)
            return c
        lax.fori_loop(0, x_ref.shape[0] // LN_ROWS, body, 0)

    o_ref[...] = _dot(h_ref[...], w_ref[...]).astype(BF16)


def _ln_proj(x2, g, b, w, tm=1024, tn=512):
    m, d = x2.shape
    n = w.shape[1]
    return pl.pallas_call(
        _ln_proj_kernel,
        grid=(m // tm, n // tn),
        in_specs=[pl.BlockSpec((tm, d), lambda i, j: (i, 0)),
                  pl.BlockSpec((1, d), lambda i, j: (0, 0)),
                  pl.BlockSpec((1, d), lambda i, j: (0, 0)),
                  pl.BlockSpec((d, tn), lambda i, j: (0, j))],
        out_specs=[pl.BlockSpec((tm, tn), lambda i, j: (i, j)),
                   pl.BlockSpec((tm, d), lambda i, j: (i, 0))],
        out_shape=[jax.ShapeDtypeStruct((m, n), BF16), jax.ShapeDtypeStruct((m, d), BF16)],
        compiler_params=_params("parallel", "arbitrary"),
        name="ln_proj",
    )(x2, g, b, w)


def _mem_kv_kernel(m_ref, w_ref, o_ref):
    o_ref[...] = _dot(m_ref[...].astype(BF16), w_ref[...]).astype(BF16)


def _mem_kv(mem2, w, tn=512):
    m, d = mem2.shape
    n = w.shape[1]
    return pl.pallas_call(
        _mem_kv_kernel,
        grid=(n // tn,),
        in_specs=[pl.BlockSpec((m, d), lambda j: (0, 0)),
                  pl.BlockSpec((d, tn), lambda j: (0, j))],
        out_specs=pl.BlockSpec((m, tn), lambda j: (0, j)),
        out_shape=jax.ShapeDtypeStruct((m, n), BF16),
        compiler_params=_params("parallel"),
        name="mem_kv",
    )(mem2, w)


def _t5_bucket(rel):
    half = REL_BUCKETS // 2
    max_exact = half // 2
    ret = jnp.where(rel > 0, half, 0)
    n = jnp.abs(rel)
    nf = jnp.maximum(n, 1).astype(F32)
    large = max_exact + (jnp.log(nf / max_exact) / math.log(REL_MAX_DIST / max_exact)
                         * (half - max_exact)).astype(jnp.int32)
    large = jnp.minimum(large, half - 1)
    return ret + jnp.where(n < max_exact, n, large)


BIAS_ROWS = 64


def _bias_lookup_kernel(tab_ref, bucket_ref, o_ref):
    h = pl.program_id(0)

    def body(r, carry):
        rows = pl.ds(pl.multiple_of(r * BIAS_ROWS, BIAS_ROWS), BIAS_ROWS)
        bk = bucket_ref[rows, :]
        out = jnp.full(bk.shape, NEG, F32)
        for b in range(REL_BUCKETS):
            out = jnp.where(bk == b, tab_ref[b, h], out)
        o_ref[0, rows, :] = out
        return carry
    lax.fori_loop(0, bucket_ref.shape[0] // BIAS_ROWS, body, 0)


def _bias_lookup(table, bucket):
    nh = table.shape[1]
    r, c = bucket.shape
    return pl.pallas_call(
        _bias_lookup_kernel,
        grid=(nh,),
        in_specs=[pl.BlockSpec(memory_space=pltpu.SMEM),
                  pl.BlockSpec((r, c), lambda h: (0, 0))],
        out_specs=pl.BlockSpec((1, r, c), lambda h: (h, 0, 0)),
        out_shape=jax.ShapeDtypeStruct((nh, r, c), F32),
        compiler_params=_params("parallel"),
        name="bias_lookup",
    )(table.astype(F32), bucket)


def _diff_bias_tiles(table, tq):
    d = jnp.arange(5)[:, None, None] - 2
    q = jnp.arange(tq)[None, :, None]
    k = jnp.arange(tq)[None, None, :]
    bucket = _t5_bucket(d * tq + k - q).astype(jnp.int32).reshape(5 * tq, tq)
    return _bias_lookup(table, bucket).reshape(table.shape[1], 5, tq, tq)


def _win_bias_tiles(table, tq):
    q = jnp.arange(tq)[:, None]
    c = jnp.arange(tq + 2 * WINDOW)[None, :]
    rel = c - WINDOW - q
    bucket = jnp.where(jnp.abs(rel) <= WINDOW, _t5_bucket(rel), -1).astype(jnp.int32)
    return _bias_lookup(table, bucket)


def _diff_kernel(q_ref, k_ref, v_ref, band_ref, lq1_ref, lk1_ref, lq2_ref, lk2_ref, g_ref, o_ref,
                 s_scr, mx_scr, *, tq, rc, n_tiles, tiles_per_head):
    t = pl.program_id(0)
    hd = v_ref.shape[2]

    @pl.when(t == 0)
    def _():
        s_scr[...] = jnp.zeros_like(s_scr)
        mx_scr[...] = jnp.zeros_like(mx_scr)

    def tick(slot):
        other = 1 - slot
        i = jnp.minimum(t, n_tiles - 1) % tiles_per_head
        nk = k_ref.shape[1] // tq
        q = (q_ref[0].astype(F32) * (DIFF_QK ** -0.5 * LOG2E)).astype(BF16)
        lane = lax.broadcasted_iota(jnp.int32, q.shape, 1)
        zero = jnp.zeros_like(q)
        qs = jnp.concatenate([jnp.where(lane < DIFF_QK, q, zero), jnp.where(lane >= DIFF_QK, q, zero)], axis=0)
        lanes = mx_scr.shape[-1]
        for j in range(nk):
            cols = slice(j * tq, (j + 1) * tq)
            bias = band_ref[0, jnp.clip(j - i, -2, 2) + 2]
            s = _dot_nt(qs, k_ref[0, cols, :])
            for half in (slice(0, tq), slice(tq, 2 * tq)):
                sb = s[half] + bias
                s_scr[slot, half, cols] = sb
                fold = functools.reduce(jnp.maximum, [sb[:, c:c + lanes] for c in range(0, tq, lanes)])
                mx_scr[slot, half, :] = fold if j == 0 else jnp.maximum(mx_scr[slot, half, :], fold)

        nchunk = 2 * tq // rc
        chunk = lambda r: slice(r * rc, (r + 1) * rc)
        p = jnp.concatenate(
            [jnp.exp2(s_scr[other, chunk(r), :] - jnp.max(mx_scr[other, chunk(r), :], axis=-1, keepdims=True))
             .astype(BF16) for r in range(nchunk)], axis=0)
        v_ones = jnp.concatenate([v_ref[0], jnp.ones(v_ref.shape[1:], BF16)], axis=1)
        pv = _dot(p, v_ones)
        lam = (jnp.exp(jnp.sum(lq1_ref[...] * lk1_ref[...], axis=-1, keepdims=True))
               - jnp.exp(jnp.sum(lq2_ref[...] * lk2_ref[...], axis=-1, keepdims=True)) + LAMBDA_INIT)
        o = pv[0:tq, 0:hd] * (1.0 / pv[0:tq, hd:]) - pv[tq:, 0:hd] * (lam / pv[tq:, hd:])
        ms = jnp.mean(o * o, axis=-1, keepdims=True)
        o_ref[0] = (o * lax.rsqrt(ms + LN_EPS) * g_ref[...] * (1.0 - LAMBDA_INIT)).astype(BF16)

    pl.when(t % 2 == 0)(functools.partial(tick, 0))
    pl.when(t % 2 == 1)(functools.partial(tick, 1))


def _diff_attn(proj, band, lq1, lk1, lq2, lk2, g, tq=256, rc=32):
    b, s, _ = proj.shape
    cb = HEAD_DIM
    ni = s // tq
    n_tiles = b * DIFF_HEADS * ni
    depth = 1

    def tile(t, lag):
        tc = jnp.clip(t - lag, 0, n_tiles - 1)
        return tc // (DIFF_HEADS * ni), (tc // ni) % DIFF_HEADS, tc % ni

    def q_map(t):
        bi, h, i = tile(t, 0)
        return bi, i, OFF_DQ // cb + h

    def k_map(t):
        bi, h, _ = tile(t, 0)
        return bi, 0, OFF_DK // cb + h

    def v_map(t):
        bi, h, _ = tile(t, depth)
        return bi, 0, OFF_DV // cb + h

    def o_map(t):
        bi, h, i = tile(t, depth)
        return bi, i, h

    vec = lambda n: pl.BlockSpec((1, n), lambda t: (0, 0))
    return pl.pallas_call(
        functools.partial(_diff_kernel, tq=tq, rc=rc, n_tiles=n_tiles, tiles_per_head=ni),
        grid=(n_tiles + depth,),
        in_specs=[pl.BlockSpec((1, tq, cb), q_map),
                  pl.BlockSpec((1, s, cb), k_map),
                  pl.BlockSpec((1, s, cb), v_map),
                  pl.BlockSpec((1, 5, tq, tq), lambda t: (tile(t, 0)[1], 0, 0, 0)),
                  vec(DIFF_QK), vec(DIFF_QK), vec(DIFF_QK), vec(DIFF_QK), vec(HEAD_DIM)],
        out_specs=pl.BlockSpec((1, tq, cb), o_map),
        out_shape=jax.ShapeDtypeStruct((b, s, BRANCH_W), BF16),
        scratch_shapes=[pltpu.VMEM((2, 2 * tq, s), F32), pltpu.VMEM((2, 2 * tq, cb), F32)],
        compiler_params=_params("arbitrary"),
        name="diff_attn",
    )(proj, proj, proj, band, lq1, lk1, lq2, lk2, g)


def _win_kernel(sink_ref, q_ref, kp_ref, km_ref, kn_ref, vp_ref, vm_ref, vn_ref, bias_ref, o_ref, *, tq):
    g = pl.program_id(1)
    i = pl.program_id(2)
    first = i == 0
    last = i == pl.num_programs(2) - 1
    kp, km, kn = kp_ref[0], km_ref[0], kn_ref[0]
    vp, vm, vn = vp_ref[0], vm_ref[0], vn_ref[0]
    scale = jnp.asarray(HEAD_DIM ** -0.5, F32)
    w = WINDOW
    for j in range(WIN_GROUP):
        cols = slice(j * HEAD_DIM, (j + 1) * HEAD_DIM)
        qh = q_ref[0, :, cols]
        sp = _dot_nt(qh, kp) * scale + bias_ref[j, :, 0:w]
        sm = _dot_nt(qh, km) * scale + bias_ref[j, :, w:w + tq]
        sn = _dot_nt(qh, kn) * scale + bias_ref[j, :, w + tq:]
        sp = jnp.where(first, NEG, sp)
        sn = jnp.where(last, NEG, sn)
        sink = sink_ref[g * WIN_GROUP + j]
        fold = lambda op, parts: functools.reduce(
            op, [x[:, c:c + w] for x in parts for c in range(0, x.shape[1], w)])
        m = jnp.maximum(jnp.max(fold(jnp.maximum, (sp, sm, sn)), axis=-1, keepdims=True), sink)
        pp, pm, pn = jnp.exp(sp - m), jnp.exp(sm - m), jnp.exp(sn - m)
        den = jnp.sum(fold(jnp.add, (pp, pm, pn)), axis=-1, keepdims=True) + jnp.exp(sink - m)
        r = 1.0 / den
        o = (_dot((pp * r).astype(BF16), vp) + _dot((pm * r).astype(BF16), vm)
             + _dot((pn * r).astype(BF16), vn))
        o_ref[0, :, cols] = o.astype(BF16)


def _win_attn(proj, bias, sink, tq=256):
    b, s, _ = proj.shape
    hd = HEAD_DIM
    nb = s // WINDOW
    r = tq // WINDOW
    prev = lambda bi, g, i: jnp.maximum(i * r - 1, 0)
    nxt = lambda bi, g, i: jnp.minimum((i + 1) * r, nb - 1)
    kcol, vcol = OFF_WK // hd, OFF_WV // hd
    return pl.pallas_call(
        functools.partial(_win_kernel, tq=tq),
        grid=(b, WIN_KV_HEADS, s // tq),
        in_specs=[pl.BlockSpec(memory_space=pltpu.SMEM),
                  pl.BlockSpec((1, tq, WIN_GROUP * hd), lambda bi, g, i: (bi, i, OFF_WQ // (WIN_GROUP * hd) + g)),
                  pl.BlockSpec((1, WINDOW, hd), lambda bi, g, i: (bi, prev(bi, g, i), kcol + g)),
                  pl.BlockSpec((1, tq, hd), lambda bi, g, i: (bi, i, kcol + g)),
                  pl.BlockSpec((1, WINDOW, hd), lambda bi, g, i: (bi, nxt(bi, g, i), kcol + g)),
                  pl.BlockSpec((1, WINDOW, hd), lambda bi, g, i: (bi, prev(bi, g, i), vcol + g)),
                  pl.BlockSpec((1, tq, hd), lambda bi, g, i: (bi, i, vcol + g)),
                  pl.BlockSpec((1, WINDOW, hd), lambda bi, g, i: (bi, nxt(bi, g, i), vcol + g)),
                  pl.BlockSpec((WIN_GROUP, tq, tq + 2 * WINDOW), lambda bi, g, i: (g, 0, 0))],
        out_specs=pl.BlockSpec((1, tq, WIN_GROUP * hd), lambda bi, g, i: (bi, i, g)),
        out_shape=jax.ShapeDtypeStruct((b, s, BRANCH_W), BF16),
        compiler_params=_params("parallel", "parallel", "arbitrary"),
        name="win_attn",
    )(sink, proj, proj, proj, proj, proj, proj, proj, bias)


def _mem_attn_kernel(q_ref, k_ref, v_ref, o_ref):
    s = _dot_nt(q_ref[0], k_ref[0]) * jnp.asarray(MEM_DIM ** -0.5, F32)
    p = jnp.exp(s - jnp.max(s, axis=-1, keepdims=True))
    p = p * (1.0 / jnp.sum(p, axis=-1, keepdims=True))
    o_ref[0] = _dot(p.astype(BF16), v_ref[0]).astype(BF16)


def _mem_attn(proj, memkv, tq=1024):
    b, s, _ = proj.shape
    md = MEM_DIM
    return pl.pallas_call(
        _mem_attn_kernel,
        grid=(b, MEM_HEADS, s // tq),
        in_specs=[pl.BlockSpec((1, tq, md), lambda bi, h, i: (bi, i, OFF_MQ // md + h)),
                  pl.BlockSpec((1, N_MEM, md), lambda bi, h, i: (bi, 0, h)),
                  pl.BlockSpec((1, N_MEM, md), lambda bi, h, i: (bi, 0, MEM_HEADS + h))],
        out_specs=pl.BlockSpec((1, tq, md), lambda bi, h, i: (bi, i, h)),
        out_shape=jax.ShapeDtypeStruct((b, s, BRANCH_W), BF16),
        compiler_params=_params("parallel", "parallel", "arbitrary"),
        name="mem_attn",
    )(proj, memkv, memkv)


def _mix_kernel(x_ref, h_ref, lg_ref, lb_ref, a_ref, b_ref, c_ref, wg0_ref, wg1_ref, wg2_ref,
                bg0_ref, bg1_ref, bg2_ref, wb_ref, wo_ref, g1_ref, b1_ref, o_ref, acc_scr):
    n = pl.program_id(1)
    nrow = x_ref.shape[0] // LN_ROWS

    @pl.when(n == 0)
    def _():
        acc_scr[...] = jnp.zeros_like(acc_scr)

    h = h_ref[...]
    mixed = None
    for br_ref, wg_ref, bg_ref, k in ((a_ref, wg0_ref, bg0_ref, 0), (b_ref, wg1_ref, bg1_ref, 1),
                                      (c_ref, wg2_ref, bg2_ref, 2)):
        gate = jax.nn.sigmoid(_dot(h, wg_ref[...]) + bg_ref[...])
        term = gate * _dot(br_ref[...], wb_ref[k])
        mixed = term if mixed is None else mixed + term
    acc_scr[...] += _dot(mixed.astype(BF16), wo_ref[...])

    @pl.when(n == pl.num_programs(1) - 1)
    def _():
        def body(r, c):
            rows = pl.ds(pl.multiple_of(r * LN_ROWS, LN_ROWS), LN_ROWS)
            hh = _ln_rows(x_ref[rows, :], lg_ref[...], lb_ref[...])
            o_ref[rows, :] = _ln_rows(ALPHA * hh + acc_scr[rows, :], g1_ref[...], b1_ref[...])
            return c
        lax.fori_loop(0, nrow, body, 0)


def _mix(x2, hb, lg, lb, a, b, c, wg, bg, wb, wo, g1, b1, tm=512, tn=256):
    m, d = x2.shape
    bw = a.shape[1]
    nn = d // tn
    row = lambda w: pl.BlockSpec((tm, w), lambda i, n: (i, 0))
    vec = pl.BlockSpec((1, d), lambda i, n: (0, 0))
    wgs = [pl.BlockSpec((d, tn), functools.partial(lambda i, n, k: (0, k * nn + n), k=k)) for k in range(3)]
    bgs = [pl.BlockSpec((1, tn), functools.partial(lambda i, n, k: (0, k * nn + n), k=k)) for k in range(3)]
    return pl.pallas_call(
        _mix_kernel,
        grid=(m // tm, nn),
        in_specs=[row(d), row(d), vec, vec, row(bw), row(bw), row(bw), *wgs, *bgs,
                  pl.BlockSpec((3, bw, tn), lambda i, n: (0, 0, n)),
                  pl.BlockSpec((tn, d), lambda i, n: (n, 0)), vec, vec],
        out_specs=pl.BlockSpec((tm, d), lambda i, n: (i, 0)),
        out_shape=jax.ShapeDtypeStruct((m, d), F32),
        scratch_shapes=[pltpu.VMEM((tm, d), F32)],
        compiler_params=_params("parallel", "arbitrary"),
        name="mix",
    )(x2, hb, lg, lb, a, b, c, wg, wg, wg, bg, bg, bg, wb, wo, g1, b1)


HALO = 16


def _gelu_tanh(x):
    return 0.5 * x * (1.0 + jnp.tanh(math.sqrt(2.0 / math.pi) * (x + 0.044715 * (x * x * x))))


def _ffn_kernel(h_ref, hp_ref, hn_ref, wv_ref, wg_ref, cwv_ref, cwg_ref, cbv_ref, cbg_ref, wd_ref,
                g2_ref, b2_ref, o_ref, hx_scr, acc_scr, *, tiles_per_seq):
    i = pl.program_id(0)
    f = pl.program_id(1)
    tm = h_ref.shape[0]
    nrow = tm // LN_ROWS

    @pl.when(f == 0)
    def _():
        seq_first = (i % tiles_per_seq) == 0
        seq_last = (i % tiles_per_seq) == tiles_per_seq - 1
        hx_scr[0:HALO, :] = jnp.where(seq_first, 0.0, hp_ref[...]).astype(BF16)
        hx_scr[HALO:HALO + tm, :] = h_ref[...].astype(BF16)
        hx_scr[HALO + tm:, :] = jnp.where(seq_last, 0.0, hn_ref[...]).astype(BF16)
        acc_scr[...] = jnp.zeros_like(acc_scr)

    hx = hx_scr[...]
    ext = tm + 2 * HALO

    def conv(u, cw_ref, cb_ref):
        prev = pltpu.roll(u, 1, 0)[HALO:HALO + tm]
        nxt = pltpu.roll(u, ext - 1, 0)[HALO:HALO + tm]
        return prev * cw_ref[0:1, :] + u[HALO:HALO + tm] * cw_ref[1:2, :] + nxt * cw_ref[2:3, :] + cb_ref[...]

    val = conv(_dot(hx, wv_ref[...]), cwv_ref, cbv_ref)
    gate = conv(_dot(hx, wg_ref[...]), cwg_ref, cbg_ref)
    acc_scr[...] += _dot((_gelu_tanh(gate) * val).astype(BF16), wd_ref[...])

    @pl.when(f == pl.num_programs(1) - 1)
    def _():
        def body(r, c):
            rows = pl.ds(pl.multiple_of(r * LN_ROWS, LN_ROWS), LN_ROWS)
            o_ref[rows, :] = _ln_rows(ALPHA * h_ref[rows, :] + acc_scr[rows, :], g2_ref[...], b2_ref[...])
            return c
        lax.fori_loop(0, nrow, body, 0)


def _ffn(h1, w_up, cw, cb, w_down, g2, b2, seq, tm=512, tf=512):
    m, d = h1.shape
    nf = D_FF_PAD // tf
    hb = tm // HALO
    nhb = m // HALO
    vec = pl.BlockSpec((1, d), lambda i, f: (0, 0))
    return pl.pallas_call(
        functools.partial(_ffn_kernel, tiles_per_seq=seq // tm),
        grid=(m // tm, nf),
        in_specs=[pl.BlockSpec((tm, d), lambda i, f: (i, 0)),
                  pl.BlockSpec((HALO, d), lambda i, f: (jnp.maximum(i * hb - 1, 0), 0)),
                  pl.BlockSpec((HALO, d), lambda i, f: (jnp.minimum((i + 1) * hb, nhb - 1), 0)),
                  pl.BlockSpec((d, tf), lambda i, f: (0, f)),
                  pl.BlockSpec((d, tf), lambda i, f: (0, nf + f)),
                  pl.BlockSpec((3, tf), lambda i, f: (0, f)),
                  pl.BlockSpec((3, tf), lambda i, f: (0, nf + f)),
                  pl.BlockSpec((1, tf), lambda i, f: (0, f)),
                  pl.BlockSpec((1, tf), lambda i, f: (0, nf + f)),
                  pl.BlockSpec((tf, d), lambda i, f: (f, 0)), vec, vec],
        out_specs=pl.BlockSpec((tm, d), lambda i, f: (i, 0)),
        out_shape=jax.ShapeDtypeStruct((m, d), F32),
        scratch_shapes=[pltpu.VMEM((tm + 2 * HALO, d), BF16), pltpu.VMEM((tm, d), F32)],
        compiler_params=_params("parallel", "arbitrary"),
        name="ffn",
    )(h1, h1, h1, w_up, w_up, cw, cw, cb, cb, w_down, g2, b2)


def _w_up_prep_kernel(w_ref, o_ref):
    o_ref[:, 0:D_FF] = w_ref[...].astype(BF16)
    o_ref[:, D_FF:] = jnp.zeros((o_ref.shape[0], D_FF_PAD - D_FF), BF16)


def _w_up_prep(w_up, tr=256):
    d = w_up.shape[0]
    return pl.pallas_call(
        _w_up_prep_kernel,
        grid=(2, d // tr),
        in_specs=[pl.BlockSpec((tr, D_FF), lambda half, r: (r, half))],
        out_specs=pl.BlockSpec((tr, D_FF_PAD), lambda half, r: (r, half)),
        out_shape=jax.ShapeDtypeStruct((d, 2 * D_FF_PAD), BF16),
        compiler_params=_params("parallel", "parallel"),
        name="w_up_prep",
    )(w_up)


def _pad_ff(t, axis):
    val, gate = jnp.split(t, 2, axis=axis)
    pad = [(0, 0)] * t.ndim
    pad[axis] = (0, D_FF_PAD - D_FF)
    return jnp.concatenate([jnp.pad(val, pad), jnp.pad(gate, pad)], axis=axis)


def kernel(x, mem, ln_in_g, ln_in_b, rel_table, w_in, w_mem_kv, diff_lq1, diff_lk1, diff_lq2, diff_lk2,
           diff_subln_g, win_sink, w_gate, b_gate, w_branch, w_o, ln1_g, ln1_b, w_up, conv_w, conv_b,
           w_down, ln2_g, ln2_b):
    assert w_in.shape[0] == DEPTH == 1
    bsz, seq, d = x.shape
    x2 = x.reshape(bsz * seq, d)
    row = lambda v: v.reshape(1, -1).astype(F32)
    l = 0
    proj, hb = _ln_proj(x2, row(ln_in_g), row(ln_in_b), w_in[l].astype(BF16))
    proj = proj.reshape(bsz, seq, IN_W)
    memkv = _mem_kv(mem.reshape(bsz * N_MEM, d), w_mem_kv[l].astype(BF16)).reshape(bsz, N_MEM, 2 * BRANCH_W)

    tq_d, tq_w = 256, 256
    a = _diff_attn(proj, _diff_bias_tiles(rel_table[:, :DIFF_HEADS] * LOG2E, tq_d), row(diff_lq1[l]), row(diff_lk1[l]),
                   row(diff_lq2[l]), row(diff_lk2[l]), row(diff_subln_g[l]), tq=tq_d)
    b = _win_attn(proj, _win_bias_tiles(rel_table[:, DIFF_HEADS:], tq_w), win_sink[l].astype(F32), tq=tq_w)
    c = _mem_attn(proj, memkv)

    m = bsz * seq
    h1 = _mix(x2, hb, row(ln_in_g), row(ln_in_b), a.reshape(m, -1), b.reshape(m, -1), c.reshape(m, -1),
              w_gate[l].astype(BF16), row(b_gate[l]), w_branch[l].astype(BF16), w_o[l].astype(BF16),
              row(ln1_g[l]), row(ln1_b[l]))

    w_up_p = _w_up_prep(w_up[l].astype(F32))
    cw_p = _pad_ff(conv_w[l].astype(F32), 1)
    cb_p = _pad_ff(row(conv_b[l]), 1)
    w_down_p = jnp.pad(w_down[l], ((0, D_FF_PAD - D_FF), (0, 0))).astype(BF16)
    out = _ffn(h1, w_up_p, cw_p, cb_p, w_down_p, row(ln2_g[l]), row(ln2_b[l]), seq)
    return out.reshape(bsz, seq, d)
```

```python
import functools
import math

import jax
import jax.numpy as jnp
from jax import lax
from jax.experimental import pallas as pl
from jax.experimental.pallas import tpu as pltpu

F32 = jnp.float32
BF16 = jnp.bfloat16

D_MODEL = 2048
SEQ = 2048
N_MEM = 256
HEAD_DIM = 128
BRANCH_W = 1024
DIFF_HEADS = 8
DIFF_QK = 64
WIN_HEADS = 8
WIN_KV_HEADS = 2
WIN_GROUP = WIN_HEADS // WIN_KV_HEADS
WINDOW = 128
MEM_HEADS = 4
MEM_DIM = 256
OFF_DQ, OFF_DK, OFF_DV, OFF_WQ, OFF_WK, OFF_WV, OFF_MQ = 0, 1024, 2048, 3072, 4096, 4352, 4608
IN_W = 5632
D_FF = 5504
D_FF_PAD = 5632
REL_BUCKETS = 32
REL_MAX_DIST = 128
DEPTH = 1
ALPHA = (2 * DEPTH) ** 0.25
LN_EPS = 1e-5
NEG = -1e30
LOG2E = math.log2(math.e)
LAMBDA_INIT = 0.8 - 0.6 * math.exp(-0.3 * 0)

VMEM_LIMIT = 56 * 1024 * 1024
LN_ROWS = 128


def _ln_rows(x, g, b):
    mu = jnp.mean(x, axis=-1, keepdims=True)
    xc = x - mu
    var = jnp.mean(xc * xc, axis=-1, keepdims=True)
    return xc * lax.rsqrt(var + LN_EPS) * g + b


def _dot(a, b):
    return jnp.dot(a, b, preferred_element_type=F32)


def _dot_nt(a, b):
    return lax.dot_general(a, b, (((1,), (1,)), ((), ())), preferred_element_type=F32)


def _params(*sem):
    return pltpu.CompilerParams(dimension_semantics=sem, vmem_limit_bytes=VMEM_LIMIT)


def _ln_proj_kernel(x_ref, g_ref, b_ref, w_ref, o_ref, h_ref):
    @pl.when(pl.program_id(1) == 0)
    def _():
        def body(r, c):
            rows = pl.ds(pl.multiple_of(r * LN_ROWS, LN_ROWS), LN_ROWS)
            h_ref[rows, :] = _ln_rows(x_ref[rows, :], g_ref[...], b_ref[...]).astype(BF16)
            return c
        lax.fori_loop(0, x_ref.shape[0] // LN_ROWS, body, 0)

    o_ref[...] = _dot(h_ref[...], w_ref[...]).astype(BF16)


def _ln_proj(x2, g, b, w, tm=1024, tn=512):
    m, d = x2.shape
    n = w.shape[1]
    return pl.pallas_call(
        _ln_proj_kernel,
        grid=(m // tm, n // tn),
        in_specs=[pl.BlockSpec((tm, d), lambda i, j: (i, 0)),
                  pl.BlockSpec((1, d), lambda i, j: (0, 0)),
                  pl.BlockSpec((1, d), lambda i, j: (0, 0)),
                  pl.BlockSpec((d, tn), lambda i, j: (0, j))],
        out_specs=[pl.BlockSpec((tm, tn), lambda i, j: (i, j)),
                   pl.BlockSpec((tm, d), lambda i, j: (i, 0))],
        out_shape=[jax.ShapeDtypeStruct((m, n), BF16), jax.ShapeDtypeStruct((m, d), BF16)],
        compiler_params=_params("parallel", "arbitrary"),
        name="ln_proj",
    )(x2, g, b, w)


def _mem_kv_kernel(m_ref, w_ref, o_ref):
    o_ref[...] = _dot(m_ref[...].astype(BF16), w_ref[...]).astype(BF16)


def _mem_kv(mem2, w, tn=512):
    m, d = mem2.shape
    n = w.shape[1]
    return pl.pallas_call(
        _mem_kv_kernel,
        grid=(n // tn,),
        in_specs=[pl.BlockSpec((m, d), lambda j: (0, 0)),
                  pl.BlockSpec((d, tn), lambda j: (0, j))],
        out_specs=pl.BlockSpec((m, tn), lambda j: (0, j)),
        out_shape=jax.ShapeDtypeStruct((m, n), BF16),
        compiler_params=_params("parallel"),
        name="mem_kv",
    )(mem2, w)


def _t5_bucket(rel):
    half = REL_BUCKETS // 2
    max_exact = half // 2
    ret = jnp.where(rel > 0, half, 0)
    n = jnp.abs(rel)
    nf = jnp.maximum(n, 1).astype(F32)
    large = max_exact + (jnp.log(nf / max_exact) / math.log(REL_MAX_DIST / max_exact)
                         * (half - max_exact)).astype(jnp.int32)
    large = jnp.minimum(large, half - 1)
    return ret + jnp.where(n < max_exact, n, large)


BIAS_ROWS = 64


def _bias_lookup_kernel(tab_ref, bucket_ref, o_ref):
    h = pl.program_id(0)

    def body(r, carry):
        rows = pl.ds(pl.multiple_of(r * BIAS_ROWS, BIAS_ROWS), BIAS_ROWS)
        bk = bucket_ref[rows, :]
        out = jnp.full(bk.shape, NEG, F32)
        for b in range(REL_BUCKETS):
            out = jnp.where(bk == b, tab_ref[b, h], out)
        o_ref[0, rows, :] = out
        return carry
    lax.fori_loop(0, bucket_ref.shape[0] // BIAS_ROWS, body, 0)


def _bias_lookup(table, bucket):
    nh = table.shape[1]
    r, c = bucket.shape
    return pl.pallas_call(
        _bias_lookup_kernel,
        grid=(nh,),
        in_specs=[pl.BlockSpec(memory_space=pltpu.SMEM),
                  pl.BlockSpec((r, c), lambda h: (0, 0))],
        out_specs=pl.BlockSpec((1, r, c), lambda h: (h, 0, 0)),
        out_shape=jax.ShapeDtypeStruct((nh, r, c), F32),
        compiler_params=_params("parallel"),
        name="bias_lookup",
    )(table.astype(F32), bucket)


def _diff_bias_tiles(table, tq):
    d = jnp.arange(5)[:, None, None] - 2
    q = jnp.arange(tq)[None, :, None]
    k = jnp.arange(tq)[None, None, :]
    bucket = _t5_bucket(d * tq + k - q).astype(jnp.int32).reshape(5 * tq, tq)
    return _bias_lookup(table, bucket).reshape(table.shape[1], 5, tq, tq)


def _win_bias_tiles(table, tq):
    q = jnp.arange(tq)[:, None]
    c = jnp.arange(tq + 2 * WINDOW)[None, :]
    rel = c - WINDOW - q
    bucket = jnp.where(jnp.abs(rel) <= WINDOW, _t5_bucket(rel), -1).astype(jnp.int32)
    return _bias_lookup(table, bucket)


def _diff_kernel(q_ref, k_ref, v_ref, band_ref, lq1_ref, lk1_ref, lq2_ref, lk2_ref, g_ref, o_ref,
                 s_scr, mx_scr, *, tq, rc, n_tiles, tiles_per_head):
    t = pl.program_id(0)
    hd = v_ref.shape[2]

    @pl.when(t == 0)
    def _():
        s_scr[...] = jnp.zeros_like(s_scr)
        mx_scr[...] = jnp.zeros_like(mx_scr)

    def tick(slot):
        other = 1 - slot
        i = jnp.minimum(t, n_tiles - 1) % tiles_per_head
        nk = k_ref.shape[1] // tq
        q = (q_ref[0].astype(F32) * (DIFF_QK ** -0.5 * LOG2E)).astype(BF16)
        lane = lax.broadcasted_iota(jnp.int32, q.shape, 1)
        zero = jnp.zeros_like(q)
        qs = jnp.concatenate([jnp.where(lane < DIFF_QK, q, zero), jnp.where(lane >= DIFF_QK, q, zero)], axis=0)
        lanes = mx_scr.shape[-1]
        for j in range(nk):
            cols = slice(j * tq, (j + 1) * tq)
            bias = band_ref[0, jnp.clip(j - i, -2, 2) + 2]
            s = _dot_nt(qs, k_ref[0, cols, :])
            for half in (slice(0, tq), slice(tq, 2 * tq)):
                sb = s[half] + bias
                s_scr[slot, half, cols] = sb
                fold = functools.reduce(jnp.maximum, [sb[:, c:c + lanes] for c in range(0, tq, lanes)])
                mx_scr[slot, half, :] = fold if j == 0 else jnp.maximum(mx_scr[slot, half, :], fold)

        nchunk = 2 * tq // rc
        chunk = lambda r: slice(r * rc, (r + 1) * rc)
        p = jnp.concatenate(
            [jnp.exp2(s_scr[other, chunk(r), :] - jnp.max(mx_scr[other, chunk(r), :], axis=-1, keepdims=True))
             .astype(BF16) for r in range(nchunk)], axis=0)
        v_ones = jnp.concatenate([v_ref[0], jnp.ones(v_ref.shape[1:], BF16)], axis=1)
        pv = _dot(p, v_ones)
        lam = (jnp.exp(jnp.sum(lq1_ref[...] * lk1_ref[...], axis=-1, keepdims=True))
               - jnp.exp(jnp.sum(lq2_ref[...] * lk2_ref[...], axis=-1, keepdims=True)) + LAMBDA_INIT)
        o = pv[0:tq, 0:hd] * (1.0 / pv[0:tq, hd:]) - pv[tq:, 0:hd] * (lam / pv[tq:, hd:])
        ms = jnp.mean(o * o, axis=-1, keepdims=True)
        o_ref[0] = (o * lax.rsqrt(ms + LN_EPS) * g_ref[...] * (1.0 - LAMBDA_INIT)).astype(BF16)

    pl.when(t % 2 == 0)(functools.partial(tick, 0))
    pl.when(t % 2 == 1)(functools.partial(tick, 1))


def _diff_attn(proj, band, lq1, lk1, lq2, lk2, g, tq=256, rc=32):
    b, s, _ = proj.shape
    cb = HEAD_DIM
    ni = s // tq
    n_tiles = b * DIFF_HEADS * ni
    depth = 1

    def tile(t, lag):
        tc = jnp.clip(t - lag, 0, n_tiles - 1)
        return tc // (DIFF_HEADS * ni), (tc // ni) % DIFF_HEADS, tc % ni

    def q_map(t):
        bi, h, i = tile(t, 0)
        return bi, i, OFF_DQ // cb + h

    def k_map(t):
        bi, h, _ = tile(t, 0)
        return bi, 0, OFF_DK // cb + h

    def v_map(t):
        bi, h, _ = tile(t, depth)
        return bi, 0, OFF_DV // cb + h

    def o_map(t):
        bi, h, i = tile(t, depth)
        return bi, i, h

    vec = lambda n: pl.BlockSpec((1, n), lambda t: (0, 0))
    return pl.pallas_call(
        functools.partial(_diff_kernel, tq=tq, rc=rc, n_tiles=n_tiles, tiles_per_head=ni),
        grid=(n_tiles + depth,),
        in_specs=[pl.BlockSpec((1, tq, cb), q_map),
                  pl.BlockSpec((1, s, cb), k_map),
                  pl.BlockSpec((1, s, cb), v_map),
                  pl.BlockSpec((1, 5, tq, tq), lambda t: (tile(t, 0)[1], 0, 0, 0)),
                  vec(DIFF_QK), vec(DIFF_QK), vec(DIFF_QK), vec(DIFF_QK), vec(HEAD_DIM)],
        out_specs=pl.BlockSpec((1, tq, cb), o_map),
        out_shape=jax.ShapeDtypeStruct((b, s, BRANCH_W), BF16),
        scratch_shapes=[pltpu.VMEM((2, 2 * tq, s), F32), pltpu.VMEM((2, 2 * tq, cb), F32)],
        compiler_params=_params("arbitrary"),
        name="diff_attn",
    )(proj, proj, proj, band, lq1, lk1, lq2, lk2, g)


def _win_kernel(sink_ref, q_ref, kp_ref, km_ref, kn_ref, vp_ref, vm_ref, vn_ref, bias_ref, o_ref, *, tq):
    g = pl.program_id(1)
    i = pl.program_id(2)
    first = i == 0
    last = i == pl.num_programs(2) - 1
    w = WINDOW
    hd = HEAD_DIM
    nkeys = tq + 2 * w
    keys = jnp.concatenate([kp_ref[0], km_ref[0], kn_ref[0]], axis=0)
    vals = jnp.concatenate([vp_ref[0], vm_ref[0], vn_ref[0]], axis=0)
    v_ones = jnp.concatenate([vals, jnp.ones((nkeys, hd), BF16)], axis=1)
    col = lax.broadcasted_iota(jnp.int32, (1, nkeys), 1)
    outside = jnp.logical_or(jnp.logical_and(first, col < w),
                             jnp.logical_and(last, col >= tq + w))
    scale = jnp.asarray(hd ** -0.5 * LOG2E, F32)
    for j in range(WIN_GROUP):
        cols = slice(j * hd, (j + 1) * hd)
        s = _dot_nt(q_ref[0, :, cols], keys) * scale + bias_ref[j]
        s = jnp.where(outside, NEG, s)
        sink = sink_ref[g * WIN_GROUP + j]
        fold = functools.reduce(jnp.maximum, [s[:, c:c + w] for c in range(0, nkeys, w)])
        m = jnp.maximum(jnp.max(fold, axis=-1, keepdims=True), sink)
        pv = _dot(jnp.exp2(s - m).astype(BF16), v_ones)
        o_ref[0, :, cols] = (pv[:, 0:hd] / (pv[:, hd:] + jnp.exp2(sink - m))).astype(BF16)


def _win_attn(proj, bias, sink, tq=256):
    b, s, _ = proj.shape
    hd = HEAD_DIM
    nb = s // WINDOW
    r = tq // WINDOW
    prev = lambda bi, g, i: jnp.maximum(i * r - 1, 0)
    nxt = lambda bi, g, i: jnp.minimum((i + 1) * r, nb - 1)
    kcol, vcol = OFF_WK // hd, OFF_WV // hd
    return pl.pallas_call(
        functools.partial(_win_kernel, tq=tq),
        grid=(b, WIN_KV_HEADS, s // tq),
        in_specs=[pl.BlockSpec(memory_space=pltpu.SMEM),
                  pl.BlockSpec((1, tq, WIN_GROUP * hd), lambda bi, g, i: (bi, i, OFF_WQ // (WIN_GROUP * hd) + g)),
                  pl.BlockSpec((1, WINDOW, hd), lambda bi, g, i: (bi, prev(bi, g, i), kcol + g)),
                  pl.BlockSpec((1, tq, hd), lambda bi, g, i: (bi, i, kcol + g)),
                  pl.BlockSpec((1, WINDOW, hd), lambda bi, g, i: (bi, nxt(bi, g, i), kcol + g)),
                  pl.BlockSpec((1, WINDOW, hd), lambda bi, g, i: (bi, prev(bi, g, i), vcol + g)),
                  pl.BlockSpec((1, tq, hd), lambda bi, g, i: (bi, i, vcol + g)),
                  pl.BlockSpec((1, WINDOW, hd), lambda bi, g, i: (bi, nxt(bi, g, i), vcol + g)),
                  pl.BlockSpec((WIN_GROUP, tq, tq + 2 * WINDOW), lambda bi, g, i: (g, 0, 0))],
        out_specs=pl.BlockSpec((1, tq, WIN_GROUP * hd), lambda bi, g, i: (bi, i, g)),
        out_shape=jax.ShapeDtypeStruct((b, s, BRANCH_W), BF16),
        compiler_params=_params("parallel", "parallel", "arbitrary"),
        name="win_attn",
    )(sink, proj, proj, proj, proj, proj, proj, proj, bias)


def _mem_attn_kernel(q_ref, k_ref, v_ref, o_ref):
    s = _dot_nt(q_ref[0], k_ref[0]) * jnp.asarray(MEM_DIM ** -0.5, F32)
    p = jnp.exp(s - jnp.max(s, axis=-1, keepdims=True))
    p = p * (1.0 / jnp.sum(p, axis=-1, keepdims=True))
    o_ref[0] = _dot(p.astype(BF16), v_ref[0]).astype(BF16)


def _mem_attn(proj, memkv, tq=1024):
    b, s, _ = proj.shape
    md = MEM_DIM
    return pl.pallas_call(
        _mem_attn_kernel,
        grid=(b, MEM_HEADS, s // tq),
        in_specs=[pl.BlockSpec((1, tq, md), lambda bi, h, i: (bi, i, OFF_MQ // md + h)),
                  pl.BlockSpec((1, N_MEM, md), lambda bi, h, i: (bi, 0, h)),
                  pl.BlockSpec((1, N_MEM, md), lambda bi, h, i: (bi, 0, MEM_HEADS + h))],
        out_specs=pl.BlockSpec((1, tq, md), lambda bi, h, i: (bi, i, h)),
        out_shape=jax.ShapeDtypeStruct((b, s, BRANCH_W), BF16),
        compiler_params=_params("parallel", "parallel", "arbitrary"),
        name="mem_attn",
    )(proj, memkv, memkv)


def _mix_kernel(x_ref, h_ref, lg_ref, lb_ref, a_ref, b_ref, c_ref, wg0_ref, wg1_ref, wg2_ref,
                bg0_ref, bg1_ref, bg2_ref, wb_ref, wo_ref, g1_ref, b1_ref, o_ref, acc_scr):
    n = pl.program_id(1)
    nrow = x_ref.shape[0] // LN_ROWS

    @pl.when(n == 0)
    def _():
        acc_scr[...] = jnp.zeros_like(acc_scr)

    h = h_ref[...]
    mixed = None
    for br_ref, wg_ref, bg_ref, k in ((a_ref, wg0_ref, bg0_ref, 0), (b_ref, wg1_ref, bg1_ref, 1),
                                      (c_ref, wg2_ref, bg2_ref, 2)):
        gate = jax.nn.sigmoid(_dot(h, wg_ref[...]) + bg_ref[...])
        term = gate * _dot(br_ref[...], wb_ref[k])
        mixed = term if mixed is None else mixed + term
    acc_scr[...] += _dot(mixed.astype(BF16), wo_ref[...])

    @pl.when(n == pl.num_programs(1) - 1)
    def _():
        def body(r, c):
            rows = pl.ds(pl.multiple_of(r * LN_ROWS, LN_ROWS), LN_ROWS)
            hh = _ln_rows(x_ref[rows, :], lg_ref[...], lb_ref[...])
            o_ref[rows, :] = _ln_rows(ALPHA * hh + acc_scr[rows, :], g1_ref[...], b1_ref[...])
            return c
        lax.fori_loop(0, nrow, body, 0)


def _mix(x2, hb, lg, lb, a, b, c, wg, bg, wb, wo, g1, b1, tm=512, tn=256):
    m, d = x2.shape
    bw = a.shape[1]
    nn = d // tn
    row = lambda w: pl.BlockSpec((tm, w), lambda i, n: (i, 0))
    vec = pl.BlockSpec((1, d), lambda i, n: (0, 0))
    wgs = [pl.BlockSpec((d, tn), functools.partial(lambda i, n, k: (0, k * nn + n), k=k)) for k in range(3)]
    bgs = [pl.BlockSpec((1, tn), functools.partial(lambda i, n, k: (0, k * nn + n), k=k)) for k in range(3)]
    return pl.pallas_call(
        _mix_kernel,
        grid=(m // tm, nn),
        in_specs=[row(d), row(d), vec, vec, row(bw), row(bw), row(bw), *wgs, *bgs,
                  pl.BlockSpec((3, bw, tn), lambda i, n: (0, 0, n)),
                  pl.BlockSpec((tn, d), lambda i, n: (n, 0)), vec, vec],
        out_specs=pl.BlockSpec((tm, d), lambda i, n: (i, 0)),
        out_shape=jax.ShapeDtypeStruct((m, d), F32),
        scratch_shapes=[pltpu.VMEM((tm, d), F32)],
        compiler_params=_params("parallel", "arbitrary"),
        name="mix",
    )(x2, hb, lg, lb, a, b, c, wg, wg, wg, bg, bg, bg, wb, wo, g1, b1)


HALO = 16


def _gelu_tanh(x):
    return 0.5 * x * (1.0 + jnp.tanh(math.sqrt(2.0 / math.pi) * (x + 0.044715 * (x * x * x))))


def _ffn_kernel(h_ref, hp_ref, hn_ref, wv_ref, wg_ref, cwv_ref, cwg_ref, cbv_ref, cbg_ref, wd_ref,
                g2_ref, b2_ref, o_ref, hx_scr, acc_scr, *, tiles_per_seq):
    i = pl.program_id(0)
    f = pl.program_id(1)
    tm = h_ref.shape[0]
    nrow = tm // LN_ROWS

    @pl.when(f == 0)
    def _():
        seq_first = (i % tiles_per_seq) == 0
        seq_last = (i % tiles_per_seq) == tiles_per_seq - 1
        hx_scr[0:HALO, :] = jnp.where(seq_first, 0.0, hp_ref[...]).astype(BF16)
        hx_scr[HALO:HALO + tm, :] = h_ref[...].astype(BF16)
        hx_scr[HALO + tm:, :] = jnp.where(seq_last, 0.0, hn_ref[...]).astype(BF16)
        acc_scr[...] = jnp.zeros_like(acc_scr)

    hx = hx_scr[...]
    ext = tm + 2 * HALO

    def conv(u, cw_ref, cb_ref):
        prev = pltpu.roll(u, 1, 0)[HALO:HALO + tm]
        nxt = pltpu.roll(u, ext - 1, 0)[HALO:HALO + tm]
        return prev * cw_ref[0:1, :] + u[HALO:HALO + tm] * cw_ref[1:2, :] + nxt * cw_ref[2:3, :] + cb_ref[...]

    val = conv(_dot(hx, wv_ref[...]), cwv_ref, cbv_ref)
    gate = conv(_dot(hx, wg_ref[...]), cwg_ref, cbg_ref)
    acc_scr[...] += _dot((_gelu_tanh(gate) * val).astype(BF16), wd_ref[...])

    @pl.when(f == pl.num_programs(1) - 1)
    def _():
        def body(r, c):
            rows = pl.ds(pl.multiple_of(r * LN_ROWS, LN_ROWS), LN_ROWS)
            o_ref[rows, :] = _ln_rows(ALPHA * h_ref[rows, :] + acc_scr[rows, :], g2_ref[...], b2_ref[...])
            return c
        lax.fori_loop(0, nrow, body, 0)


def _ffn(h1, w_up, cw, cb, w_down, g2, b2, seq, tm=512, tf=512):
    m, d = h1.shape
    nf = D_FF_PAD // tf
    hb = tm // HALO
    nhb = m // HALO
    vec = pl.BlockSpec((1, d), lambda i, f: (0, 0))
    return pl.pallas_call(
        functools.partial(_ffn_kernel, tiles_per_seq=seq // tm),
        grid=(m // tm, nf),
        in_specs=[pl.BlockSpec((tm, d), lambda i, f: (i, 0)),
                  pl.BlockSpec((HALO, d), lambda i, f: (jnp.maximum(i * hb - 1, 0), 0)),
                  pl.BlockSpec((HALO, d), lambda i, f: (jnp.minimum((i + 1) * hb, nhb - 1), 0)),
                  pl.BlockSpec((d, tf), lambda i, f: (0, f)),
                  pl.BlockSpec((d, tf), lambda i, f: (0, nf + f)),
                  pl.BlockSpec((3, tf), lambda i, f: (0, f)),
                  pl.BlockSpec((3, tf), lambda i, f: (0, nf + f)),
                  pl.BlockSpec((1, tf), lambda i, f: (0, f)),
                  pl.BlockSpec((1, tf), lambda i, f: (0, nf + f)),
                  pl.BlockSpec((tf, d), lambda i, f: (f, 0)), vec, vec],
        out_specs=pl.BlockSpec((tm, d), lambda i, f: (i, 0)),
        out_shape=jax.ShapeDtypeStruct((m, d), F32),
        scratch_shapes=[pltpu.VMEM((tm + 2 * HALO, d), BF16), pltpu.VMEM((tm, d), F32)],
        compiler_params=_params("parallel", "arbitrary"),
        name="ffn",
    )(h1, h1, h1, w_up, w_up, cw, cw, cb, cb, w_down, g2, b2)


def _w_up_prep_kernel(w_ref, o_ref):
    o_ref[:, 0:D_FF] = w_ref[...].astype(BF16)
    o_ref[:, D_FF:] = jnp.zeros((o_ref.shape[0], D_FF_PAD - D_FF), BF16)


def _w_up_prep(w_up, tr=256):
    d = w_up.shape[0]
    return pl.pallas_call(
        _w_up_prep_kernel,
        grid=(2, d // tr),
        in_specs=[pl.BlockSpec((tr, D_FF), lambda half, r: (r, half))],
        out_specs=pl.BlockSpec((tr, D_FF_PAD), lambda half, r: (r, half)),
        out_shape=jax.ShapeDtypeStruct((d, 2 * D_FF_PAD), BF16),
        compiler_params=_params("parallel", "parallel"),
        name="w_up_prep",
    )(w_up)


def _pad_ff(t, axis):
    val, gate = jnp.split(t, 2, axis=axis)
    pad = [(0, 0)] * t.ndim
    pad[axis] = (0, D_FF_PAD - D_FF)
    return jnp.concatenate([jnp.pad(val, pad), jnp.pad(gate, pad)], axis=axis)


def kernel(x, mem, ln_in_g, ln_in_b, rel_table, w_in, w_mem_kv, diff_lq1, diff_lk1, diff_lq2, diff_lk2,
           diff_subln_g, win_sink, w_gate, b_gate, w_branch, w_o, ln1_g, ln1_b, w_up, conv_w, conv_b,
           w_down, ln2_g, ln2_b):
    assert w_in.shape[0] == DEPTH == 1
    bsz, seq, d = x.shape
    x2 = x.reshape(bsz * seq, d)
    row = lambda v: v.reshape(1, -1).astype(F32)
    l = 0
    proj, hb = _ln_proj(x2, row(ln_in_g), row(ln_in_b), w_in[l].astype(BF16))
    proj = proj.reshape(bsz, seq, IN_W)
    memkv = _mem_kv(mem.reshape(bsz * N_MEM, d), w_mem_kv[l].astype(BF16)).reshape(bsz, N_MEM, 2 * BRANCH_W)

    tq_d, tq_w = 256, 256
    a = _diff_attn(proj, _diff_bias_tiles(rel_table[:, :DIFF_HEADS] * LOG2E, tq_d), row(diff_lq1[l]), row(diff_lk1[l]),
                   row(diff_lq2[l]), row(diff_lk2[l]), row(diff_subln_g[l]), tq=tq_d)
    b = _win_attn(proj, _win_bias_tiles(rel_table[:, DIFF_HEADS:] * LOG2E, tq_w), win_sink[l].astype(F32) * LOG2E,
                  tq=tq_w)
    c = _mem_attn(proj, memkv)

    m = bsz * seq
    h1 = _mix(x2, hb, row(ln_in_g), row(ln_in_b), a.reshape(m, -1), b.reshape(m, -1), c.reshape(m, -1),
              w_gate[l].astype(BF16), row(b_gate[l]), w_branch[l].astype(BF16), w_o[l].astype(BF16),
              row(ln1_g[l]), row(ln1_b[l]))

    w_up_p = _w_up_prep(w_up[l].astype(F32))
    cw_p = _pad_ff(conv_w[l].astype(F32), 1)
    cb_p = _pad_ff(row(conv_b[l]), 1)
    w_down_p = jnp.pad(w_down[l], ((0, D_FF_PAD - D_FF), (0, 0))).astype(BF16)
    out = _ffn(h1, w_up_p, cw_p, cb_p, w_down_p, row(ln2_g[l]), row(ln2_b[l]), seq)
    return out.reshape(bsz, seq, d)
```

```python
import functools
import math

import jax
import jax.numpy as jnp
from jax import lax
from jax.experimental import pallas as pl
from jax.experimental.pallas import tpu as pltpu

F32 = jnp.float32
BF16 = jnp.bfloat16

D_MODEL = 2048
SEQ = 2048
N_MEM = 256
HEAD_DIM = 128
BRANCH_W = 1024
DIFF_HEADS = 8
DIFF_QK = 64
WIN_HEADS = 8
WIN_KV_HEADS = 2
WIN_GROUP = WIN_HEADS // WIN_KV_HEADS
WINDOW = 128
MEM_HEADS = 4
MEM_DIM = 256
OFF_DQ, OFF_DK, OFF_DV, OFF_WQ, OFF_WK, OFF_WV, OFF_MQ = 0, 1024, 2048, 3072, 4096, 4352, 4608
IN_W = 5632
D_FF = 5504
D_FF_PAD = 5632
REL_BUCKETS = 32
REL_MAX_DIST = 128
DEPTH = 1
ALPHA = (2 * DEPTH) ** 0.25
LN_EPS = 1e-5
NEG = -1e30
LOG2E = math.log2(math.e)
LAMBDA_INIT = 0.8 - 0.6 * math.exp(-0.3 * 0)

VMEM_LIMIT = 56 * 1024 * 1024
LN_ROWS = 128


def _ln_rows(x, g, b):
    mu = jnp.mean(x, axis=-1, keepdims=True)
    xc = x - mu
    var = jnp.mean(xc * xc, axis=-1, keepdims=True)
    return xc * lax.rsqrt(var + LN_EPS) * g + b


def _dot(a, b):
    return jnp.dot(a, b, preferred_element_type=F32)


def _dot_nt(a, b):
    return lax.dot_general(a, b, (((1,), (1,)), ((), ())), preferred_element_type=F32)


def _params(*sem):
    return pltpu.CompilerParams(dimension_semantics=sem, vmem_limit_bytes=VMEM_LIMIT)


def _ln_proj_kernel(x_ref, g_ref, b_ref, w_ref, o_ref, h_ref):
    @pl.when(pl.program_id(1) == 0)
    def _():
        def body(r, c):
            rows = pl.ds(pl.multiple_of(r * LN_ROWS, LN_ROWS), LN_ROWS)
            h_ref[rows, :] = _ln_rows(x_ref[rows, :], g_ref[...], b_ref[...]).astype(BF16)
            return c
        lax.fori_loop(0, x_ref.shape[0] // LN_ROWS, body, 0)

    o_ref[...] = _dot(h_ref[...], w_ref[...]).astype(BF16)


def _ln_proj(x2, g, b, w, tm=1024, tn=512):
    m, d = x2.shape
    n = w.shape[1]
    return pl.pallas_call(
        _ln_proj_kernel,
        grid=(m // tm, n // tn),
        in_specs=[pl.BlockSpec((tm, d), lambda i, j: (i, 0)),
                  pl.BlockSpec((1, d), lambda i, j: (0, 0)),
                  pl.BlockSpec((1, d), lambda i, j: (0, 0)),
                  pl.BlockSpec((d, tn), lambda i, j: (0, j))],
        out_specs=[pl.BlockSpec((tm, tn), lambda i, j: (i, j)),
                   pl.BlockSpec((tm, d), lambda i, j: (i, 0))],
        out_shape=[jax.ShapeDtypeStruct((m, n), BF16), jax.ShapeDtypeStruct((m, d), BF16)],
        compiler_params=_params("parallel", "arbitrary"),
        name="ln_proj",
    )(x2, g, b, w)


def _mem_kv_kernel(m_ref, w_ref, o_ref):
    o_ref[...] = _dot(m_ref[...].astype(BF16), w_ref[...]).astype(BF16)


def _mem_kv(mem2, w, tn=512):
    m, d = mem2.shape
    n = w.shape[1]
    return pl.pallas_call(
        _mem_kv_kernel,
        grid=(n // tn,),
        in_specs=[pl.BlockSpec((m, d), lambda j: (0, 0)),
                  pl.BlockSpec((d, tn), lambda j: (0, j))],
        out_specs=pl.BlockSpec((m, tn), lambda j: (0, j)),
        out_shape=jax.ShapeDtypeStruct((m, n), BF16),
        compiler_params=_params("parallel"),
        name="mem_kv",
    )(mem2, w)


def _t5_bucket(rel):
    half = REL_BUCKETS // 2
    max_exact = half // 2
    ret = jnp.where(rel > 0, half, 0)
    n = jnp.abs(rel)
    nf = jnp.maximum(n, 1).astype(F32)
    large = max_exact + (jnp.log(nf / max_exact) / math.log(REL_MAX_DIST / max_exact)
                         * (half - max_exact)).astype(jnp.int32)
    large = jnp.minimum(large, half - 1)
    return ret + jnp.where(n < max_exact, n, large)


BIAS_ROWS = 64


def _bias_lookup_kernel(tab_ref, bucket_ref, o_ref):
    h = pl.program_id(0)

    def body(r, carry):
        rows = pl.ds(pl.multiple_of(r * BIAS_ROWS, BIAS_ROWS), BIAS_ROWS)
        bk = bucket_ref[rows, :]
        out = jnp.full(bk.shape, NEG, F32)
        for b in range(REL_BUCKETS):
            out = jnp.where(bk == b, tab_ref[b, h], out)
        o_ref[0, rows, :] = out
        return carry
    lax.fori_loop(0, bucket_ref.shape[0] // BIAS_ROWS, body, 0)


def _bias_lookup(table, bucket):
    nh = table.shape[1]
    r, c = bucket.shape
    return pl.pallas_call(
        _bias_lookup_kernel,
        grid=(nh,),
        in_specs=[pl.BlockSpec(memory_space=pltpu.SMEM),
                  pl.BlockSpec((r, c), lambda h: (0, 0))],
        out_specs=pl.BlockSpec((1, r, c), lambda h: (h, 0, 0)),
        out_shape=jax.ShapeDtypeStruct((nh, r, c), F32),
        compiler_params=_params("parallel"),
        name="bias_lookup",
    )(table.astype(F32), bucket)


def _diff_bias_tiles(table, tq):
    d = jnp.arange(5)[:, None, None] - 2
    q = jnp.arange(tq)[None, :, None]
    k = jnp.arange(tq)[None, None, :]
    bucket = _t5_bucket(d * tq + k - q).astype(jnp.int32).reshape(5 * tq, tq)
    return _bias_lookup(table, bucket).reshape(table.shape[1], 5, tq, tq)


def _win_bias_tiles(table, tq):
    q = jnp.arange(tq)[:, None]
    c = jnp.arange(tq + 2 * WINDOW)[None, :]
    rel = c - WINDOW - q
    bucket = jnp.where(jnp.abs(rel) <= WINDOW, _t5_bucket(rel), -1).astype(jnp.int32)
    return _bias_lookup(table, bucket)


def _diff_kernel(q_ref, k_ref, v_ref, band_ref, lq1_ref, lk1_ref, lq2_ref, lk2_ref, g_ref, o_ref,
                 s_scr, mx_scr, *, tq, rc, n_tiles, tiles_per_head):
    t = pl.program_id(0)
    hd = v_ref.shape[2]

    @pl.when(t == 0)
    def _():
        s_scr[...] = jnp.zeros_like(s_scr)
        mx_scr[...] = jnp.zeros_like(mx_scr)

    def tick(slot):
        other = 1 - slot
        i = jnp.minimum(t, n_tiles - 1) % tiles_per_head
        nk = k_ref.shape[1] // tq
        q = (q_ref[0].astype(F32) * (DIFF_QK ** -0.5 * LOG2E)).astype(BF16)
        lane = lax.broadcasted_iota(jnp.int32, q.shape, 1)
        zero = jnp.zeros_like(q)
        qs = jnp.concatenate([jnp.where(lane < DIFF_QK, q, zero), jnp.where(lane >= DIFF_QK, q, zero)], axis=0)
        lanes = mx_scr.shape[-1]
        for j in range(nk):
            cols = slice(j * tq, (j + 1) * tq)
            bias = band_ref[0, jnp.clip(j - i, -2, 2) + 2]
            s = _dot_nt(qs, k_ref[0, cols, :])
            for half in (slice(0, tq), slice(tq, 2 * tq)):
                sb = s[half] + bias
                s_scr[slot, half, cols] = sb
                fold = functools.reduce(jnp.maximum, [sb[:, c:c + lanes] for c in range(0, tq, lanes)])
                mx_scr[slot, half, :] = fold if j == 0 else jnp.maximum(mx_scr[slot, half, :], fold)

        nchunk = 2 * tq // rc
        chunk = lambda r: slice(r * rc, (r + 1) * rc)
        p = jnp.concatenate(
            [jnp.exp2(s_scr[other, chunk(r), :] - jnp.max(mx_scr[other, chunk(r), :], axis=-1, keepdims=True))
             .astype(BF16) for r in range(nchunk)], axis=0)
        v_ones = jnp.concatenate([v_ref[0], jnp.ones(v_ref.shape[1:], BF16)], axis=1)
        pv = _dot(p, v_ones)
        lam = (jnp.exp(jnp.sum(lq1_ref[...] * lk1_ref[...], axis=-1, keepdims=True))
               - jnp.exp(jnp.sum(lq2_ref[...] * lk2_ref[...], axis=-1, keepdims=True)) + LAMBDA_INIT)
        o = pv[0:tq, 0:hd] * (1.0 / pv[0:tq, hd:]) - pv[tq:, 0:hd] * (lam / pv[tq:, hd:])
        ms = jnp.mean(o * o, axis=-1, keepdims=True)
        o_ref[0] = (o * lax.rsqrt(ms + LN_EPS) * g_ref[...] * (1.0 - LAMBDA_INIT)).astype(BF16)

    pl.when(t % 2 == 0)(functools.partial(tick, 0))
    pl.when(t % 2 == 1)(functools.partial(tick, 1))


def _diff_attn(proj, band, lq1, lk1, lq2, lk2, g, tq=256, rc=32):
    b, s, _ = proj.shape
    cb = HEAD_DIM
    ni = s // tq
    n_tiles = b * DIFF_HEADS * ni
    depth = 1

    def tile(t, lag):
        tc = jnp.clip(t - lag, 0, n_tiles - 1)
        return tc // (DIFF_HEADS * ni), (tc // ni) % DIFF_HEADS, tc % ni

    def q_map(t):
        bi, h, i = tile(t, 0)
        return bi, i, OFF_DQ // cb + h

    def k_map(t):
        bi, h, _ = tile(t, 0)
        return bi, 0, OFF_DK // cb + h

    def v_map(t):
        bi, h, _ = tile(t, depth)
        return bi, 0, OFF_DV // cb + h

    def o_map(t):
        bi, h, i = tile(t, depth)
        return bi, i, h

    vec = lambda n: pl.BlockSpec((1, n), lambda t: (0, 0))
    return pl.pallas_call(
        functools.partial(_diff_kernel, tq=tq, rc=rc, n_tiles=n_tiles, tiles_per_head=ni),
        grid=(n_tiles + depth,),
        in_specs=[pl.BlockSpec((1, tq, cb), q_map),
                  pl.BlockSpec((1, s, cb), k_map),
                  pl.BlockSpec((1, s, cb), v_map),
                  pl.BlockSpec((1, 5, tq, tq), lambda t: (tile(t, 0)[1], 0, 0, 0)),
                  vec(DIFF_QK), vec(DIFF_QK), vec(DIFF_QK), vec(DIFF_QK), vec(HEAD_DIM)],
        out_specs=pl.BlockSpec((1, tq, cb), o_map),
        out_shape=jax.ShapeDtypeStruct((b, s, BRANCH_W), BF16),
        scratch_shapes=[pltpu.VMEM((2, 2 * tq, s), F32), pltpu.VMEM((2, 2 * tq, cb), F32)],
        compiler_params=_params("arbitrary"),
        name="diff_attn",
    )(proj, proj, proj, band, lq1, lk1, lq2, lk2, g)


def _win_kernel(sink_ref, q_ref, kp_ref, km_ref, kn_ref, vp_ref, vm_ref, vn_ref, bias_ref, o_ref, *, tq):
    g = pl.program_id(1)
    i = pl.program_id(2)
    first = i == 0
    last = i == pl.num_programs(2) - 1
    w = WINDOW
    hd = HEAD_DIM
    nkeys = tq + 2 * w
    keys = jnp.concatenate([kp_ref[0], km_ref[0], kn_ref[0]], axis=0)
    vals = jnp.concatenate([vp_ref[0], vm_ref[0], vn_ref[0]], axis=0)
    v_ones = jnp.concatenate([vals, jnp.ones((nkeys, hd), BF16)], axis=1)
    col = lax.broadcasted_iota(jnp.int32, (1, nkeys), 1)
    outside = jnp.logical_or(jnp.logical_and(first, col < w),
                             jnp.logical_and(last, col >= tq + w))
    scale = jnp.asarray(hd ** -0.5 * LOG2E, F32)
    for j in range(WIN_GROUP):
        cols = slice(j * hd, (j + 1) * hd)
        s = _dot_nt(q_ref[0, :, cols], keys) * scale + bias_ref[j]
        s = jnp.where(outside, NEG, s)
        sink = sink_ref[g * WIN_GROUP + j]
        fold = functools.reduce(jnp.maximum, [s[:, c:c + w] for c in range(0, nkeys, w)])
        m = jnp.maximum(jnp.max(fold, axis=-1, keepdims=True), sink)
        pv = _dot(jnp.exp2(s - m).astype(BF16), v_ones)
        o_ref[0, :, cols] = (pv[:, 0:hd] / (pv[:, hd:] + jnp.exp2(sink - m))).astype(BF16)


def _win_attn(proj, bias, sink, tq=256):
    b, s, _ = proj.shape
    hd = HEAD_DIM
    nb = s // WINDOW
    r = tq // WINDOW
    prev = lambda bi, g, i: jnp.maximum(i * r - 1, 0)
    nxt = lambda bi, g, i: jnp.minimum((i + 1) * r, nb - 1)
    kcol, vcol = OFF_WK // hd, OFF_WV // hd
    return pl.pallas_call(
        functools.partial(_win_kernel, tq=tq),
        grid=(b, WIN_KV_HEADS, s // tq),
        in_specs=[pl.BlockSpec(memory_space=pltpu.SMEM),
                  pl.BlockSpec((1, tq, WIN_GROUP * hd), lambda bi, g, i: (bi, i, OFF_WQ // (WIN_GROUP * hd) + g)),
                  pl.BlockSpec((1, WINDOW, hd), lambda bi, g, i: (bi, prev(bi, g, i), kcol + g)),
                  pl.BlockSpec((1, tq, hd), lambda bi, g, i: (bi, i, kcol + g)),
                  pl.BlockSpec((1, WINDOW, hd), lambda bi, g, i: (bi, nxt(bi, g, i), kcol + g)),
                  pl.BlockSpec((1, WINDOW, hd), lambda bi, g, i: (bi, prev(bi, g, i), vcol + g)),
                  pl.BlockSpec((1, tq, hd), lambda bi, g, i: (bi, i, vcol + g)),
                  pl.BlockSpec((1, WINDOW, hd), lambda bi, g, i: (bi, nxt(bi, g, i), vcol + g)),
                  pl.BlockSpec((WIN_GROUP, tq, tq + 2 * WINDOW), lambda bi, g, i: (g, 0, 0))],
        out_specs=pl.BlockSpec((1, tq, WIN_GROUP * hd), lambda bi, g, i: (bi, i, g)),
        out_shape=jax.ShapeDtypeStruct((b, s, BRANCH_W), BF16),
        compiler_params=_params("parallel", "parallel", "arbitrary"),
        name="win_attn",
    )(sink, proj, proj, proj, proj, proj, proj, proj, bias)


def _mem_attn_kernel(q_ref, k_ref, v_ref, o_ref):
    s = _dot_nt(q_ref[0], k_ref[0]) * jnp.asarray(MEM_DIM ** -0.5, F32)
    p = jnp.exp(s - jnp.max(s, axis=-1, keepdims=True))
    p = p * (1.0 / jnp.sum(p, axis=-1, keepdims=True))
    o_ref[0] = _dot(p.astype(BF16), v_ref[0]).astype(BF16)


def _mem_attn(proj, memkv, tq=1024):
    b, s, _ = proj.shape
    md = MEM_DIM
    return pl.pallas_call(
        _mem_attn_kernel,
        grid=(b, MEM_HEADS, s // tq),
        in_specs=[pl.BlockSpec((1, tq, md), lambda bi, h, i: (bi, i, OFF_MQ // md + h)),
                  pl.BlockSpec((1, N_MEM, md), lambda bi, h, i: (bi, 0, h)),
                  pl.BlockSpec((1, N_MEM, md), lambda bi, h, i: (bi, 0, MEM_HEADS + h))],
        out_specs=pl.BlockSpec((1, tq, md), lambda bi, h, i: (bi, i, h)),
        out_shape=jax.ShapeDtypeStruct((b, s, BRANCH_W), BF16),
        compiler_params=_params("parallel", "parallel", "arbitrary"),
        name="mem_attn",
    )(proj, memkv, memkv)


def _mix_kernel(x_ref, h_ref, lg_ref, lb_ref, a_ref, b_ref, c_ref, wg0_ref, wg1_ref, wg2_ref,
                bg0_ref, bg1_ref, bg2_ref, wb_ref, wo_ref, g1_ref, b1_ref, o_ref, acc_scr):
    n = pl.program_id(1)
    nrow = x_ref.shape[0] // LN_ROWS

    @pl.when(n == 0)
    def _():
        acc_scr[...] = jnp.zeros_like(acc_scr)

    h = h_ref[...]
    mixed = None
    for br_ref, wg_ref, bg_ref, k in ((a_ref, wg0_ref, bg0_ref, 0), (b_ref, wg1_ref, bg1_ref, 1),
                                      (c_ref, wg2_ref, bg2_ref, 2)):
        gate = jax.nn.sigmoid(_dot(h, wg_ref[...]) + bg_ref[...])
        term = gate * _dot(br_ref[...], wb_ref[k])
        mixed = term if mixed is None else mixed + term
    acc_scr[...] += _dot(mixed.astype(BF16), wo_ref[...])

    @pl.when(n == pl.num_programs(1) - 1)
    def _():
        def body(r, c):
            rows = pl.ds(pl.multiple_of(r * LN_ROWS, LN_ROWS), LN_ROWS)
            hh = _ln_rows(x_ref[rows, :], lg_ref[...], lb_ref[...])
            o_ref[rows, :] = _ln_rows(ALPHA * hh + acc_scr[rows, :], g1_ref[...], b1_ref[...])
            return c
        lax.fori_loop(0, nrow, body, 0)


def _mix(x2, hb, lg, lb, a, b, c, wg, bg, wb, wo, g1, b1, tm=512, tn=256):
    m, d = x2.shape
    bw = a.shape[1]
    nn = d // tn
    row = lambda w: pl.BlockSpec((tm, w), lambda i, n: (i, 0))
    vec = pl.BlockSpec((1, d), lambda i, n: (0, 0))
    wgs = [pl.BlockSpec((d, tn), functools.partial(lambda i, n, k: (0, k * nn + n), k=k)) for k in range(3)]
    bgs = [pl.BlockSpec((1, tn), functools.partial(lambda i, n, k: (0, k * nn + n), k=k)) for k in range(3)]
    return pl.pallas_call(
        _mix_kernel,
        grid=(m // tm, nn),
        in_specs=[row(d), row(d), vec, vec, row(bw), row(bw), row(bw), *wgs, *bgs,
                  pl.BlockSpec((3, bw, tn), lambda i, n: (0, 0, n)),
                  pl.BlockSpec((tn, d), lambda i, n: (n, 0)), vec, vec],
        out_specs=pl.BlockSpec((tm, d), lambda i, n: (i, 0)),
        out_shape=jax.ShapeDtypeStruct((m, d), F32),
        scratch_shapes=[pltpu.VMEM((tm, d), F32)],
        compiler_params=_params("parallel", "arbitrary"),
        name="mix",
    )(x2, hb, lg, lb, a, b, c, wg, wg, wg, bg, bg, bg, wb, wo, g1, b1)


HALO = 16


def _gelu_tanh(x):
    return 0.5 * x * (1.0 + jnp.tanh(math.sqrt(2.0 / math.pi) * (x + 0.044715 * (x * x * x))))


def _ffn_kernel(h_ref, hp_ref, hn_ref, wv_ref, wg_ref, cwv_ref, cwg_ref, cbv_ref, cbg_ref, wd_ref,
                g2_ref, b2_ref, o_ref, hx_scr, acc_scr, *, tiles_per_seq):
    i = pl.program_id(0)
    f = pl.program_id(1)
    tm = h_ref.shape[0]
    nrow = tm // LN_ROWS

    @pl.when(f == 0)
    def _():
        seq_first = (i % tiles_per_seq) == 0
        seq_last = (i % tiles_per_seq) == tiles_per_seq - 1
        hx_scr[0:HALO, :] = jnp.where(seq_first, 0.0, hp_ref[...]).astype(BF16)
        hx_scr[HALO:HALO + tm, :] = h_ref[...].astype(BF16)
        hx_scr[HALO + tm:, :] = jnp.where(seq_last, 0.0, hn_ref[...]).astype(BF16)
        acc_scr[...] = jnp.zeros_like(acc_scr)

    hx = hx_scr[...]
    ext = tm + 2 * HALO

    def conv(u, cw_ref, cb_ref):
        prev = pltpu.roll(u, 1, 0)[HALO:HALO + tm]
        nxt = pltpu.roll(u, ext - 1, 0)[HALO:HALO + tm]
        return prev * cw_ref[0:1, :] + u[HALO:HALO + tm] * cw_ref[1:2, :] + nxt * cw_ref[2:3, :] + cb_ref[...]

    gelu_gate = _gelu_tanh(conv(_dot(hx, wg_ref[...]), cwg_ref, cbg_ref))
    val = conv(_dot(hx, wv_ref[...]), cwv_ref, cbv_ref)
    acc_scr[...] += _dot((gelu_gate * val).astype(BF16), wd_ref[...])

    @pl.when(f == pl.num_programs(1) - 1)
    def _():
        def body(r, c):
            rows = pl.ds(pl.multiple_of(r * LN_ROWS, LN_ROWS), LN_ROWS)
            o_ref[rows, :] = _ln_rows(ALPHA * h_ref[rows, :] + acc_scr[rows, :], g2_ref[...], b2_ref[...])
            return c
        lax.fori_loop(0, nrow, body, 0)


def _ffn(h1, w_up, cw, cb, w_down, g2, b2, seq, tm=512, tf=512):
    m, d = h1.shape
    nf = D_FF_PAD // tf
    hb = tm // HALO
    nhb = m // HALO
    vec = pl.BlockSpec((1, d), lambda i, f: (0, 0))
    return pl.pallas_call(
        functools.partial(_ffn_kernel, tiles_per_seq=seq // tm),
        grid=(m // tm, nf),
        in_specs=[pl.BlockSpec((tm, d), lambda i, f: (i, 0)),
                  pl.BlockSpec((HALO, d), lambda i, f: (jnp.maximum(i * hb - 1, 0), 0)),
                  pl.BlockSpec((HALO, d), lambda i, f: (jnp.minimum((i + 1) * hb, nhb - 1), 0)),
                  pl.BlockSpec((d, tf), lambda i, f: (0, f)),
                  pl.BlockSpec((d, tf), lambda i, f: (0, nf + f)),
                  pl.BlockSpec((3, tf), lambda i, f: (0, f)),
                  pl.BlockSpec((3, tf), lambda i, f: (0, nf + f)),
                  pl.BlockSpec((1, tf), lambda i, f: (0, f)),
                  pl.BlockSpec((1, tf), lambda i, f: (0, nf + f)),
                  pl.BlockSpec((tf, d), lambda i, f: (f, 0)), vec, vec],
        out_specs=pl.BlockSpec((tm, d), lambda i, f: (i, 0)),
        out_shape=jax.ShapeDtypeStruct((m, d), F32),
        scratch_shapes=[pltpu.VMEM((tm + 2 * HALO, d), BF16), pltpu.VMEM((tm, d), F32)],
        compiler_params=_params("parallel", "arbitrary"),
        name="ffn",
    )(h1, h1, h1, w_up, w_up, cw, cw, cb, cb, w_down, g2, b2)


def _w_up_prep_kernel(w_ref, o_ref):
    o_ref[:, 0:D_FF] = w_ref[...].astype(BF16)
    o_ref[:, D_FF:] = jnp.zeros((o_ref.shape[0], D_FF_PAD - D_FF), BF16)


def _w_up_prep(w_up, tr=256):
    d = w_up.shape[0]
    return pl.pallas_call(
        _w_up_prep_kernel,
        grid=(2, d // tr),
        in_specs=[pl.BlockSpec((tr, D_FF), lambda half, r: (r, half))],
        out_specs=pl.BlockSpec((tr, D_FF_PAD), lambda half, r: (r, half)),
        out_shape=jax.ShapeDtypeStruct((d, 2 * D_FF_PAD), BF16),
        compiler_params=_params("parallel", "parallel"),
        name="w_up_prep",
    )(w_up)


def _pad_ff(t, axis):
    val, gate = jnp.split(t, 2, axis=axis)
    pad = [(0, 0)] * t.ndim
    pad[axis] = (0, D_FF_PAD - D_FF)
    return jnp.concatenate([jnp.pad(val, pad), jnp.pad(gate, pad)], axis=axis)


def kernel(x, mem, ln_in_g, ln_in_b, rel_table, w_in, w_mem_kv, diff_lq1, diff_lk1, diff_lq2, diff_lk2,
           diff_subln_g, win_sink, w_gate, b_gate, w_branch, w_o, ln1_g, ln1_b, w_up, conv_w, conv_b,
           w_down, ln2_g, ln2_b):
    assert w_in.shape[0] == DEPTH == 1
    bsz, seq, d = x.shape
    x2 = x.reshape(bsz * seq, d)
    row = lambda v: v.reshape(1, -1).astype(F32)
    l = 0
    proj, hb = _ln_proj(x2, row(ln_in_g), row(ln_in_b), w_in[l].astype(BF16))
    proj = proj.reshape(bsz, seq, IN_W)
    memkv = _mem_kv(mem.reshape(bsz * N_MEM, d), w_mem_kv[l].astype(BF16)).reshape(bsz, N_MEM, 2 * BRANCH_W)

    tq_d, tq_w = 256, 256
    a = _diff_attn(proj, _diff_bias_tiles(rel_table[:, :DIFF_HEADS] * LOG2E, tq_d), row(diff_lq1[l]), row(diff_lk1[l]),
                   row(diff_lq2[l]), row(diff_lk2[l]), row(diff_subln_g[l]), tq=tq_d)
    b = _win_attn(proj, _win_bias_tiles(rel_table[:, DIFF_HEADS:] * LOG2E, tq_w), win_sink[l].astype(F32) * LOG2E,
                  tq=tq_w)
    c = _mem_attn(proj, memkv)

    m = bsz * seq
    h1 = _mix(x2, hb, row(ln_in_g), row(ln_in_b), a.reshape(m, -1), b.reshape(m, -1), c.reshape(m, -1),
              w_gate[l].astype(BF16), row(b_gate[l]), w_branch[l].astype(BF16), w_o[l].astype(BF16),
              row(ln1_g[l]), row(ln1_b[l]))

    w_up_p = _w_up_prep(w_up[l].astype(F32))
    cw_p = _pad_ff(conv_w[l].astype(F32), 1)
    cb_p = _pad_ff(row(conv_b[l]), 1)
    w_down_p = jnp.pad(w_down[l], ((0, D_FF_PAD - D_FF), (0, 0))).astype(BF16)
    out = _ffn(h1, w_up_p, cw_p, cb_p, w_down_p, row(ln2_g[l]), row(ln2_b[l]), seq)
    return out.reshape(bsz, seq, d)
```

```python
import functools
import math

import jax
import jax.numpy as jnp
from jax import lax
from jax.experimental import pallas as pl
from jax.experimental.pallas import tpu as pltpu

F32 = jnp.float32
BF16 = jnp.bfloat16

D_MODEL = 2048
SEQ = 2048
N_MEM = 256
HEAD_DIM = 128
BRANCH_W = 1024
DIFF_HEADS = 8
DIFF_QK = 64
WIN_HEADS = 8
WIN_KV_HEADS = 2
WIN_GROUP = WIN_HEADS // WIN_KV_HEADS
WINDOW = 128
MEM_HEADS = 4
MEM_DIM = 256
OFF_DQ, OFF_DK, OFF_DV, OFF_WQ, OFF_WK, OFF_WV, OFF_MQ = 0, 1024, 2048, 3072, 4096, 4352, 4608
IN_W = 5632
D_FF = 5504
D_FF_PAD = 5632
REL_BUCKETS = 32
REL_MAX_DIST = 128
DEPTH = 1
ALPHA = (2 * DEPTH) ** 0.25
LN_EPS = 1e-5
NEG = -1e30
LOG2E = math.log2(math.e)
LAMBDA_INIT = 0.8 - 0.6 * math.exp(-0.3 * 0)

VMEM_LIMIT = 56 * 1024 * 1024
LN_ROWS = 128


def _ln_rows(x, g, b):
    mu = jnp.mean(x, axis=-1, keepdims=True)
    xc = x - mu
    var = jnp.mean(xc * xc, axis=-1, keepdims=True)
    return xc * lax.rsqrt(var + LN_EPS) * g + b


def _dot(a, b):
    return jnp.dot(a, b, preferred_element_type=F32)


def _dot_nt(a, b):
    return lax.dot_general(a, b, (((1,), (1,)), ((), ())), preferred_element_type=F32)


def _params(*sem):
    return pltpu.CompilerParams(dimension_semantics=sem, vmem_limit_bytes=VMEM_LIMIT)


def _ln_proj_kernel(x_ref, g_ref, b_ref, w_ref, o_ref, h_ref):
    @pl.when(pl.program_id(1) == 0)
    def _():
        def body(r, c):
            rows = pl.ds(pl.multiple_of(r * LN_ROWS, LN_ROWS), LN_ROWS)
            h_ref[rows, :] = _ln_rows(x_ref[rows, :], g_ref[...], b_ref[...]).astype(BF16)
            return c
        lax.fori_loop(0, x_ref.shape[0] // LN_ROWS, body, 0)

    o_ref[...] = _dot(h_ref[...], w_ref[...].astype(BF16)).astype(BF16)


def _ln_proj(x2, g, b, w, tm=1024, tn=512):
    m, d = x2.shape
    n = w.shape[1]
    return pl.pallas_call(
        _ln_proj_kernel,
        grid=(m // tm, n // tn),
        in_specs=[pl.BlockSpec((tm, d), lambda i, j: (i, 0)),
                  pl.BlockSpec((1, d), lambda i, j: (0, 0)),
                  pl.BlockSpec((1, d), lambda i, j: (0, 0)),
                  pl.BlockSpec((d, tn), lambda i, j: (0, j))],
        out_specs=[pl.BlockSpec((tm, tn), lambda i, j: (i, j)),
                   pl.BlockSpec((tm, d), lambda i, j: (i, 0))],
        out_shape=[jax.ShapeDtypeStruct((m, n), BF16), jax.ShapeDtypeStruct((m, d), BF16)],
        compiler_params=_params("parallel", "arbitrary"),
        name="ln_proj",
    )(x2, g, b, w)


def _mem_kv_kernel(m_ref, w_ref, o_ref):
    o_ref[...] = _dot(m_ref[...].astype(BF16), w_ref[...].astype(BF16)).astype(BF16)


def _mem_kv(mem2, w, tn=512):
    m, d = mem2.shape
    n = w.shape[1]
    return pl.pallas_call(
        _mem_kv_kernel,
        grid=(n // tn,),
        in_specs=[pl.BlockSpec((m, d), lambda j: (0, 0)),
                  pl.BlockSpec((d, tn), lambda j: (0, j))],
        out_specs=pl.BlockSpec((m, tn), lambda j: (0, j)),
        out_shape=jax.ShapeDtypeStruct((m, n), BF16),
        compiler_params=_params("parallel"),
        name="mem_kv",
    )(mem2, w)


def _bucket_lower_bounds():
    half = REL_BUCKETS // 2
    max_exact = half // 2
    n = jnp.arange(REL_MAX_DIST + 1)
    log_part = (jnp.log(jnp.maximum(n, 1).astype(F32) / max_exact) / math.log(REL_MAX_DIST / max_exact)
                * (half - max_exact)).astype(jnp.int32)
    bucket = jnp.where(n < max_exact, n, jnp.minimum(max_exact + log_part, half - 1))
    bucket = jnp.arange(half, dtype=jnp.int32)[bucket]
    ids = jnp.arange(half, dtype=jnp.int32)
    return jnp.sum((bucket[None, :] < ids[:, None]).astype(jnp.int32), axis=1)


BIAS_ROWS = 64


def _bias_kernel(lb_ref, tab_ref, off_ref, o_ref, *, limit):
    h = pl.program_id(0)
    half = REL_BUCKETS // 2
    ncol = o_ref.shape[2]

    def body(r, carry):
        rows = pl.ds(pl.multiple_of(r * BIAS_ROWS, BIAS_ROWS), BIAS_ROWS)
        rel = lax.broadcasted_iota(jnp.int32, (BIAS_ROWS, ncol), 1) + off_ref[rows, :]
        n = jnp.abs(rel)
        neg = jnp.full(rel.shape, tab_ref[0, h], F32)
        pos = jnp.full(rel.shape, tab_ref[half, h], F32)
        for b in range(1, half):
            reached = n >= lb_ref[b]
            neg = jnp.where(reached, tab_ref[b, h], neg)
            pos = jnp.where(reached, tab_ref[half + b, h], pos)
        out = jnp.where(rel > 0, pos, neg)
        if limit is not None:
            out = jnp.where(n > limit, NEG, out)
        o_ref[0, rows, :] = out
        return carry
    lax.fori_loop(0, o_ref.shape[1] // BIAS_ROWS, body, 0)


def _bias_tiles(table, row_offset, ncol, limit=None):
    nh = table.shape[1]
    nrow = row_offset.shape[0]
    return pl.pallas_call(
        functools.partial(_bias_kernel, limit=limit),
        grid=(nh,),
        in_specs=[pl.BlockSpec(memory_space=pltpu.SMEM), pl.BlockSpec(memory_space=pltpu.SMEM),
                  pl.BlockSpec((nrow, 1), lambda h: (0, 0))],
        out_specs=pl.BlockSpec((1, nrow, ncol), lambda h: (h, 0, 0)),
        out_shape=jax.ShapeDtypeStruct((nh, nrow, ncol), F32),
        compiler_params=_params("parallel"),
        name="bias_tiles",
    )(_bucket_lower_bounds(), table.astype(F32), row_offset.astype(jnp.int32).reshape(nrow, 1))


def _diff_bias_tiles(table, tq):
    r = jnp.arange(5 * tq)
    return _bias_tiles(table, (r // tq - 2) * tq - r % tq, tq).reshape(table.shape[1], 5, tq, tq)


def _win_bias_tiles(table, tq):
    return _bias_tiles(table, -WINDOW - jnp.arange(tq), tq + 2 * WINDOW, limit=WINDOW)


def _diff_kernel(q_ref, k_ref, v_ref, band_ref, lq1_ref, lk1_ref, lq2_ref, lk2_ref, g_ref, o_ref,
                 s_scr, mx_scr, *, tq, rc, n_tiles, tiles_per_head):
    t = pl.program_id(0)
    hd = v_ref.shape[2]

    @pl.when(t == 0)
    def _():
        s_scr[...] = jnp.zeros_like(s_scr)
        mx_scr[...] = jnp.zeros_like(mx_scr)

    def tick(slot):
        other = 1 - slot
        i = jnp.minimum(t, n_tiles - 1) % tiles_per_head
        nk = k_ref.shape[1] // tq
        q = (q_ref[0].astype(F32) * (DIFF_QK ** -0.5 * LOG2E)).astype(BF16)
        lane = lax.broadcasted_iota(jnp.int32, q.shape, 1)
        zero = jnp.zeros_like(q)
        qs = jnp.concatenate([jnp.where(lane < DIFF_QK, q, zero), jnp.where(lane >= DIFF_QK, q, zero)], axis=0)
        lanes = mx_scr.shape[-1]
        for j in range(nk):
            cols = slice(j * tq, (j + 1) * tq)
            bias = band_ref[0, jnp.clip(j - i, -2, 2) + 2]
            s = _dot_nt(qs, k_ref[0, cols, :])
            for half in (slice(0, tq), slice(tq, 2 * tq)):
                sb = s[half] + bias
                s_scr[slot, half, cols] = sb
                fold = functools.reduce(jnp.maximum, [sb[:, c:c + lanes] for c in range(0, tq, lanes)])
                mx_scr[slot, half, :] = fold if j == 0 else jnp.maximum(mx_scr[slot, half, :], fold)

        nchunk = 2 * tq // rc
        chunk = lambda r: slice(r * rc, (r + 1) * rc)
        p = jnp.concatenate(
            [jnp.exp2(s_scr[other, chunk(r), :] - jnp.max(mx_scr[other, chunk(r), :], axis=-1, keepdims=True))
             .astype(BF16) for r in range(nchunk)], axis=0)
        v_ones = jnp.concatenate([v_ref[0], jnp.ones(v_ref.shape[1:], BF16)], axis=1)
        pv = _dot(p, v_ones)
        lam = (jnp.exp(jnp.sum(lq1_ref[...] * lk1_ref[...], axis=-1, keepdims=True))
               - jnp.exp(jnp.sum(lq2_ref[...] * lk2_ref[...], axis=-1, keepdims=True)) + LAMBDA_INIT)
        o = pv[0:tq, 0:hd] * (1.0 / pv[0:tq, hd:]) - pv[tq:, 0:hd] * (lam / pv[tq:, hd:])
        ms = jnp.mean(o * o, axis=-1, keepdims=True)
        o_ref[0] = (o * lax.rsqrt(ms + LN_EPS) * g_ref[...] * (1.0 - LAMBDA_INIT)).astype(BF16)

    pl.when(t % 2 == 0)(functools.partial(tick, 0))
    pl.when(t % 2 == 1)(functools.partial(tick, 1))


def _diff_attn(proj, band, lq1, lk1, lq2, lk2, g, tq=256, rc=32):
    b, s, _ = proj.shape
    cb = HEAD_DIM
    ni = s // tq
    n_tiles = b * DIFF_HEADS * ni
    depth = 1

    def tile(t, lag):
        tc = jnp.clip(t - lag, 0, n_tiles - 1)
        return tc // (DIFF_HEADS * ni), (tc // ni) % DIFF_HEADS, tc % ni

    def q_map(t):
        bi, h, i = tile(t, 0)
        return bi, i, OFF_DQ // cb + h

    def k_map(t):
        bi, h, _ = tile(t, 0)
        return bi, 0, OFF_DK // cb + h

    def v_map(t):
        bi, h, _ = tile(t, depth)
        return bi, 0, OFF_DV // cb + h

    def o_map(t):
        bi, h, i = tile(t, depth)
        return bi, i, h

    vec = lambda n: pl.BlockSpec((1, n), lambda t: (0, 0))
    return pl.pallas_call(
        functools.partial(_diff_kernel, tq=tq, rc=rc, n_tiles=n_tiles, tiles_per_head=ni),
        grid=(n_tiles + depth,),
        in_specs=[pl.BlockSpec((1, tq, cb), q_map),
                  pl.BlockSpec((1, s, cb), k_map),
                  pl.BlockSpec((1, s, cb), v_map),
                  pl.BlockSpec((1, 5, tq, tq), lambda t: (tile(t, 0)[1], 0, 0, 0)),
                  vec(DIFF_QK), vec(DIFF_QK), vec(DIFF_QK), vec(DIFF_QK), vec(HEAD_DIM)],
        out_specs=pl.BlockSpec((1, tq, cb), o_map),
        out_shape=jax.ShapeDtypeStruct((b, s, BRANCH_W), BF16),
        scratch_shapes=[pltpu.VMEM((2, 2 * tq, s), F32), pltpu.VMEM((2, 2 * tq, cb), F32)],
        compiler_params=_params("arbitrary"),
        name="diff_attn",
    )(proj, proj, proj, band, lq1, lk1, lq2, lk2, g)


def _win_kernel(sink_ref, q_ref, kp_ref, km_ref, kn_ref, vp_ref, vm_ref, vn_ref, bias_ref, o_ref, *, tq):
    g = pl.program_id(1)
    i = pl.program_id(2)
    first = i == 0
    last = i == pl.num_programs(2) - 1
    w = WINDOW
    hd = HEAD_DIM
    nkeys = tq + 2 * w
    keys = jnp.concatenate([kp_ref[0], km_ref[0], kn_ref[0]], axis=0)
    vals = jnp.concatenate([vp_ref[0], vm_ref[0], vn_ref[0]], axis=0)
    v_ones = jnp.concatenate([vals, jnp.ones((nkeys, hd), BF16)], axis=1)
    col = lax.broadcasted_iota(jnp.int32, (1, nkeys), 1)
    outside = jnp.logical_or(jnp.logical_and(first, col < w),
                             jnp.logical_and(last, col >= tq + w))
    scale = jnp.asarray(hd ** -0.5 * LOG2E, F32)
    for j in range(WIN_GROUP):
        cols = slice(j * hd, (j + 1) * hd)
        s = _dot_nt(q_ref[0, :, cols], keys) * scale + bias_ref[j]
        s = jnp.where(outside, NEG, s)
        sink = sink_ref[g * WIN_GROUP + j]
        fold = functools.reduce(jnp.maximum, [s[:, c:c + w] for c in range(0, nkeys, w)])
        m = jnp.maximum(jnp.max(fold, axis=-1, keepdims=True), sink)
        pv = _dot(jnp.exp2(s - m).astype(BF16), v_ones)
        o_ref[0, :, cols] = (pv[:, 0:hd] / (pv[:, hd:] + jnp.exp2(sink - m))).astype(BF16)


def _win_attn(proj, bias, sink, tq=256):
    b, s, _ = proj.shape
    hd = HEAD_DIM
    nb = s // WINDOW
    r = tq // WINDOW
    prev = lambda bi, g, i: jnp.maximum(i * r - 1, 0)
    nxt = lambda bi, g, i: jnp.minimum((i + 1) * r, nb - 1)
    kcol, vcol = OFF_WK // hd, OFF_WV // hd
    return pl.pallas_call(
        functools.partial(_win_kernel, tq=tq),
        grid=(b, WIN_KV_HEADS, s // tq),
        in_specs=[pl.BlockSpec(memory_space=pltpu.SMEM),
                  pl.BlockSpec((1, tq, WIN_GROUP * hd), lambda bi, g, i: (bi, i, OFF_WQ // (WIN_GROUP * hd) + g)),
                  pl.BlockSpec((1, WINDOW, hd), lambda bi, g, i: (bi, prev(bi, g, i), kcol + g)),
                  pl.BlockSpec((1, tq, hd), lambda bi, g, i: (bi, i, kcol + g)),
                  pl.BlockSpec((1, WINDOW, hd), lambda bi, g, i: (bi, nxt(bi, g, i), kcol + g)),
                  pl.BlockSpec((1, WINDOW, hd), lambda bi, g, i: (bi, prev(bi, g, i), vcol + g)),
                  pl.BlockSpec((1, tq, hd), lambda bi, g, i: (bi, i, vcol + g)),
                  pl.BlockSpec((1, WINDOW, hd), lambda bi, g, i: (bi, nxt(bi, g, i), vcol + g)),
                  pl.BlockSpec((WIN_GROUP, tq, tq + 2 * WINDOW), lambda bi, g, i: (g, 0, 0))],
        out_specs=pl.BlockSpec((1, tq, WIN_GROUP * hd), lambda bi, g, i: (bi, i, g)),
        out_shape=jax.ShapeDtypeStruct((b, s, BRANCH_W), BF16),
        compiler_params=_params("parallel", "parallel", "arbitrary"),
        name="win_attn",
    )(sink, proj, proj, proj, proj, proj, proj, proj, bias)


def _mem_attn_kernel(q_ref, k_ref, v_ref, o_ref):
    s = _dot_nt(q_ref[0], k_ref[0]) * jnp.asarray(MEM_DIM ** -0.5, F32)
    p = jnp.exp(s - jnp.max(s, axis=-1, keepdims=True))
    p = p * (1.0 / jnp.sum(p, axis=-1, keepdims=True))
    o_ref[0] = _dot(p.astype(BF16), v_ref[0]).astype(BF16)


def _mem_attn(proj, memkv, tq=1024):
    b, s, _ = proj.shape
    md = MEM_DIM
    return pl.pallas_call(
        _mem_attn_kernel,
        grid=(b, MEM_HEADS, s // tq),
        in_specs=[pl.BlockSpec((1, tq, md), lambda bi, h, i: (bi, i, OFF_MQ // md + h)),
                  pl.BlockSpec((1, N_MEM, md), lambda bi, h, i: (bi, 0, h)),
                  pl.BlockSpec((1, N_MEM, md), lambda bi, h, i: (bi, 0, MEM_HEADS + h))],
        out_specs=pl.BlockSpec((1, tq, md), lambda bi, h, i: (bi, i, h)),
        out_shape=jax.ShapeDtypeStruct((b, s, BRANCH_W), BF16),
        compiler_params=_params("parallel", "parallel", "arbitrary"),
        name="mem_attn",
    )(proj, memkv, memkv)


def _mix_kernel(x_ref, h_ref, lg_ref, lb_ref, a_ref, b_ref, c_ref, wg0_ref, wg1_ref, wg2_ref,
                bg0_ref, bg1_ref, bg2_ref, wb_ref, wo_ref, g1_ref, b1_ref, o_ref, acc_scr):
    n = pl.program_id(1)
    nrow = x_ref.shape[0] // LN_ROWS

    @pl.when(n == 0)
    def _():
        acc_scr[...] = jnp.zeros_like(acc_scr)

    h = h_ref[...]
    gates = [jax.nn.sigmoid(_dot(h, wg_ref[...]) + bg_ref[...])
             for wg_ref, bg_ref in ((wg0_ref, bg0_ref), (wg1_ref, bg1_ref), (wg2_ref, bg2_ref))]
    mixed = None
    for k, br_ref in enumerate((a_ref, b_ref, c_ref)):
        term = gates[k] * _dot(br_ref[...], wb_ref[k])
        mixed = term if mixed is None else mixed + term
    acc_scr[...] += _dot(mixed.astype(BF16), wo_ref[...])

    @pl.when(n == pl.num_programs(1) - 1)
    def _():
        def body(r, c):
            rows = pl.ds(pl.multiple_of(r * LN_ROWS, LN_ROWS), LN_ROWS)
            hh = _ln_rows(x_ref[rows, :], lg_ref[...], lb_ref[...])
            o_ref[rows, :] = _ln_rows(ALPHA * hh + acc_scr[rows, :], g1_ref[...], b1_ref[...])
            return c
        lax.fori_loop(0, nrow, body, 0)


def _mix(x2, hb, lg, lb, a, b, c, wg, bg, wb, wo, g1, b1, tm=512, tn=256):
    m, d = x2.shape
    bw = a.shape[1]
    nn = d // tn
    row = lambda w: pl.BlockSpec((tm, w), lambda i, n: (i, 0))
    vec = pl.BlockSpec((1, d), lambda i, n: (0, 0))
    wgs = [pl.BlockSpec((d, tn), functools.partial(lambda i, n, k: (0, k * nn + n), k=k)) for k in range(3)]
    bgs = [pl.BlockSpec((1, tn), functools.partial(lambda i, n, k: (0, k * nn + n), k=k)) for k in range(3)]
    return pl.pallas_call(
        _mix_kernel,
        grid=(m // tm, nn),
        in_specs=[row(d), row(d), vec, vec, row(bw), row(bw), row(bw), *wgs, *bgs,
                  pl.BlockSpec((3, bw, tn), lambda i, n: (0, 0, n)),
                  pl.BlockSpec((tn, d), lambda i, n: (n, 0)), vec, vec],
        out_specs=pl.BlockSpec((tm, d), lambda i, n: (i, 0)),
        out_shape=jax.ShapeDtypeStruct((m, d), F32),
        scratch_shapes=[pltpu.VMEM((tm, d), F32)],
        compiler_params=_params("parallel", "arbitrary"),
        name="mix",
    )(x2, hb, lg, lb, a, b, c, wg, wg, wg, bg, bg, bg, wb, wo, g1, b1)


HALO = 16


def _gelu_tanh(x):
    return 0.5 * x * (1.0 + jnp.tanh(math.sqrt(2.0 / math.pi) * (x + 0.044715 * (x * x * x))))


def _ffn_kernel(h_ref, hp_ref, hn_ref, wv_ref, wg_ref, cwv_ref, cwg_ref, cbv_ref, cbg_ref, wd_ref,
                g2_ref, b2_ref, o_ref, hx_scr, acc_scr, *, tiles_per_seq):
    i = pl.program_id(0)
    f = pl.program_id(1)
    tm = h_ref.shape[0]
    nrow = tm // LN_ROWS

    @pl.when(f == 0)
    def _():
        seq_first = (i % tiles_per_seq) == 0
        seq_last = (i % tiles_per_seq) == tiles_per_seq - 1
        hx_scr[0:HALO, :] = jnp.where(seq_first, 0.0, hp_ref[...]).astype(BF16)
        hx_scr[HALO:HALO + tm, :] = h_ref[...].astype(BF16)
        hx_scr[HALO + tm:, :] = jnp.where(seq_last, 0.0, hn_ref[...]).astype(BF16)
        acc_scr[...] = jnp.zeros_like(acc_scr)

    hx = hx_scr[...]
    ext = tm + 2 * HALO

    def conv(u, cw_ref, cb_ref):
        prev = pltpu.roll(u, 1, 0)[HALO:HALO + tm]
        nxt = pltpu.roll(u, ext - 1, 0)[HALO:HALO + tm]
        return prev * cw_ref[0:1, :] + u[HALO:HALO + tm] * cw_ref[1:2, :] + nxt * cw_ref[2:3, :] + cb_ref[...]

    gelu_gate = _gelu_tanh(conv(_dot(hx, wg_ref[...]), cwg_ref, cbg_ref))
    val = conv(_dot(hx, wv_ref[...]), cwv_ref, cbv_ref)
    acc_scr[...] += _dot((gelu_gate * val).astype(BF16), wd_ref[...])

    @pl.when(f == pl.num_programs(1) - 1)
    def _():
        def body(r, c):
            rows = pl.ds(pl.multiple_of(r * LN_ROWS, LN_ROWS), LN_ROWS)
            o_ref[rows, :] = _ln_rows(ALPHA * h_ref[rows, :] + acc_scr[rows, :], g2_ref[...], b2_ref[...])
            return c
        lax.fori_loop(0, nrow, body, 0)


def _ffn(h1, w_up, cw, cb, w_down, g2, b2, seq, tm=512, tf=512):
    m, d = h1.shape
    nf = D_FF_PAD // tf
    hb = tm // HALO
    nhb = m // HALO
    vec = pl.BlockSpec((1, d), lambda i, f: (0, 0))
    return pl.pallas_call(
        functools.partial(_ffn_kernel, tiles_per_seq=seq // tm),
        grid=(m // tm, nf),
        in_specs=[pl.BlockSpec((tm, d), lambda i, f: (i, 0)),
                  pl.BlockSpec((HALO, d), lambda i, f: (jnp.maximum(i * hb - 1, 0), 0)),
                  pl.BlockSpec((HALO, d), lambda i, f: (jnp.minimum((i + 1) * hb, nhb - 1), 0)),
                  pl.BlockSpec((d, tf), lambda i, f: (0, f)),
                  pl.BlockSpec((d, tf), lambda i, f: (0, nf + f)),
                  pl.BlockSpec((3, tf), lambda i, f: (0, f)),
                  pl.BlockSpec((3, tf), lambda i, f: (0, nf + f)),
                  pl.BlockSpec((1, tf), lambda i, f: (0, f)),
                  pl.BlockSpec((1, tf), lambda i, f: (0, nf + f)),
                  pl.BlockSpec((tf, d), lambda i, f: (f, 0)), vec, vec],
        out_specs=pl.BlockSpec((tm, d), lambda i, f: (i, 0)),
        out_shape=jax.ShapeDtypeStruct((m, d), F32),
        scratch_shapes=[pltpu.VMEM((tm + 2 * HALO, d), BF16), pltpu.VMEM((tm, d), F32)],
        compiler_params=_params("parallel", "arbitrary"),
        name="ffn",
    )(h1, h1, h1, w_up, w_up, cw, cw, cb, cb, w_down, g2, b2)


def _w_up_prep_kernel(w_ref, o_ref):
    o_ref[:, 0:D_FF] = w_ref[...].astype(BF16)
    o_ref[:, D_FF:] = jnp.zeros((o_ref.shape[0], D_FF_PAD - D_FF), BF16)


def _w_up_prep(w_up, tr=256):
    d = w_up.shape[0]
    return pl.pallas_call(
        _w_up_prep_kernel,
        grid=(2, d // tr),
        in_specs=[pl.BlockSpec((tr, D_FF), lambda half, r: (r, half))],
        out_specs=pl.BlockSpec((tr, D_FF_PAD), lambda half, r: (r, half)),
        out_shape=jax.ShapeDtypeStruct((d, 2 * D_FF_PAD), BF16),
        compiler_params=_params("parallel", "parallel"),
        name="w_up_prep",
    )(w_up)


def _pad_ff(t, axis):
    val, gate = jnp.split(t, 2, axis=axis)
    pad = [(0, 0)] * t.ndim
    pad[axis] = (0, D_FF_PAD - D_FF)
    return jnp.concatenate([jnp.pad(val, pad), jnp.pad(gate, pad)], axis=axis)


def kernel(x, mem, ln_in_g, ln_in_b, rel_table, w_in, w_mem_kv, diff_lq1, diff_lk1, diff_lq2, diff_lk2,
           diff_subln_g, win_sink, w_gate, b_gate, w_branch, w_o, ln1_g, ln1_b, w_up, conv_w, conv_b,
           w_down, ln2_g, ln2_b):
    assert w_in.shape[0] == DEPTH == 1
    bsz, seq, d = x.shape
    x2 = x.reshape(bsz * seq, d)
    row = lambda v: v.reshape(1, -1).astype(F32)
    l = 0
    proj, hb = _ln_proj(x2, row(ln_in_g), row(ln_in_b), w_in[l])
    proj = proj.reshape(bsz, seq, IN_W)
    memkv = _mem_kv(mem.reshape(bsz * N_MEM, d), w_mem_kv[l]).reshape(bsz, N_MEM, 2 * BRANCH_W)

    tq_d, tq_w = 256, 256
    a = _diff_attn(proj, _diff_bias_tiles(rel_table[:, :DIFF_HEADS] * LOG2E, tq_d), row(diff_lq1[l]), row(diff_lk1[l]),
                   row(diff_lq2[l]), row(diff_lk2[l]), row(diff_subln_g[l]), tq=tq_d)
    b = _win_attn(proj, _win_bias_tiles(rel_table[:, DIFF_HEADS:] * LOG2E, tq_w), win_sink[l].astype(F32) * LOG2E,
                  tq=tq_w)
    c = _mem_attn(proj, memkv)

    m = bsz * seq
    h1 = _mix(x2, hb, row(ln_in_g), row(ln_in_b), a.reshape(m, -1), b.reshape(m, -1), c.reshape(m, -1),
              w_gate[l].astype(BF16), row(b_gate[l]), w_branch[l].astype(BF16), w_o[l].astype(BF16),
              row(ln1_g[l]), row(ln1_b[l]))

    w_up_p = _w_up_prep(w_up[l].astype(F32))
    cw_p = _pad_ff(conv_w[l].astype(F32), 1)
    cb_p = _pad_ff(row(conv_b[l]), 1)
    w_down_p = jnp.pad(w_down[l], ((0, D_FF_PAD - D_FF), (0, 0))).astype(BF16)
    out = _ffn(h1, w_up_p, cw_p, cb_p, w_down_p, row(ln2_g[l]), row(ln2_b[l]), seq)
    return out.reshape(bsz, seq, d)
```

```python
import functools
import math

import jax
import jax.numpy as jnp
from jax import lax
from jax.experimental import pallas as pl
from jax.experimental.pallas import tpu as pltpu

F32 = jnp.float32
BF16 = jnp.bfloat16

D_MODEL = 2048
SEQ = 2048
N_MEM = 256
HEAD_DIM = 128
BRANCH_W = 1024
DIFF_HEADS = 8
DIFF_QK = 64
WIN_HEADS = 8
WIN_KV_HEADS = 2
WIN_GROUP = WIN_HEADS // WIN_KV_HEADS
WINDOW = 128
MEM_HEADS = 4
MEM_DIM = 256
OFF_DQ, OFF_DK, OFF_DV, OFF_WQ, OFF_WK, OFF_WV, OFF_MQ = 0, 1024, 2048, 3072, 4096, 4352, 4608
IN_W = 5632
D_FF = 5504
D_FF_PAD = 5632
REL_BUCKETS = 32
REL_MAX_DIST = 128
DEPTH = 1
ALPHA = (2 * DEPTH) ** 0.25
LN_EPS = 1e-5
NEG = -1e30
LOG2E = math.log2(math.e)
LAMBDA_INIT = 0.8 - 0.6 * math.exp(-0.3 * 0)

VMEM_LIMIT = 56 * 1024 * 1024
LN_ROWS = 128


def _ln_rows(x, g, b):
    mu = jnp.mean(x, axis=-1, keepdims=True)
    xc = x - mu
    var = jnp.mean(xc * xc, axis=-1, keepdims=True)
    return xc * lax.rsqrt(var + LN_EPS) * g + b


def _dot(a, b):
    return jnp.dot(a, b, preferred_element_type=F32)


def _dot_nt(a, b):
    return lax.dot_general(a, b, (((1,), (1,)), ((), ())), preferred_element_type=F32)


def _params(*sem):
    return pltpu.CompilerParams(dimension_semantics=sem, vmem_limit_bytes=VMEM_LIMIT)


def _ln_proj_kernel(x_ref, g_ref, b_ref, w_ref, o_ref, h_ref):
    @pl.when(pl.program_id(1) == 0)
    def _():
        def body(r, c):
            rows = pl.ds(pl.multiple_of(r * LN_ROWS, LN_ROWS), LN_ROWS)
            h_ref[rows, :] = _ln_rows(x_ref[rows, :], g_ref[...], b_ref[...]).astype(BF16)
            return c
        lax.fori_loop(0, x_ref.shape[0] // LN_ROWS, body, 0)

    o_ref[...] = _dot(h_ref[...], w_ref[...].astype(BF16)).astype(BF16)


def _ln_proj(x2, g, b, w, tm=1024, tn=512):
    m, d = x2.shape
    n = w.shape[1]
    return pl.pallas_call(
        _ln_proj_kernel,
        grid=(m // tm, n // tn),
        in_specs=[pl.BlockSpec((tm, d), lambda i, j: (i, 0)),
                  pl.BlockSpec((1, d), lambda i, j: (0, 0)),
                  pl.BlockSpec((1, d), lambda i, j: (0, 0)),
                  pl.BlockSpec((d, tn), lambda i, j: (0, j))],
        out_specs=[pl.BlockSpec((tm, tn), lambda i, j: (i, j)),
                   pl.BlockSpec((tm, d), lambda i, j: (i, 0))],
        out_shape=[jax.ShapeDtypeStruct((m, n), BF16), jax.ShapeDtypeStruct((m, d), BF16)],
        compiler_params=_params("parallel", "arbitrary"),
        name="ln_proj",
    )(x2, g, b, w)


def _mem_kv_kernel(m_ref, w_ref, o_ref):
    o_ref[...] = _dot(m_ref[...].astype(BF16), w_ref[...].astype(BF16)).astype(BF16)


def _mem_kv(mem2, w, tn=512):
    m, d = mem2.shape
    n = w.shape[1]
    return pl.pallas_call(
        _mem_kv_kernel,
        grid=(n // tn,),
        in_specs=[pl.BlockSpec((m, d), lambda j: (0, 0)),
                  pl.BlockSpec((d, tn), lambda j: (0, j))],
        out_specs=pl.BlockSpec((m, tn), lambda j: (0, j)),
        out_shape=jax.ShapeDtypeStruct((m, n), BF16),
        compiler_params=_params("parallel"),
        name="mem_kv",
    )(mem2, w)


def _bucket_lower_bounds():
    half = REL_BUCKETS // 2
    max_exact = half // 2
    n = jnp.arange(REL_MAX_DIST + 1)
    log_part = (jnp.log(jnp.maximum(n, 1).astype(F32) / max_exact) / math.log(REL_MAX_DIST / max_exact)
                * (half - max_exact)).astype(jnp.int32)
    bucket = jnp.where(n < max_exact, n, jnp.minimum(max_exact + log_part, half - 1))
    bucket = jnp.arange(half, dtype=jnp.int32)[bucket]
    ids = jnp.arange(half, dtype=jnp.int32)
    return jnp.sum((bucket[None, :] < ids[:, None]).astype(jnp.int32), axis=1)


BIAS_ROWS = 64


def _bias_kernel(lb_ref, tab_ref, off_ref, o_ref, *, limit):
    h = pl.program_id(0)
    half = REL_BUCKETS // 2
    ncol = o_ref.shape[2]

    def body(r, carry):
        rows = pl.ds(pl.multiple_of(r * BIAS_ROWS, BIAS_ROWS), BIAS_ROWS)
        rel = lax.broadcasted_iota(jnp.int32, (BIAS_ROWS, ncol), 1) + off_ref[rows, :]
        n = jnp.abs(rel)
        neg = jnp.full(rel.shape, tab_ref[0, h], F32)
        pos = jnp.full(rel.shape, tab_ref[half, h], F32)
        for b in range(1, half):
            reached = n >= lb_ref[b]
            neg = jnp.where(reached, tab_ref[b, h], neg)
            pos = jnp.where(reached, tab_ref[half + b, h], pos)
        out = jnp.where(rel > 0, pos, neg)
        if limit is not None:
            out = jnp.where(n > limit, NEG, out)
        o_ref[0, rows, :] = out
        return carry
    lax.fori_loop(0, o_ref.shape[1] // BIAS_ROWS, body, 0)


def _bias_tiles(table, row_offset, ncol, limit=None):
    nh = table.shape[1]
    nrow = row_offset.shape[0]
    return pl.pallas_call(
        functools.partial(_bias_kernel, limit=limit),
        grid=(nh,),
        in_specs=[pl.BlockSpec(memory_space=pltpu.SMEM), pl.BlockSpec(memory_space=pltpu.SMEM),
                  pl.BlockSpec((nrow, 1), lambda h: (0, 0))],
        out_specs=pl.BlockSpec((1, nrow, ncol), lambda h: (h, 0, 0)),
        out_shape=jax.ShapeDtypeStruct((nh, nrow, ncol), F32),
        compiler_params=_params("parallel"),
        name="bias_tiles",
    )(_bucket_lower_bounds(), table.astype(F32), row_offset.astype(jnp.int32).reshape(nrow, 1))


def _diff_bias_tiles(table, tq):
    r = jnp.arange(5 * tq)
    return _bias_tiles(table, (r // tq - 2) * tq - r % tq, tq).reshape(table.shape[1], 5, tq, tq)


def _win_bias_tiles(table, tq):
    return _bias_tiles(table, -WINDOW - jnp.arange(tq), tq + 2 * WINDOW, limit=WINDOW)


def _diff_kernel(q_ref, k_ref, v_ref, band_ref, lq1_ref, lk1_ref, lq2_ref, lk2_ref, g_ref, o_ref,
                 s_scr, mx_scr, *, tq, rc, n_tiles, tiles_per_head):
    t = pl.program_id(0)
    hd = v_ref.shape[2]

    @pl.when(t == 0)
    def _():
        s_scr[...] = jnp.zeros_like(s_scr)
        mx_scr[...] = jnp.zeros_like(mx_scr)

    def tick(slot):
        other = 1 - slot
        i = jnp.minimum(t, n_tiles - 1) % tiles_per_head
        nk = k_ref.shape[1] // tq
        q = (q_ref[0].astype(F32) * (DIFF_QK ** -0.5 * LOG2E)).astype(BF16)
        lane = lax.broadcasted_iota(jnp.int32, q.shape, 1)
        zero = jnp.zeros_like(q)
        qs = jnp.concatenate([jnp.where(lane < DIFF_QK, q, zero), jnp.where(lane >= DIFF_QK, q, zero)], axis=0)
        lanes = mx_scr.shape[-1]
        for j in range(nk):
            cols = slice(j * tq, (j + 1) * tq)
            bias = band_ref[0, jnp.clip(j - i, -2, 2) + 2]
            s = _dot_nt(qs, k_ref[0, cols, :])
            for half in (slice(0, tq), slice(tq, 2 * tq)):
                sb = s[half] + bias
                s_scr[slot, half, cols] = sb
                fold = functools.reduce(jnp.maximum, [sb[:, c:c + lanes] for c in range(0, tq, lanes)])
                mx_scr[slot, half, :] = fold if j == 0 else jnp.maximum(mx_scr[slot, half, :], fold)

        nchunk = 2 * tq // rc
        chunk = lambda r: slice(r * rc, (r + 1) * rc)
        p = jnp.concatenate(
            [jnp.exp2(s_scr[other, chunk(r), :] - jnp.max(mx_scr[other, chunk(r), :], axis=-1, keepdims=True))
             .astype(BF16) for r in range(nchunk)], axis=0)
        v_ones = jnp.concatenate([v_ref[0], jnp.ones(v_ref.shape[1:], BF16)], axis=1)
        pv = _dot(p, v_ones)
        lam = (jnp.exp(jnp.sum(lq1_ref[...] * lk1_ref[...], axis=-1, keepdims=True))
               - jnp.exp(jnp.sum(lq2_ref[...] * lk2_ref[...], axis=-1, keepdims=True)) + LAMBDA_INIT)
        o = pv[0:tq, 0:hd] * (1.0 / pv[0:tq, hd:]) - pv[tq:, 0:hd] * (lam / pv[tq:, hd:])
        ms = jnp.mean(o * o, axis=-1, keepdims=True)
        o_ref[0] = (o * lax.rsqrt(ms + LN_EPS) * g_ref[...] * (1.0 - LAMBDA_INIT)).astype(BF16)

    pl.when(t % 2 == 0)(functools.partial(tick, 0))
    pl.when(t % 2 == 1)(functools.partial(tick, 1))


def _diff_attn(proj, band, lq1, lk1, lq2, lk2, g, tq=256, rc=32):
    b, s, _ = proj.shape
    cb = HEAD_DIM
    ni = s // tq
    n_tiles = b * DIFF_HEADS * ni
    depth = 1

    def tile(t, lag):
        tc = jnp.clip(t - lag, 0, n_tiles - 1)
        return tc // (DIFF_HEADS * ni), (tc // ni) % DIFF_HEADS, tc % ni

    def q_map(t):
        bi, h, i = tile(t, 0)
        return bi, i, OFF_DQ // cb + h

    def k_map(t):
        bi, h, _ = tile(t, 0)
        return bi, 0, OFF_DK // cb + h

    def v_map(t):
        bi, h, _ = tile(t, depth)
        return bi, 0, OFF_DV // cb + h

    def o_map(t):
        bi, h, i = tile(t, depth)
        return bi, i, h

    vec = lambda n: pl.BlockSpec((1, n), lambda t: (0, 0))
    return pl.pallas_call(
        functools.partial(_diff_kernel, tq=tq, rc=rc, n_tiles=n_tiles, tiles_per_head=ni),
        grid=(n_tiles + depth,),
        in_specs=[pl.BlockSpec((1, tq, cb), q_map),
                  pl.BlockSpec((1, s, cb), k_map),
                  pl.BlockSpec((1, s, cb), v_map),
                  pl.BlockSpec((1, 5, tq, tq), lambda t: (tile(t, 0)[1], 0, 0, 0)),
                  vec(DIFF_QK), vec(DIFF_QK), vec(DIFF_QK), vec(DIFF_QK), vec(HEAD_DIM)],
        out_specs=pl.BlockSpec((1, tq, cb), o_map),
        out_shape=jax.ShapeDtypeStruct((b, s, BRANCH_W), BF16),
        scratch_shapes=[pltpu.VMEM((2, 2 * tq, s), F32), pltpu.VMEM((2, 2 * tq, cb), F32)],
        compiler_params=_params("arbitrary"),
        name="diff_attn",
    )(proj, proj, proj, band, lq1, lk1, lq2, lk2, g)


def _win_kernel(sink_ref, q_ref, kp_ref, km_ref, kn_ref, vp_ref, vm_ref, vn_ref, bias_ref, o_ref, *, tq):
    g = pl.program_id(1)
    i = pl.program_id(2)
    first = i == 0
    last = i == pl.num_programs(2) - 1
    w = WINDOW
    hd = HEAD_DIM
    nkeys = tq + 2 * w
    keys = jnp.concatenate([kp_ref[0], km_ref[0], kn_ref[0]], axis=0)
    vals = jnp.concatenate([vp_ref[0], vm_ref[0], vn_ref[0]], axis=0)
    v_ones = jnp.concatenate([vals, jnp.ones((nkeys, hd), BF16)], axis=1)
    col = lax.broadcasted_iota(jnp.int32, (1, nkeys), 1)
    outside = jnp.logical_or(jnp.logical_and(first, col < w),
                             jnp.logical_and(last, col >= tq + w))
    scale = jnp.asarray(hd ** -0.5 * LOG2E, F32)
    for j in range(WIN_GROUP):
        cols = slice(j * hd, (j + 1) * hd)
        s = _dot_nt(q_ref[0, :, cols], keys) * scale + bias_ref[j]
        s = jnp.where(outside, NEG, s)
        sink = sink_ref[g * WIN_GROUP + j]
        fold = functools.reduce(jnp.maximum, [s[:, c:c + w] for c in range(0, nkeys, w)])
        m = jnp.maximum(jnp.max(fold, axis=-1, keepdims=True), sink)
        pv = _dot(jnp.exp2(s - m).astype(BF16), v_ones)
        o_ref[0, :, cols] = (pv[:, 0:hd] / (pv[:, hd:] + jnp.exp2(sink - m))).astype(BF16)


def _win_attn(proj, bias, sink, tq=256):
    b, s, _ = proj.shape
    hd = HEAD_DIM
    nb = s // WINDOW
    r = tq // WINDOW
    prev = lambda bi, g, i: jnp.maximum(i * r - 1, 0)
    nxt = lambda bi, g, i: jnp.minimum((i + 1) * r, nb - 1)
    kcol, vcol = OFF_WK // hd, OFF_WV // hd
    return pl.pallas_call(
        functools.partial(_win_kernel, tq=tq),
        grid=(b, WIN_KV_HEADS, s // tq),
        in_specs=[pl.BlockSpec(memory_space=pltpu.SMEM),
                  pl.BlockSpec((1, tq, WIN_GROUP * hd), lambda bi, g, i: (bi, i, OFF_WQ // (WIN_GROUP * hd) + g)),
                  pl.BlockSpec((1, WINDOW, hd), lambda bi, g, i: (bi, prev(bi, g, i), kcol + g)),
                  pl.BlockSpec((1, tq, hd), lambda bi, g, i: (bi, i, kcol + g)),
                  pl.BlockSpec((1, WINDOW, hd), lambda bi, g, i: (bi, nxt(bi, g, i), kcol + g)),
                  pl.BlockSpec((1, WINDOW, hd), lambda bi, g, i: (bi, prev(bi, g, i), vcol + g)),
                  pl.BlockSpec((1, tq, hd), lambda bi, g, i: (bi, i, vcol + g)),
                  pl.BlockSpec((1, WINDOW, hd), lambda bi, g, i: (bi, nxt(bi, g, i), vcol + g)),
                  pl.BlockSpec((WIN_GROUP, tq, tq + 2 * WINDOW), lambda bi, g, i: (g, 0, 0))],
        out_specs=pl.BlockSpec((1, tq, WIN_GROUP * hd), lambda bi, g, i: (bi, i, g)),
        out_shape=jax.ShapeDtypeStruct((b, s, BRANCH_W), BF16),
        compiler_params=_params("parallel", "parallel", "arbitrary"),
        name="win_attn",
    )(sink, proj, proj, proj, proj, proj, proj, proj, bias)


def _mem_attn_kernel(q_ref, k_ref, v_ref, o_ref):
    s = _dot_nt(q_ref[0], k_ref[0]) * jnp.asarray(MEM_DIM ** -0.5, F32)
    p = jnp.exp(s - jnp.max(s, axis=-1, keepdims=True))
    p = p * (1.0 / jnp.sum(p, axis=-1, keepdims=True))
    o_ref[0] = _dot(p.astype(BF16), v_ref[0]).astype(BF16)


def _mem_attn(proj, memkv, tq=1024):
    b, s, _ = proj.shape
    md = MEM_DIM
    return pl.pallas_call(
        _mem_attn_kernel,
        grid=(b, MEM_HEADS, s // tq),
        in_specs=[pl.BlockSpec((1, tq, md), lambda bi, h, i: (bi, i, OFF_MQ // md + h)),
                  pl.BlockSpec((1, N_MEM, md), lambda bi, h, i: (bi, 0, h)),
                  pl.BlockSpec((1, N_MEM, md), lambda bi, h, i: (bi, 0, MEM_HEADS + h))],
        out_specs=pl.BlockSpec((1, tq, md), lambda bi, h, i: (bi, i, h)),
        out_shape=jax.ShapeDtypeStruct((b, s, BRANCH_W), BF16),
        compiler_params=_params("parallel", "parallel", "arbitrary"),
        name="mem_attn",
    )(proj, memkv, memkv)


def _mix_kernel(x_ref, h_ref, lg_ref, lb_ref, a_ref, b_ref, c_ref, wg0_ref, wg1_ref, wg2_ref,
                bg0_ref, bg1_ref, bg2_ref, wb_ref, wo_ref, g1_ref, b1_ref, o_ref, acc_scr):
    n = pl.program_id(1)
    nrow = x_ref.shape[0] // LN_ROWS

    @pl.when(n == 0)
    def _():
        acc_scr[...] = jnp.zeros_like(acc_scr)

    h = h_ref[...]
    gates = [jax.nn.sigmoid(_dot(h, wg_ref[...]) + bg_ref[...])
             for wg_ref, bg_ref in ((wg0_ref, bg0_ref), (wg1_ref, bg1_ref), (wg2_ref, bg2_ref))]
    mixed = None
    for k, br_ref in enumerate((a_ref, b_ref, c_ref)):
        term = gates[k] * _dot(br_ref[...], wb_ref[k])
        mixed = term if mixed is None else mixed + term
    acc_scr[...] += _dot(mixed.astype(BF16), wo_ref[...])

    @pl.when(n == pl.num_programs(1) - 1)
    def _():
        def body(r, c):
            rows = pl.ds(pl.multiple_of(r * LN_ROWS, LN_ROWS), LN_ROWS)
            hh = _ln_rows(x_ref[rows, :], lg_ref[...], lb_ref[...])
            o_ref[rows, :] = _ln_rows(ALPHA * hh + acc_scr[rows, :], g1_ref[...], b1_ref[...])
            return c
        lax.fori_loop(0, nrow, body, 0)


def _mix(x2, hb, lg, lb, a, b, c, wg, bg, wb, wo, g1, b1, tm=512, tn=256):
    m, d = x2.shape
    bw = a.shape[1]
    nn = d // tn
    row = lambda w: pl.BlockSpec((tm, w), lambda i, n: (i, 0))
    vec = pl.BlockSpec((1, d), lambda i, n: (0, 0))
    wgs = [pl.BlockSpec((d, tn), functools.partial(lambda i, n, k: (0, k * nn + n), k=k)) for k in range(3)]
    bgs = [pl.BlockSpec((1, tn), functools.partial(lambda i, n, k: (0, k * nn + n), k=k)) for k in range(3)]
    return pl.pallas_call(
        _mix_kernel,
        grid=(m // tm, nn),
        in_specs=[row(d), row(d), vec, vec, row(bw), row(bw), row(bw), *wgs, *bgs,
                  pl.BlockSpec((3, bw, tn), lambda i, n: (0, 0, n)),
                  pl.BlockSpec((tn, d), lambda i, n: (n, 0)), vec, vec],
        out_specs=pl.BlockSpec((tm, d), lambda i, n: (i, 0)),
        out_shape=jax.ShapeDtypeStruct((m, d), F32),
        scratch_shapes=[pltpu.VMEM((tm, d), F32)],
        compiler_params=_params("parallel", "arbitrary"),
        name="mix",
    )(x2, hb, lg, lb, a, b, c, wg, wg, wg, bg, bg, bg, wb, wo, g1, b1)


HALO = 16


def _gelu_tanh(x):
    return 0.5 * x * (1.0 + jnp.tanh(math.sqrt(2.0 / math.pi) * (x + 0.044715 * (x * x * x))))


def _ffn_kernel(h_ref, hp_ref, hn_ref, wv_ref, wg_ref, cwv_ref, cwg_ref, cbv_ref, cbg_ref, wd_ref,
                g2_ref, b2_ref, o_ref, hx_scr, acc_scr, *, tiles_per_seq):
    i = pl.program_id(0)
    f = pl.program_id(1)
    tm = h_ref.shape[0]
    nrow = tm // LN_ROWS

    @pl.when(f == 0)
    def _():
        seq_first = (i % tiles_per_seq) == 0
        seq_last = (i % tiles_per_seq) == tiles_per_seq - 1
        hx_scr[0:HALO, :] = jnp.where(seq_first, 0.0, hp_ref[...]).astype(BF16)
        hx_scr[HALO:HALO + tm, :] = h_ref[...].astype(BF16)
        hx_scr[HALO + tm:, :] = jnp.where(seq_last, 0.0, hn_ref[...]).astype(BF16)
        acc_scr[...] = jnp.zeros_like(acc_scr)

    hx = hx_scr[...]
    ext = tm + 2 * HALO

    def conv(u, cw_ref, cb_ref):
        prev = pltpu.roll(u, 1, 0)[HALO:HALO + tm]
        nxt = pltpu.roll(u, ext - 1, 0)[HALO:HALO + tm]
        return prev * cw_ref[0:1, :] + u[HALO:HALO + tm] * cw_ref[1:2, :] + nxt * cw_ref[2:3, :] + cb_ref[...]

    gelu_gate = _gelu_tanh(conv(_dot(hx, wg_ref[...]), cwg_ref, cbg_ref))
    val = conv(_dot(hx, wv_ref[...]), cwv_ref, cbv_ref)
    acc_scr[...] += _dot((gelu_gate * val).astype(BF16), wd_ref[...])

    @pl.when(f == pl.num_programs(1) - 1)
    def _():
        def body(r, c):
            rows = pl.ds(pl.multiple_of(r * LN_ROWS, LN_ROWS), LN_ROWS)
            o_ref[rows, :] = _ln_rows(ALPHA * h_ref[rows, :] + acc_scr[rows, :], g2_ref[...], b2_ref[...])
            return c
        lax.fori_loop(0, nrow, body, 0)


def _ffn(h1, w_up, cw, cb, w_down, g2, b2, seq, tm=512, tf=512):
    m, d = h1.shape
    nf = D_FF_PAD // tf
    hb = tm // HALO
    nhb = m // HALO
    vec = pl.BlockSpec((1, d), lambda i, f: (0, 0))
    return pl.pallas_call(
        functools.partial(_ffn_kernel, tiles_per_seq=seq // tm),
        grid=(m // tm, nf),
        in_specs=[pl.BlockSpec((tm, d), lambda i, f: (i, 0)),
                  pl.BlockSpec((HALO, d), lambda i, f: (jnp.maximum(i * hb - 1, 0), 0)),
                  pl.BlockSpec((HALO, d), lambda i, f: (jnp.minimum((i + 1) * hb, nhb - 1), 0)),
                  pl.BlockSpec((d, tf), lambda i, f: (0, f)),
                  pl.BlockSpec((d, tf), lambda i, f: (0, nf + f)),
                  pl.BlockSpec((3, tf), lambda i, f: (0, f)),
                  pl.BlockSpec((3, tf), lambda i, f: (0, nf + f)),
                  pl.BlockSpec((1, tf), lambda i, f: (0, f)),
                  pl.BlockSpec((1, tf), lambda i, f: (0, nf + f)),
                  pl.BlockSpec((tf, d), lambda i, f: (f, 0)), vec, vec],
        out_specs=pl.BlockSpec((tm, d), lambda i, f: (i, 0)),
        out_shape=jax.ShapeDtypeStruct((m, d), F32),
        scratch_shapes=[pltpu.VMEM((tm + 2 * HALO, d), BF16), pltpu.VMEM((tm, d), F32)],
        compiler_params=_params("parallel", "arbitrary"),
        name="ffn",
    )(h1, h1, h1, w_up, w_up, cw, cw, cb, cb, w_down, g2, b2)


def _w_up_prep_kernel(w_ref, o_ref):
    o_ref[:, 0:D_FF] = w_ref[...].astype(BF16)
    o_ref[:, D_FF:] = jnp.zeros((o_ref.shape[0], D_FF_PAD - D_FF), BF16)


def _w_up_prep(w_up, tr=256):
    d = w_up.shape[0]
    return pl.pallas_call(
        _w_up_prep_kernel,
        grid=(2, d // tr),
        in_specs=[pl.BlockSpec((tr, D_FF), lambda half, r: (r, half))],
        out_specs=pl.BlockSpec((tr, D_FF_PAD), lambda half, r: (r, half)),
        out_shape=jax.ShapeDtypeStruct((d, 2 * D_FF_PAD), BF16),
        compiler_params=_params("parallel", "parallel"),
        name="w_up_prep",
    )(w_up)


def _pad_ff(t, axis):
    val, gate = jnp.split(t, 2, axis=axis)
    pad = [(0, 0)] * t.ndim
    pad[axis] = (0, D_FF_PAD - D_FF)
    return jnp.concatenate([jnp.pad(val, pad), jnp.pad(gate, pad)], axis=axis)


def kernel(x, mem, ln_in_g, ln_in_b, rel_table, w_in, w_mem_kv, diff_lq1, diff_lk1, diff_lq2, diff_lk2,
           diff_subln_g, win_sink, w_gate, b_gate, w_branch, w_o, ln1_g, ln1_b, w_up, conv_w, conv_b,
           w_down, ln2_g, ln2_b):
    assert w_in.shape[0] == DEPTH == 1
    bsz, seq, d = x.shape
    x2 = x.reshape(bsz * seq, d)
    row = lambda v: v.reshape(1, -1).astype(F32)
    l = 0
    proj, hb = _ln_proj(x2, row(ln_in_g), row(ln_in_b), w_in[l])
    proj = proj.reshape(bsz, seq, IN_W)
    memkv = _mem_kv(mem.reshape(bsz * N_MEM, d), w_mem_kv[l]).reshape(bsz, N_MEM, 2 * BRANCH_W)

    tq_d, tq_w = 512, 256
    a = _diff_attn(proj, _diff_bias_tiles(rel_table[:, :DIFF_HEADS] * LOG2E, tq_d), row(diff_lq1[l]), row(diff_lk1[l]),
                   row(diff_lq2[l]), row(diff_lk2[l]), row(diff_subln_g[l]), tq=tq_d)
    b = _win_attn(proj, _win_bias_tiles(rel_table[:, DIFF_HEADS:] * LOG2E, tq_w), win_sink[l].astype(F32) * LOG2E,
                  tq=tq_w)
    c = _mem_attn(proj, memkv)

    m = bsz * seq
    h1 = _mix(x2, hb, row(ln_in_g), row(ln_in_b), a.reshape(m, -1), b.reshape(m, -1), c.reshape(m, -1),
              w_gate[l].astype(BF16), row(b_gate[l]), w_branch[l].astype(BF16), w_o[l].astype(BF16),
              row(ln1_g[l]), row(ln1_b[l]))

    w_up_p = _w_up_prep(w_up[l].astype(F32))
    cw_p = _pad_ff(conv_w[l].astype(F32), 1)
    cb_p = _pad_ff(row(conv_b[l]), 1)
    w_down_p = jnp.pad(w_down[l], ((0, D_FF_PAD - D_FF), (0, 0))).astype(BF16)
    out = _ffn(h1, w_up_p, cw_p, cb_p, w_down_p, row(ln2_g[l]), row(ln2_b[l]), seq)
    return out.reshape(bsz, seq, d)
```

```python
import functools
import math

import jax
import jax.numpy as jnp
from jax import lax
from jax.experimental import pallas as pl
from jax.experimental.pallas import tpu as pltpu

F32 = jnp.float32
BF16 = jnp.bfloat16

D_MODEL = 2048
SEQ = 2048
N_MEM = 256
HEAD_DIM = 128
BRANCH_W = 1024
DIFF_HEADS = 8
DIFF_QK = 64
WIN_HEADS = 8
WIN_KV_HEADS = 2
WIN_GROUP = WIN_HEADS // WIN_KV_HEADS
WINDOW = 128
MEM_HEADS = 4
MEM_DIM = 256
OFF_DQ, OFF_DK, OFF_DV, OFF_WQ, OFF_WK, OFF_WV, OFF_MQ = 0, 1024, 2048, 3072, 4096, 4352, 4608
IN_W = 5632
D_FF = 5504
D_FF_PAD = 5632
REL_BUCKETS = 32
REL_MAX_DIST = 128
DEPTH = 1
ALPHA = (2 * DEPTH) ** 0.25
LN_EPS = 1e-5
NEG = -1e30
LOG2E = math.log2(math.e)
LAMBDA_INIT = 0.8 - 0.6 * math.exp(-0.3 * 0)

VMEM_LIMIT = 56 * 1024 * 1024
LN_ROWS = 128


def _ln_rows(x, g, b):
    mu = jnp.mean(x, axis=-1, keepdims=True)
    xc = x - mu
    var = jnp.mean(xc * xc, axis=-1, keepdims=True)
    return xc * lax.rsqrt(var + LN_EPS) * g + b


def _dot(a, b):
    return jnp.dot(a, b, preferred_element_type=F32)


def _dot_nt(a, b):
    return lax.dot_general(a, b, (((1,), (1,)), ((), ())), preferred_element_type=F32)


def _params(*sem):
    return pltpu.CompilerParams(dimension_semantics=sem, vmem_limit_bytes=VMEM_LIMIT)


def _ln_proj_kernel(x_ref, g_ref, b_ref, w_ref, o_ref, h_ref):
    @pl.when(pl.program_id(1) == 0)
    def _():
        def body(r, c):
            rows = pl.ds(pl.multiple_of(r * LN_ROWS, LN_ROWS), LN_ROWS)
            h_ref[rows, :] = _ln_rows(x_ref[rows, :], g_ref[...], b_ref[...]).astype(BF16)
            return c
        lax.fori_loop(0, x_ref.shape[0] // LN_ROWS, body, 0)

    o_ref[...] = _dot(h_ref[...], w_ref[...].astype(BF16)).astype(BF16)


def _ln_proj(x2, g, b, w, tm=1024, tn=512):
    m, d = x2.shape
    n = w.shape[1]
    return pl.pallas_call(
        _ln_proj_kernel,
        grid=(m // tm, n // tn),
        in_specs=[pl.BlockSpec((tm, d), lambda i, j: (i, 0)),
                  pl.BlockSpec((1, d), lambda i, j: (0, 0)),
                  pl.BlockSpec((1, d), lambda i, j: (0, 0)),
                  pl.BlockSpec((d, tn), lambda i, j: (0, j))],
        out_specs=[pl.BlockSpec((tm, tn), lambda i, j: (i, j)),
                   pl.BlockSpec((tm, d), lambda i, j: (i, 0))],
        out_shape=[jax.ShapeDtypeStruct((m, n), BF16), jax.ShapeDtypeStruct((m, d), BF16)],
        compiler_params=_params("parallel", "arbitrary"),
        name="ln_proj",
    )(x2, g, b, w)


def _mem_kv_kernel(m_ref, w_ref, o_ref):
    o_ref[...] = _dot(m_ref[...].astype(BF16), w_ref[...].astype(BF16)).astype(BF16)


def _mem_kv(mem2, w, tn=512):
    m, d = mem2.shape
    n = w.shape[1]
    return pl.pallas_call(
        _mem_kv_kernel,
        grid=(n // tn,),
        in_specs=[pl.BlockSpec((m, d), lambda j: (0, 0)),
                  pl.BlockSpec((d, tn), lambda j: (0, j))],
        out_specs=pl.BlockSpec((m, tn), lambda j: (0, j)),
        out_shape=jax.ShapeDtypeStruct((m, n), BF16),
        compiler_params=_params("parallel"),
        name="mem_kv",
    )(mem2, w)


def _bucket_lower_bounds():
    half = REL_BUCKETS // 2
    max_exact = half // 2
    n = jnp.arange(REL_MAX_DIST + 1)
    log_part = (jnp.log(jnp.maximum(n, 1).astype(F32) / max_exact) / math.log(REL_MAX_DIST / max_exact)
                * (half - max_exact)).astype(jnp.int32)
    bucket = jnp.where(n < max_exact, n, jnp.minimum(max_exact + log_part, half - 1))
    bucket = jnp.arange(half, dtype=jnp.int32)[bucket]
    ids = jnp.arange(half, dtype=jnp.int32)
    return jnp.sum((bucket[None, :] < ids[:, None]).astype(jnp.int32), axis=1)


BIAS_ROWS = 64


def _bias_kernel(lb_ref, tab_ref, off_ref, o_ref, *, limit):
    h = pl.program_id(0)
    half = REL_BUCKETS // 2
    ncol = o_ref.shape[2]

    def body(r, carry):
        rows = pl.ds(pl.multiple_of(r * BIAS_ROWS, BIAS_ROWS), BIAS_ROWS)
        rel = lax.broadcasted_iota(jnp.int32, (BIAS_ROWS, ncol), 1) + off_ref[rows, :]
        n = jnp.abs(rel)
        neg = jnp.full(rel.shape, tab_ref[0, h], F32)
        pos = jnp.full(rel.shape, tab_ref[half, h], F32)
        for b in range(1, half):
            reached = n >= lb_ref[b]
            neg = jnp.where(reached, tab_ref[b, h], neg)
            pos = jnp.where(reached, tab_ref[half + b, h], pos)
        out = jnp.where(rel > 0, pos, neg)
        if limit is not None:
            out = jnp.where(n > limit, NEG, out)
        o_ref[0, rows, :] = out
        return carry
    lax.fori_loop(0, o_ref.shape[1] // BIAS_ROWS, body, 0)


def _bias_tiles(table, row_offset, ncol, limit=None):
    nh = table.shape[1]
    nrow = row_offset.shape[0]
    return pl.pallas_call(
        functools.partial(_bias_kernel, limit=limit),
        grid=(nh,),
        in_specs=[pl.BlockSpec(memory_space=pltpu.SMEM), pl.BlockSpec(memory_space=pltpu.SMEM),
                  pl.BlockSpec((nrow, 1), lambda h: (0, 0))],
        out_specs=pl.BlockSpec((1, nrow, ncol), lambda h: (h, 0, 0)),
        out_shape=jax.ShapeDtypeStruct((nh, nrow, ncol), F32),
        compiler_params=_params("parallel"),
        name="bias_tiles",
    )(_bucket_lower_bounds(), table.astype(F32), row_offset.astype(jnp.int32).reshape(nrow, 1))


def _diff_bias_tiles(table, tq):
    r = jnp.arange(5 * tq)
    return _bias_tiles(table, (r // tq - 2) * tq - r % tq, tq).reshape(table.shape[1], 5, tq, tq)


def _win_bias_tiles(table, tq):
    return _bias_tiles(table, -WINDOW - jnp.arange(tq), tq + 2 * WINDOW, limit=WINDOW)


def _diff_kernel(q_ref, k_ref, v_ref, band_ref, lq1_ref, lk1_ref, lq2_ref, lk2_ref, g_ref, o_ref,
                 s_scr, mx_scr, *, tq, rc, n_tiles, tiles_per_head):
    t = pl.program_id(0)
    hd = v_ref.shape[2]

    @pl.when(t == 0)
    def _():
        s_scr[...] = jnp.zeros_like(s_scr)
        mx_scr[...] = jnp.zeros_like(mx_scr)

    def tick(slot):
        other = 1 - slot
        i = jnp.minimum(t, n_tiles - 1) % tiles_per_head
        nk = k_ref.shape[1] // tq
        q = (q_ref[0].astype(F32) * (DIFF_QK ** -0.5 * LOG2E)).astype(BF16)
        lane = lax.broadcasted_iota(jnp.int32, q.shape, 1)
        zero = jnp.zeros_like(q)
        qs = jnp.concatenate([jnp.where(lane < DIFF_QK, q, zero), jnp.where(lane >= DIFF_QK, q, zero)], axis=0)
        lanes = mx_scr.shape[-1]
        tb = band_ref.shape[2]
        nb = tq // tb
        for j in range(nk):
            s = _dot_nt(qs, k_ref[0, j * tq:(j + 1) * tq, :])
            for r0 in range(0, 2 * tq, tb):
                rows = slice(r0, r0 + tb)
                fold = None
                for cb in range(nb):
                    d = (j - i) * nb + cb - (r0 // tb) % nb
                    sb = s[rows, cb * tb:(cb + 1) * tb] + band_ref[0, jnp.clip(d, -2, 2) + 2]
                    s_scr[slot, rows, j * tq + cb * tb:j * tq + (cb + 1) * tb] = sb
                    fold = functools.reduce(jnp.maximum, [sb[:, c:c + lanes] for c in range(0, tb, lanes)]
                                            + ([] if fold is None else [fold]))
                mx_scr[slot, rows, :] = fold if j == 0 else jnp.maximum(mx_scr[slot, rows, :], fold)

        nchunk = 2 * tq // rc
        chunk = lambda r: slice(r * rc, (r + 1) * rc)
        p = jnp.concatenate(
            [jnp.exp2(s_scr[other, chunk(r), :] - jnp.max(mx_scr[other, chunk(r), :], axis=-1, keepdims=True))
             .astype(BF16) for r in range(nchunk)], axis=0)
        v_ones = jnp.concatenate([v_ref[0], jnp.ones(v_ref.shape[1:], BF16)], axis=1)
        pv = _dot(p, v_ones)
        lam = (jnp.exp(jnp.sum(lq1_ref[...] * lk1_ref[...], axis=-1, keepdims=True))
               - jnp.exp(jnp.sum(lq2_ref[...] * lk2_ref[...], axis=-1, keepdims=True)) + LAMBDA_INIT)
        o = pv[0:tq, 0:hd] * (1.0 / pv[0:tq, hd:]) - pv[tq:, 0:hd] * (lam / pv[tq:, hd:])
        ms = jnp.mean(o * o, axis=-1, keepdims=True)
        o_ref[0] = (o * lax.rsqrt(ms + LN_EPS) * g_ref[...] * (1.0 - LAMBDA_INIT)).astype(BF16)

    pl.when(t % 2 == 0)(functools.partial(tick, 0))
    pl.when(t % 2 == 1)(functools.partial(tick, 1))


def _diff_attn(proj, band, lq1, lk1, lq2, lk2, g, tq=256, rc=32):
    b, s, _ = proj.shape
    cb = HEAD_DIM
    ni = s // tq
    n_tiles = b * DIFF_HEADS * ni
    depth = 1

    def tile(t, lag):
        tc = jnp.clip(t - lag, 0, n_tiles - 1)
        return tc // (DIFF_HEADS * ni), (tc // ni) % DIFF_HEADS, tc % ni

    def q_map(t):
        bi, h, i = tile(t, 0)
        return bi, i, OFF_DQ // cb + h

    def k_map(t):
        bi, h, _ = tile(t, 0)
        return bi, 0, OFF_DK // cb + h

    def v_map(t):
        bi, h, _ = tile(t, depth)
        return bi, 0, OFF_DV // cb + h

    def o_map(t):
        bi, h, i = tile(t, depth)
        return bi, i, h

    vec = lambda n: pl.BlockSpec((1, n), lambda t: (0, 0))
    return pl.pallas_call(
        functools.partial(_diff_kernel, tq=tq, rc=rc, n_tiles=n_tiles, tiles_per_head=ni),
        grid=(n_tiles + depth,),
        in_specs=[pl.BlockSpec((1, tq, cb), q_map),
                  pl.BlockSpec((1, s, cb), k_map),
                  pl.BlockSpec((1, s, cb), v_map),
                  pl.BlockSpec((1,) + band.shape[1:], lambda t: (tile(t, 0)[1], 0, 0, 0)),
                  vec(DIFF_QK), vec(DIFF_QK), vec(DIFF_QK), vec(DIFF_QK), vec(HEAD_DIM)],
        out_specs=pl.BlockSpec((1, tq, cb), o_map),
        out_shape=jax.ShapeDtypeStruct((b, s, BRANCH_W), BF16),
        scratch_shapes=[pltpu.VMEM((2, 2 * tq, s), F32), pltpu.VMEM((2, 2 * tq, cb), F32)],
        compiler_params=_params("arbitrary"),
        name="diff_attn",
    )(proj, proj, proj, band, lq1, lk1, lq2, lk2, g)


def _win_kernel(sink_ref, q_ref, kp_ref, km_ref, kn_ref, vp_ref, vm_ref, vn_ref, bias_ref, o_ref, *, tq):
    g = pl.program_id(1)
    i = pl.program_id(2)
    first = i == 0
    last = i == pl.num_programs(2) - 1
    w = WINDOW
    hd = HEAD_DIM
    nkeys = tq + 2 * w
    keys = jnp.concatenate([kp_ref[0], km_ref[0], kn_ref[0]], axis=0)
    vals = jnp.concatenate([vp_ref[0], vm_ref[0], vn_ref[0]], axis=0)
    v_ones = jnp.concatenate([vals, jnp.ones((nkeys, hd), BF16)], axis=1)
    col = lax.broadcasted_iota(jnp.int32, (1, nkeys), 1)
    outside = jnp.logical_or(jnp.logical_and(first, col < w),
                             jnp.logical_and(last, col >= tq + w))
    scale = jnp.asarray(hd ** -0.5 * LOG2E, F32)
    for j in range(WIN_GROUP):
        cols = slice(j * hd, (j + 1) * hd)
        s = _dot_nt(q_ref[0, :, cols], keys) * scale + bias_ref[j]
        s = jnp.where(outside, NEG, s)
        sink = sink_ref[g * WIN_GROUP + j]
        fold = functools.reduce(jnp.maximum, [s[:, c:c + w] for c in range(0, nkeys, w)])
        m = jnp.maximum(jnp.max(fold, axis=-1, keepdims=True), sink)
        pv = _dot(jnp.exp2(s - m).astype(BF16), v_ones)
        o_ref[0, :, cols] = (pv[:, 0:hd] / (pv[:, hd:] + jnp.exp2(sink - m))).astype(BF16)


def _win_attn(proj, bias, sink, tq=256):
    b, s, _ = proj.shape
    hd = HEAD_DIM
    nb = s // WINDOW
    r = tq // WINDOW
    prev = lambda bi, g, i: jnp.maximum(i * r - 1, 0)
    nxt = lambda bi, g, i: jnp.minimum((i + 1) * r, nb - 1)
    kcol, vcol = OFF_WK // hd, OFF_WV // hd
    return pl.pallas_call(
        functools.partial(_win_kernel, tq=tq),
        grid=(b, WIN_KV_HEADS, s // tq),
        in_specs=[pl.BlockSpec(memory_space=pltpu.SMEM),
                  pl.BlockSpec((1, tq, WIN_GROUP * hd), lambda bi, g, i: (bi, i, OFF_WQ // (WIN_GROUP * hd) + g)),
                  pl.BlockSpec((1, WINDOW, hd), lambda bi, g, i: (bi, prev(bi, g, i), kcol + g)),
                  pl.BlockSpec((1, tq, hd), lambda bi, g, i: (bi, i, kcol + g)),
                  pl.BlockSpec((1, WINDOW, hd), lambda bi, g, i: (bi, nxt(bi, g, i), kcol + g)),
                  pl.BlockSpec((1, WINDOW, hd), lambda bi, g, i: (bi, prev(bi, g, i), vcol + g)),
                  pl.BlockSpec((1, tq, hd), lambda bi, g, i: (bi, i, vcol + g)),
                  pl.BlockSpec((1, WINDOW, hd), lambda bi, g, i: (bi, nxt(bi, g, i), vcol + g)),
                  pl.BlockSpec((WIN_GROUP, tq, tq + 2 * WINDOW), lambda bi, g, i: (g, 0, 0))],
        out_specs=pl.BlockSpec((1, tq, WIN_GROUP * hd), lambda bi, g, i: (bi, i, g)),
        out_shape=jax.ShapeDtypeStruct((b, s, BRANCH_W), BF16),
        compiler_params=_params("parallel", "parallel", "arbitrary"),
        name="win_attn",
    )(sink, proj, proj, proj, proj, proj, proj, proj, bias)


def _mem_attn_kernel(q_ref, k_ref, v_ref, o_ref):
    s = _dot_nt(q_ref[0], k_ref[0]) * jnp.asarray(MEM_DIM ** -0.5, F32)
    p = jnp.exp(s - jnp.max(s, axis=-1, keepdims=True))
    p = p * (1.0 / jnp.sum(p, axis=-1, keepdims=True))
    o_ref[0] = _dot(p.astype(BF16), v_ref[0]).astype(BF16)


def _mem_attn(proj, memkv, tq=1024):
    b, s, _ = proj.shape
    md = MEM_DIM
    return pl.pallas_call(
        _mem_attn_kernel,
        grid=(b, MEM_HEADS, s // tq),
        in_specs=[pl.BlockSpec((1, tq, md), lambda bi, h, i: (bi, i, OFF_MQ // md + h)),
                  pl.BlockSpec((1, N_MEM, md), lambda bi, h, i: (bi, 0, h)),
                  pl.BlockSpec((1, N_MEM, md), lambda bi, h, i: (bi, 0, MEM_HEADS + h))],
        out_specs=pl.BlockSpec((1, tq, md), lambda bi, h, i: (bi, i, h)),
        out_shape=jax.ShapeDtypeStruct((b, s, BRANCH_W), BF16),
        compiler_params=_params("parallel", "parallel", "arbitrary"),
        name="mem_attn",
    )(proj, memkv, memkv)


def _mix_kernel(x_ref, h_ref, lg_ref, lb_ref, a_ref, b_ref, c_ref, wg0_ref, wg1_ref, wg2_ref,
                bg0_ref, bg1_ref, bg2_ref, wb_ref, wo_ref, g1_ref, b1_ref, o_ref, acc_scr):
    n = pl.program_id(1)
    nrow = x_ref.shape[0] // LN_ROWS

    @pl.when(n == 0)
    def _():
        acc_scr[...] = jnp.zeros_like(acc_scr)

    h = h_ref[...]
    gates = [jax.nn.sigmoid(_dot(h, wg_ref[...]) + bg_ref[...])
             for wg_ref, bg_ref in ((wg0_ref, bg0_ref), (wg1_ref, bg1_ref), (wg2_ref, bg2_ref))]
    mixed = None
    for k, br_ref in enumerate((a_ref, b_ref, c_ref)):
        term = gates[k] * _dot(br_ref[...], wb_ref[k])
        mixed = term if mixed is None else mixed + term
    acc_scr[...] += _dot(mixed.astype(BF16), wo_ref[...])

    @pl.when(n == pl.num_programs(1) - 1)
    def _():
        def body(r, c):
            rows = pl.ds(pl.multiple_of(r * LN_ROWS, LN_ROWS), LN_ROWS)
            hh = _ln_rows(x_ref[rows, :], lg_ref[...], lb_ref[...])
            o_ref[rows, :] = _ln_rows(ALPHA * hh + acc_scr[rows, :], g1_ref[...], b1_ref[...])
            return c
        lax.fori_loop(0, nrow, body, 0)


def _mix(x2, hb, lg, lb, a, b, c, wg, bg, wb, wo, g1, b1, tm=512, tn=256):
    m, d = x2.shape
    bw = a.shape[1]
    nn = d // tn
    row = lambda w: pl.BlockSpec((tm, w), lambda i, n: (i, 0))
    vec = pl.BlockSpec((1, d), lambda i, n: (0, 0))
    wgs = [pl.BlockSpec((d, tn), functools.partial(lambda i, n, k: (0, k * nn + n), k=k)) for k in range(3)]
    bgs = [pl.BlockSpec((1, tn), functools.partial(lambda i, n, k: (0, k * nn + n), k=k)) for k in range(3)]
    return pl.pallas_call(
        _mix_kernel,
        grid=(m // tm, nn),
        in_specs=[row(d), row(d), vec, vec, row(bw), row(bw), row(bw), *wgs, *bgs,
                  pl.BlockSpec((3, bw, tn), lambda i, n: (0, 0, n)),
                  pl.BlockSpec((tn, d), lambda i, n: (n, 0)), vec, vec],
        out_specs=pl.BlockSpec((tm, d), lambda i, n: (i, 0)),
        out_shape=jax.ShapeDtypeStruct((m, d), F32),
        scratch_shapes=[pltpu.VMEM((tm, d), F32)],
        compiler_params=_params("parallel", "arbitrary"),
        name="mix",
    )(x2, hb, lg, lb, a, b, c, wg, wg, wg, bg, bg, bg, wb, wo, g1, b1)


HALO = 16


def _gelu_tanh(x):
    return 0.5 * x * (1.0 + jnp.tanh(math.sqrt(2.0 / math.pi) * (x + 0.044715 * (x * x * x))))


def _ffn_kernel(h_ref, hp_ref, hn_ref, wv_ref, wg_ref, cwv_ref, cwg_ref, cbv_ref, cbg_ref, wd_ref,
                g2_ref, b2_ref, o_ref, hx_scr, acc_scr, *, tiles_per_seq):
    i = pl.program_id(0)
    f = pl.program_id(1)
    tm = h_ref.shape[0]
    nrow = tm // LN_ROWS

    @pl.when(f == 0)
    def _():
        seq_first = (i % tiles_per_seq) == 0
        seq_last = (i % tiles_per_seq) == tiles_per_seq - 1
        hx_scr[0:HALO, :] = jnp.where(seq_first, 0.0, hp_ref[...]).astype(BF16)
        hx_scr[HALO:HALO + tm, :] = h_ref[...].astype(BF16)
        hx_scr[HALO + tm:, :] = jnp.where(seq_last, 0.0, hn_ref[...]).astype(BF16)
        acc_scr[...] = jnp.zeros_like(acc_scr)

    hx = hx_scr[...]
    ext = tm + 2 * HALO

    def conv(u, cw_ref, cb_ref):
        prev = pltpu.roll(u, 1, 0)[HALO:HALO + tm]
        nxt = pltpu.roll(u, ext - 1, 0)[HALO:HALO + tm]
        return prev * cw_ref[0:1, :] + u[HALO:HALO + tm] * cw_ref[1:2, :] + nxt * cw_ref[2:3, :] + cb_ref[...]

    gelu_gate = _gelu_tanh(conv(_dot(hx, wg_ref[...]), cwg_ref, cbg_ref))
    val = conv(_dot(hx, wv_ref[...]), cwv_ref, cbv_ref)
    acc_scr[...] += _dot((gelu_gate * val).astype(BF16), wd_ref[...])

    @pl.when(f == pl.num_programs(1) - 1)
    def _():
        def body(r, c):
            rows = pl.ds(pl.multiple_of(r * LN_ROWS, LN_ROWS), LN_ROWS)
            o_ref[rows, :] = _ln_rows(ALPHA * h_ref[rows, :] + acc_scr[rows, :], g2_ref[...], b2_ref[...])
            return c
        lax.fori_loop(0, nrow, body, 0)


def _ffn(h1, w_up, cw, cb, w_down, g2, b2, seq, tm=512, tf=512):
    m, d = h1.shape
    nf = D_FF_PAD // tf
    hb = tm // HALO
    nhb = m // HALO
    vec = pl.BlockSpec((1, d), lambda i, f: (0, 0))
    return pl.pallas_call(
        functools.partial(_ffn_kernel, tiles_per_seq=seq // tm),
        grid=(m // tm, nf),
        in_specs=[pl.BlockSpec((tm, d), lambda i, f: (i, 0)),
                  pl.BlockSpec((HALO, d), lambda i, f: (jnp.maximum(i * hb - 1, 0), 0)),
                  pl.BlockSpec((HALO, d), lambda i, f: (jnp.minimum((i + 1) * hb, nhb - 1), 0)),
                  pl.BlockSpec((d, tf), lambda i, f: (0, f)),
                  pl.BlockSpec((d, tf), lambda i, f: (0, nf + f)),
                  pl.BlockSpec((3, tf), lambda i, f: (0, f)),
                  pl.BlockSpec((3, tf), lambda i, f: (0, nf + f)),
                  pl.BlockSpec((1, tf), lambda i, f: (0, f)),
                  pl.BlockSpec((1, tf), lambda i, f: (0, nf + f)),
                  pl.BlockSpec((tf, d), lambda i, f: (f, 0)), vec, vec],
        out_specs=pl.BlockSpec((tm, d), lambda i, f: (i, 0)),
        out_shape=jax.ShapeDtypeStruct((m, d), F32),
        scratch_shapes=[pltpu.VMEM((tm + 2 * HALO, d), BF16), pltpu.VMEM((tm, d), F32)],
        compiler_params=_params("parallel", "arbitrary"),
        name="ffn",
    )(h1, h1, h1, w_up, w_up, cw, cw, cb, cb, w_down, g2, b2)


def _w_up_prep_kernel(w_ref, o_ref):
    o_ref[:, 0:D_FF] = w_ref[...].astype(BF16)
    o_ref[:, D_FF:] = jnp.zeros((o_ref.shape[0], D_FF_PAD - D_FF), BF16)


def _w_up_prep(w_up, tr=256):
    d = w_up.shape[0]
    return pl.pallas_call(
        _w_up_prep_kernel,
        grid=(2, d // tr),
        in_specs=[pl.BlockSpec((tr, D_FF), lambda half, r: (r, half))],
        out_specs=pl.BlockSpec((tr, D_FF_PAD), lambda half, r: (r, half)),
        out_shape=jax.ShapeDtypeStruct((d, 2 * D_FF_PAD), BF16),
        compiler_params=_params("parallel", "parallel"),
        name="w_up_prep",
    )(w_up)


def _pad_ff(t, axis):
    val, gate = jnp.split(t, 2, axis=axis)
    pad = [(0, 0)] * t.ndim
    pad[axis] = (0, D_FF_PAD - D_FF)
    return jnp.concatenate([jnp.pad(val, pad), jnp.pad(gate, pad)], axis=axis)


def kernel(x, mem, ln_in_g, ln_in_b, rel_table, w_in, w_mem_kv, diff_lq1, diff_lk1, diff_lq2, diff_lk2,
           diff_subln_g, win_sink, w_gate, b_gate, w_branch, w_o, ln1_g, ln1_b, w_up, conv_w, conv_b,
           w_down, ln2_g, ln2_b):
    assert w_in.shape[0] == DEPTH == 1
    bsz, seq, d = x.shape
    x2 = x.reshape(bsz * seq, d)
    row = lambda v: v.reshape(1, -1).astype(F32)
    l = 0
    proj, hb = _ln_proj(x2, row(ln_in_g), row(ln_in_b), w_in[l])
    proj = proj.reshape(bsz, seq, IN_W)
    memkv = _mem_kv(mem.reshape(bsz * N_MEM, d), w_mem_kv[l]).reshape(bsz, N_MEM, 2 * BRANCH_W)

    tq_d, tb_d, tq_w = 512, 256, 256
    a = _diff_attn(proj, _diff_bias_tiles(rel_table[:, :DIFF_HEADS] * LOG2E, tb_d), row(diff_lq1[l]), row(diff_lk1[l]),
                   row(diff_lq2[l]), row(diff_lk2[l]), row(diff_subln_g[l]), tq=tq_d)
    b = _win_attn(proj, _win_bias_tiles(rel_table[:, DIFF_HEADS:] * LOG2E, tq_w), win_sink[l].astype(F32) * LOG2E,
                  tq=tq_w)
    c = _mem_attn(proj, memkv)

    m = bsz * seq
    h1 = _mix(x2, hb, row(ln_in_g), row(ln_in_b), a.reshape(m, -1), b.reshape(m, -1), c.reshape(m, -1),
              w_gate[l].astype(BF16), row(b_gate[l]), w_branch[l].astype(BF16), w_o[l].astype(BF16),
              row(ln1_g[l]), row(ln1_b[l]))

    w_up_p = _w_up_prep(w_up[l].astype(F32))
    cw_p = _pad_ff(conv_w[l].astype(F32), 1)
    cb_p = _pad_ff(row(conv_b[l]), 1)
    w_down_p = jnp.pad(w_down[l], ((0, D_FF_PAD - D_FF), (0, 0))).astype(BF16)
    out = _ffn(h1, w_up_p, cw_p, cb_p, w_down_p, row(ln2_g[l]), row(ln2_b[l]), seq)
    return out.reshape(bsz, seq, d)
```

```python
import functools
import math

import jax
import jax.numpy as jnp
from jax import lax
from jax.experimental import pallas as pl
from jax.experimental.pallas import tpu as pltpu

F32 = jnp.float32
BF16 = jnp.bfloat16

D_MODEL = 2048
SEQ = 2048
N_MEM = 256
HEAD_DIM = 128
BRANCH_W = 1024
DIFF_HEADS = 8
DIFF_QK = 64
WIN_HEADS = 8
WIN_KV_HEADS = 2
WIN_GROUP = WIN_HEADS // WIN_KV_HEADS
WINDOW = 128
MEM_HEADS = 4
MEM_DIM = 256
OFF_DQ, OFF_DK, OFF_DV, OFF_WQ, OFF_WK, OFF_WV, OFF_MQ = 0, 1024, 2048, 3072, 4096, 4352, 4608
IN_W = 5632
D_FF = 5504
D_FF_PAD = 5632
REL_BUCKETS = 32
REL_MAX_DIST = 128
DEPTH = 1
ALPHA = (2 * DEPTH) ** 0.25
LN_EPS = 1e-5
NEG = -1e30
LOG2E = math.log2(math.e)
LAMBDA_INIT = 0.8 - 0.6 * math.exp(-0.3 * 0)

VMEM_LIMIT = 56 * 1024 * 1024
LN_ROWS = 128


def _ln_rows(x, g, b):
    mu = jnp.mean(x, axis=-1, keepdims=True)
    xc = x - mu
    var = jnp.mean(xc * xc, axis=-1, keepdims=True)
    return xc * lax.rsqrt(var + LN_EPS) * g + b


def _dot(a, b):
    return jnp.dot(a, b, preferred_element_type=F32)


def _dot_nt(a, b):
    return lax.dot_general(a, b, (((1,), (1,)), ((), ())), preferred_element_type=F32)


def _params(*sem):
    return pltpu.CompilerParams(dimension_semantics=sem, vmem_limit_bytes=VMEM_LIMIT)


def _ln_proj_kernel(x_ref, g_ref, b_ref, w_ref, o_ref, h_ref):
    @pl.when(pl.program_id(1) == 0)
    def _():
        def body(r, c):
            rows = pl.ds(pl.multiple_of(r * LN_ROWS, LN_ROWS), LN_ROWS)
            h_ref[rows, :] = _ln_rows(x_ref[rows, :], g_ref[...], b_ref[...]).astype(BF16)
            return c
        lax.fori_loop(0, x_ref.shape[0] // LN_ROWS, body, 0)

    o_ref[...] = _dot(h_ref[...], w_ref[...].astype(BF16)).astype(BF16)


def _ln_proj(x2, g, b, w, tm=1024, tn=512):
    m, d = x2.shape
    n = w.shape[1]
    return pl.pallas_call(
        _ln_proj_kernel,
        grid=(m // tm, n // tn),
        in_specs=[pl.BlockSpec((tm, d), lambda i, j: (i, 0)),
                  pl.BlockSpec((1, d), lambda i, j: (0, 0)),
                  pl.BlockSpec((1, d), lambda i, j: (0, 0)),
                  pl.BlockSpec((d, tn), lambda i, j: (0, j))],
        out_specs=[pl.BlockSpec((tm, tn), lambda i, j: (i, j)),
                   pl.BlockSpec((tm, d), lambda i, j: (i, 0))],
        out_shape=[jax.ShapeDtypeStruct((m, n), BF16), jax.ShapeDtypeStruct((m, d), BF16)],
        compiler_params=_params("parallel", "arbitrary"),
        name="ln_proj",
    )(x2, g, b, w)


def _mem_kv_kernel(m_ref, w_ref, o_ref):
    o_ref[...] = _dot(m_ref[...].astype(BF16), w_ref[...].astype(BF16)).astype(BF16)


def _mem_kv(mem2, w, tn=512):
    m, d = mem2.shape
    n = w.shape[1]
    return pl.pallas_call(
        _mem_kv_kernel,
        grid=(n // tn,),
        in_specs=[pl.BlockSpec((m, d), lambda j: (0, 0)),
                  pl.BlockSpec((d, tn), lambda j: (0, j))],
        out_specs=pl.BlockSpec((m, tn), lambda j: (0, j)),
        out_shape=jax.ShapeDtypeStruct((m, n), BF16),
        compiler_params=_params("parallel"),
        name="mem_kv",
    )(mem2, w)


def _bucket_lower_bounds():
    half = REL_BUCKETS // 2
    max_exact = half // 2
    n = jnp.arange(REL_MAX_DIST + 1)
    log_part = (jnp.log(jnp.maximum(n, 1).astype(F32) / max_exact) / math.log(REL_MAX_DIST / max_exact)
                * (half - max_exact)).astype(jnp.int32)
    bucket = jnp.where(n < max_exact, n, jnp.minimum(max_exact + log_part, half - 1))
    bucket = jnp.arange(half, dtype=jnp.int32)[bucket]
    ids = jnp.arange(half, dtype=jnp.int32)
    return jnp.sum((bucket[None, :] < ids[:, None]).astype(jnp.int32), axis=1)


BIAS_ROWS = 64


def _bias_kernel(lb_ref, tab_ref, off_ref, o_ref, *, limit):
    h = pl.program_id(0)
    half = REL_BUCKETS // 2
    ncol = o_ref.shape[2]

    def body(r, carry):
        rows = pl.ds(pl.multiple_of(r * BIAS_ROWS, BIAS_ROWS), BIAS_ROWS)
        rel = lax.broadcasted_iota(jnp.int32, (BIAS_ROWS, ncol), 1) + off_ref[rows, :]
        n = jnp.abs(rel)
        neg = jnp.full(rel.shape, tab_ref[0, h], F32)
        pos = jnp.full(rel.shape, tab_ref[half, h], F32)
        for b in range(1, half):
            reached = n >= lb_ref[b]
            neg = jnp.where(reached, tab_ref[b, h], neg)
            pos = jnp.where(reached, tab_ref[half + b, h], pos)
        out = jnp.where(rel > 0, pos, neg)
        if limit is not None:
            out = jnp.where(n > limit, NEG, out)
        o_ref[0, rows, :] = out
        return carry
    lax.fori_loop(0, o_ref.shape[1] // BIAS_ROWS, body, 0)


def _bias_tiles(table, row_offset, ncol, limit=None):
    nh = table.shape[1]
    nrow = row_offset.shape[0]
    return pl.pallas_call(
        functools.partial(_bias_kernel, limit=limit),
        grid=(nh,),
        in_specs=[pl.BlockSpec(memory_space=pltpu.SMEM), pl.BlockSpec(memory_space=pltpu.SMEM),
                  pl.BlockSpec((nrow, 1), lambda h: (0, 0))],
        out_specs=pl.BlockSpec((1, nrow, ncol), lambda h: (h, 0, 0)),
        out_shape=jax.ShapeDtypeStruct((nh, nrow, ncol), F32),
        compiler_params=_params("parallel"),
        name="bias_tiles",
    )(_bucket_lower_bounds(), table.astype(F32), row_offset.astype(jnp.int32).reshape(nrow, 1))


def _diff_bias_tiles(table, tq):
    r = jnp.arange(5 * tq)
    return _bias_tiles(table, (r // tq - 2) * tq - r % tq, tq).reshape(table.shape[1], 5, tq, tq)


def _win_bias_tiles(table, tq):
    return _bias_tiles(table, -WINDOW - jnp.arange(tq), tq + 2 * WINDOW, limit=WINDOW)


def _diff_kernel(q_ref, k_ref, v_ref, band_ref, lq1_ref, lk1_ref, lq2_ref, lk2_ref, g_ref, o_ref,
                 s_scr, mx_scr, *, tq, rc, n_tiles, tiles_per_head):
    t = pl.program_id(0)
    hd = v_ref.shape[2]

    @pl.when(t == 0)
    def _():
        s_scr[...] = jnp.zeros_like(s_scr)
        mx_scr[...] = jnp.zeros_like(mx_scr)

    def tick(slot):
        other = 1 - slot
        i = jnp.minimum(t, n_tiles - 1) % tiles_per_head
        nk = k_ref.shape[1] // tq
        q = (q_ref[0].astype(F32) * (DIFF_QK ** -0.5 * LOG2E)).astype(BF16)
        lane = lax.broadcasted_iota(jnp.int32, q.shape, 1)
        zero = jnp.zeros_like(q)
        qs = jnp.concatenate([jnp.where(lane < DIFF_QK, q, zero), jnp.where(lane >= DIFF_QK, q, zero)], axis=0)
        lanes = mx_scr.shape[-1]
        tb = band_ref.shape[2]
        nb = tq // tb
        for j in range(nk):
            s = _dot_nt(qs, k_ref[0, j * tq:(j + 1) * tq, :])
            for r0 in range(0, 2 * tq, tb):
                rows = slice(r0, r0 + tb)
                fold = None
                for cb in range(nb):
                    d = (j - i) * nb + cb - (r0 // tb) % nb
                    sb = s[rows, cb * tb:(cb + 1) * tb] + band_ref[0, jnp.clip(d, -2, 2) + 2]
                    s_scr[slot, rows, j * tq + cb * tb:j * tq + (cb + 1) * tb] = sb
                    fold = functools.reduce(jnp.maximum, [sb[:, c:c + lanes] for c in range(0, tb, lanes)]
                                            + ([] if fold is None else [fold]))
                mx_scr[slot, rows, :] = fold if j == 0 else jnp.maximum(mx_scr[slot, rows, :], fold)

        nchunk = 2 * tq // rc
        chunk = lambda r: slice(r * rc, (r + 1) * rc)
        p = jnp.concatenate(
            [jnp.exp2(s_scr[other, chunk(r), :] - jnp.max(mx_scr[other, chunk(r), :], axis=-1, keepdims=True))
             .astype(BF16) for r in range(nchunk)], axis=0)
        v_ones = jnp.concatenate([v_ref[0], jnp.ones(v_ref.shape[1:], BF16)], axis=1)
        pv = _dot(p, v_ones)
        lam = (jnp.exp(jnp.sum(lq1_ref[...] * lk1_ref[...], axis=-1, keepdims=True))
               - jnp.exp(jnp.sum(lq2_ref[...] * lk2_ref[...], axis=-1, keepdims=True)) + LAMBDA_INIT)
        o = pv[0:tq, 0:hd] * (1.0 / pv[0:tq, hd:]) - pv[tq:, 0:hd] * (lam / pv[tq:, hd:])
        ms = jnp.mean(o * o, axis=-1, keepdims=True)
        o_ref[0] = (o * lax.rsqrt(ms + LN_EPS) * g_ref[...] * (1.0 - LAMBDA_INIT)).astype(BF16)

    pl.when(t % 2 == 0)(functools.partial(tick, 0))
    pl.when(t % 2 == 1)(functools.partial(tick, 1))


def _diff_attn(proj, band, lq1, lk1, lq2, lk2, g, tq=256, rc=32):
    b, s, _ = proj.shape
    cb = HEAD_DIM
    ni = s // tq
    n_tiles = b * DIFF_HEADS * ni
    depth = 1

    def tile(t, lag):
        tc = jnp.clip(t - lag, 0, n_tiles - 1)
        return tc // (DIFF_HEADS * ni), (tc // ni) % DIFF_HEADS, tc % ni

    def q_map(t):
        bi, h, i = tile(t, 0)
        return bi, i, OFF_DQ // cb + h

    def k_map(t):
        bi, h, _ = tile(t, 0)
        return bi, 0, OFF_DK // cb + h

    def v_map(t):
        bi, h, _ = tile(t, depth)
        return bi, 0, OFF_DV // cb + h

    def o_map(t):
        bi, h, i = tile(t, depth)
        return bi, i, h

    vec = lambda n: pl.BlockSpec((1, n), lambda t: (0, 0))
    return pl.pallas_call(
        functools.partial(_diff_kernel, tq=tq, rc=rc, n_tiles=n_tiles, tiles_per_head=ni),
        grid=(n_tiles + depth,),
        in_specs=[pl.BlockSpec((1, tq, cb), q_map),
                  pl.BlockSpec((1, s, cb), k_map),
                  pl.BlockSpec((1, s, cb), v_map),
                  pl.BlockSpec((1,) + band.shape[1:], lambda t: (tile(t, 0)[1], 0, 0, 0)),
                  vec(DIFF_QK), vec(DIFF_QK), vec(DIFF_QK), vec(DIFF_QK), vec(HEAD_DIM)],
        out_specs=pl.BlockSpec((1, tq, cb), o_map),
        out_shape=jax.ShapeDtypeStruct((b, s, BRANCH_W), BF16),
        scratch_shapes=[pltpu.VMEM((2, 2 * tq, s), F32), pltpu.VMEM((2, 2 * tq, cb), F32)],
        compiler_params=_params("arbitrary"),
        name="diff_attn",
    )(proj, proj, proj, band, lq1, lk1, lq2, lk2, g)


def _win_kernel(sink_ref, q_ref, kp_ref, km_ref, kn_ref, vp_ref, vm_ref, vn_ref, bias_ref, o_ref, *, tq):
    g = pl.program_id(1)
    i = pl.program_id(2)
    first = i == 0
    last = i == pl.num_programs(2) - 1
    w = WINDOW
    hd = HEAD_DIM
    nkeys = tq + 2 * w
    keys = jnp.concatenate([kp_ref[0], km_ref[0], kn_ref[0]], axis=0)
    vals = jnp.concatenate([vp_ref[0], vm_ref[0], vn_ref[0]], axis=0)
    v_ones = jnp.concatenate([vals, jnp.ones((nkeys, hd), BF16)], axis=1)
    col = lax.broadcasted_iota(jnp.int32, (1, nkeys), 1)
    outside = jnp.logical_or(jnp.logical_and(first, col < w),
                             jnp.logical_and(last, col >= tq + w))
    scale = jnp.asarray(hd ** -0.5 * LOG2E, F32)
    for j in range(WIN_GROUP):
        cols = slice(j * hd, (j + 1) * hd)
        s = _dot_nt(q_ref[0, :, cols], keys) * scale + bias_ref[j]
        s = jnp.where(outside, NEG, s)
        sink = sink_ref[g * WIN_GROUP + j]
        fold = functools.reduce(jnp.maximum, [s[:, c:c + w] for c in range(0, nkeys, w)])
        m = jnp.maximum(jnp.max(fold, axis=-1, keepdims=True), sink)
        pv = _dot(jnp.exp2(s - m).astype(BF16), v_ones)
        o_ref[0, :, cols] = (pv[:, 0:hd] / (pv[:, hd:] + jnp.exp2(sink - m))).astype(BF16)


def _win_attn(proj, bias, sink, tq=256):
    b, s, _ = proj.shape
    hd = HEAD_DIM
    nb = s // WINDOW
    r = tq // WINDOW
    prev = lambda bi, g, i: jnp.maximum(i * r - 1, 0)
    nxt = lambda bi, g, i: jnp.minimum((i + 1) * r, nb - 1)
    kcol, vcol = OFF_WK // hd, OFF_WV // hd
    return pl.pallas_call(
        functools.partial(_win_kernel, tq=tq),
        grid=(b, WIN_KV_HEADS, s // tq),
        in_specs=[pl.BlockSpec(memory_space=pltpu.SMEM),
                  pl.BlockSpec((1, tq, WIN_GROUP * hd), lambda bi, g, i: (bi, i, OFF_WQ // (WIN_GROUP * hd) + g)),
                  pl.BlockSpec((1, WINDOW, hd), lambda bi, g, i: (bi, prev(bi, g, i), kcol + g)),
                  pl.BlockSpec((1, tq, hd), lambda bi, g, i: (bi, i, kcol + g)),
                  pl.BlockSpec((1, WINDOW, hd), lambda bi, g, i: (bi, nxt(bi, g, i), kcol + g)),
                  pl.BlockSpec((1, WINDOW, hd), lambda bi, g, i: (bi, prev(bi, g, i), vcol + g)),
                  pl.BlockSpec((1, tq, hd), lambda bi, g, i: (bi, i, vcol + g)),
                  pl.BlockSpec((1, WINDOW, hd), lambda bi, g, i: (bi, nxt(bi, g, i), vcol + g)),
                  pl.BlockSpec((WIN_GROUP, tq, tq + 2 * WINDOW), lambda bi, g, i: (g, 0, 0))],
        out_specs=pl.BlockSpec((1, tq, WIN_GROUP * hd), lambda bi, g, i: (bi, i, g)),
        out_shape=jax.ShapeDtypeStruct((b, s, BRANCH_W), BF16),
        compiler_params=_params("parallel", "parallel", "arbitrary"),
        name="win_attn",
    )(sink, proj, proj, proj, proj, proj, proj, proj, bias)


def _mem_attn_kernel(q_ref, k_ref, v_ref, o_ref):
    s = _dot_nt(q_ref[0], k_ref[0]) * jnp.asarray(MEM_DIM ** -0.5, F32)
    p = jnp.exp(s - jnp.max(s, axis=-1, keepdims=True))
    p = p * (1.0 / jnp.sum(p, axis=-1, keepdims=True))
    o_ref[0] = _dot(p.astype(BF16), v_ref[0]).astype(BF16)


def _mem_attn(proj, memkv, tq=1024):
    b, s, _ = proj.shape
    md = MEM_DIM
    return pl.pallas_call(
        _mem_attn_kernel,
        grid=(b, MEM_HEADS, s // tq),
        in_specs=[pl.BlockSpec((1, tq, md), lambda bi, h, i: (bi, i, OFF_MQ // md + h)),
                  pl.BlockSpec((1, N_MEM, md), lambda bi, h, i: (bi, 0, h)),
                  pl.BlockSpec((1, N_MEM, md), lambda bi, h, i: (bi, 0, MEM_HEADS + h))],
        out_specs=pl.BlockSpec((1, tq, md), lambda bi, h, i: (bi, i, h)),
        out_shape=jax.ShapeDtypeStruct((b, s, BRANCH_W), BF16),
        compiler_params=_params("parallel", "parallel", "arbitrary"),
        name="mem_attn",
    )(proj, memkv, memkv)


def _mix_kernel(x_ref, h_ref, lg_ref, lb_ref, a_ref, b_ref, c_ref, wg0_ref, wg1_ref, wg2_ref,
                bg0_ref, bg1_ref, bg2_ref, wb_ref, wo_ref, g1_ref, b1_ref, o_ref, acc_scr):
    n = pl.program_id(1)
    nrow = x_ref.shape[0] // LN_ROWS

    @pl.when(n == 0)
    def _():
        acc_scr[...] = jnp.zeros_like(acc_scr)

    h = h_ref[...]
    gates = [jax.nn.sigmoid(_dot(h, wg_ref[...]) + bg_ref[...])
             for wg_ref, bg_ref in ((wg0_ref, bg0_ref), (wg1_ref, bg1_ref), (wg2_ref, bg2_ref))]
    mixed = None
    for k, br_ref in enumerate((a_ref, b_ref, c_ref)):
        term = gates[k] * _dot(br_ref[...], wb_ref[k].astype(BF16))
        mixed = term if mixed is None else mixed + term
    acc_scr[...] += _dot(mixed.astype(BF16), wo_ref[...].astype(BF16))

    @pl.when(n == pl.num_programs(1) - 1)
    def _():
        def body(r, c):
            rows = pl.ds(pl.multiple_of(r * LN_ROWS, LN_ROWS), LN_ROWS)
            hh = _ln_rows(x_ref[rows, :], lg_ref[...], lb_ref[...])
            o_ref[rows, :] = _ln_rows(ALPHA * hh + acc_scr[rows, :], g1_ref[...], b1_ref[...])
            return c
        lax.fori_loop(0, nrow, body, 0)


def _mix(x2, hb, lg, lb, a, b, c, wg, bg, wb, wo, g1, b1, tm=512, tn=256):
    m, d = x2.shape
    bw = a.shape[1]
    nn = d // tn
    row = lambda w: pl.BlockSpec((tm, w), lambda i, n: (i, 0))
    vec = pl.BlockSpec((1, d), lambda i, n: (0, 0))
    wgs = [pl.BlockSpec((d, tn), functools.partial(lambda i, n, k: (0, k * nn + n), k=k)) for k in range(3)]
    bgs = [pl.BlockSpec((1, tn), functools.partial(lambda i, n, k: (0, k * nn + n), k=k)) for k in range(3)]
    return pl.pallas_call(
        _mix_kernel,
        grid=(m // tm, nn),
        in_specs=[row(d), row(d), vec, vec, row(bw), row(bw), row(bw), *wgs, *bgs,
                  pl.BlockSpec((3, bw, tn), lambda i, n: (0, 0, n)),
                  pl.BlockSpec((tn, d), lambda i, n: (n, 0)), vec, vec],
        out_specs=pl.BlockSpec((tm, d), lambda i, n: (i, 0)),
        out_shape=jax.ShapeDtypeStruct((m, d), F32),
        scratch_shapes=[pltpu.VMEM((tm, d), F32)],
        compiler_params=_params("parallel", "arbitrary"),
        name="mix",
    )(x2, hb, lg, lb, a, b, c, wg, wg, wg, bg, bg, bg, wb, wo, g1, b1)


HALO = 16


def _gelu_tanh(x):
    return 0.5 * x * (1.0 + jnp.tanh(math.sqrt(2.0 / math.pi) * (x + 0.044715 * (x * x * x))))


def _ffn_kernel(h_ref, hp_ref, hn_ref, wv_ref, wg_ref, cwv_ref, cwg_ref, cbv_ref, cbg_ref, wd_ref,
                g2_ref, b2_ref, o_ref, hx_scr, acc_scr, *, tiles_per_seq):
    i = pl.program_id(0)
    f = pl.program_id(1)
    tm = h_ref.shape[0]
    nrow = tm // LN_ROWS

    @pl.when(f == 0)
    def _():
        seq_first = (i % tiles_per_seq) == 0
        seq_last = (i % tiles_per_seq) == tiles_per_seq - 1
        hx_scr[0:HALO, :] = jnp.where(seq_first, 0.0, hp_ref[...]).astype(BF16)
        hx_scr[HALO:HALO + tm, :] = h_ref[...].astype(BF16)
        hx_scr[HALO + tm:, :] = jnp.where(seq_last, 0.0, hn_ref[...]).astype(BF16)
        acc_scr[...] = jnp.zeros_like(acc_scr)

    hx = hx_scr[...]
    ext = tm + 2 * HALO

    def conv(u, cw_ref, cb_ref):
        prev = pltpu.roll(u, 1, 0)[HALO:HALO + tm]
        nxt = pltpu.roll(u, ext - 1, 0)[HALO:HALO + tm]
        return prev * cw_ref[0:1, :] + u[HALO:HALO + tm] * cw_ref[1:2, :] + nxt * cw_ref[2:3, :] + cb_ref[...]

    gelu_gate = _gelu_tanh(conv(_dot(hx, wg_ref[...]), cwg_ref, cbg_ref))
    val = conv(_dot(hx, wv_ref[...]), cwv_ref, cbv_ref)
    acc_scr[...] += _dot((gelu_gate * val).astype(BF16), wd_ref[...])

    @pl.when(f == pl.num_programs(1) - 1)
    def _():
        def body(r, c):
            rows = pl.ds(pl.multiple_of(r * LN_ROWS, LN_ROWS), LN_ROWS)
            o_ref[rows, :] = _ln_rows(ALPHA * h_ref[rows, :] + acc_scr[rows, :], g2_ref[...], b2_ref[...])
            return c
        lax.fori_loop(0, nrow, body, 0)


def _ffn(h1, w_up, cw, cb, w_down, g2, b2, seq, tm=512, tf=512):
    m, d = h1.shape
    nf = D_FF_PAD // tf
    hb = tm // HALO
    nhb = m // HALO
    vec = pl.BlockSpec((1, d), lambda i, f: (0, 0))
    return pl.pallas_call(
        functools.partial(_ffn_kernel, tiles_per_seq=seq // tm),
        grid=(m // tm, nf),
        in_specs=[pl.BlockSpec((tm, d), lambda i, f: (i, 0)),
                  pl.BlockSpec((HALO, d), lambda i, f: (jnp.maximum(i * hb - 1, 0), 0)),
                  pl.BlockSpec((HALO, d), lambda i, f: (jnp.minimum((i + 1) * hb, nhb - 1), 0)),
                  pl.BlockSpec((d, tf), lambda i, f: (0, f)),
                  pl.BlockSpec((d, tf), lambda i, f: (0, nf + f)),
                  pl.BlockSpec((3, tf), lambda i, f: (0, f)),
                  pl.BlockSpec((3, tf), lambda i, f: (0, nf + f)),
                  pl.BlockSpec((1, tf), lambda i, f: (0, f)),
                  pl.BlockSpec((1, tf), lambda i, f: (0, nf + f)),
                  pl.BlockSpec((tf, d), lambda i, f: (f, 0)), vec, vec],
        out_specs=pl.BlockSpec((tm, d), lambda i, f: (i, 0)),
        out_shape=jax.ShapeDtypeStruct((m, d), F32),
        scratch_shapes=[pltpu.VMEM((tm + 2 * HALO, d), BF16), pltpu.VMEM((tm, d), F32)],
        compiler_params=_params("parallel", "arbitrary"),
        name="ffn",
    )(h1, h1, h1, w_up, w_up, cw, cw, cb, cb, w_down, g2, b2)


def _w_up_prep_kernel(w_ref, o_ref):
    o_ref[:, 0:D_FF] = w_ref[...].astype(BF16)
    o_ref[:, D_FF:] = jnp.zeros((o_ref.shape[0], D_FF_PAD - D_FF), BF16)


def _w_up_prep(w_up, tr=256):
    d = w_up.shape[0]
    return pl.pallas_call(
        _w_up_prep_kernel,
        grid=(2, d // tr),
        in_specs=[pl.BlockSpec((tr, D_FF), lambda half, r: (r, half))],
        out_specs=pl.BlockSpec((tr, D_FF_PAD), lambda half, r: (r, half)),
        out_shape=jax.ShapeDtypeStruct((d, 2 * D_FF_PAD), BF16),
        compiler_params=_params("parallel", "parallel"),
        name="w_up_prep",
    )(w_up)


def _pad_ff(t, axis):
    val, gate = jnp.split(t, 2, axis=axis)
    pad = [(0, 0)] * t.ndim
    pad[axis] = (0, D_FF_PAD - D_FF)
    return jnp.concatenate([jnp.pad(val, pad), jnp.pad(gate, pad)], axis=axis)


def kernel(x, mem, ln_in_g, ln_in_b, rel_table, w_in, w_mem_kv, diff_lq1, diff_lk1, diff_lq2, diff_lk2,
           diff_subln_g, win_sink, w_gate, b_gate, w_branch, w_o, ln1_g, ln1_b, w_up, conv_w, conv_b,
           w_down, ln2_g, ln2_b):
    assert w_in.shape[0] == DEPTH == 1
    bsz, seq, d = x.shape
    x2 = x.reshape(bsz * seq, d)
    row = lambda v: v.reshape(1, -1).astype(F32)
    l = 0
    proj, hb = _ln_proj(x2, row(ln_in_g), row(ln_in_b), w_in[l])
    proj = proj.reshape(bsz, seq, IN_W)
    memkv = _mem_kv(mem.reshape(bsz * N_MEM, d), w_mem_kv[l]).reshape(bsz, N_MEM, 2 * BRANCH_W)

    tq_d, tb_d, tq_w = 512, 256, 256
    a = _diff_attn(proj, _diff_bias_tiles(rel_table[:, :DIFF_HEADS] * LOG2E, tb_d), row(diff_lq1[l]), row(diff_lk1[l]),
                   row(diff_lq2[l]), row(diff_lk2[l]), row(diff_subln_g[l]), tq=tq_d)
    b = _win_attn(proj, _win_bias_tiles(rel_table[:, DIFF_HEADS:] * LOG2E, tq_w), win_sink[l].astype(F32) * LOG2E,
                  tq=tq_w)
    c = _mem_attn(proj, memkv)

    m = bsz * seq
    h1 = _mix(x2, hb, row(ln_in_g), row(ln_in_b), a.reshape(m, -1), b.reshape(m, -1), c.reshape(m, -1),
              w_gate[l].astype(BF16), row(b_gate[l]), w_branch[l], w_o[l],
              row(ln1_g[l]), row(ln1_b[l]))

    w_up_p = _w_up_prep(w_up[l].astype(F32))
    cw_p = _pad_ff(conv_w[l].astype(F32), 1)
    cb_p = _pad_ff(row(conv_b[l]), 1)
    w_down_p = jnp.pad(w_down[l], ((0, D_FF_PAD - D_FF), (0, 0))).astype(BF16)
    out = _ffn(h1, w_up_p, cw_p, cb_p, w_down_p, row(ln2_g[l]), row(ln2_b[l]), seq)
    return out.reshape(bsz, seq, d)
```

```python
import functools
import math

import jax
import jax.numpy as jnp
from jax import lax
from jax.experimental import pallas as pl
from jax.experimental.pallas import tpu as pltpu

F32 = jnp.float32
BF16 = jnp.bfloat16

D_MODEL = 2048
SEQ = 2048
N_MEM = 256
HEAD_DIM = 128
BRANCH_W = 1024
DIFF_HEADS = 8
DIFF_QK = 64
WIN_HEADS = 8
WIN_KV_HEADS = 2
WIN_GROUP = WIN_HEADS // WIN_KV_HEADS
WINDOW = 128
MEM_HEADS = 4
MEM_DIM = 256
OFF_DQ, OFF_DK, OFF_DV, OFF_WQ, OFF_WK, OFF_WV, OFF_MQ = 0, 1024, 2048, 3072, 4096, 4352, 4608
IN_W = 5632
D_FF = 5504
D_FF_PAD = 5632
REL_BUCKETS = 32
REL_MAX_DIST = 128
DEPTH = 1
ALPHA = (2 * DEPTH) ** 0.25
LN_EPS = 1e-5
NEG = -1e30
LOG2E = math.log2(math.e)
LAMBDA_INIT = 0.8 - 0.6 * math.exp(-0.3 * 0)

VMEM_LIMIT = 56 * 1024 * 1024
LN_ROWS = 128


def _ln_rows(x, g, b):
    mu = jnp.mean(x, axis=-1, keepdims=True)
    xc = x - mu
    var = jnp.mean(xc * xc, axis=-1, keepdims=True)
    return xc * lax.rsqrt(var + LN_EPS) * g + b


def _dot(a, b):
    return jnp.dot(a, b, preferred_element_type=F32)


def _dot_nt(a, b):
    return lax.dot_general(a, b, (((1,), (1,)), ((), ())), preferred_element_type=F32)


def _params(*sem):
    return pltpu.CompilerParams(dimension_semantics=sem, vmem_limit_bytes=VMEM_LIMIT)


def _ln_proj_kernel(x_ref, g_ref, b_ref, w_ref, o_ref, h_ref):
    @pl.when(pl.program_id(1) == 0)
    def _():
        def body(r, c):
            rows = pl.ds(pl.multiple_of(r * LN_ROWS, LN_ROWS), LN_ROWS)
            h_ref[rows, :] = _ln_rows(x_ref[rows, :], g_ref[...], b_ref[...]).astype(BF16)
            return c
        lax.fori_loop(0, x_ref.shape[0] // LN_ROWS, body, 0)

    o_ref[...] = _dot(h_ref[...], w_ref[...].astype(BF16)).astype(BF16)


def _ln_proj(x2, g, b, w, tm=1024, tn=512):
    m, d = x2.shape
    n = w.shape[1]
    return pl.pallas_call(
        _ln_proj_kernel,
        grid=(m // tm, n // tn),
        in_specs=[pl.BlockSpec((tm, d), lambda i, j: (i, 0)),
                  pl.BlockSpec((1, d), lambda i, j: (0, 0)),
                  pl.BlockSpec((1, d), lambda i, j: (0, 0)),
                  pl.BlockSpec((d, tn), lambda i, j: (0, j))],
        out_specs=[pl.BlockSpec((tm, tn), lambda i, j: (i, j)),
                   pl.BlockSpec((tm, d), lambda i, j: (i, 0))],
        out_shape=[jax.ShapeDtypeStruct((m, n), BF16), jax.ShapeDtypeStruct((m, d), BF16)],
        compiler_params=_params("parallel", "arbitrary"),
        name="ln_proj",
    )(x2, g, b, w)


def _mem_kv_kernel(m_ref, w_ref, o_ref):
    o_ref[...] = _dot(m_ref[...].astype(BF16), w_ref[...].astype(BF16)).astype(BF16)


def _mem_kv(mem2, w, tn=512):
    m, d = mem2.shape
    n = w.shape[1]
    return pl.pallas_call(
        _mem_kv_kernel,
        grid=(n // tn,),
        in_specs=[pl.BlockSpec((m, d), lambda j: (0, 0)),
                  pl.BlockSpec((d, tn), lambda j: (0, j))],
        out_specs=pl.BlockSpec((m, tn), lambda j: (0, j)),
        out_shape=jax.ShapeDtypeStruct((m, n), BF16),
        compiler_params=_params("parallel"),
        name="mem_kv",
    )(mem2, w)


def _bucket_lower_bounds():
    half = REL_BUCKETS // 2
    max_exact = half // 2
    n = jnp.arange(REL_MAX_DIST + 1)
    log_part = (jnp.log(jnp.maximum(n, 1).astype(F32) / max_exact) / math.log(REL_MAX_DIST / max_exact)
                * (half - max_exact)).astype(jnp.int32)
    bucket = jnp.where(n < max_exact, n, jnp.minimum(max_exact + log_part, half - 1))
    bucket = jnp.arange(half, dtype=jnp.int32)[bucket]
    ids = jnp.arange(half, dtype=jnp.int32)
    return jnp.sum((bucket[None, :] < ids[:, None]).astype(jnp.int32), axis=1)


BIAS_ROWS = 64


def _bias_kernel(lb_ref, tab_ref, off_ref, o_ref, *, limit):
    h = pl.program_id(0)
    half = REL_BUCKETS // 2
    ncol = o_ref.shape[2]

    def body(r, carry):
        rows = pl.ds(pl.multiple_of(r * BIAS_ROWS, BIAS_ROWS), BIAS_ROWS)
        rel = lax.broadcasted_iota(jnp.int32, (BIAS_ROWS, ncol), 1) + off_ref[rows, :]
        n = jnp.abs(rel)
        neg = jnp.full(rel.shape, tab_ref[0, h], F32)
        pos = jnp.full(rel.shape, tab_ref[half, h], F32)
        for b in range(1, half):
            reached = n >= lb_ref[b]
            neg = jnp.where(reached, tab_ref[b, h], neg)
            pos = jnp.where(reached, tab_ref[half + b, h], pos)
        out = jnp.where(rel > 0, pos, neg)
        if limit is not None:
            out = jnp.where(n > limit, NEG, out)
        o_ref[0, rows, :] = out
        return carry
    lax.fori_loop(0, o_ref.shape[1] // BIAS_ROWS, body, 0)


def _bias_tiles(table, row_offset, ncol, limit=None):
    nh = table.shape[1]
    nrow = row_offset.shape[0]
    return pl.pallas_call(
        functools.partial(_bias_kernel, limit=limit),
        grid=(nh,),
        in_specs=[pl.BlockSpec(memory_space=pltpu.SMEM), pl.BlockSpec(memory_space=pltpu.SMEM),
                  pl.BlockSpec((nrow, 1), lambda h: (0, 0))],
        out_specs=pl.BlockSpec((1, nrow, ncol), lambda h: (h, 0, 0)),
        out_shape=jax.ShapeDtypeStruct((nh, nrow, ncol), F32),
        compiler_params=_params("parallel"),
        name="bias_tiles",
    )(_bucket_lower_bounds(), table.astype(F32), row_offset.astype(jnp.int32).reshape(nrow, 1))


def _diff_bias_tiles(table, tq):
    r = jnp.arange(5 * tq)
    return _bias_tiles(table, (r // tq - 2) * tq - r % tq, tq).reshape(table.shape[1], 5, tq, tq)


def _win_bias_tiles(table, tq):
    return _bias_tiles(table, -WINDOW - jnp.arange(tq), tq + 2 * WINDOW, limit=WINDOW)


def _diff_kernel(q_ref, k_ref, v_ref, band_ref, lq1_ref, lk1_ref, lq2_ref, lk2_ref, g_ref, o_ref,
                 s_scr, mx_scr, *, tq, rc, n_tiles, tiles_per_head):
    t = pl.program_id(0)
    hd = v_ref.shape[2]

    @pl.when(t == 0)
    def _():
        s_scr[...] = jnp.zeros_like(s_scr)
        mx_scr[...] = jnp.zeros_like(mx_scr)

    def tick(slot):
        other = 1 - slot
        i = jnp.minimum(t, n_tiles - 1) % tiles_per_head
        nk = k_ref.shape[1] // tq
        q = (q_ref[0].astype(F32) * (DIFF_QK ** -0.5 * LOG2E)).astype(BF16)
        lane = lax.broadcasted_iota(jnp.int32, q.shape, 1)
        zero = jnp.zeros_like(q)
        qs = jnp.concatenate([jnp.where(lane < DIFF_QK, q, zero), jnp.where(lane >= DIFF_QK, q, zero)], axis=0)
        lanes = mx_scr.shape[-1]
        tb = band_ref.shape[2]
        nb = tq // tb
        for j in range(nk):
            s = _dot_nt(qs, k_ref[0, j * tq:(j + 1) * tq, :])
            for r0 in range(0, 2 * tq, tb):
                rows = slice(r0, r0 + tb)
                fold = None
                for cb in range(nb):
                    d = (j - i) * nb + cb - (r0 // tb) % nb
                    sb = s[rows, cb * tb:(cb + 1) * tb] + band_ref[0, jnp.clip(d, -2, 2) + 2]
                    s_scr[slot, rows, j * tq + cb * tb:j * tq + (cb + 1) * tb] = sb
                    fold = functools.reduce(jnp.maximum, [sb[:, c:c + lanes] for c in range(0, tb, lanes)]
                                            + ([] if fold is None else [fold]))
                mx_scr[slot, rows, :] = fold if j == 0 else jnp.maximum(mx_scr[slot, rows, :], fold)

        nchunk = 2 * tq // rc
        chunk = lambda r: slice(r * rc, (r + 1) * rc)
        p = jnp.concatenate(
            [jnp.exp2(s_scr[other, chunk(r), :] - jnp.max(mx_scr[other, chunk(r), :], axis=-1, keepdims=True))
             .astype(BF16) for r in range(nchunk)], axis=0)
        v_ones = jnp.concatenate([v_ref[0], jnp.ones(v_ref.shape[1:], BF16)], axis=1)
        pv = _dot(p, v_ones)
        lam = (jnp.exp(jnp.sum(lq1_ref[...] * lk1_ref[...], axis=-1, keepdims=True))
               - jnp.exp(jnp.sum(lq2_ref[...] * lk2_ref[...], axis=-1, keepdims=True)) + LAMBDA_INIT)
        o = pv[0:tq, 0:hd] * (1.0 / pv[0:tq, hd:]) - pv[tq:, 0:hd] * (lam / pv[tq:, hd:])
        ms = jnp.mean(o * o, axis=-1, keepdims=True)
        o_ref[0] = (o * lax.rsqrt(ms + LN_EPS) * g_ref[...] * (1.0 - LAMBDA_INIT)).astype(BF16)

    pl.when(t % 2 == 0)(functools.partial(tick, 0))
    pl.when(t % 2 == 1)(functools.partial(tick, 1))


def _diff_attn(proj, band, lq1, lk1, lq2, lk2, g, tq=256, rc=32):
    b, s, _ = proj.shape
    cb = HEAD_DIM
    ni = s // tq
    n_tiles = b * DIFF_HEADS * ni
    depth = 1

    def tile(t, lag):
        tc = jnp.clip(t - lag, 0, n_tiles - 1)
        return tc // (DIFF_HEADS * ni), (tc // ni) % DIFF_HEADS, tc % ni

    def q_map(t):
        bi, h, i = tile(t, 0)
        return bi, i, OFF_DQ // cb + h

    def k_map(t):
        bi, h, _ = tile(t, 0)
        return bi, 0, OFF_DK // cb + h

    def v_map(t):
        bi, h, _ = tile(t, depth)
        return bi, 0, OFF_DV // cb + h

    def o_map(t):
        bi, h, i = tile(t, depth)
        return bi, i, h

    vec = lambda n: pl.BlockSpec((1, n), lambda t: (0, 0))
    return pl.pallas_call(
        functools.partial(_diff_kernel, tq=tq, rc=rc, n_tiles=n_tiles, tiles_per_head=ni),
        grid=(n_tiles + depth,),
        in_specs=[pl.BlockSpec((1, tq, cb), q_map),
                  pl.BlockSpec((1, s, cb), k_map),
                  pl.BlockSpec((1, s, cb), v_map),
                  pl.BlockSpec((1,) + band.shape[1:], lambda t: (tile(t, 0)[1], 0, 0, 0)),
                  vec(DIFF_QK), vec(DIFF_QK), vec(DIFF_QK), vec(DIFF_QK), vec(HEAD_DIM)],
        out_specs=pl.BlockSpec((1, tq, cb), o_map),
        out_shape=jax.ShapeDtypeStruct((b, s, BRANCH_W), BF16),
        scratch_shapes=[pltpu.VMEM((2, 2 * tq, s), F32), pltpu.VMEM((2, 2 * tq, cb), F32)],
        compiler_params=_params("arbitrary"),
        name="diff_attn",
    )(proj, proj, proj, band, lq1, lk1, lq2, lk2, g)


def _win_kernel(sink_ref, q_ref, kp_ref, km_ref, kn_ref, vp_ref, vm_ref, vn_ref, bias_ref, o_ref, *, tq):
    g = pl.program_id(1)
    i = pl.program_id(2)
    first = i == 0
    last = i == pl.num_programs(2) - 1
    w = WINDOW
    hd = HEAD_DIM
    tb = bias_ref.shape[1]
    nkeys = tb + 2 * w
    keys = jnp.concatenate([kp_ref[0], km_ref[0], kn_ref[0]], axis=0)
    vals = jnp.concatenate([vp_ref[0], vm_ref[0], vn_ref[0]], axis=0)
    v_ones = jnp.concatenate([vals, jnp.ones((tq + 2 * w, hd), BF16)], axis=1)
    col = lax.broadcasted_iota(jnp.int32, (1, nkeys), 1)
    scale = jnp.asarray(hd ** -0.5 * LOG2E, F32)
    for r0 in range(0, tq, tb):
        rows = slice(r0, r0 + tb)
        band = slice(r0, r0 + nkeys)
        outside = None
        if r0 == 0:
            outside = jnp.logical_and(first, col < w)
        if r0 + tb == tq:
            past_end = jnp.logical_and(last, col >= tb + w)
            outside = past_end if outside is None else jnp.logical_or(outside, past_end)
        for j in range(WIN_GROUP):
            cols = slice(j * hd, (j + 1) * hd)
            s = _dot_nt(q_ref[0, rows, cols], keys[band]) * scale + bias_ref[j]
            if outside is not None:
                s = jnp.where(outside, NEG, s)
            sink = sink_ref[g * WIN_GROUP + j]
            fold = functools.reduce(jnp.maximum, [s[:, c:c + w] for c in range(0, nkeys, w)])
            m = jnp.maximum(jnp.max(fold, axis=-1, keepdims=True), sink)
            pv = _dot(jnp.exp2(s - m).astype(BF16), v_ones[band])
            o_ref[0, rows, cols] = (pv[:, 0:hd] / (pv[:, hd:] + jnp.exp2(sink - m))).astype(BF16)


def _win_attn(proj, bias, sink, tq=256):
    b, s, _ = proj.shape
    hd = HEAD_DIM
    nb = s // WINDOW
    r = tq // WINDOW
    prev = lambda bi, g, i: jnp.maximum(i * r - 1, 0)
    nxt = lambda bi, g, i: jnp.minimum((i + 1) * r, nb - 1)
    kcol, vcol = OFF_WK // hd, OFF_WV // hd
    return pl.pallas_call(
        functools.partial(_win_kernel, tq=tq),
        grid=(b, WIN_KV_HEADS, s // tq),
        in_specs=[pl.BlockSpec(memory_space=pltpu.SMEM),
                  pl.BlockSpec((1, tq, WIN_GROUP * hd), lambda bi, g, i: (bi, i, OFF_WQ // (WIN_GROUP * hd) + g)),
                  pl.BlockSpec((1, WINDOW, hd), lambda bi, g, i: (bi, prev(bi, g, i), kcol + g)),
                  pl.BlockSpec((1, tq, hd), lambda bi, g, i: (bi, i, kcol + g)),
                  pl.BlockSpec((1, WINDOW, hd), lambda bi, g, i: (bi, nxt(bi, g, i), kcol + g)),
                  pl.BlockSpec((1, WINDOW, hd), lambda bi, g, i: (bi, prev(bi, g, i), vcol + g)),
                  pl.BlockSpec((1, tq, hd), lambda bi, g, i: (bi, i, vcol + g)),
                  pl.BlockSpec((1, WINDOW, hd), lambda bi, g, i: (bi, nxt(bi, g, i), vcol + g)),
                  pl.BlockSpec((WIN_GROUP,) + bias.shape[1:], lambda bi, g, i: (g, 0, 0))],
        out_specs=pl.BlockSpec((1, tq, WIN_GROUP * hd), lambda bi, g, i: (bi, i, g)),
        out_shape=jax.ShapeDtypeStruct((b, s, BRANCH_W), BF16),
        compiler_params=_params("parallel", "parallel", "arbitrary"),
        name="win_attn",
    )(sink, proj, proj, proj, proj, proj, proj, proj, bias)


def _mem_attn_kernel(q_ref, k_ref, v_ref, o_ref):
    s = _dot_nt(q_ref[0], k_ref[0]) * jnp.asarray(MEM_DIM ** -0.5, F32)
    p = jnp.exp(s - jnp.max(s, axis=-1, keepdims=True))
    p = p * (1.0 / jnp.sum(p, axis=-1, keepdims=True))
    o_ref[0] = _dot(p.astype(BF16), v_ref[0]).astype(BF16)


def _mem_attn(proj, memkv, tq=2048):
    b, s, _ = proj.shape
    md = MEM_DIM
    return pl.pallas_call(
        _mem_attn_kernel,
        grid=(b, MEM_HEADS, s // tq),
        in_specs=[pl.BlockSpec((1, tq, md), lambda bi, h, i: (bi, i, OFF_MQ // md + h)),
                  pl.BlockSpec((1, N_MEM, md), lambda bi, h, i: (bi, 0, h)),
                  pl.BlockSpec((1, N_MEM, md), lambda bi, h, i: (bi, 0, MEM_HEADS + h))],
        out_specs=pl.BlockSpec((1, tq, md), lambda bi, h, i: (bi, i, h)),
        out_shape=jax.ShapeDtypeStruct((b, s, BRANCH_W), BF16),
        compiler_params=_params("parallel", "parallel", "arbitrary"),
        name="mem_attn",
    )(proj, memkv, memkv)


def _mix_kernel(x_ref, h_ref, lg_ref, lb_ref, a_ref, b_ref, c_ref, wg0_ref, wg1_ref, wg2_ref,
                bg0_ref, bg1_ref, bg2_ref, wb_ref, wo_ref, g1_ref, b1_ref, o_ref, acc_scr):
    n = pl.program_id(1)
    nrow = x_ref.shape[0] // LN_ROWS

    @pl.when(n == 0)
    def _():
        acc_scr[...] = jnp.zeros_like(acc_scr)

    h = h_ref[...]
    gates = [jax.nn.sigmoid(_dot(h, wg_ref[...]) + bg_ref[...])
             for wg_ref, bg_ref in ((wg0_ref, bg0_ref), (wg1_ref, bg1_ref), (wg2_ref, bg2_ref))]
    mixed = None
    for k, br_ref in enumerate((a_ref, b_ref, c_ref)):
        term = gates[k] * _dot(br_ref[...], wb_ref[k].astype(BF16))
        mixed = term if mixed is None else mixed + term
    acc_scr[...] += _dot(mixed.astype(BF16), wo_ref[...].astype(BF16))

    @pl.when(n == pl.num_programs(1) - 1)
    def _():
        def body(r, c):
            rows = pl.ds(pl.multiple_of(r * LN_ROWS, LN_ROWS), LN_ROWS)
            hh = _ln_rows(x_ref[rows, :], lg_ref[...], lb_ref[...])
            o_ref[rows, :] = _ln_rows(ALPHA * hh + acc_scr[rows, :], g1_ref[...], b1_ref[...])
            return c
        lax.fori_loop(0, nrow, body, 0)


def _mix(x2, hb, lg, lb, a, b, c, wg, bg, wb, wo, g1, b1, tm=512, tn=256):
    m, d = x2.shape
    bw = a.shape[1]
    nn = d // tn
    row = lambda w: pl.BlockSpec((tm, w), lambda i, n: (i, 0))
    vec = pl.BlockSpec((1, d), lambda i, n: (0, 0))
    wgs = [pl.BlockSpec((d, tn), functools.partial(lambda i, n, k: (0, k * nn + n), k=k)) for k in range(3)]
    bgs = [pl.BlockSpec((1, tn), functools.partial(lambda i, n, k: (0, k * nn + n), k=k)) for k in range(3)]
    return pl.pallas_call(
        _mix_kernel,
        grid=(m // tm, nn),
        in_specs=[row(d), row(d), vec, vec, row(bw), row(bw), row(bw), *wgs, *bgs,
                  pl.BlockSpec((3, bw, tn), lambda i, n: (0, 0, n)),
                  pl.BlockSpec((tn, d), lambda i, n: (n, 0)), vec, vec],
        out_specs=pl.BlockSpec((tm, d), lambda i, n: (i, 0)),
        out_shape=jax.ShapeDtypeStruct((m, d), F32),
        scratch_shapes=[pltpu.VMEM((tm, d), F32)],
        compiler_params=_params("parallel", "arbitrary"),
        name="mix",
    )(x2, hb, lg, lb, a, b, c, wg, wg, wg, bg, bg, bg, wb, wo, g1, b1)


HALO = 16


def _gelu_tanh(x):
    return 0.5 * x * (1.0 + jnp.tanh(math.sqrt(2.0 / math.pi) * (x + 0.044715 * (x * x * x))))


def _ffn_kernel(h_ref, hp_ref, hn_ref, wv_ref, wg_ref, cwv_ref, cwg_ref, cbv_ref, cbg_ref, wd_ref,
                g2_ref, b2_ref, o_ref, hx_scr, acc_scr, *, tiles_per_seq):
    i = pl.program_id(0)
    f = pl.program_id(1)
    tm = h_ref.shape[0]
    nrow = tm // LN_ROWS

    @pl.when(f == 0)
    def _():
        seq_first = (i % tiles_per_seq) == 0
        seq_last = (i % tiles_per_seq) == tiles_per_seq - 1
        hx_scr[0:HALO, :] = jnp.where(seq_first, 0.0, hp_ref[...]).astype(BF16)
        hx_scr[HALO:HALO + tm, :] = h_ref[...].astype(BF16)
        hx_scr[HALO + tm:, :] = jnp.where(seq_last, 0.0, hn_ref[...]).astype(BF16)
        acc_scr[...] = jnp.zeros_like(acc_scr)

    hx = hx_scr[...]
    ext = tm + 2 * HALO

    def conv(u, cw_ref, cb_ref):
        prev = pltpu.roll(u, 1, 0)[HALO:HALO + tm]
        nxt = pltpu.roll(u, ext - 1, 0)[HALO:HALO + tm]
        return prev * cw_ref[0:1, :] + u[HALO:HALO + tm] * cw_ref[1:2, :] + nxt * cw_ref[2:3, :] + cb_ref[...]

    gelu_gate = _gelu_tanh(conv(_dot(hx, wg_ref[...]), cwg_ref, cbg_ref))
    val = conv(_dot(hx, wv_ref[...]), cwv_ref, cbv_ref)
    acc_scr[...] += _dot((gelu_gate * val).astype(BF16), wd_ref[...])

    @pl.when(f == pl.num_programs(1) - 1)
    def _():
        def body(r, c):
            rows = pl.ds(pl.multiple_of(r * LN_ROWS, LN_ROWS), LN_ROWS)
            o_ref[rows, :] = _ln_rows(ALPHA * h_ref[rows, :] + acc_scr[rows, :], g2_ref[...], b2_ref[...])
            return c
        lax.fori_loop(0, nrow, body, 0)


def _ffn(h1, w_up, cw, cb, w_down, g2, b2, seq, tm=512, tf=512):
    m, d = h1.shape
    nf = D_FF_PAD // tf
    hb = tm // HALO
    nhb = m // HALO
    vec = pl.BlockSpec((1, d), lambda i, f: (0, 0))
    return pl.pallas_call(
        functools.partial(_ffn_kernel, tiles_per_seq=seq // tm),
        grid=(m // tm, nf),
        in_specs=[pl.BlockSpec((tm, d), lambda i, f: (i, 0)),
                  pl.BlockSpec((HALO, d), lambda i, f: (jnp.maximum(i * hb - 1, 0), 0)),
                  pl.BlockSpec((HALO, d), lambda i, f: (jnp.minimum((i + 1) * hb, nhb - 1), 0)),
                  pl.BlockSpec((d, tf), lambda i, f: (0, f)),
                  pl.BlockSpec((d, tf), lambda i, f: (0, nf + f)),
                  pl.BlockSpec((3, tf), lambda i, f: (0, f)),
                  pl.BlockSpec((3, tf), lambda i, f: (0, nf + f)),
                  pl.BlockSpec((1, tf), lambda i, f: (0, f)),
                  pl.BlockSpec((1, tf), lambda i, f: (0, nf + f)),
                  pl.BlockSpec((tf, d), lambda i, f: (f, 0)), vec, vec],
        out_specs=pl.BlockSpec((tm, d), lambda i, f: (i, 0)),
        out_shape=jax.ShapeDtypeStruct((m, d), F32),
        scratch_shapes=[pltpu.VMEM((tm + 2 * HALO, d), BF16), pltpu.VMEM((tm, d), F32)],
        compiler_params=_params("parallel", "arbitrary"),
        name="ffn",
    )(h1, h1, h1, w_up, w_up, cw, cw, cb, cb, w_down, g2, b2)


def _w_up_prep_kernel(w_ref, o_ref):
    o_ref[:, 0:D_FF] = w_ref[...].astype(BF16)
    o_ref[:, D_FF:] = jnp.zeros((o_ref.shape[0], D_FF_PAD - D_FF), BF16)


def _w_up_prep(w_up, tr=256):
    d = w_up.shape[0]
    return pl.pallas_call(
        _w_up_prep_kernel,
        grid=(2, d // tr),
        in_specs=[pl.BlockSpec((tr, D_FF), lambda half, r: (r, half))],
        out_specs=pl.BlockSpec((tr, D_FF_PAD), lambda half, r: (r, half)),
        out_shape=jax.ShapeDtypeStruct((d, 2 * D_FF_PAD), BF16),
        compiler_params=_params("parallel", "parallel"),
        name="w_up_prep",
    )(w_up)


def _w_down_prep_kernel(w_ref, o_ref, *, n_real):
    real = pl.program_id(0) < n_real
    o_ref[...] = jnp.where(real, w_ref[...], 0.0).astype(BF16)


def _w_down_prep(w_down, tr=128):
    d = w_down.shape[1]
    n_real = D_FF // tr
    return pl.pallas_call(
        functools.partial(_w_down_prep_kernel, n_real=n_real),
        grid=(D_FF_PAD // tr,),
        in_specs=[pl.BlockSpec((tr, d), lambda r: (jnp.minimum(r, n_real - 1), 0))],
        out_specs=pl.BlockSpec((tr, d), lambda r: (r, 0)),
        out_shape=jax.ShapeDtypeStruct((D_FF_PAD, d), BF16),
        compiler_params=_params("parallel"),
        name="w_down_prep",
    )(w_down)


def _pad_ff(t, axis):
    val, gate = jnp.split(t, 2, axis=axis)
    pad = [(0, 0)] * t.ndim
    pad[axis] = (0, D_FF_PAD - D_FF)
    return jnp.concatenate([jnp.pad(val, pad), jnp.pad(gate, pad)], axis=axis)


def kernel(x, mem, ln_in_g, ln_in_b, rel_table, w_in, w_mem_kv, diff_lq1, diff_lk1, diff_lq2, diff_lk2,
           diff_subln_g, win_sink, w_gate, b_gate, w_branch, w_o, ln1_g, ln1_b, w_up, conv_w, conv_b,
           w_down, ln2_g, ln2_b):
    assert w_in.shape[0] == DEPTH == 1
    bsz, seq, d = x.shape
    x2 = x.reshape(bsz * seq, d)
    row = lambda v: v.reshape(1, -1).astype(F32)
    l = 0
    proj, hb = _ln_proj(x2, row(ln_in_g), row(ln_in_b), w_in[l])
    proj = proj.reshape(bsz, seq, IN_W)
    memkv = _mem_kv(mem.reshape(bsz * N_MEM, d), w_mem_kv[l]).reshape(bsz, N_MEM, 2 * BRANCH_W)

    tq_d, tb_d, tq_w, tb_w = 512, 256, 512, 256
    a = _diff_attn(proj, _diff_bias_tiles(rel_table[:, :DIFF_HEADS] * LOG2E, tb_d), row(diff_lq1[l]), row(diff_lk1[l]),
                   row(diff_lq2[l]), row(diff_lk2[l]), row(diff_subln_g[l]), tq=tq_d)
    b = _win_attn(proj, _win_bias_tiles(rel_table[:, DIFF_HEADS:] * LOG2E, tb_w), win_sink[l].astype(F32) * LOG2E,
                  tq=tq_w)
    c = _mem_attn(proj, memkv)

    m = bsz * seq
    h1 = _mix(x2, hb, row(ln_in_g), row(ln_in_b), a.reshape(m, -1), b.reshape(m, -1), c.reshape(m, -1),
              w_gate[l].astype(BF16), row(b_gate[l]), w_branch[l], w_o[l],
              row(ln1_g[l]), row(ln1_b[l]))

    w_up_p = _w_up_prep(w_up[l].astype(F32))
    cw_p = _pad_ff(conv_w[l].astype(F32), 1)
    cb_p = _pad_ff(row(conv_b[l]), 1)
    w_down_p = _w_down_prep(w_down[l].astype(F32))
    out = _ffn(h1, w_up_p, cw_p, cb_p, w_down_p, row(ln2_g[l]), row(ln2_b[l]), seq)
    return out.reshape(bsz, seq, d)
```

```python
import functools
import math

import jax
import jax.numpy as jnp
from jax import lax
from jax.experimental import pallas as pl
from jax.experimental.pallas import tpu as pltpu

F32 = jnp.float32
BF16 = jnp.bfloat16

D_MODEL = 2048
SEQ = 2048
N_MEM = 256
HEAD_DIM = 128
BRANCH_W = 1024
DIFF_HEADS = 8
DIFF_QK = 64
WIN_HEADS = 8
WIN_KV_HEADS = 2
WIN_GROUP = WIN_HEADS // WIN_KV_HEADS
WINDOW = 128
MEM_HEADS = 4
MEM_DIM = 256
OFF_DQ, OFF_DK, OFF_DV, OFF_WQ, OFF_WK, OFF_WV, OFF_MQ = 0, 1024, 2048, 3072, 4096, 4352, 4608
IN_W = 5632
D_FF = 5504
D_FF_PAD = 5632
REL_BUCKETS = 32
REL_MAX_DIST = 128
DEPTH = 1
ALPHA = (2 * DEPTH) ** 0.25
LN_EPS = 1e-5
NEG = -1e30
LOG2E = math.log2(math.e)
LAMBDA_INIT = 0.8 - 0.6 * math.exp(-0.3 * 0)

VMEM_LIMIT = 56 * 1024 * 1024
LN_ROWS = 128


def _ln_rows(x, g, b):
    mu = jnp.mean(x, axis=-1, keepdims=True)
    xc = x - mu
    var = jnp.mean(xc * xc, axis=-1, keepdims=True)
    return xc * lax.rsqrt(var + LN_EPS) * g + b


def _dot(a, b):
    return jnp.dot(a, b, preferred_element_type=F32)


def _dot_nt(a, b):
    return lax.dot_general(a, b, (((1,), (1,)), ((), ())), preferred_element_type=F32)


def _params(*sem):
    return pltpu.CompilerParams(dimension_semantics=sem, vmem_limit_bytes=VMEM_LIMIT)


def _ln_proj_kernel(x_ref, g_ref, b_ref, w_ref, o_ref, h_ref):
    @pl.when(pl.program_id(1) == 0)
    def _():
        def body(r, c):
            rows = pl.ds(pl.multiple_of(r * LN_ROWS, LN_ROWS), LN_ROWS)
            h_ref[rows, :] = _ln_rows(x_ref[rows, :], g_ref[...], b_ref[...]).astype(BF16)
            return c
        lax.fori_loop(0, x_ref.shape[0] // LN_ROWS, body, 0)

    o_ref[...] = _dot(h_ref[...], w_ref[...].astype(BF16)).astype(BF16)


def _ln_proj(x2, g, b, w, tm=1024, tn=512):
    m, d = x2.shape
    n = w.shape[1]
    return pl.pallas_call(
        _ln_proj_kernel,
        grid=(m // tm, n // tn),
        in_specs=[pl.BlockSpec((tm, d), lambda i, j: (i, 0)),
                  pl.BlockSpec((1, d), lambda i, j: (0, 0)),
                  pl.BlockSpec((1, d), lambda i, j: (0, 0)),
                  pl.BlockSpec((d, tn), lambda i, j: (0, j))],
        out_specs=[pl.BlockSpec((tm, tn), lambda i, j: (i, j)),
                   pl.BlockSpec((tm, d), lambda i, j: (i, 0))],
        out_shape=[jax.ShapeDtypeStruct((m, n), BF16), jax.ShapeDtypeStruct((m, d), BF16)],
        compiler_params=_params("parallel", "arbitrary"),
        name="ln_proj",
    )(x2, g, b, w)


def _mem_kv_kernel(m_ref, w_ref, o_ref):
    o_ref[...] = _dot(m_ref[...].astype(BF16), w_ref[...].astype(BF16)).astype(BF16)


def _mem_kv(mem2, w, tn=512):
    m, d = mem2.shape
    n = w.shape[1]
    return pl.pallas_call(
        _mem_kv_kernel,
        grid=(n // tn,),
        in_specs=[pl.BlockSpec((m, d), lambda j: (0, 0)),
                  pl.BlockSpec((d, tn), lambda j: (0, j))],
        out_specs=pl.BlockSpec((m, tn), lambda j: (0, j)),
        out_shape=jax.ShapeDtypeStruct((m, n), BF16),
        compiler_params=_params("parallel"),
        name="mem_kv",
    )(mem2, w)


def _bucket_lower_bounds():
    half = REL_BUCKETS // 2
    max_exact = half // 2
    n = jnp.arange(REL_MAX_DIST + 1)
    log_part = (jnp.log(jnp.maximum(n, 1).astype(F32) / max_exact) / math.log(REL_MAX_DIST / max_exact)
                * (half - max_exact)).astype(jnp.int32)
    bucket = jnp.where(n < max_exact, n, jnp.minimum(max_exact + log_part, half - 1))
    bucket = jnp.arange(half, dtype=jnp.int32)[bucket]
    ids = jnp.arange(half, dtype=jnp.int32)
    return jnp.sum((bucket[None, :] < ids[:, None]).astype(jnp.int32), axis=1)


BIAS_ROWS = 64


def _bias_kernel(lb_ref, tab_ref, off_ref, o_ref, *, limit):
    h = pl.program_id(0)
    half = REL_BUCKETS // 2
    ncol = o_ref.shape[2]

    def body(r, carry):
        rows = pl.ds(pl.multiple_of(r * BIAS_ROWS, BIAS_ROWS), BIAS_ROWS)
        rel = lax.broadcasted_iota(jnp.int32, (BIAS_ROWS, ncol), 1) + off_ref[rows, :]
        n = jnp.abs(rel)
        neg = jnp.full(rel.shape, tab_ref[0, h], F32)
        pos = jnp.full(rel.shape, tab_ref[half, h], F32)
        for b in range(1, half):
            reached = n >= lb_ref[b]
            neg = jnp.where(reached, tab_ref[b, h], neg)
            pos = jnp.where(reached, tab_ref[half + b, h], pos)
        out = jnp.where(rel > 0, pos, neg)
        if limit is not None:
            out = jnp.where(n > limit, NEG, out)
        o_ref[0, rows, :] = out
        return carry
    lax.fori_loop(0, o_ref.shape[1] // BIAS_ROWS, body, 0)


def _bias_tiles(table, row_offset, ncol, limit=None):
    nh = table.shape[1]
    nrow = row_offset.shape[0]
    return pl.pallas_call(
        functools.partial(_bias_kernel, limit=limit),
        grid=(nh,),
        in_specs=[pl.BlockSpec(memory_space=pltpu.SMEM), pl.BlockSpec(memory_space=pltpu.SMEM),
                  pl.BlockSpec((nrow, 1), lambda h: (0, 0))],
        out_specs=pl.BlockSpec((1, nrow, ncol), lambda h: (h, 0, 0)),
        out_shape=jax.ShapeDtypeStruct((nh, nrow, ncol), F32),
        compiler_params=_params("parallel"),
        name="bias_tiles",
    )(_bucket_lower_bounds(), table.astype(F32), row_offset.astype(jnp.int32).reshape(nrow, 1))


def _diff_bias_tiles(table, tq):
    r = jnp.arange(5 * tq)
    return _bias_tiles(table, (r // tq - 2) * tq - r % tq, tq).reshape(table.shape[1], 5, tq, tq)


def _win_bias_tiles(table, tq):
    return _bias_tiles(table, -WINDOW - jnp.arange(tq), tq + 2 * WINDOW, limit=WINDOW)


def _diff_kernel(q_ref, k_ref, v_ref, band_ref, lq1_ref, lk1_ref, lq2_ref, lk2_ref, g_ref, o_ref,
                 s_scr, mx_scr, *, tq, rc, n_tiles, tiles_per_head):
    t = pl.program_id(0)
    hd = v_ref.shape[2]

    @pl.when(t == 0)
    def _():
        s_scr[...] = jnp.zeros_like(s_scr)
        mx_scr[...] = jnp.zeros_like(mx_scr)

    def tick(slot):
        other = 1 - slot
        i = jnp.minimum(t, n_tiles - 1) % tiles_per_head
        nk = k_ref.shape[1] // tq
        q = (q_ref[0].astype(F32) * (DIFF_QK ** -0.5 * LOG2E)).astype(BF16)
        lane = lax.broadcasted_iota(jnp.int32, q.shape, 1)
        zero = jnp.zeros_like(q)
        qs = jnp.concatenate([jnp.where(lane < DIFF_QK, q, zero), jnp.where(lane >= DIFF_QK, q, zero)], axis=0)
        lanes = mx_scr.shape[-1]
        tb = band_ref.shape[2]
        nb = tq // tb
        for j in range(nk):
            s = _dot_nt(qs, k_ref[0, j * tq:(j + 1) * tq, :])
            for r0 in range(0, 2 * tq, tb):
                rows = slice(r0, r0 + tb)
                fold = None
                for cb in range(nb):
                    d = (j - i) * nb + cb - (r0 // tb) % nb
                    sb = s[rows, cb * tb:(cb + 1) * tb] + band_ref[0, jnp.clip(d, -2, 2) + 2]
                    s_scr[slot, rows, j * tq + cb * tb:j * tq + (cb + 1) * tb] = sb
                    fold = functools.reduce(jnp.maximum, [sb[:, c:c + lanes] for c in range(0, tb, lanes)]
                                            + ([] if fold is None else [fold]))
                mx_scr[slot, rows, :] = fold if j == 0 else jnp.maximum(mx_scr[slot, rows, :], fold)

        nchunk = 2 * tq // rc
        chunk = lambda r: slice(r * rc, (r + 1) * rc)
        p = jnp.concatenate(
            [jnp.exp2(s_scr[other, chunk(r), :] - jnp.max(mx_scr[other, chunk(r), :], axis=-1, keepdims=True))
             .astype(BF16) for r in range(nchunk)], axis=0)
        v_ones = jnp.concatenate([v_ref[0], jnp.ones(v_ref.shape[1:], BF16)], axis=1)
        pv = _dot(p, v_ones)
        lam = (jnp.exp(jnp.sum(lq1_ref[...] * lk1_ref[...], axis=-1, keepdims=True))
               - jnp.exp(jnp.sum(lq2_ref[...] * lk2_ref[...], axis=-1, keepdims=True)) + LAMBDA_INIT)
        o = pv[0:tq, 0:hd] * (1.0 / pv[0:tq, hd:]) - pv[tq:, 0:hd] * (lam / pv[tq:, hd:])
        ms = jnp.mean(o * o, axis=-1, keepdims=True)
        o_ref[0] = (o * lax.rsqrt(ms + LN_EPS) * g_ref[...] * (1.0 - LAMBDA_INIT)).astype(BF16)

    pl.when(t % 2 == 0)(functools.partial(tick, 0))
    pl.when(t % 2 == 1)(functools.partial(tick, 1))


def _diff_attn(proj, band, lq1, lk1, lq2, lk2, g, tq=256, rc=32):
    b, s, _ = proj.shape
    cb = HEAD_DIM
    ni = s // tq
    n_tiles = b * DIFF_HEADS * ni
    depth = 1

    def tile(t, lag):
        tc = jnp.clip(t - lag, 0, n_tiles - 1)
        return tc // (DIFF_HEADS * ni), (tc // ni) % DIFF_HEADS, tc % ni

    def q_map(t):
        bi, h, i = tile(t, 0)
        return bi, i, OFF_DQ // cb + h

    def k_map(t):
        bi, h, _ = tile(t, 0)
        return bi, 0, OFF_DK // cb + h

    def v_map(t):
        bi, h, _ = tile(t, depth)
        return bi, 0, OFF_DV // cb + h

    def o_map(t):
        bi, h, i = tile(t, depth)
        return bi, i, h

    vec = lambda n: pl.BlockSpec((1, n), lambda t: (0, 0))
    return pl.pallas_call(
        functools.partial(_diff_kernel, tq=tq, rc=rc, n_tiles=n_tiles, tiles_per_head=ni),
        grid=(n_tiles + depth,),
        in_specs=[pl.BlockSpec((1, tq, cb), q_map),
                  pl.BlockSpec((1, s, cb), k_map),
                  pl.BlockSpec((1, s, cb), v_map),
                  pl.BlockSpec((1,) + band.shape[1:], lambda t: (tile(t, 0)[1], 0, 0, 0)),
                  vec(DIFF_QK), vec(DIFF_QK), vec(DIFF_QK), vec(DIFF_QK), vec(HEAD_DIM)],
        out_specs=pl.BlockSpec((1, tq, cb), o_map),
        out_shape=jax.ShapeDtypeStruct((b, s, BRANCH_W), BF16),
        scratch_shapes=[pltpu.VMEM((2, 2 * tq, s), F32), pltpu.VMEM((2, 2 * tq, cb), F32)],
        compiler_params=_params("arbitrary"),
        name="diff_attn",
    )(proj, proj, proj, band, lq1, lk1, lq2, lk2, g)


def _win_kernel(sink_ref, q_ref, kp_ref, km_ref, kn_ref, vp_ref, vm_ref, vn_ref, bias_ref, o_ref, *, tq):
    g = pl.program_id(1)
    i = pl.program_id(2)
    first = i == 0
    last = i == pl.num_programs(2) - 1
    w = WINDOW
    hd = HEAD_DIM
    tb = bias_ref.shape[1]
    nkeys = tb + 2 * w
    keys = jnp.concatenate([kp_ref[0], km_ref[0], kn_ref[0]], axis=0)
    vals = jnp.concatenate([vp_ref[0], vm_ref[0], vn_ref[0]], axis=0)
    v_ones = jnp.concatenate([vals, jnp.ones((tq + 2 * w, hd), BF16)], axis=1)
    col = lax.broadcasted_iota(jnp.int32, (1, nkeys), 1)
    scale = jnp.asarray(hd ** -0.5 * LOG2E, F32)
    for r0 in range(0, tq, tb):
        rows = slice(r0, r0 + tb)
        band = slice(r0, r0 + nkeys)
        outside = None
        if r0 == 0:
            outside = jnp.logical_and(first, col < w)
        if r0 + tb == tq:
            past_end = jnp.logical_and(last, col >= tb + w)
            outside = past_end if outside is None else jnp.logical_or(outside, past_end)
        for j in range(WIN_GROUP):
            cols = slice(j * hd, (j + 1) * hd)
            s = _dot_nt(q_ref[0, rows, cols], keys[band]) * scale + bias_ref[j]
            if outside is not None:
                s = jnp.where(outside, NEG, s)
            sink = sink_ref[g * WIN_GROUP + j]
            fold = functools.reduce(jnp.maximum, [s[:, c:c + w] for c in range(0, nkeys, w)])
            m = jnp.maximum(jnp.max(fold, axis=-1, keepdims=True), sink)
            pv = _dot(jnp.exp2(s - m).astype(BF16), v_ones[band])
            o_ref[0, rows, cols] = (pv[:, 0:hd] / (pv[:, hd:] + jnp.exp2(sink - m))).astype(BF16)


def _win_attn(proj, bias, sink, tq=256):
    b, s, _ = proj.shape
    hd = HEAD_DIM
    nb = s // WINDOW
    r = tq // WINDOW
    prev = lambda bi, g, i: jnp.maximum(i * r - 1, 0)
    nxt = lambda bi, g, i: jnp.minimum((i + 1) * r, nb - 1)
    kcol, vcol = OFF_WK // hd, OFF_WV // hd
    return pl.pallas_call(
        functools.partial(_win_kernel, tq=tq),
        grid=(b, WIN_KV_HEADS, s // tq),
        in_specs=[pl.BlockSpec(memory_space=pltpu.SMEM),
                  pl.BlockSpec((1, tq, WIN_GROUP * hd), lambda bi, g, i: (bi, i, OFF_WQ // (WIN_GROUP * hd) + g)),
                  pl.BlockSpec((1, WINDOW, hd), lambda bi, g, i: (bi, prev(bi, g, i), kcol + g)),
                  pl.BlockSpec((1, tq, hd), lambda bi, g, i: (bi, i, kcol + g)),
                  pl.BlockSpec((1, WINDOW, hd), lambda bi, g, i: (bi, nxt(bi, g, i), kcol + g)),
                  pl.BlockSpec((1, WINDOW, hd), lambda bi, g, i: (bi, prev(bi, g, i), vcol + g)),
                  pl.BlockSpec((1, tq, hd), lambda bi, g, i: (bi, i, vcol + g)),
                  pl.BlockSpec((1, WINDOW, hd), lambda bi, g, i: (bi, nxt(bi, g, i), vcol + g)),
                  pl.BlockSpec((WIN_GROUP,) + bias.shape[1:], lambda bi, g, i: (g, 0, 0))],
        out_specs=pl.BlockSpec((1, tq, WIN_GROUP * hd), lambda bi, g, i: (bi, i, g)),
        out_shape=jax.ShapeDtypeStruct((b, s, BRANCH_W), BF16),
        compiler_params=_params("parallel", "parallel", "arbitrary"),
        name="win_attn",
    )(sink, proj, proj, proj, proj, proj, proj, proj, bias)


def _mem_attn_kernel(q_ref, k_ref, v_ref, o_ref):
    s = _dot_nt(q_ref[0], k_ref[0]) * jnp.asarray(MEM_DIM ** -0.5, F32)
    p = jnp.exp(s - jnp.max(s, axis=-1, keepdims=True))
    p = p * (1.0 / jnp.sum(p, axis=-1, keepdims=True))
    o_ref[0] = _dot(p.astype(BF16), v_ref[0]).astype(BF16)


def _mem_attn(proj, memkv, tq=2048):
    b, s, _ = proj.shape
    md = MEM_DIM
    return pl.pallas_call(
        _mem_attn_kernel,
        grid=(b, MEM_HEADS, s // tq),
        in_specs=[pl.BlockSpec((1, tq, md), lambda bi, h, i: (bi, i, OFF_MQ // md + h)),
                  pl.BlockSpec((1, N_MEM, md), lambda bi, h, i: (bi, 0, h)),
                  pl.BlockSpec((1, N_MEM, md), lambda bi, h, i: (bi, 0, MEM_HEADS + h))],
        out_specs=pl.BlockSpec((1, tq, md), lambda bi, h, i: (bi, i, h)),
        out_shape=jax.ShapeDtypeStruct((b, s, BRANCH_W), BF16),
        compiler_params=_params("parallel", "parallel", "arbitrary"),
        name="mem_attn",
    )(proj, memkv, memkv)


def _mix_kernel(x_ref, h_ref, lg_ref, lb_ref, a_ref, b_ref, c_ref, wg0_ref, wg1_ref, wg2_ref,
                bg0_ref, bg1_ref, bg2_ref, wb_ref, wo_ref, g1_ref, b1_ref, o_ref, acc_scr):
    n = pl.program_id(1)
    nrow = x_ref.shape[0] // LN_ROWS

    @pl.when(n == 0)
    def _():
        acc_scr[...] = jnp.zeros_like(acc_scr)

    h = h_ref[...]
    gates = [jax.nn.sigmoid(_dot(h, wg_ref[...]) + bg_ref[...])
             for wg_ref, bg_ref in ((wg0_ref, bg0_ref), (wg1_ref, bg1_ref), (wg2_ref, bg2_ref))]
    mixed = None
    for k, br_ref in enumerate((a_ref, b_ref, c_ref)):
        term = gates[k] * _dot(br_ref[...], wb_ref[k].astype(BF16))
        mixed = term if mixed is None else mixed + term
    acc_scr[...] += _dot(mixed.astype(BF16), wo_ref[...].astype(BF16))

    @pl.when(n == pl.num_programs(1) - 1)
    def _():
        def body(r, c):
            rows = pl.ds(pl.multiple_of(r * LN_ROWS, LN_ROWS), LN_ROWS)
            hh = _ln_rows(x_ref[rows, :], lg_ref[...], lb_ref[...])
            o_ref[rows, :] = _ln_rows(ALPHA * hh + acc_scr[rows, :], g1_ref[...], b1_ref[...])
            return c
        lax.fori_loop(0, nrow, body, 0)


def _mix(x2, hb, lg, lb, a, b, c, wg, bg, wb, wo, g1, b1, tm=512, tn=256):
    m, d = x2.shape
    bw = a.shape[1]
    nn = d // tn
    row = lambda w: pl.BlockSpec((tm, w), lambda i, n: (i, 0))
    vec = pl.BlockSpec((1, d), lambda i, n: (0, 0))
    wgs = [pl.BlockSpec((d, tn), functools.partial(lambda i, n, k: (0, k * nn + n), k=k)) for k in range(3)]
    bgs = [pl.BlockSpec((1, tn), functools.partial(lambda i, n, k: (0, k * nn + n), k=k)) for k in range(3)]
    return pl.pallas_call(
        _mix_kernel,
        grid=(m // tm, nn),
        in_specs=[row(d), row(d), vec, vec, row(bw), row(bw), row(bw), *wgs, *bgs,
                  pl.BlockSpec((3, bw, tn), lambda i, n: (0, 0, n)),
                  pl.BlockSpec((tn, d), lambda i, n: (n, 0)), vec, vec],
        out_specs=pl.BlockSpec((tm, d), lambda i, n: (i, 0)),
        out_shape=jax.ShapeDtypeStruct((m, d), F32),
        scratch_shapes=[pltpu.VMEM((tm, d), F32)],
        compiler_params=_params("parallel", "arbitrary"),
        name="mix",
    )(x2, hb, lg, lb, a, b, c, wg, wg, wg, bg, bg, bg, wb, wo, g1, b1)


HALO = 16


def _gelu_tanh(x):
    return 0.5 * x * (1.0 + jnp.tanh(math.sqrt(2.0 / math.pi) * (x + 0.044715 * (x * x * x))))


def _ffn_kernel(h_ref, hp_ref, hn_ref, wv_ref, wg_ref, cwv_ref, cwg_ref, cbv_ref, cbg_ref, wd_ref,
                g2_ref, b2_ref, o_ref, hx_scr, acc_scr, *, tiles_per_seq):
    i = pl.program_id(0)
    f = pl.program_id(1)
    tm = h_ref.shape[0]
    nrow = tm // LN_ROWS

    @pl.when(f == 0)
    def _():
        seq_first = (i % tiles_per_seq) == 0
        seq_last = (i % tiles_per_seq) == tiles_per_seq - 1
        hx_scr[0:HALO, :] = jnp.where(seq_first, 0.0, hp_ref[...]).astype(BF16)
        hx_scr[HALO:HALO + tm, :] = h_ref[...].astype(BF16)
        hx_scr[HALO + tm:, :] = jnp.where(seq_last, 0.0, hn_ref[...]).astype(BF16)
        acc_scr[...] = jnp.zeros_like(acc_scr)

    hx = hx_scr[...]
    ext = tm + 2 * HALO

    def conv(u, cw_ref, cb_ref):
        prev = pltpu.roll(u, 1, 0)[HALO:HALO + tm]
        nxt = pltpu.roll(u, ext - 1, 0)[HALO:HALO + tm]
        return prev * cw_ref[0:1, :] + u[HALO:HALO + tm] * cw_ref[1:2, :] + nxt * cw_ref[2:3, :] + cb_ref[...]

    gelu_gate = _gelu_tanh(conv(_dot(hx, wg_ref[...]), cwg_ref, cbg_ref))
    val = conv(_dot(hx, wv_ref[...]), cwv_ref, cbv_ref)
    acc_scr[...] += _dot((gelu_gate * val).astype(BF16), wd_ref[...])

    @pl.when(f == pl.num_programs(1) - 1)
    def _():
        def body(r, c):
            rows = pl.ds(pl.multiple_of(r * LN_ROWS, LN_ROWS), LN_ROWS)
            o_ref[rows, :] = _ln_rows(ALPHA * h_ref[rows, :] + acc_scr[rows, :], g2_ref[...], b2_ref[...])
            return c
        lax.fori_loop(0, nrow, body, 0)


def _ffn(h1, w_up, cw, cb, w_down, g2, b2, seq, tm=512, tf=512):
    m, d = h1.shape
    nf = D_FF_PAD // tf
    hb = tm // HALO
    nhb = m // HALO
    vec = pl.BlockSpec((1, d), lambda i, f: (0, 0))
    return pl.pallas_call(
        functools.partial(_ffn_kernel, tiles_per_seq=seq // tm),
        grid=(m // tm, nf),
        in_specs=[pl.BlockSpec((tm, d), lambda i, f: (i, 0)),
                  pl.BlockSpec((HALO, d), lambda i, f: (jnp.maximum(i * hb - 1, 0), 0)),
                  pl.BlockSpec((HALO, d), lambda i, f: (jnp.minimum((i + 1) * hb, nhb - 1), 0)),
                  pl.BlockSpec((d, tf), lambda i, f: (0, f)),
                  pl.BlockSpec((d, tf), lambda i, f: (0, nf + f)),
                  pl.BlockSpec((3, tf), lambda i, f: (0, f)),
                  pl.BlockSpec((3, tf), lambda i, f: (0, nf + f)),
                  pl.BlockSpec((1, tf), lambda i, f: (0, f)),
                  pl.BlockSpec((1, tf), lambda i, f: (0, nf + f)),
                  pl.BlockSpec((tf, d), lambda i, f: (f, 0)), vec, vec],
        out_specs=pl.BlockSpec((tm, d), lambda i, f: (i, 0)),
        out_shape=jax.ShapeDtypeStruct((m, d), F32),
        scratch_shapes=[pltpu.VMEM((tm + 2 * HALO, d), BF16), pltpu.VMEM((tm, d), F32)],
        compiler_params=_params("parallel", "arbitrary"),
        name="ffn",
    )(h1, h1, h1, w_up, w_up, cw, cw, cb, cb, w_down, g2, b2)


def _w_up_prep_kernel(w_ref, o_ref):
    o_ref[:, 0:D_FF] = w_ref[...].astype(BF16)
    o_ref[:, D_FF:] = jnp.zeros((o_ref.shape[0], D_FF_PAD - D_FF), BF16)


def _w_up_prep(w_up, tr=256):
    d = w_up.shape[0]
    return pl.pallas_call(
        _w_up_prep_kernel,
        grid=(2, d // tr),
        in_specs=[pl.BlockSpec((tr, D_FF), lambda half, r: (r, half))],
        out_specs=pl.BlockSpec((tr, D_FF_PAD), lambda half, r: (r, half)),
        out_shape=jax.ShapeDtypeStruct((d, 2 * D_FF_PAD), BF16),
        compiler_params=_params("parallel", "parallel"),
        name="w_up_prep",
    )(w_up)


def _w_down_prep_kernel(w_ref, o_ref):
    o_ref[0:D_FF, :] = w_ref[...].astype(BF16)
    o_ref[D_FF:, :] = jnp.zeros((D_FF_PAD - D_FF, o_ref.shape[1]), BF16)


def _w_down_prep(w_down, tc=256):
    d = w_down.shape[1]
    return pl.pallas_call(
        _w_down_prep_kernel,
        grid=(d // tc,),
        in_specs=[pl.BlockSpec((D_FF, tc), lambda c: (0, c))],
        out_specs=pl.BlockSpec((D_FF_PAD, tc), lambda c: (0, c)),
        out_shape=jax.ShapeDtypeStruct((D_FF_PAD, d), BF16),
        compiler_params=_params("parallel"),
        name="w_down_prep",
    )(w_down)


def _pad_ff(t, axis):
    val, gate = jnp.split(t, 2, axis=axis)
    pad = [(0, 0)] * t.ndim
    pad[axis] = (0, D_FF_PAD - D_FF)
    return jnp.concatenate([jnp.pad(val, pad), jnp.pad(gate, pad)], axis=axis)


def kernel(x, mem, ln_in_g, ln_in_b, rel_table, w_in, w_mem_kv, diff_lq1, diff_lk1, diff_lq2, diff_lk2,
           diff_subln_g, win_sink, w_gate, b_gate, w_branch, w_o, ln1_g, ln1_b, w_up, conv_w, conv_b,
           w_down, ln2_g, ln2_b):
    assert w_in.shape[0] == DEPTH == 1
    bsz, seq, d = x.shape
    x2 = x.reshape(bsz * seq, d)
    row = lambda v: v.reshape(1, -1).astype(F32)
    l = 0
    proj, hb = _ln_proj(x2, row(ln_in_g), row(ln_in_b), w_in[l])
    proj = proj.reshape(bsz, seq, IN_W)
    memkv = _mem_kv(mem.reshape(bsz * N_MEM, d), w_mem_kv[l]).reshape(bsz, N_MEM, 2 * BRANCH_W)

    tq_d, tb_d, tq_w, tb_w = 512, 256, 512, 256
    a = _diff_attn(proj, _diff_bias_tiles(rel_table[:, :DIFF_HEADS] * LOG2E, tb_d), row(diff_lq1[l]), row(diff_lk1[l]),
                   row(diff_lq2[l]), row(diff_lk2[l]), row(diff_subln_g[l]), tq=tq_d)
    b = _win_attn(proj, _win_bias_tiles(rel_table[:, DIFF_HEADS:] * LOG2E, tb_w), win_sink[l].astype(F32) * LOG2E,
                  tq=tq_w)
    c = _mem_attn(proj, memkv)

    m = bsz * seq
    h1 = _mix(x2, hb, row(ln_in_g), row(ln_in_b), a.reshape(m, -1), b.reshape(m, -1), c.reshape(m, -1),
              w_gate[l].astype(BF16), row(b_gate[l]), w_branch[l], w_o[l],
              row(ln1_g[l]), row(ln1_b[l]))

    w_up_p = _w_up_prep(w_up[l].astype(F32))
    cw_p = _pad_ff(conv_w[l].astype(F32), 1)
    cb_p = _pad_ff(row(conv_b[l]), 1)
    w_down_p = _w_down_prep(w_down[l].astype(F32))
    out = _ffn(h1, w_up_p, cw_p, cb_p, w_down_p, row(ln2_g[l]), row(ln2_b[l]), seq)
    return out.reshape(bsz, seq, d)
```

```python
import functools
import math

import jax
import jax.numpy as jnp
from jax import lax
from jax.experimental import pallas as pl
from jax.experimental.pallas import tpu as pltpu

F32 = jnp.float32
BF16 = jnp.bfloat16

D_MODEL = 2048
SEQ = 2048
N_MEM = 256
HEAD_DIM = 128
BRANCH_W = 1024
DIFF_HEADS = 8
DIFF_QK = 64
WIN_HEADS = 8
WIN_KV_HEADS = 2
WIN_GROUP = WIN_HEADS // WIN_KV_HEADS
WINDOW = 128
MEM_HEADS = 4
MEM_DIM = 256
OFF_DQ, OFF_DK, OFF_DV, OFF_WQ, OFF_WK, OFF_WV, OFF_MQ = 0, 1024, 2048, 3072, 4096, 4352, 4608
IN_W = 5632
D_FF = 5504
D_FF_PAD = 5632
REL_BUCKETS = 32
REL_MAX_DIST = 128
DEPTH = 1
ALPHA = (2 * DEPTH) ** 0.25
LN_EPS = 1e-5
NEG = -1e30
LOG2E = math.log2(math.e)
LAMBDA_INIT = 0.8 - 0.6 * math.exp(-0.3 * 0)

VMEM_LIMIT = 56 * 1024 * 1024
LN_ROWS = 128


def _ln_rows(x, g, b):
    mu = jnp.mean(x, axis=-1, keepdims=True)
    xc = x - mu
    var = jnp.mean(xc * xc, axis=-1, keepdims=True)
    return xc * lax.rsqrt(var + LN_EPS) * g + b


def _dot(a, b):
    return jnp.dot(a, b, preferred_element_type=F32)


def _dot_nt(a, b):
    return lax.dot_general(a, b, (((1,), (1,)), ((), ())), preferred_element_type=F32)


def _params(*sem):
    return pltpu.CompilerParams(dimension_semantics=sem, vmem_limit_bytes=VMEM_LIMIT)


def _ln_proj_kernel(x_ref, g_ref, b_ref, w_ref, o_ref, h_ref):
    @pl.when(pl.program_id(1) == 0)
    def _():
        def body(r, c):
            rows = pl.ds(pl.multiple_of(r * LN_ROWS, LN_ROWS), LN_ROWS)
            h_ref[rows, :] = _ln_rows(x_ref[rows, :], g_ref[...], b_ref[...]).astype(BF16)
            return c
        lax.fori_loop(0, x_ref.shape[0] // LN_ROWS, body, 0)

    o_ref[...] = _dot(h_ref[...], w_ref[...].astype(BF16)).astype(BF16)


def _ln_proj(x2, g, b, w, tm=1024, tn=512):
    m, d = x2.shape
    n = w.shape[1]
    return pl.pallas_call(
        _ln_proj_kernel,
        grid=(m // tm, n // tn),
        in_specs=[pl.BlockSpec((tm, d), lambda i, j: (i, 0)),
                  pl.BlockSpec((1, d), lambda i, j: (0, 0)),
                  pl.BlockSpec((1, d), lambda i, j: (0, 0)),
                  pl.BlockSpec((d, tn), lambda i, j: (0, j))],
        out_specs=[pl.BlockSpec((tm, tn), lambda i, j: (i, j)),
                   pl.BlockSpec((tm, d), lambda i, j: (i, 0))],
        out_shape=[jax.ShapeDtypeStruct((m, n), BF16), jax.ShapeDtypeStruct((m, d), BF16)],
        compiler_params=_params("parallel", "arbitrary"),
        name="ln_proj",
    )(x2, g, b, w)


def _mem_kv_kernel(m_ref, w_ref, o_ref):
    o_ref[...] = _dot(m_ref[...].astype(BF16), w_ref[...].astype(BF16)).astype(BF16)


def _mem_kv(mem2, w, tn=512):
    m, d = mem2.shape
    n = w.shape[1]
    return pl.pallas_call(
        _mem_kv_kernel,
        grid=(n // tn,),
        in_specs=[pl.BlockSpec((m, d), lambda j: (0, 0)),
                  pl.BlockSpec((d, tn), lambda j: (0, j))],
        out_specs=pl.BlockSpec((m, tn), lambda j: (0, j)),
        out_shape=jax.ShapeDtypeStruct((m, n), BF16),
        compiler_params=_params("parallel"),
        name="mem_kv",
    )(mem2, w)


def _bucket_lower_bounds():
    half = REL_BUCKETS // 2
    max_exact = half // 2
    n = jnp.arange(REL_MAX_DIST + 1)
    log_part = (jnp.log(jnp.maximum(n, 1).astype(F32) / max_exact) / math.log(REL_MAX_DIST / max_exact)
                * (half - max_exact)).astype(jnp.int32)
    bucket = jnp.where(n < max_exact, n, jnp.minimum(max_exact + log_part, half - 1))
    bucket = jnp.arange(half, dtype=jnp.int32)[bucket]
    ids = jnp.arange(half, dtype=jnp.int32)
    return jnp.sum((bucket[None, :] < ids[:, None]).astype(jnp.int32), axis=1)


BIAS_ROWS = 64


def _bias_kernel(lb_ref, tab_ref, lo_ref, hi_ref, off_ref, o_ref, *, limit):
    h = pl.program_id(0)
    half = REL_BUCKETS // 2
    ncol = o_ref.shape[2]

    def body(r, carry):
        rows = pl.ds(pl.multiple_of(r * BIAS_ROWS, BIAS_ROWS), BIAS_ROWS)

        def general():
            rel = lax.broadcasted_iota(jnp.int32, (BIAS_ROWS, ncol), 1) + off_ref[rows, :]
            n = jnp.abs(rel)
            neg = jnp.full(rel.shape, tab_ref[0, h], F32)
            pos = jnp.full(rel.shape, tab_ref[half, h], F32)
            for b in range(1, half):
                reached = n >= lb_ref[b]
                neg = jnp.where(reached, tab_ref[b, h], neg)
                pos = jnp.where(reached, tab_ref[half + b, h], pos)
            out = jnp.where(rel > 0, pos, neg)
            if limit is not None:
                out = jnp.where(n > limit, NEG, out)
            o_ref[0, rows, :] = out

        if limit is not None:
            general()
            return carry
        far = lb_ref[half - 1]
        all_after = lo_ref[r] >= far
        all_before = -(hi_ref[r] + (ncol - 1)) >= far

        @pl.when(all_after)
        def _():
            o_ref[0, rows, :] = jnp.full((BIAS_ROWS, ncol), tab_ref[2 * half - 1, h], F32)

        @pl.when(all_before)
        def _():
            o_ref[0, rows, :] = jnp.full((BIAS_ROWS, ncol), tab_ref[half - 1, h], F32)

        pl.when(jnp.logical_not(jnp.logical_or(all_after, all_before)))(general)
        return carry
    lax.fori_loop(0, o_ref.shape[1] // BIAS_ROWS, body, 0)


def _bias_tiles(table, row_offset, ncol, limit=None):
    nh = table.shape[1]
    nrow = row_offset.shape[0]
    off = row_offset.astype(jnp.int32)
    chunks = off.reshape(nrow // BIAS_ROWS, BIAS_ROWS)
    smem = pl.BlockSpec(memory_space=pltpu.SMEM)
    return pl.pallas_call(
        functools.partial(_bias_kernel, limit=limit),
        grid=(nh,),
        in_specs=[smem, smem, smem, smem, pl.BlockSpec((nrow, 1), lambda h: (0, 0))],
        out_specs=pl.BlockSpec((1, nrow, ncol), lambda h: (h, 0, 0)),
        out_shape=jax.ShapeDtypeStruct((nh, nrow, ncol), F32),
        compiler_params=_params("parallel"),
        name="bias_tiles",
    )(_bucket_lower_bounds(), table.astype(F32), jnp.min(chunks, axis=1), jnp.max(chunks, axis=1),
      off.reshape(nrow, 1))


def _diff_bias_tiles(table, tq):
    r = jnp.arange(5 * tq)
    return _bias_tiles(table, (r // tq - 2) * tq - r % tq, tq).reshape(table.shape[1], 5, tq, tq)


def _win_bias_tiles(table, tq):
    return _bias_tiles(table, -WINDOW - jnp.arange(tq), tq + 2 * WINDOW, limit=WINDOW)


def _diff_kernel(q_ref, k_ref, v_ref, band_ref, lq1_ref, lk1_ref, lq2_ref, lk2_ref, g_ref, o_ref,
                 s_scr, mx_scr, *, tq, rc, n_tiles, tiles_per_head):
    t = pl.program_id(0)
    hd = v_ref.shape[2]

    @pl.when(t == 0)
    def _():
        s_scr[...] = jnp.zeros_like(s_scr)
        mx_scr[...] = jnp.zeros_like(mx_scr)

    def tick(slot):
        other = 1 - slot
        i = jnp.minimum(t, n_tiles - 1) % tiles_per_head
        nk = k_ref.shape[1] // tq
        q = (q_ref[0].astype(F32) * (DIFF_QK ** -0.5 * LOG2E)).astype(BF16)
        lane = lax.broadcasted_iota(jnp.int32, q.shape, 1)
        zero = jnp.zeros_like(q)
        qs = jnp.concatenate([jnp.where(lane < DIFF_QK, q, zero), jnp.where(lane >= DIFF_QK, q, zero)], axis=0)
        lanes = mx_scr.shape[-1]
        tb = band_ref.shape[2]
        nb = tq // tb
        for j in range(nk):
            s = _dot_nt(qs, k_ref[0, j * tq:(j + 1) * tq, :])
            for r0 in range(0, 2 * tq, tb):
                rows = slice(r0, r0 + tb)
                fold = None
                for cb in range(nb):
                    d = (j - i) * nb + cb - (r0 // tb) % nb
                    sb = s[rows, cb * tb:(cb + 1) * tb] + band_ref[0, jnp.clip(d, -2, 2) + 2]
                    s_scr[slot, rows, j * tq + cb * tb:j * tq + (cb + 1) * tb] = sb
                    fold = functools.reduce(jnp.maximum, [sb[:, c:c + lanes] for c in range(0, tb, lanes)]
                                            + ([] if fold is None else [fold]))
                mx_scr[slot, rows, :] = fold if j == 0 else jnp.maximum(mx_scr[slot, rows, :], fold)

        nchunk = 2 * tq // rc
        chunk = lambda r: slice(r * rc, (r + 1) * rc)
        p = jnp.concatenate(
            [jnp.exp2(s_scr[other, chunk(r), :] - jnp.max(mx_scr[other, chunk(r), :], axis=-1, keepdims=True))
             .astype(BF16) for r in range(nchunk)], axis=0)
        v_ones = jnp.concatenate([v_ref[0], jnp.ones(v_ref.shape[1:], BF16)], axis=1)
        pv = _dot(p, v_ones)
        lam = (jnp.exp(jnp.sum(lq1_ref[...] * lk1_ref[...], axis=-1, keepdims=True))
               - jnp.exp(jnp.sum(lq2_ref[...] * lk2_ref[...], axis=-1, keepdims=True)) + LAMBDA_INIT)
        o = pv[0:tq, 0:hd] * (1.0 / pv[0:tq, hd:]) - pv[tq:, 0:hd] * (lam / pv[tq:, hd:])
        ms = jnp.mean(o * o, axis=-1, keepdims=True)
        o_ref[0] = (o * lax.rsqrt(ms + LN_EPS) * g_ref[...] * (1.0 - LAMBDA_INIT)).astype(BF16)

    pl.when(t % 2 == 0)(functools.partial(tick, 0))
    pl.when(t % 2 == 1)(functools.partial(tick, 1))


def _diff_attn(proj, band, lq1, lk1, lq2, lk2, g, tq=256, rc=32):
    b, s, _ = proj.shape
    cb = HEAD_DIM
    ni = s // tq
    n_tiles = b * DIFF_HEADS * ni
    depth = 1

    def tile(t, lag):
        tc = jnp.clip(t - lag, 0, n_tiles - 1)
        return tc // (DIFF_HEADS * ni), (tc // ni) % DIFF_HEADS, tc % ni

    def q_map(t):
        bi, h, i = tile(t, 0)
        return bi, i, OFF_DQ // cb + h

    def k_map(t):
        bi, h, _ = tile(t, 0)
        return bi, 0, OFF_DK // cb + h

    def v_map(t):
        bi, h, _ = tile(t, depth)
        return bi, 0, OFF_DV // cb + h

    def o_map(t):
        bi, h, i = tile(t, depth)
        return bi, i, h

    vec = lambda n: pl.BlockSpec((1, n), lambda t: (0, 0))
    return pl.pallas_call(
        functools.partial(_diff_kernel, tq=tq, rc=rc, n_tiles=n_tiles, tiles_per_head=ni),
        grid=(n_tiles + depth,),
        in_specs=[pl.BlockSpec((1, tq, cb), q_map),
                  pl.BlockSpec((1, s, cb), k_map),
                  pl.BlockSpec((1, s, cb), v_map),
                  pl.BlockSpec((1,) + band.shape[1:], lambda t: (tile(t, 0)[1], 0, 0, 0)),
                  vec(DIFF_QK), vec(DIFF_QK), vec(DIFF_QK), vec(DIFF_QK), vec(HEAD_DIM)],
        out_specs=pl.BlockSpec((1, tq, cb), o_map),
        out_shape=jax.ShapeDtypeStruct((b, s, BRANCH_W), BF16),
        scratch_shapes=[pltpu.VMEM((2, 2 * tq, s), F32), pltpu.VMEM((2, 2 * tq, cb), F32)],
        compiler_params=_params("arbitrary"),
        name="diff_attn",
    )(proj, proj, proj, band, lq1, lk1, lq2, lk2, g)


def _win_kernel(sink_ref, q_ref, kp_ref, km_ref, kn_ref, vp_ref, vm_ref, vn_ref, bias_ref, o_ref, *, tq):
    g = pl.program_id(1)
    i = pl.program_id(2)
    first = i == 0
    last = i == pl.num_programs(2) - 1
    w = WINDOW
    hd = HEAD_DIM
    tb = bias_ref.shape[1]
    nkeys = tb + 2 * w
    keys = jnp.concatenate([kp_ref[0], km_ref[0], kn_ref[0]], axis=0)
    vals = jnp.concatenate([vp_ref[0], vm_ref[0], vn_ref[0]], axis=0)
    v_ones = jnp.concatenate([vals, jnp.ones((tq + 2 * w, hd), BF16)], axis=1)
    col = lax.broadcasted_iota(jnp.int32, (1, nkeys), 1)
    scale = jnp.asarray(hd ** -0.5 * LOG2E, F32)
    for r0 in range(0, tq, tb):
        rows = slice(r0, r0 + tb)
        band = slice(r0, r0 + nkeys)
        outside = None
        if r0 == 0:
            outside = jnp.logical_and(first, col < w)
        if r0 + tb == tq:
            past_end = jnp.logical_and(last, col >= tb + w)
            outside = past_end if outside is None else jnp.logical_or(outside, past_end)
        for j in range(WIN_GROUP):
            cols = slice(j * hd, (j + 1) * hd)
            s = _dot_nt(q_ref[0, rows, cols], keys[band]) * scale + bias_ref[j]
            if outside is not None:
                s = jnp.where(outside, NEG, s)
            sink = sink_ref[g * WIN_GROUP + j]
            fold = functools.reduce(jnp.maximum, [s[:, c:c + w] for c in range(0, nkeys, w)])
            m = jnp.maximum(jnp.max(fold, axis=-1, keepdims=True), sink)
            pv = _dot(jnp.exp2(s - m).astype(BF16), v_ones[band])
            o_ref[0, rows, cols] = (pv[:, 0:hd] / (pv[:, hd:] + jnp.exp2(sink - m))).astype(BF16)


def _win_attn(proj, bias, sink, tq=256):
    b, s, _ = proj.shape
    hd = HEAD_DIM
    nb = s // WINDOW
    r = tq // WINDOW
    prev = lambda bi, g, i: jnp.maximum(i * r - 1, 0)
    nxt = lambda bi, g, i: jnp.minimum((i + 1) * r, nb - 1)
    kcol, vcol = OFF_WK // hd, OFF_WV // hd
    return pl.pallas_call(
        functools.partial(_win_kernel, tq=tq),
        grid=(b, WIN_KV_HEADS, s // tq),
        in_specs=[pl.BlockSpec(memory_space=pltpu.SMEM),
                  pl.BlockSpec((1, tq, WIN_GROUP * hd), lambda bi, g, i: (bi, i, OFF_WQ // (WIN_GROUP * hd) + g)),
                  pl.BlockSpec((1, WINDOW, hd), lambda bi, g, i: (bi, prev(bi, g, i), kcol + g)),
                  pl.BlockSpec((1, tq, hd), lambda bi, g, i: (bi, i, kcol + g)),
                  pl.BlockSpec((1, WINDOW, hd), lambda bi, g, i: (bi, nxt(bi, g, i), kcol + g)),
                  pl.BlockSpec((1, WINDOW, hd), lambda bi, g, i: (bi, prev(bi, g, i), vcol + g)),
                  pl.BlockSpec((1, tq, hd), lambda bi, g, i: (bi, i, vcol + g)),
                  pl.BlockSpec((1, WINDOW, hd), lambda bi, g, i: (bi, nxt(bi, g, i), vcol + g)),
                  pl.BlockSpec((WIN_GROUP,) + bias.shape[1:], lambda bi, g, i: (g, 0, 0))],
        out_specs=pl.BlockSpec((1, tq, WIN_GROUP * hd), lambda bi, g, i: (bi, i, g)),
        out_shape=jax.ShapeDtypeStruct((b, s, BRANCH_W), BF16),
        compiler_params=_params("parallel", "parallel", "arbitrary"),
        name="win_attn",
    )(sink, proj, proj, proj, proj, proj, proj, proj, bias)


def _mem_attn_kernel(q_ref, k_ref, v_ref, o_ref):
    s = _dot_nt(q_ref[0], k_ref[0]) * jnp.asarray(MEM_DIM ** -0.5, F32)
    p = jnp.exp(s - jnp.max(s, axis=-1, keepdims=True))
    p = p * (1.0 / jnp.sum(p, axis=-1, keepdims=True))
    o_ref[0] = _dot(p.astype(BF16), v_ref[0]).astype(BF16)


def _mem_attn(proj, memkv, tq=2048):
    b, s, _ = proj.shape
    md = MEM_DIM
    return pl.pallas_call(
        _mem_attn_kernel,
        grid=(b, MEM_HEADS, s // tq),
        in_specs=[pl.BlockSpec((1, tq, md), lambda bi, h, i: (bi, i, OFF_MQ // md + h)),
                  pl.BlockSpec((1, N_MEM, md), lambda bi, h, i: (bi, 0, h)),
                  pl.BlockSpec((1, N_MEM, md), lambda bi, h, i: (bi, 0, MEM_HEADS + h))],
        out_specs=pl.BlockSpec((1, tq, md), lambda bi, h, i: (bi, i, h)),
        out_shape=jax.ShapeDtypeStruct((b, s, BRANCH_W), BF16),
        compiler_params=_params("parallel", "parallel", "arbitrary"),
        name="mem_attn",
    )(proj, memkv, memkv)


def _mix_kernel(x_ref, h_ref, lg_ref, lb_ref, a_ref, b_ref, c_ref, wg0_ref, wg1_ref, wg2_ref,
                bg0_ref, bg1_ref, bg2_ref, wb_ref, wo_ref, g1_ref, b1_ref, o_ref, acc_scr):
    n = pl.program_id(1)
    nrow = x_ref.shape[0] // LN_ROWS

    @pl.when(n == 0)
    def _():
        acc_scr[...] = jnp.zeros_like(acc_scr)

    h = h_ref[...]
    gates = [jax.nn.sigmoid(_dot(h, wg_ref[...]) + bg_ref[...])
             for wg_ref, bg_ref in ((wg0_ref, bg0_ref), (wg1_ref, bg1_ref), (wg2_ref, bg2_ref))]
    mixed = None
    for k, br_ref in enumerate((a_ref, b_ref, c_ref)):
        term = gates[k] * _dot(br_ref[...], wb_ref[k].astype(BF16))
        mixed = term if mixed is None else mixed + term
    acc_scr[...] += _dot(mixed.astype(BF16), wo_ref[...].astype(BF16))

    @pl.when(n == pl.num_programs(1) - 1)
    def _():
        def body(r, c):
            rows = pl.ds(pl.multiple_of(r * LN_ROWS, LN_ROWS), LN_ROWS)
            hh = _ln_rows(x_ref[rows, :], lg_ref[...], lb_ref[...])
            o_ref[rows, :] = _ln_rows(ALPHA * hh + acc_scr[rows, :], g1_ref[...], b1_ref[...])
            return c
        lax.fori_loop(0, nrow, body, 0)


def _mix(x2, hb, lg, lb, a, b, c, wg, bg, wb, wo, g1, b1, tm=512, tn=256):
    m, d = x2.shape
    bw = a.shape[1]
    nn = d // tn
    row = lambda w: pl.BlockSpec((tm, w), lambda i, n: (i, 0))
    vec = pl.BlockSpec((1, d), lambda i, n: (0, 0))
    wgs = [pl.BlockSpec((d, tn), functools.partial(lambda i, n, k: (0, k * nn + n), k=k)) for k in range(3)]
    bgs = [pl.BlockSpec((1, tn), functools.partial(lambda i, n, k: (0, k * nn + n), k=k)) for k in range(3)]
    return pl.pallas_call(
        _mix_kernel,
        grid=(m // tm, nn),
        in_specs=[row(d), row(d), vec, vec, row(bw), row(bw), row(bw), *wgs, *bgs,
                  pl.BlockSpec((3, bw, tn), lambda i, n: (0, 0, n)),
                  pl.BlockSpec((tn, d), lambda i, n: (n, 0)), vec, vec],
        out_specs=pl.BlockSpec((tm, d), lambda i, n: (i, 0)),
        out_shape=jax.ShapeDtypeStruct((m, d), F32),
        scratch_shapes=[pltpu.VMEM((tm, d), F32)],
        compiler_params=_params("parallel", "arbitrary"),
        name="mix",
    )(x2, hb, lg, lb, a, b, c, wg, wg, wg, bg, bg, bg, wb, wo, g1, b1)


HALO = 16


def _gelu_tanh(x):
    return 0.5 * x * (1.0 + jnp.tanh(math.sqrt(2.0 / math.pi) * (x + 0.044715 * (x * x * x))))


def _ffn_kernel(h_ref, hp_ref, hn_ref, wv_ref, wg_ref, cwv_ref, cwg_ref, cbv_ref, cbg_ref, wd_ref,
                g2_ref, b2_ref, o_ref, hx_scr, acc_scr, *, tiles_per_seq):
    i = pl.program_id(0)
    f = pl.program_id(1)
    tm = h_ref.shape[0]
    nrow = tm // LN_ROWS

    @pl.when(f == 0)
    def _():
        seq_first = (i % tiles_per_seq) == 0
        seq_last = (i % tiles_per_seq) == tiles_per_seq - 1
        hx_scr[0:HALO, :] = jnp.where(seq_first, 0.0, hp_ref[...]).astype(BF16)
        hx_scr[HALO:HALO + tm, :] = h_ref[...].astype(BF16)
        hx_scr[HALO + tm:, :] = jnp.where(seq_last, 0.0, hn_ref[...]).astype(BF16)
        acc_scr[...] = jnp.zeros_like(acc_scr)

    hx = hx_scr[...]
    ext = tm + 2 * HALO

    def conv(u, cw_ref, cb_ref):
        prev = pltpu.roll(u, 1, 0)[HALO:HALO + tm]
        nxt = pltpu.roll(u, ext - 1, 0)[HALO:HALO + tm]
        return prev * cw_ref[0:1, :] + u[HALO:HALO + tm] * cw_ref[1:2, :] + nxt * cw_ref[2:3, :] + cb_ref[...]

    gelu_gate = _gelu_tanh(conv(_dot(hx, wg_ref[...]), cwg_ref, cbg_ref))
    val = conv(_dot(hx, wv_ref[...]), cwv_ref, cbv_ref)
    acc_scr[...] += _dot((gelu_gate * val).astype(BF16), wd_ref[...])

    @pl.when(f == pl.num_programs(1) - 1)
    def _():
        def body(r, c):
            rows = pl.ds(pl.multiple_of(r * LN_ROWS, LN_ROWS), LN_ROWS)
            o_ref[rows, :] = _ln_rows(ALPHA * h_ref[rows, :] + acc_scr[rows, :], g2_ref[...], b2_ref[...])
            return c
        lax.fori_loop(0, nrow, body, 0)


def _ffn(h1, w_up, cw, cb, w_down, g2, b2, seq, tm=512, tf=512):
    m, d = h1.shape
    nf = D_FF_PAD // tf
    hb = tm // HALO
    nhb = m // HALO
    vec = pl.BlockSpec((1, d), lambda i, f: (0, 0))
    return pl.pallas_call(
        functools.partial(_ffn_kernel, tiles_per_seq=seq // tm),
        grid=(m // tm, nf),
        in_specs=[pl.BlockSpec((tm, d), lambda i, f: (i, 0)),
                  pl.BlockSpec((HALO, d), lambda i, f: (jnp.maximum(i * hb - 1, 0), 0)),
                  pl.BlockSpec((HALO, d), lambda i, f: (jnp.minimum((i + 1) * hb, nhb - 1), 0)),
                  pl.BlockSpec((d, tf), lambda i, f: (0, f)),
                  pl.BlockSpec((d, tf), lambda i, f: (0, nf + f)),
                  pl.BlockSpec((3, tf), lambda i, f: (0, f)),
                  pl.BlockSpec((3, tf), lambda i, f: (0, nf + f)),
                  pl.BlockSpec((1, tf), lambda i, f: (0, f)),
                  pl.BlockSpec((1, tf), lambda i, f: (0, nf + f)),
                  pl.BlockSpec((tf, d), lambda i, f: (f, 0)), vec, vec],
        out_specs=pl.BlockSpec((tm, d), lambda i, f: (i, 0)),
        out_shape=jax.ShapeDtypeStruct((m, d), F32),
        scratch_shapes=[pltpu.VMEM((tm + 2 * HALO, d), BF16), pltpu.VMEM((tm, d), F32)],
        compiler_params=_params("parallel", "arbitrary"),
        name="ffn",
    )(h1, h1, h1, w_up, w_up, cw, cw, cb, cb, w_down, g2, b2)


def _w_up_prep_kernel(w_ref, o_ref):
    o_ref[:, 0:D_FF] = w_ref[...].astype(BF16)
    o_ref[:, D_FF:] = jnp.zeros((o_ref.shape[0], D_FF_PAD - D_FF), BF16)


def _w_up_prep(w_up, tr=256):
    d = w_up.shape[0]
    return pl.pallas_call(
        _w_up_prep_kernel,
        grid=(2, d // tr),
        in_specs=[pl.BlockSpec((tr, D_FF), lambda half, r: (r, half))],
        out_specs=pl.BlockSpec((tr, D_FF_PAD), lambda half, r: (r, half)),
        out_shape=jax.ShapeDtypeStruct((d, 2 * D_FF_PAD), BF16),
        compiler_params=_params("parallel", "parallel"),
        name="w_up_prep",
    )(w_up)


def _w_down_prep_kernel(w_ref, o_ref):
    o_ref[0:D_FF, :] = w_ref[...].astype(BF16)
    o_ref[D_FF:, :] = jnp.zeros((D_FF_PAD - D_FF, o_ref.shape[1]), BF16)


def _w_down_prep(w_down, tc=256):
    d = w_down.shape[1]
    return pl.pallas_call(
        _w_down_prep_kernel,
        grid=(d // tc,),
        in_specs=[pl.BlockSpec((D_FF, tc), lambda c: (0, c))],
        out_specs=pl.BlockSpec((D_FF_PAD, tc), lambda c: (0, c)),
        out_shape=jax.ShapeDtypeStruct((D_FF_PAD, d), BF16),
        compiler_params=_params("parallel"),
        name="w_down_prep",
    )(w_down)


def _pad_ff(t, axis):
    val, gate = jnp.split(t, 2, axis=axis)
    pad = [(0, 0)] * t.ndim
    pad[axis] = (0, D_FF_PAD - D_FF)
    return jnp.concatenate([jnp.pad(val, pad), jnp.pad(gate, pad)], axis=axis)


def kernel(x, mem, ln_in_g, ln_in_b, rel_table, w_in, w_mem_kv, diff_lq1, diff_lk1, diff_lq2, diff_lk2,
           diff_subln_g, win_sink, w_gate, b_gate, w_branch, w_o, ln1_g, ln1_b, w_up, conv_w, conv_b,
           w_down, ln2_g, ln2_b):
    assert w_in.shape[0] == DEPTH == 1
    bsz, seq, d = x.shape
    x2 = x.reshape(bsz * seq, d)
    row = lambda v: v.reshape(1, -1).astype(F32)
    l = 0
    proj, hb = _ln_proj(x2, row(ln_in_g), row(ln_in_b), w_in[l])
    proj = proj.reshape(bsz, seq, IN_W)
    memkv = _mem_kv(mem.reshape(bsz * N_MEM, d), w_mem_kv[l]).reshape(bsz, N_MEM, 2 * BRANCH_W)

    tq_d, tb_d, tq_w, tb_w = 512, 256, 512, 256
    a = _diff_attn(proj, _diff_bias_tiles(rel_table[:, :DIFF_HEADS] * LOG2E, tb_d), row(diff_lq1[l]), row(diff_lk1[l]),
                   row(diff_lq2[l]), row(diff_lk2[l]), row(diff_subln_g[l]), tq=tq_d)
    b = _win_attn(proj, _win_bias_tiles(rel_table[:, DIFF_HEADS:] * LOG2E, tb_w), win_sink[l].astype(F32) * LOG2E,
                  tq=tq_w)
    c = _mem_attn(proj, memkv)

    m = bsz * seq
    h1 = _mix(x2, hb, row(ln_in_g), row(ln_in_b), a.reshape(m, -1), b.reshape(m, -1), c.reshape(m, -1),
              w_gate[l].astype(BF16), row(b_gate[l]), w_branch[l], w_o[l],
              row(ln1_g[l]), row(ln1_b[l]))

    w_up_p = _w_up_prep(w_up[l].astype(F32))
    cw_p = _pad_ff(conv_w[l].astype(F32), 1)
    cb_p = _pad_ff(row(conv_b[l]), 1)
    w_down_p = _w_down_prep(w_down[l].astype(F32))
    out = _ffn(h1, w_up_p, cw_p, cb_p, w_down_p, row(ln2_g[l]), row(ln2_b[l]), seq)
    return out.reshape(bsz, seq, d)
```

```python
import functools
import math

import jax
import jax.numpy as jnp
from jax import lax
from jax.experimental import pallas as pl
from jax.experimental.pallas import tpu as pltpu

F32 = jnp.float32
BF16 = jnp.bfloat16

D_MODEL = 2048
SEQ = 2048
N_MEM = 256
HEAD_DIM = 128
BRANCH_W = 1024
DIFF_HEADS = 8
DIFF_QK = 64
WIN_HEADS = 8
WIN_KV_HEADS = 2
WIN_GROUP = WIN_HEADS // WIN_KV_HEADS
WINDOW = 128
MEM_HEADS = 4
MEM_DIM = 256
OFF_DQ, OFF_DK, OFF_DV, OFF_WQ, OFF_WK, OFF_WV, OFF_MQ = 0, 1024, 2048, 3072, 4096, 4352, 4608
IN_W = 5632
D_FF = 5504
D_FF_PAD = 5632
REL_BUCKETS = 32
REL_MAX_DIST = 128
DEPTH = 1
ALPHA = (2 * DEPTH) ** 0.25
LN_EPS = 1e-5
NEG = -1e30
LOG2E = math.log2(math.e)
LAMBDA_INIT = 0.8 - 0.6 * math.exp(-0.3 * 0)

VMEM_LIMIT = 56 * 1024 * 1024
LN_ROWS = 128


def _ln_rows(x, g, b):
    mu = jnp.mean(x, axis=-1, keepdims=True)
    xc = x - mu
    var = jnp.mean(xc * xc, axis=-1, keepdims=True)
    return xc * lax.rsqrt(var + LN_EPS) * g + b


def _dot(a, b):
    return jnp.dot(a, b, preferred_element_type=F32)


def _dot_nt(a, b):
    return lax.dot_general(a, b, (((1,), (1,)), ((), ())), preferred_element_type=F32)


def _params(*sem):
    return pltpu.CompilerParams(dimension_semantics=sem, vmem_limit_bytes=VMEM_LIMIT)


def _ln_kernel(x_ref, g_ref, b_ref, h_ref):
    def body(r, c):
        rows = pl.ds(pl.multiple_of(r * LN_ROWS, LN_ROWS), LN_ROWS)
        h_ref[rows, :] = _ln_rows(x_ref[rows, :], g_ref[...], b_ref[...]).astype(BF16)
        return c
    lax.fori_loop(0, x_ref.shape[0] // LN_ROWS, body, 0)


def _ln(x2, g, b, tm=512):
    m, d = x2.shape
    vec = pl.BlockSpec((1, d), lambda i: (0, 0))
    return pl.pallas_call(
        _ln_kernel,
        grid=(m // tm,),
        in_specs=[pl.BlockSpec((tm, d), lambda i: (i, 0)), vec, vec],
        out_specs=pl.BlockSpec((tm, d), lambda i: (i, 0)),
        out_shape=jax.ShapeDtypeStruct((m, d), BF16),
        compiler_params=_params("parallel"),
        name="ln_in",
    )(x2, g, b)


def _proj_kernel(h_ref, w_ref, o_ref):
    o_ref[...] = _dot(h_ref[...], w_ref[...].astype(BF16)).astype(BF16)


def _proj(hb, w, tm=2048, tn=512):
    m, d = hb.shape
    n = w.shape[1]
    return pl.pallas_call(
        _proj_kernel,
        grid=(m // tm, n // tn),
        in_specs=[pl.BlockSpec((tm, d), lambda i, j: (i, 0)),
                  pl.BlockSpec((d, tn), lambda i, j: (0, j))],
        out_specs=pl.BlockSpec((tm, tn), lambda i, j: (i, j)),
        out_shape=jax.ShapeDtypeStruct((m, n), BF16),
        compiler_params=_params("parallel", "arbitrary"),
        name="proj",
    )(hb, w)


def _mem_kv_kernel(m_ref, w_ref, o_ref):
    o_ref[...] = _dot(m_ref[...].astype(BF16), w_ref[...].astype(BF16)).astype(BF16)


def _mem_kv(mem2, w, tn=512):
    m, d = mem2.shape
    n = w.shape[1]
    return pl.pallas_call(
        _mem_kv_kernel,
        grid=(n // tn,),
        in_specs=[pl.BlockSpec((m, d), lambda j: (0, 0)),
                  pl.BlockSpec((d, tn), lambda j: (0, j))],
        out_specs=pl.BlockSpec((m, tn), lambda j: (0, j)),
        out_shape=jax.ShapeDtypeStruct((m, n), BF16),
        compiler_params=_params("parallel"),
        name="mem_kv",
    )(mem2, w)


def _bucket_lower_bounds():
    half = REL_BUCKETS // 2
    max_exact = half // 2
    n = jnp.arange(REL_MAX_DIST + 1)
    log_part = (jnp.log(jnp.maximum(n, 1).astype(F32) / max_exact) / math.log(REL_MAX_DIST / max_exact)
                * (half - max_exact)).astype(jnp.int32)
    bucket = jnp.where(n < max_exact, n, jnp.minimum(max_exact + log_part, half - 1))
    bucket = jnp.arange(half, dtype=jnp.int32)[bucket]
    ids = jnp.arange(half, dtype=jnp.int32)
    return jnp.sum((bucket[None, :] < ids[:, None]).astype(jnp.int32), axis=1)


BIAS_ROWS = 64


def _bias_kernel(lb_ref, tab_ref, lo_ref, hi_ref, off_ref, o_ref, *, limit):
    h = pl.program_id(0)
    half = REL_BUCKETS // 2
    ncol = o_ref.shape[2]

    def body(r, carry):
        rows = pl.ds(pl.multiple_of(r * BIAS_ROWS, BIAS_ROWS), BIAS_ROWS)

        def general():
            rel = lax.broadcasted_iota(jnp.int32, (BIAS_ROWS, ncol), 1) + off_ref[rows, :]
            n = jnp.abs(rel)
            neg = jnp.full(rel.shape, tab_ref[0, h], F32)
            pos = jnp.full(rel.shape, tab_ref[half, h], F32)
            for b in range(1, half):
                reached = n >= lb_ref[b]
                neg = jnp.where(reached, tab_ref[b, h], neg)
                pos = jnp.where(reached, tab_ref[half + b, h], pos)
            out = jnp.where(rel > 0, pos, neg)
            if limit is not None:
                out = jnp.where(n > limit, NEG, out)
            o_ref[0, rows, :] = out

        if limit is not None:
            general()
            return carry
        far = lb_ref[half - 1]
        all_after = lo_ref[r] >= far
        all_before = -(hi_ref[r] + (ncol - 1)) >= far

        @pl.when(all_after)
        def _():
            o_ref[0, rows, :] = jnp.full((BIAS_ROWS, ncol), tab_ref[2 * half - 1, h], F32)

        @pl.when(all_before)
        def _():
            o_ref[0, rows, :] = jnp.full((BIAS_ROWS, ncol), tab_ref[half - 1, h], F32)

        pl.when(jnp.logical_not(jnp.logical_or(all_after, all_before)))(general)
        return carry
    lax.fori_loop(0, o_ref.shape[1] // BIAS_ROWS, body, 0)


def _bias_tiles(table, row_offset, ncol, limit=None):
    nh = table.shape[1]
    nrow = row_offset.shape[0]
    off = row_offset.astype(jnp.int32)
    chunks = off.reshape(nrow // BIAS_ROWS, BIAS_ROWS)
    smem = pl.BlockSpec(memory_space=pltpu.SMEM)
    return pl.pallas_call(
        functools.partial(_bias_kernel, limit=limit),
        grid=(nh,),
        in_specs=[smem, smem, smem, smem, pl.BlockSpec((nrow, 1), lambda h: (0, 0))],
        out_specs=pl.BlockSpec((1, nrow, ncol), lambda h: (h, 0, 0)),
        out_shape=jax.ShapeDtypeStruct((nh, nrow, ncol), F32),
        compiler_params=_params("parallel"),
        name="bias_tiles",
    )(_bucket_lower_bounds(), table.astype(F32), jnp.min(chunks, axis=1), jnp.max(chunks, axis=1),
      off.reshape(nrow, 1))


def _diff_bias_tiles(table, tq):
    r = jnp.arange(5 * tq)
    return _bias_tiles(table, (r // tq - 2) * tq - r % tq, tq).reshape(table.shape[1], 5, tq, tq)


def _win_bias_tiles(table, tq):
    return _bias_tiles(table, -WINDOW - jnp.arange(tq), tq + 2 * WINDOW, limit=WINDOW)


def _diff_kernel(q_ref, k_ref, v_ref, band_ref, lq1_ref, lk1_ref, lq2_ref, lk2_ref, g_ref, o_ref,
                 s_scr, mx_scr, *, tq, rc, n_tiles, tiles_per_head):
    t = pl.program_id(0)
    hd = v_ref.shape[2]

    @pl.when(t == 0)
    def _():
        s_scr[...] = jnp.zeros_like(s_scr)
        mx_scr[...] = jnp.zeros_like(mx_scr)

    def tick(slot):
        other = 1 - slot
        i = jnp.minimum(t, n_tiles - 1) % tiles_per_head
        nk = k_ref.shape[1] // tq
        q = (q_ref[0].astype(F32) * (DIFF_QK ** -0.5 * LOG2E)).astype(BF16)
        lane = lax.broadcasted_iota(jnp.int32, q.shape, 1)
        zero = jnp.zeros_like(q)
        qs = jnp.concatenate([jnp.where(lane < DIFF_QK, q, zero), jnp.where(lane >= DIFF_QK, q, zero)], axis=0)
        lanes = mx_scr.shape[-1]
        tb = band_ref.shape[2]
        nb = tq // tb
        for j in range(nk):
            s = _dot_nt(qs, k_ref[0, j * tq:(j + 1) * tq, :])
            for r0 in range(0, 2 * tq, tb):
                rows = slice(r0, r0 + tb)
                fold = None
                for cb in range(nb):
                    d = (j - i) * nb + cb - (r0 // tb) % nb
                    sb = s[rows, cb * tb:(cb + 1) * tb] + band_ref[0, jnp.clip(d, -2, 2) + 2]
                    s_scr[slot, rows, j * tq + cb * tb:j * tq + (cb + 1) * tb] = sb
                    fold = functools.reduce(jnp.maximum, [sb[:, c:c + lanes] for c in range(0, tb, lanes)]
                                            + ([] if fold is None else [fold]))
                mx_scr[slot, rows, :] = fold if j == 0 else jnp.maximum(mx_scr[slot, rows, :], fold)

        nchunk = 2 * tq // rc
        chunk = lambda r: slice(r * rc, (r + 1) * rc)
        p = jnp.concatenate(
            [jnp.exp2(s_scr[other, chunk(r), :] - jnp.max(mx_scr[other, chunk(r), :], axis=-1, keepdims=True))
             .astype(BF16) for r in range(nchunk)], axis=0)
        v_ones = jnp.concatenate([v_ref[0], jnp.ones(v_ref.shape[1:], BF16)], axis=1)
        pv = _dot(p, v_ones)
        lam = (jnp.exp(jnp.sum(lq1_ref[...] * lk1_ref[...], axis=-1, keepdims=True))
               - jnp.exp(jnp.sum(lq2_ref[...] * lk2_ref[...], axis=-1, keepdims=True)) + LAMBDA_INIT)
        o = pv[0:tq, 0:hd] * (1.0 / pv[0:tq, hd:]) - pv[tq:, 0:hd] * (lam / pv[tq:, hd:])
        ms = jnp.mean(o * o, axis=-1, keepdims=True)
        o_ref[0] = (o * lax.rsqrt(ms + LN_EPS) * g_ref[...] * (1.0 - LAMBDA_INIT)).astype(BF16)

    pl.when(t % 2 == 0)(functools.partial(tick, 0))
    pl.when(t % 2 == 1)(functools.partial(tick, 1))


def _diff_attn(proj, band, lq1, lk1, lq2, lk2, g, tq=256, rc=32):
    b, s, _ = proj.shape
    cb = HEAD_DIM
    ni = s // tq
    n_tiles = b * DIFF_HEADS * ni
    depth = 1

    def tile(t, lag):
        tc = jnp.clip(t - lag, 0, n_tiles - 1)
        return tc // (DIFF_HEADS * ni), (tc // ni) % DIFF_HEADS, tc % ni

    def q_map(t):
        bi, h, i = tile(t, 0)
        return bi, i, OFF_DQ // cb + h

    def k_map(t):
        bi, h, _ = tile(t, 0)
        return bi, 0, OFF_DK // cb + h

    def v_map(t):
        bi, h, _ = tile(t, depth)
        return bi, 0, OFF_DV // cb + h

    def o_map(t):
        bi, h, i = tile(t, depth)
        return bi, i, h

    vec = lambda n: pl.BlockSpec((1, n), lambda t: (0, 0))
    return pl.pallas_call(
        functools.partial(_diff_kernel, tq=tq, rc=rc, n_tiles=n_tiles, tiles_per_head=ni),
        grid=(n_tiles + depth,),
        in_specs=[pl.BlockSpec((1, tq, cb), q_map),
                  pl.BlockSpec((1, s, cb), k_map),
                  pl.BlockSpec((1, s, cb), v_map),
                  pl.BlockSpec((1,) + band.shape[1:], lambda t: (tile(t, 0)[1], 0, 0, 0)),
                  vec(DIFF_QK), vec(DIFF_QK), vec(DIFF_QK), vec(DIFF_QK), vec(HEAD_DIM)],
        out_specs=pl.BlockSpec((1, tq, cb), o_map),
        out_shape=jax.ShapeDtypeStruct((b, s, BRANCH_W), BF16),
        scratch_shapes=[pltpu.VMEM((2, 2 * tq, s), F32), pltpu.VMEM((2, 2 * tq, cb), F32)],
        compiler_params=_params("arbitrary"),
        name="diff_attn",
    )(proj, proj, proj, band, lq1, lk1, lq2, lk2, g)


def _win_kernel(sink_ref, q_ref, kp_ref, km_ref, kn_ref, vp_ref, vm_ref, vn_ref, bias_ref, o_ref, *, tq):
    g = pl.program_id(1)
    i = pl.program_id(2)
    first = i == 0
    last = i == pl.num_programs(2) - 1
    w = WINDOW
    hd = HEAD_DIM
    tb = bias_ref.shape[1]
    nkeys = tb + 2 * w
    keys = jnp.concatenate([kp_ref[0], km_ref[0], kn_ref[0]], axis=0)
    vals = jnp.concatenate([vp_ref[0], vm_ref[0], vn_ref[0]], axis=0)
    v_ones = jnp.concatenate([vals, jnp.ones((tq + 2 * w, hd), BF16)], axis=1)
    col = lax.broadcasted_iota(jnp.int32, (1, nkeys), 1)
    scale = jnp.asarray(hd ** -0.5 * LOG2E, F32)
    for r0 in range(0, tq, tb):
        rows = slice(r0, r0 + tb)
        band = slice(r0, r0 + nkeys)
        outside = None
        if r0 == 0:
            outside = jnp.logical_and(first, col < w)
        if r0 + tb == tq:
            past_end = jnp.logical_and(last, col >= tb + w)
            outside = past_end if outside is None else jnp.logical_or(outside, past_end)
        for j in range(WIN_GROUP):
            cols = slice(j * hd, (j + 1) * hd)
            s = _dot_nt(q_ref[0, rows, cols], keys[band]) * scale + bias_ref[j]
            if outside is not None:
                s = jnp.where(outside, NEG, s)
            sink = sink_ref[g * WIN_GROUP + j]
            fold = functools.reduce(jnp.maximum, [s[:, c:c + w] for c in range(0, nkeys, w)])
            m = jnp.maximum(jnp.max(fold, axis=-1, keepdims=True), sink)
            pv = _dot(jnp.exp2(s - m).astype(BF16), v_ones[band])
            o_ref[0, rows, cols] = (pv[:, 0:hd] / (pv[:, hd:] + jnp.exp2(sink - m))).astype(BF16)


def _win_attn(proj, bias, sink, tq=256):
    b, s, _ = proj.shape
    hd = HEAD_DIM
    nb = s // WINDOW
    r = tq // WINDOW
    prev = lambda bi, g, i: jnp.maximum(i * r - 1, 0)
    nxt = lambda bi, g, i: jnp.minimum((i + 1) * r, nb - 1)
    kcol, vcol = OFF_WK // hd, OFF_WV // hd
    return pl.pallas_call(
        functools.partial(_win_kernel, tq=tq),
        grid=(b, WIN_KV_HEADS, s // tq),
        in_specs=[pl.BlockSpec(memory_space=pltpu.SMEM),
                  pl.BlockSpec((1, tq, WIN_GROUP * hd), lambda bi, g, i: (bi, i, OFF_WQ // (WIN_GROUP * hd) + g)),
                  pl.BlockSpec((1, WINDOW, hd), lambda bi, g, i: (bi, prev(bi, g, i), kcol + g)),
                  pl.BlockSpec((1, tq, hd), lambda bi, g, i: (bi, i, kcol + g)),
                  pl.BlockSpec((1, WINDOW, hd), lambda bi, g, i: (bi, nxt(bi, g, i), kcol + g)),
                  pl.BlockSpec((1, WINDOW, hd), lambda bi, g, i: (bi, prev(bi, g, i), vcol + g)),
                  pl.BlockSpec((1, tq, hd), lambda bi, g, i: (bi, i, vcol + g)),
                  pl.BlockSpec((1, WINDOW, hd), lambda bi, g, i: (bi, nxt(bi, g, i), vcol + g)),
                  pl.BlockSpec((WIN_GROUP,) + bias.shape[1:], lambda bi, g, i: (g, 0, 0))],
        out_specs=pl.BlockSpec((1, tq, WIN_GROUP * hd), lambda bi, g, i: (bi, i, g)),
        out_shape=jax.ShapeDtypeStruct((b, s, BRANCH_W), BF16),
        compiler_params=_params("parallel", "parallel", "arbitrary"),
        name="win_attn",
    )(sink, proj, proj, proj, proj, proj, proj, proj, bias)


def _mem_attn_kernel(q_ref, k_ref, v_ref, o_ref):
    s = _dot_nt(q_ref[0], k_ref[0]) * jnp.asarray(MEM_DIM ** -0.5, F32)
    p = jnp.exp(s - jnp.max(s, axis=-1, keepdims=True))
    p = p * (1.0 / jnp.sum(p, axis=-1, keepdims=True))
    o_ref[0] = _dot(p.astype(BF16), v_ref[0]).astype(BF16)


def _mem_attn(proj, memkv, tq=2048):
    b, s, _ = proj.shape
    md = MEM_DIM
    return pl.pallas_call(
        _mem_attn_kernel,
        grid=(b, MEM_HEADS, s // tq),
        in_specs=[pl.BlockSpec((1, tq, md), lambda bi, h, i: (bi, i, OFF_MQ // md + h)),
                  pl.BlockSpec((1, N_MEM, md), lambda bi, h, i: (bi, 0, h)),
                  pl.BlockSpec((1, N_MEM, md), lambda bi, h, i: (bi, 0, MEM_HEADS + h))],
        out_specs=pl.BlockSpec((1, tq, md), lambda bi, h, i: (bi, i, h)),
        out_shape=jax.ShapeDtypeStruct((b, s, BRANCH_W), BF16),
        compiler_params=_params("parallel", "parallel", "arbitrary"),
        name="mem_attn",
    )(proj, memkv, memkv)


def _mix_kernel(x_ref, h_ref, lg_ref, lb_ref, a_ref, b_ref, c_ref, wg0_ref, wg1_ref, wg2_ref,
                bg0_ref, bg1_ref, bg2_ref, wb_ref, wo_ref, g1_ref, b1_ref, o_ref, acc_scr):
    n = pl.program_id(1)
    nrow = x_ref.shape[0] // LN_ROWS

    @pl.when(n == 0)
    def _():
        acc_scr[...] = jnp.zeros_like(acc_scr)

    h = h_ref[...]
    gates = [jax.nn.sigmoid(_dot(h, wg_ref[...]) + bg_ref[...])
             for wg_ref, bg_ref in ((wg0_ref, bg0_ref), (wg1_ref, bg1_ref), (wg2_ref, bg2_ref))]
    mixed = None
    for k, br_ref in enumerate((a_ref, b_ref, c_ref)):
        term = gates[k] * _dot(br_ref[...], wb_ref[k].astype(BF16))
        mixed = term if mixed is None else mixed + term
    acc_scr[...] += _dot(mixed.astype(BF16), wo_ref[...].astype(BF16))

    @pl.when(n == pl.num_programs(1) - 1)
    def _():
        def body(r, c):
            rows = pl.ds(pl.multiple_of(r * LN_ROWS, LN_ROWS), LN_ROWS)
            hh = _ln_rows(x_ref[rows, :], lg_ref[...], lb_ref[...])
            o_ref[rows, :] = _ln_rows(ALPHA * hh + acc_scr[rows, :], g1_ref[...], b1_ref[...])
            return c
        lax.fori_loop(0, nrow, body, 0)


def _mix(x2, hb, lg, lb, a, b, c, wg, bg, wb, wo, g1, b1, tm=512, tn=256):
    m, d = x2.shape
    bw = a.shape[1]
    nn = d // tn
    row = lambda w: pl.BlockSpec((tm, w), lambda i, n: (i, 0))
    vec = pl.BlockSpec((1, d), lambda i, n: (0, 0))
    wgs = [pl.BlockSpec((d, tn), functools.partial(lambda i, n, k: (0, k * nn + n), k=k)) for k in range(3)]
    bgs = [pl.BlockSpec((1, tn), functools.partial(lambda i, n, k: (0, k * nn + n), k=k)) for k in range(3)]
    return pl.pallas_call(
        _mix_kernel,
        grid=(m // tm, nn),
        in_specs=[row(d), row(d), vec, vec, row(bw), row(bw), row(bw), *wgs, *bgs,
                  pl.BlockSpec((3, bw, tn), lambda i, n: (0, 0, n)),
                  pl.BlockSpec((tn, d), lambda i, n: (n, 0)), vec, vec],
        out_specs=pl.BlockSpec((tm, d), lambda i, n: (i, 0)),
        out_shape=jax.ShapeDtypeStruct((m, d), F32),
        scratch_shapes=[pltpu.VMEM((tm, d), F32)],
        compiler_params=_params("parallel", "arbitrary"),
        name="mix",
    )(x2, hb, lg, lb, a, b, c, wg, wg, wg, bg, bg, bg, wb, wo, g1, b1)


HALO = 16


def _gelu_tanh(x):
    return 0.5 * x * (1.0 + jnp.tanh(math.sqrt(2.0 / math.pi) * (x + 0.044715 * (x * x * x))))


def _ffn_kernel(h_ref, hp_ref, hn_ref, wv_ref, wg_ref, cwv_ref, cwg_ref, cbv_ref, cbg_ref, wd_ref,
                g2_ref, b2_ref, o_ref, hx_scr, acc_scr, *, tiles_per_seq):
    i = pl.program_id(0)
    f = pl.program_id(1)
    tm = h_ref.shape[0]
    nrow = tm // LN_ROWS

    @pl.when(f == 0)
    def _():
        seq_first = (i % tiles_per_seq) == 0
        seq_last = (i % tiles_per_seq) == tiles_per_seq - 1
        hx_scr[0:HALO, :] = jnp.where(seq_first, 0.0, hp_ref[...]).astype(BF16)
        hx_scr[HALO:HALO + tm, :] = h_ref[...].astype(BF16)
        hx_scr[HALO + tm:, :] = jnp.where(seq_last, 0.0, hn_ref[...]).astype(BF16)
        acc_scr[...] = jnp.zeros_like(acc_scr)

    hx = hx_scr[...]
    ext = tm + 2 * HALO

    def conv(u, cw_ref, cb_ref):
        prev = pltpu.roll(u, 1, 0)[HALO:HALO + tm]
        nxt = pltpu.roll(u, ext - 1, 0)[HALO:HALO + tm]
        return prev * cw_ref[0:1, :] + u[HALO:HALO + tm] * cw_ref[1:2, :] + nxt * cw_ref[2:3, :] + cb_ref[...]

    gelu_gate = _gelu_tanh(conv(_dot(hx, wg_ref[...]), cwg_ref, cbg_ref))
    val = conv(_dot(hx, wv_ref[...]), cwv_ref, cbv_ref)
    acc_scr[...] += _dot((gelu_gate * val).astype(BF16), wd_ref[...])

    @pl.when(f == pl.num_programs(1) - 1)
    def _():
        def body(r, c):
            rows = pl.ds(pl.multiple_of(r * LN_ROWS, LN_ROWS), LN_ROWS)
            o_ref[rows, :] = _ln_rows(ALPHA * h_ref[rows, :] + acc_scr[rows, :], g2_ref[...], b2_ref[...])
            return c
        lax.fori_loop(0, nrow, body, 0)


def _ffn(h1, w_up, cw, cb, w_down, g2, b2, seq, tm=512, tf=512):
    m, d = h1.shape
    nf = D_FF_PAD // tf
    hb = tm // HALO
    nhb = m // HALO
    vec = pl.BlockSpec((1, d), lambda i, f: (0, 0))
    return pl.pallas_call(
        functools.partial(_ffn_kernel, tiles_per_seq=seq // tm),
        grid=(m // tm, nf),
        in_specs=[pl.BlockSpec((tm, d), lambda i, f: (i, 0)),
                  pl.BlockSpec((HALO, d), lambda i, f: (jnp.maximum(i * hb - 1, 0), 0)),
                  pl.BlockSpec((HALO, d), lambda i, f: (jnp.minimum((i + 1) * hb, nhb - 1), 0)),
                  pl.BlockSpec((d, tf), lambda i, f: (0, f)),
                  pl.BlockSpec((d, tf), lambda i, f: (0, nf + f)),
                  pl.BlockSpec((3, tf), lambda i, f: (0, f)),
                  pl.BlockSpec((3, tf), lambda i, f: (0, nf + f)),
                  pl.BlockSpec((1, tf), lambda i, f: (0, f)),
                  pl.BlockSpec((1, tf), lambda i, f: (0, nf + f)),
                  pl.BlockSpec((tf, d), lambda i, f: (f, 0)), vec, vec],
        out_specs=pl.BlockSpec((tm, d), lambda i, f: (i, 0)),
        out_shape=jax.ShapeDtypeStruct((m, d), F32),
        scratch_shapes=[pltpu.VMEM((tm + 2 * HALO, d), BF16), pltpu.VMEM((tm, d), F32)],
        compiler_params=_params("parallel", "arbitrary"),
        name="ffn",
    )(h1, h1, h1, w_up, w_up, cw, cw, cb, cb, w_down, g2, b2)


def _w_up_prep_kernel(w_ref, o_ref):
    o_ref[:, 0:D_FF] = w_ref[...].astype(BF16)
    o_ref[:, D_FF:] = jnp.zeros((o_ref.shape[0], D_FF_PAD - D_FF), BF16)


def _w_up_prep(w_up, tr=256):
    d = w_up.shape[0]
    return pl.pallas_call(
        _w_up_prep_kernel,
        grid=(2, d // tr),
        in_specs=[pl.BlockSpec((tr, D_FF), lambda half, r: (r, half))],
        out_specs=pl.BlockSpec((tr, D_FF_PAD), lambda half, r: (r, half)),
        out_shape=jax.ShapeDtypeStruct((d, 2 * D_FF_PAD), BF16),
        compiler_params=_params("parallel", "parallel"),
        name="w_up_prep",
    )(w_up)


def _w_down_prep_kernel(w_ref, o_ref):
    o_ref[0:D_FF, :] = w_ref[...].astype(BF16)
    o_ref[D_FF:, :] = jnp.zeros((D_FF_PAD - D_FF, o_ref.shape[1]), BF16)


def _w_down_prep(w_down, tc=256):
    d = w_down.shape[1]
    return pl.pallas_call(
        _w_down_prep_kernel,
        grid=(d // tc,),
        in_specs=[pl.BlockSpec((D_FF, tc), lambda c: (0, c))],
        out_specs=pl.BlockSpec((D_FF_PAD, tc), lambda c: (0, c)),
        out_shape=jax.ShapeDtypeStruct((D_FF_PAD, d), BF16),
        compiler_params=_params("parallel"),
        name="w_down_prep",
    )(w_down)


def _pad_ff(t, axis):
    val, gate = jnp.split(t, 2, axis=axis)
    pad = [(0, 0)] * t.ndim
    pad[axis] = (0, D_FF_PAD - D_FF)
    return jnp.concatenate([jnp.pad(val, pad), jnp.pad(gate, pad)], axis=axis)


def kernel(x, mem, ln_in_g, ln_in_b, rel_table, w_in, w_mem_kv, diff_lq1, diff_lk1, diff_lq2, diff_lk2,
           diff_subln_g, win_sink, w_gate, b_gate, w_branch, w_o, ln1_g, ln1_b, w_up, conv_w, conv_b,
           w_down, ln2_g, ln2_b):
    assert w_in.shape[0] == DEPTH == 1
    bsz, seq, d = x.shape
    x2 = x.reshape(bsz * seq, d)
    row = lambda v: v.reshape(1, -1).astype(F32)
    l = 0
    hb = _ln(x2, row(ln_in_g), row(ln_in_b))
    proj = _proj(hb, w_in[l])
    proj = proj.reshape(bsz, seq, IN_W)
    memkv = _mem_kv(mem.reshape(bsz * N_MEM, d), w_mem_kv[l]).reshape(bsz, N_MEM, 2 * BRANCH_W)

    tq_d, tb_d, tq_w, tb_w = 512, 256, 512, 256
    a = _diff_attn(proj, _diff_bias_tiles(rel_table[:, :DIFF_HEADS] * LOG2E, tb_d), row(diff_lq1[l]), row(diff_lk1[l]),
                   row(diff_lq2[l]), row(diff_lk2[l]), row(diff_subln_g[l]), tq=tq_d)
    b = _win_attn(proj, _win_bias_tiles(rel_table[:, DIFF_HEADS:] * LOG2E, tb_w), win_sink[l].astype(F32) * LOG2E,
                  tq=tq_w)
    c = _mem_attn(proj, memkv)

    m = bsz * seq
    h1 = _mix(x2, hb, row(ln_in_g), row(ln_in_b), a.reshape(m, -1), b.reshape(m, -1), c.reshape(m, -1),
              w_gate[l].astype(BF16), row(b_gate[l]), w_branch[l], w_o[l],
              row(ln1_g[l]), row(ln1_b[l]))

    w_up_p = _w_up_prep(w_up[l].astype(F32))
    cw_p = _pad_ff(conv_w[l].astype(F32), 1)
    cb_p = _pad_ff(row(conv_b[l]), 1)
    w_down_p = _w_down_prep(w_down[l].astype(F32))
    out = _ffn(h1, w_up_p, cw_p, cb_p, w_down_p, row(ln2_g[l]), row(ln2_b[l]), seq)
    return out.reshape(bsz, seq, d)
```

```python
import functools
import math

import jax
import jax.numpy as jnp
from jax import lax
from jax.experimental import pallas as pl
from jax.experimental.pallas import tpu as pltpu

F32 = jnp.float32
BF16 = jnp.bfloat16

D_MODEL = 2048
SEQ = 2048
N_MEM = 256
HEAD_DIM = 128
BRANCH_W = 1024
DIFF_HEADS = 8
DIFF_QK = 64
WIN_HEADS = 8
WIN_KV_HEADS = 2
WIN_GROUP = WIN_HEADS // WIN_KV_HEADS
WINDOW = 128
MEM_HEADS = 4
MEM_DIM = 256
OFF_DQ, OFF_DK, OFF_DV, OFF_WQ, OFF_WK, OFF_WV, OFF_MQ = 0, 1024, 2048, 3072, 4096, 4352, 4608
IN_W = 5632
D_FF = 5504
D_FF_PAD = 5632
REL_BUCKETS = 32
REL_MAX_DIST = 128
DEPTH = 1
ALPHA = (2 * DEPTH) ** 0.25
LN_EPS = 1e-5
NEG = -1e30
LOG2E = math.log2(math.e)
LAMBDA_INIT = 0.8 - 0.6 * math.exp(-0.3 * 0)

VMEM_LIMIT = 56 * 1024 * 1024
LN_ROWS = 128


def _ln_rows(x, g, b):
    mu = jnp.mean(x, axis=-1, keepdims=True)
    xc = x - mu
    var = jnp.mean(xc * xc, axis=-1, keepdims=True)
    return xc * lax.rsqrt(var + LN_EPS) * g + b


def _dot(a, b):
    return jnp.dot(a, b, preferred_element_type=F32)


def _dot_nt(a, b):
    return lax.dot_general(a, b, (((1,), (1,)), ((), ())), preferred_element_type=F32)


def _params(*sem):
    return pltpu.CompilerParams(dimension_semantics=sem, vmem_limit_bytes=VMEM_LIMIT)


def _ln_kernel(x_ref, g_ref, b_ref, h_ref):
    def body(r, c):
        rows = pl.ds(pl.multiple_of(r * LN_ROWS, LN_ROWS), LN_ROWS)
        h_ref[rows, :] = _ln_rows(x_ref[rows, :], g_ref[...], b_ref[...]).astype(BF16)
        return c
    lax.fori_loop(0, x_ref.shape[0] // LN_ROWS, body, 0)


def _ln(x2, g, b, tm=512):
    m, d = x2.shape
    vec = pl.BlockSpec((1, d), lambda i: (0, 0))
    return pl.pallas_call(
        _ln_kernel,
        grid=(m // tm,),
        in_specs=[pl.BlockSpec((tm, d), lambda i: (i, 0)), vec, vec],
        out_specs=pl.BlockSpec((tm, d), lambda i: (i, 0)),
        out_shape=jax.ShapeDtypeStruct((m, d), BF16),
        compiler_params=_params("parallel"),
        name="ln_in",
    )(x2, g, b)


def _proj_kernel(h_ref, w_ref, o_ref):
    o_ref[...] = _dot(h_ref[...], w_ref[...].astype(BF16)).astype(BF16)


def _proj(hb, w, tm=2048, tn=512):
    m, d = hb.shape
    n = w.shape[1]
    return pl.pallas_call(
        _proj_kernel,
        grid=(m // tm, n // tn),
        in_specs=[pl.BlockSpec((tm, d), lambda i, j: (i, 0)),
                  pl.BlockSpec((d, tn), lambda i, j: (0, j))],
        out_specs=pl.BlockSpec((tm, tn), lambda i, j: (i, j)),
        out_shape=jax.ShapeDtypeStruct((m, n), BF16),
        compiler_params=_params("parallel", "arbitrary"),
        name="proj",
    )(hb, w)


def _gates_kernel(h_ref, w_ref, b_ref, o_ref):
    o_ref[...] = jax.nn.sigmoid(_dot(h_ref[...], w_ref[...].astype(BF16)) + b_ref[...]).astype(BF16)


def _gates(hb, w, b, tm=2048, tn=512):
    m, d = hb.shape
    n = w.shape[1]
    return pl.pallas_call(
        _gates_kernel,
        grid=(m // tm, n // tn),
        in_specs=[pl.BlockSpec((tm, d), lambda i, j: (i, 0)),
                  pl.BlockSpec((d, tn), lambda i, j: (0, j)),
                  pl.BlockSpec((1, tn), lambda i, j: (0, j))],
        out_specs=pl.BlockSpec((tm, tn), lambda i, j: (i, j)),
        out_shape=jax.ShapeDtypeStruct((m, n), BF16),
        compiler_params=_params("parallel", "arbitrary"),
        name="gates",
    )(hb, w, b)


def _mem_kv_kernel(m_ref, w_ref, o_ref):
    o_ref[...] = _dot(m_ref[...].astype(BF16), w_ref[...].astype(BF16)).astype(BF16)


def _mem_kv(mem2, w, tn=512):
    m, d = mem2.shape
    n = w.shape[1]
    return pl.pallas_call(
        _mem_kv_kernel,
        grid=(n // tn,),
        in_specs=[pl.BlockSpec((m, d), lambda j: (0, 0)),
                  pl.BlockSpec((d, tn), lambda j: (0, j))],
        out_specs=pl.BlockSpec((m, tn), lambda j: (0, j)),
        out_shape=jax.ShapeDtypeStruct((m, n), BF16),
        compiler_params=_params("parallel"),
        name="mem_kv",
    )(mem2, w)


def _bucket_lower_bounds():
    half = REL_BUCKETS // 2
    max_exact = half // 2
    n = jnp.arange(REL_MAX_DIST + 1)
    log_part = (jnp.log(jnp.maximum(n, 1).astype(F32) / max_exact) / math.log(REL_MAX_DIST / max_exact)
                * (half - max_exact)).astype(jnp.int32)
    bucket = jnp.where(n < max_exact, n, jnp.minimum(max_exact + log_part, half - 1))
    bucket = jnp.arange(half, dtype=jnp.int32)[bucket]
    ids = jnp.arange(half, dtype=jnp.int32)
    return jnp.sum((bucket[None, :] < ids[:, None]).astype(jnp.int32), axis=1)


BIAS_ROWS = 64


def _bias_kernel(lb_ref, tab_ref, lo_ref, hi_ref, off_ref, o_ref, *, limit):
    h = pl.program_id(0)
    half = REL_BUCKETS // 2
    ncol = o_ref.shape[2]

    def body(r, carry):
        rows = pl.ds(pl.multiple_of(r * BIAS_ROWS, BIAS_ROWS), BIAS_ROWS)

        def general():
            rel = lax.broadcasted_iota(jnp.int32, (BIAS_ROWS, ncol), 1) + off_ref[rows, :]
            n = jnp.abs(rel)
            neg = jnp.full(rel.shape, tab_ref[0, h], F32)
            pos = jnp.full(rel.shape, tab_ref[half, h], F32)
            for b in range(1, half):
                reached = n >= lb_ref[b]
                neg = jnp.where(reached, tab_ref[b, h], neg)
                pos = jnp.where(reached, tab_ref[half + b, h], pos)
            out = jnp.where(rel > 0, pos, neg)
            if limit is not None:
                out = jnp.where(n > limit, NEG, out)
            o_ref[0, rows, :] = out

        if limit is not None:
            general()
            return carry
        far = lb_ref[half - 1]
        all_after = lo_ref[r] >= far
        all_before = -(hi_ref[r] + (ncol - 1)) >= far

        @pl.when(all_after)
        def _():
            o_ref[0, rows, :] = jnp.full((BIAS_ROWS, ncol), tab_ref[2 * half - 1, h], F32)

        @pl.when(all_before)
        def _():
            o_ref[0, rows, :] = jnp.full((BIAS_ROWS, ncol), tab_ref[half - 1, h], F32)

        pl.when(jnp.logical_not(jnp.logical_or(all_after, all_before)))(general)
        return carry
    lax.fori_loop(0, o_ref.shape[1] // BIAS_ROWS, body, 0)


def _bias_tiles(table, row_offset, ncol, limit=None):
    nh = table.shape[1]
    nrow = row_offset.shape[0]
    off = row_offset.astype(jnp.int32)
    chunks = off.reshape(nrow // BIAS_ROWS, BIAS_ROWS)
    smem = pl.BlockSpec(memory_space=pltpu.SMEM)
    return pl.pallas_call(
        functools.partial(_bias_kernel, limit=limit),
        grid=(nh,),
        in_specs=[smem, smem, smem, smem, pl.BlockSpec((nrow, 1), lambda h: (0, 0))],
        out_specs=pl.BlockSpec((1, nrow, ncol), lambda h: (h, 0, 0)),
        out_shape=jax.ShapeDtypeStruct((nh, nrow, ncol), F32),
        compiler_params=_params("parallel"),
        name="bias_tiles",
    )(_bucket_lower_bounds(), table.astype(F32), jnp.min(chunks, axis=1), jnp.max(chunks, axis=1),
      off.reshape(nrow, 1))


def _diff_bias_tiles(table, tq):
    r = jnp.arange(5 * tq)
    return _bias_tiles(table, (r // tq - 2) * tq - r % tq, tq).reshape(table.shape[1], 5, tq, tq)


def _win_bias_tiles(table, tq):
    return _bias_tiles(table, -WINDOW - jnp.arange(tq), tq + 2 * WINDOW, limit=WINDOW)


def _diff_kernel(q_ref, k_ref, v_ref, band_ref, lq1_ref, lk1_ref, lq2_ref, lk2_ref, g_ref, o_ref,
                 s_scr, mx_scr, *, tq, rc, n_tiles, tiles_per_head):
    t = pl.program_id(0)
    hd = v_ref.shape[2]

    @pl.when(t == 0)
    def _():
        s_scr[...] = jnp.zeros_like(s_scr)
        mx_scr[...] = jnp.zeros_like(mx_scr)

    def tick(slot):
        other = 1 - slot
        i = jnp.minimum(t, n_tiles - 1) % tiles_per_head
        nk = k_ref.shape[1] // tq
        q = (q_ref[0].astype(F32) * (DIFF_QK ** -0.5 * LOG2E)).astype(BF16)
        lane = lax.broadcasted_iota(jnp.int32, q.shape, 1)
        zero = jnp.zeros_like(q)
        qs = jnp.concatenate([jnp.where(lane < DIFF_QK, q, zero), jnp.where(lane >= DIFF_QK, q, zero)], axis=0)
        lanes = mx_scr.shape[-1]
        tb = band_ref.shape[2]
        nb = tq // tb
        for j in range(nk):
            s = _dot_nt(qs, k_ref[0, j * tq:(j + 1) * tq, :])
            for r0 in range(0, 2 * tq, tb):
                rows = slice(r0, r0 + tb)
                fold = None
                for cb in range(nb):
                    d = (j - i) * nb + cb - (r0 // tb) % nb
                    sb = s[rows, cb * tb:(cb + 1) * tb] + band_ref[0, jnp.clip(d, -2, 2) + 2]
                    s_scr[slot, rows, j * tq + cb * tb:j * tq + (cb + 1) * tb] = sb
                    fold = functools.reduce(jnp.maximum, [sb[:, c:c + lanes] for c in range(0, tb, lanes)]
                                            + ([] if fold is None else [fold]))
                mx_scr[slot, rows, :] = fold if j == 0 else jnp.maximum(mx_scr[slot, rows, :], fold)

        nchunk = 2 * tq // rc
        chunk = lambda r: slice(r * rc, (r + 1) * rc)
        p = jnp.concatenate(
            [jnp.exp2(s_scr[other, chunk(r), :] - jnp.max(mx_scr[other, chunk(r), :], axis=-1, keepdims=True))
             .astype(BF16) for r in range(nchunk)], axis=0)
        v_ones = jnp.concatenate([v_ref[0], jnp.ones(v_ref.shape[1:], BF16)], axis=1)
        pv = _dot(p, v_ones)
        lam = (jnp.exp(jnp.sum(lq1_ref[...] * lk1_ref[...], axis=-1, keepdims=True))
               - jnp.exp(jnp.sum(lq2_ref[...] * lk2_ref[...], axis=-1, keepdims=True)) + LAMBDA_INIT)
        o = pv[0:tq, 0:hd] * (1.0 / pv[0:tq, hd:]) - pv[tq:, 0:hd] * (lam / pv[tq:, hd:])
        ms = jnp.mean(o * o, axis=-1, keepdims=True)
        o_ref[0] = (o * lax.rsqrt(ms + LN_EPS) * g_ref[...] * (1.0 - LAMBDA_INIT)).astype(BF16)

    pl.when(t % 2 == 0)(functools.partial(tick, 0))
    pl.when(t % 2 == 1)(functools.partial(tick, 1))


def _diff_attn(proj, band, lq1, lk1, lq2, lk2, g, tq=256, rc=32):
    b, s, _ = proj.shape
    cb = HEAD_DIM
    ni = s // tq
    n_tiles = b * DIFF_HEADS * ni
    depth = 1

    def tile(t, lag):
        tc = jnp.clip(t - lag, 0, n_tiles - 1)
        return tc // (DIFF_HEADS * ni), (tc // ni) % DIFF_HEADS, tc % ni

    def q_map(t):
        bi, h, i = tile(t, 0)
        return bi, i, OFF_DQ // cb + h

    def k_map(t):
        bi, h, _ = tile(t, 0)
        return bi, 0, OFF_DK // cb + h

    def v_map(t):
        bi, h, _ = tile(t, depth)
        return bi, 0, OFF_DV // cb + h

    def o_map(t):
        bi, h, i = tile(t, depth)
        return bi, i, h

    vec = lambda n: pl.BlockSpec((1, n), lambda t: (0, 0))
    return pl.pallas_call(
        functools.partial(_diff_kernel, tq=tq, rc=rc, n_tiles=n_tiles, tiles_per_head=ni),
        grid=(n_tiles + depth,),
        in_specs=[pl.BlockSpec((1, tq, cb), q_map),
                  pl.BlockSpec((1, s, cb), k_map),
                  pl.BlockSpec((1, s, cb), v_map),
                  pl.BlockSpec((1,) + band.shape[1:], lambda t: (tile(t, 0)[1], 0, 0, 0)),
                  vec(DIFF_QK), vec(DIFF_QK), vec(DIFF_QK), vec(DIFF_QK), vec(HEAD_DIM)],
        out_specs=pl.BlockSpec((1, tq, cb), o_map),
        out_shape=jax.ShapeDtypeStruct((b, s, BRANCH_W), BF16),
        scratch_shapes=[pltpu.VMEM((2, 2 * tq, s), F32), pltpu.VMEM((2, 2 * tq, cb), F32)],
        compiler_params=_params("arbitrary"),
        name="diff_attn",
    )(proj, proj, proj, band, lq1, lk1, lq2, lk2, g)


def _win_kernel(sink_ref, q_ref, kp_ref, km_ref, kn_ref, vp_ref, vm_ref, vn_ref, bias_ref, o_ref, *, tq):
    g = pl.program_id(1)
    i = pl.program_id(2)
    first = i == 0
    last = i == pl.num_programs(2) - 1
    w = WINDOW
    hd = HEAD_DIM
    tb = bias_ref.shape[1]
    nkeys = tb + 2 * w
    keys = jnp.concatenate([kp_ref[0], km_ref[0], kn_ref[0]], axis=0)
    vals = jnp.concatenate([vp_ref[0], vm_ref[0], vn_ref[0]], axis=0)
    v_ones = jnp.concatenate([vals, jnp.ones((tq + 2 * w, hd), BF16)], axis=1)
    col = lax.broadcasted_iota(jnp.int32, (1, nkeys), 1)
    scale = jnp.asarray(hd ** -0.5 * LOG2E, F32)
    for r0 in range(0, tq, tb):
        rows = slice(r0, r0 + tb)
        band = slice(r0, r0 + nkeys)
        outside = None
        if r0 == 0:
            outside = jnp.logical_and(first, col < w)
        if r0 + tb == tq:
            past_end = jnp.logical_and(last, col >= tb + w)
            outside = past_end if outside is None else jnp.logical_or(outside, past_end)
        for j in range(WIN_GROUP):
            cols = slice(j * hd, (j + 1) * hd)
            s = _dot_nt(q_ref[0, rows, cols], keys[band]) * scale + bias_ref[j]
            if outside is not None:
                s = jnp.where(outside, NEG, s)
            sink = sink_ref[g * WIN_GROUP + j]
            fold = functools.reduce(jnp.maximum, [s[:, c:c + w] for c in range(0, nkeys, w)])
            m = jnp.maximum(jnp.max(fold, axis=-1, keepdims=True), sink)
            pv = _dot(jnp.exp2(s - m).astype(BF16), v_ones[band])
            o_ref[0, rows, cols] = (pv[:, 0:hd] / (pv[:, hd:] + jnp.exp2(sink - m))).astype(BF16)


def _win_attn(proj, bias, sink, tq=256):
    b, s, _ = proj.shape
    hd = HEAD_DIM
    nb = s // WINDOW
    r = tq // WINDOW
    prev = lambda bi, g, i: jnp.maximum(i * r - 1, 0)
    nxt = lambda bi, g, i: jnp.minimum((i + 1) * r, nb - 1)
    kcol, vcol = OFF_WK // hd, OFF_WV // hd
    return pl.pallas_call(
        functools.partial(_win_kernel, tq=tq),
        grid=(b, WIN_KV_HEADS, s // tq),
        in_specs=[pl.BlockSpec(memory_space=pltpu.SMEM),
                  pl.BlockSpec((1, tq, WIN_GROUP * hd), lambda bi, g, i: (bi, i, OFF_WQ // (WIN_GROUP * hd) + g)),
                  pl.BlockSpec((1, WINDOW, hd), lambda bi, g, i: (bi, prev(bi, g, i), kcol + g)),
                  pl.BlockSpec((1, tq, hd), lambda bi, g, i: (bi, i, kcol + g)),
                  pl.BlockSpec((1, WINDOW, hd), lambda bi, g, i: (bi, nxt(bi, g, i), kcol + g)),
                  pl.BlockSpec((1, WINDOW, hd), lambda bi, g, i: (bi, prev(bi, g, i), vcol + g)),
                  pl.BlockSpec((1, tq, hd), lambda bi, g, i: (bi, i, vcol + g)),
                  pl.BlockSpec((1, WINDOW, hd), lambda bi, g, i: (bi, nxt(bi, g, i), vcol + g)),
                  pl.BlockSpec((WIN_GROUP,) + bias.shape[1:], lambda bi, g, i: (g, 0, 0))],
        out_specs=pl.BlockSpec((1, tq, WIN_GROUP * hd), lambda bi, g, i: (bi, i, g)),
        out_shape=jax.ShapeDtypeStruct((b, s, BRANCH_W), BF16),
        compiler_params=_params("parallel", "parallel", "arbitrary"),
        name="win_attn",
    )(sink, proj, proj, proj, proj, proj, proj, proj, bias)


def _mem_attn_kernel(q_ref, k_ref, v_ref, o_ref):
    s = _dot_nt(q_ref[0], k_ref[0]) * jnp.asarray(MEM_DIM ** -0.5, F32)
    p = jnp.exp(s - jnp.max(s, axis=-1, keepdims=True))
    p = p * (1.0 / jnp.sum(p, axis=-1, keepdims=True))
    o_ref[0] = _dot(p.astype(BF16), v_ref[0]).astype(BF16)


def _mem_attn(proj, memkv, tq=2048):
    b, s, _ = proj.shape
    md = MEM_DIM
    return pl.pallas_call(
        _mem_attn_kernel,
        grid=(b, MEM_HEADS, s // tq),
        in_specs=[pl.BlockSpec((1, tq, md), lambda bi, h, i: (bi, i, OFF_MQ // md + h)),
                  pl.BlockSpec((1, N_MEM, md), lambda bi, h, i: (bi, 0, h)),
                  pl.BlockSpec((1, N_MEM, md), lambda bi, h, i: (bi, 0, MEM_HEADS + h))],
        out_specs=pl.BlockSpec((1, tq, md), lambda bi, h, i: (bi, i, h)),
        out_shape=jax.ShapeDtypeStruct((b, s, BRANCH_W), BF16),
        compiler_params=_params("parallel", "parallel", "arbitrary"),
        name="mem_attn",
    )(proj, memkv, memkv)


def _mix_kernel(x_ref, lg_ref, lb_ref, a_ref, b_ref, c_ref, g0_ref, g1_ref, g2_ref, wb_ref, wo_ref,
                ln_g_ref, ln_b_ref, o_ref, acc_scr):
    n = pl.program_id(1)
    nrow = x_ref.shape[0] // LN_ROWS

    @pl.when(n == 0)
    def _():
        acc_scr[...] = jnp.zeros_like(acc_scr)

    mixed = None
    for k, (br_ref, gate_ref) in enumerate(((a_ref, g0_ref), (b_ref, g1_ref), (c_ref, g2_ref))):
        term = gate_ref[...].astype(F32) * _dot(br_ref[...], wb_ref[k].astype(BF16))
        mixed = term if mixed is None else mixed + term
    acc_scr[...] += _dot(mixed.astype(BF16), wo_ref[...].astype(BF16))

    @pl.when(n == pl.num_programs(1) - 1)
    def _():
        def body(r, c):
            rows = pl.ds(pl.multiple_of(r * LN_ROWS, LN_ROWS), LN_ROWS)
            hh = _ln_rows(x_ref[rows, :], lg_ref[...], lb_ref[...])
            o_ref[rows, :] = _ln_rows(ALPHA * hh + acc_scr[rows, :], ln_g_ref[...], ln_b_ref[...])
            return c
        lax.fori_loop(0, nrow, body, 0)


def _mix(x2, lg, lb, a, b, c, gates, wb, wo, g1, b1, tm=512, tn=512):
    m, d = x2.shape
    bw = a.shape[1]
    nn = d // tn
    row = lambda w: pl.BlockSpec((tm, w), lambda i, n: (i, 0))
    vec = pl.BlockSpec((1, d), lambda i, n: (0, 0))
    gts = [pl.BlockSpec((tm, tn), functools.partial(lambda i, n, k: (i, k * nn + n), k=k)) for k in range(3)]
    return pl.pallas_call(
        _mix_kernel,
        grid=(m // tm, nn),
        in_specs=[row(d), vec, vec, row(bw), row(bw), row(bw), *gts,
                  pl.BlockSpec((3, bw, tn), lambda i, n: (0, 0, n)),
                  pl.BlockSpec((tn, d), lambda i, n: (n, 0)), vec, vec],
        out_specs=pl.BlockSpec((tm, d), lambda i, n: (i, 0)),
        out_shape=jax.ShapeDtypeStruct((m, d), F32),
        scratch_shapes=[pltpu.VMEM((tm, d), F32)],
        compiler_params=_params("parallel", "arbitrary"),
        name="mix",
    )(x2, lg, lb, a, b, c, gates, gates, gates, wb, wo, g1, b1)


HALO = 16


def _gelu_tanh(x):
    return 0.5 * x * (1.0 + jnp.tanh(math.sqrt(2.0 / math.pi) * (x + 0.044715 * (x * x * x))))


def _ffn_kernel(h_ref, hp_ref, hn_ref, wv_ref, wg_ref, cwv_ref, cwg_ref, cbv_ref, cbg_ref, wd_ref,
                g2_ref, b2_ref, o_ref, hx_scr, acc_scr, *, tiles_per_seq):
    i = pl.program_id(0)
    f = pl.program_id(1)
    tm = h_ref.shape[0]
    nrow = tm // LN_ROWS

    @pl.when(f == 0)
    def _():
        seq_first = (i % tiles_per_seq) == 0
        seq_last = (i % tiles_per_seq) == tiles_per_seq - 1
        hx_scr[0:HALO, :] = jnp.where(seq_first, 0.0, hp_ref[...]).astype(BF16)
        hx_scr[HALO:HALO + tm, :] = h_ref[...].astype(BF16)
        hx_scr[HALO + tm:, :] = jnp.where(seq_last, 0.0, hn_ref[...]).astype(BF16)
        acc_scr[...] = jnp.zeros_like(acc_scr)

    hx = hx_scr[...]
    ext = tm + 2 * HALO

    def conv(u, cw_ref, cb_ref):
        prev = pltpu.roll(u, 1, 0)[HALO:HALO + tm]
        nxt = pltpu.roll(u, ext - 1, 0)[HALO:HALO + tm]
        return prev * cw_ref[0:1, :] + u[HALO:HALO + tm] * cw_ref[1:2, :] + nxt * cw_ref[2:3, :] + cb_ref[...]

    gelu_gate = _gelu_tanh(conv(_dot(hx, wg_ref[...]), cwg_ref, cbg_ref))
    val = conv(_dot(hx, wv_ref[...]), cwv_ref, cbv_ref)
    acc_scr[...] += _dot((gelu_gate * val).astype(BF16), wd_ref[...])

    @pl.when(f == pl.num_programs(1) - 1)
    def _():
        def body(r, c):
            rows = pl.ds(pl.multiple_of(r * LN_ROWS, LN_ROWS), LN_ROWS)
            o_ref[rows, :] = _ln_rows(ALPHA * h_ref[rows, :] + acc_scr[rows, :], g2_ref[...], b2_ref[...])
            return c
        lax.fori_loop(0, nrow, body, 0)


def _ffn(h1, w_up, cw, cb, w_down, g2, b2, seq, tm=512, tf=512):
    m, d = h1.shape
    nf = D_FF_PAD // tf
    hb = tm // HALO
    nhb = m // HALO
    vec = pl.BlockSpec((1, d), lambda i, f: (0, 0))
    return pl.pallas_call(
        functools.partial(_ffn_kernel, tiles_per_seq=seq // tm),
        grid=(m // tm, nf),
        in_specs=[pl.BlockSpec((tm, d), lambda i, f: (i, 0)),
                  pl.BlockSpec((HALO, d), lambda i, f: (jnp.maximum(i * hb - 1, 0), 0)),
                  pl.BlockSpec((HALO, d), lambda i, f: (jnp.minimum((i + 1) * hb, nhb - 1), 0)),
                  pl.BlockSpec((d, tf), lambda i, f: (0, f)),
                  pl.BlockSpec((d, tf), lambda i, f: (0, nf + f)),
                  pl.BlockSpec((3, tf), lambda i, f: (0, f)),
                  pl.BlockSpec((3, tf), lambda i, f: (0, nf + f)),
                  pl.BlockSpec((1, tf), lambda i, f: (0, f)),
                  pl.BlockSpec((1, tf), lambda i, f: (0, nf + f)),
                  pl.BlockSpec((tf, d), lambda i, f: (f, 0)), vec, vec],
        out_specs=pl.BlockSpec((tm, d), lambda i, f: (i, 0)),
        out_shape=jax.ShapeDtypeStruct((m, d), F32),
        scratch_shapes=[pltpu.VMEM((tm + 2 * HALO, d), BF16), pltpu.VMEM((tm, d), F32)],
        compiler_params=_params("parallel", "arbitrary"),
        name="ffn",
    )(h1, h1, h1, w_up, w_up, cw, cw, cb, cb, w_down, g2, b2)


def _w_up_prep_kernel(w_ref, o_ref):
    o_ref[:, 0:D_FF] = w_ref[...].astype(BF16)
    o_ref[:, D_FF:] = jnp.zeros((o_ref.shape[0], D_FF_PAD - D_FF), BF16)


def _w_up_prep(w_up, tr=256):
    d = w_up.shape[0]
    return pl.pallas_call(
        _w_up_prep_kernel,
        grid=(2, d // tr),
        in_specs=[pl.BlockSpec((tr, D_FF), lambda half, r: (r, half))],
        out_specs=pl.BlockSpec((tr, D_FF_PAD), lambda half, r: (r, half)),
        out_shape=jax.ShapeDtypeStruct((d, 2 * D_FF_PAD), BF16),
        compiler_params=_params("parallel", "parallel"),
        name="w_up_prep",
    )(w_up)


def _w_down_prep_kernel(w_ref, o_ref):
    o_ref[0:D_FF, :] = w_ref[...].astype(BF16)
    o_ref[D_FF:, :] = jnp.zeros((D_FF_PAD - D_FF, o_ref.shape[1]), BF16)


def _w_down_prep(w_down, tc=256):
    d = w_down.shape[1]
    return pl.pallas_call(
        _w_down_prep_kernel,
        grid=(d // tc,),
        in_specs=[pl.BlockSpec((D_FF, tc), lambda c: (0, c))],
        out_specs=pl.BlockSpec((D_FF_PAD, tc), lambda c: (0, c)),
        out_shape=jax.ShapeDtypeStruct((D_FF_PAD, d), BF16),
        compiler_params=_params("parallel"),
        name="w_down_prep",
    )(w_down)


def _pad_ff(t, axis):
    val, gate = jnp.split(t, 2, axis=axis)
    pad = [(0, 0)] * t.ndim
    pad[axis] = (0, D_FF_PAD - D_FF)
    return jnp.concatenate([jnp.pad(val, pad), jnp.pad(gate, pad)], axis=axis)


def kernel(x, mem, ln_in_g, ln_in_b, rel_table, w_in, w_mem_kv, diff_lq1, diff_lk1, diff_lq2, diff_lk2,
           diff_subln_g, win_sink, w_gate, b_gate, w_branch, w_o, ln1_g, ln1_b, w_up, conv_w, conv_b,
           w_down, ln2_g, ln2_b):
    assert w_in.shape[0] == DEPTH == 1
    bsz, seq, d = x.shape
    x2 = x.reshape(bsz * seq, d)
    row = lambda v: v.reshape(1, -1).astype(F32)
    l = 0
    hb = _ln(x2, row(ln_in_g), row(ln_in_b))
    proj = _proj(hb, w_in[l])
    proj = proj.reshape(bsz, seq, IN_W)
    memkv = _mem_kv(mem.reshape(bsz * N_MEM, d), w_mem_kv[l]).reshape(bsz, N_MEM, 2 * BRANCH_W)

    tq_d, tb_d, tq_w, tb_w = 512, 256, 512, 256
    a = _diff_attn(proj, _diff_bias_tiles(rel_table[:, :DIFF_HEADS] * LOG2E, tb_d), row(diff_lq1[l]), row(diff_lk1[l]),
                   row(diff_lq2[l]), row(diff_lk2[l]), row(diff_subln_g[l]), tq=tq_d)
    b = _win_attn(proj, _win_bias_tiles(rel_table[:, DIFF_HEADS:] * LOG2E, tb_w), win_sink[l].astype(F32) * LOG2E,
                  tq=tq_w)
    c = _mem_attn(proj, memkv)

    m = bsz * seq
    gates = _gates(hb, w_gate[l], row(b_gate[l]))
    h1 = _mix(x2, row(ln_in_g), row(ln_in_b), a.reshape(m, -1), b.reshape(m, -1), c.reshape(m, -1), gates,
              w_branch[l], w_o[l], row(ln1_g[l]), row(ln1_b[l]))

    w_up_p = _w_up_prep(w_up[l].astype(F32))
    cw_p = _pad_ff(conv_w[l].astype(F32), 1)
    cb_p = _pad_ff(row(conv_b[l]), 1)
    w_down_p = _w_down_prep(w_down[l].astype(F32))
    out = _ffn(h1, w_up_p, cw_p, cb_p, w_down_p, row(ln2_g[l]), row(ln2_b[l]), seq)
    return out.reshape(bsz, seq, d)
```

```python
import functools
import math

import jax
import jax.numpy as jnp
from jax import lax
from jax.experimental import pallas as pl
from jax.experimental.pallas import tpu as pltpu

F32 = jnp.float32
BF16 = jnp.bfloat16

D_MODEL = 2048
SEQ = 2048
N_MEM = 256
HEAD_DIM = 128
BRANCH_W = 1024
DIFF_HEADS = 8
DIFF_QK = 64
WIN_HEADS = 8
WIN_KV_HEADS = 2
WIN_GROUP = WIN_HEADS // WIN_KV_HEADS
WINDOW = 128
MEM_HEADS = 4
MEM_DIM = 256
OFF_DQ, OFF_DK, OFF_DV, OFF_WQ, OFF_WK, OFF_WV, OFF_MQ = 0, 1024, 2048, 3072, 4096, 4352, 4608
IN_W = 5632
D_FF = 5504
D_FF_PAD = 5632
REL_BUCKETS = 32
REL_MAX_DIST = 128
DEPTH = 1
ALPHA = (2 * DEPTH) ** 0.25
LN_EPS = 1e-5
NEG = -1e30
LOG2E = math.log2(math.e)
LAMBDA_INIT = 0.8 - 0.6 * math.exp(-0.3 * 0)

VMEM_LIMIT = 56 * 1024 * 1024
LN_ROWS = 128


def _ln_rows(x, g, b):
    mu = jnp.mean(x, axis=-1, keepdims=True)
    xc = x - mu
    var = jnp.mean(xc * xc, axis=-1, keepdims=True)
    return xc * lax.rsqrt(var + LN_EPS) * g + b


def _dot(a, b):
    return jnp.dot(a, b, preferred_element_type=F32)


def _dot_nt(a, b):
    return lax.dot_general(a, b, (((1,), (1,)), ((), ())), preferred_element_type=F32)


def _params(*sem):
    return pltpu.CompilerParams(dimension_semantics=sem, vmem_limit_bytes=VMEM_LIMIT)


def _ln_kernel(x_ref, g_ref, b_ref, h_ref):
    def body(r, c):
        rows = pl.ds(pl.multiple_of(r * LN_ROWS, LN_ROWS), LN_ROWS)
        h_ref[rows, :] = _ln_rows(x_ref[rows, :], g_ref[...], b_ref[...]).astype(BF16)
        return c
    lax.fori_loop(0, x_ref.shape[0] // LN_ROWS, body, 0)


def _ln(x2, g, b, tm=512):
    m, d = x2.shape
    vec = pl.BlockSpec((1, d), lambda i: (0, 0))
    return pl.pallas_call(
        _ln_kernel,
        grid=(m // tm,),
        in_specs=[pl.BlockSpec((tm, d), lambda i: (i, 0)), vec, vec],
        out_specs=pl.BlockSpec((tm, d), lambda i: (i, 0)),
        out_shape=jax.ShapeDtypeStruct((m, d), BF16),
        compiler_params=_params("parallel"),
        name="ln_in",
    )(x2, g, b)


def _proj_kernel(h_ref, w_ref, o_ref):
    o_ref[...] = _dot(h_ref[...], w_ref[...].astype(BF16)).astype(BF16)


def _proj(hb, w, tm=2048, tn=512):
    m, d = hb.shape
    n = w.shape[1]
    return pl.pallas_call(
        _proj_kernel,
        grid=(m // tm, n // tn),
        in_specs=[pl.BlockSpec((tm, d), lambda i, j: (i, 0)),
                  pl.BlockSpec((d, tn), lambda i, j: (0, j))],
        out_specs=pl.BlockSpec((tm, tn), lambda i, j: (i, j)),
        out_shape=jax.ShapeDtypeStruct((m, n), BF16),
        compiler_params=_params("parallel", "arbitrary"),
        name="proj",
    )(hb, w)


def _gates_kernel(h_ref, w_ref, b_ref, o_ref):
    o_ref[...] = jax.nn.sigmoid(_dot(h_ref[...], w_ref[...].astype(BF16)) + b_ref[...]).astype(BF16)


def _gates(hb, w, b, tm=2048, tn=512):
    m, d = hb.shape
    n = w.shape[1]
    return pl.pallas_call(
        _gates_kernel,
        grid=(m // tm, n // tn),
        in_specs=[pl.BlockSpec((tm, d), lambda i, j: (i, 0)),
                  pl.BlockSpec((d, tn), lambda i, j: (0, j)),
                  pl.BlockSpec((1, tn), lambda i, j: (0, j))],
        out_specs=pl.BlockSpec((tm, tn), lambda i, j: (i, j)),
        out_shape=jax.ShapeDtypeStruct((m, n), BF16),
        compiler_params=_params("parallel", "arbitrary"),
        name="gates",
    )(hb, w, b)


def _mem_kv_kernel(m_ref, w_ref, o_ref):
    o_ref[...] = _dot(m_ref[...].astype(BF16), w_ref[...].astype(BF16)).astype(BF16)


def _mem_kv(mem2, w, tn=512):
    m, d = mem2.shape
    n = w.shape[1]
    return pl.pallas_call(
        _mem_kv_kernel,
        grid=(n // tn,),
        in_specs=[pl.BlockSpec((m, d), lambda j: (0, 0)),
                  pl.BlockSpec((d, tn), lambda j: (0, j))],
        out_specs=pl.BlockSpec((m, tn), lambda j: (0, j)),
        out_shape=jax.ShapeDtypeStruct((m, n), BF16),
        compiler_params=_params("parallel"),
        name="mem_kv",
    )(mem2, w)


def _bucket_lower_bounds():
    half = REL_BUCKETS // 2
    max_exact = half // 2
    n = jnp.arange(REL_MAX_DIST + 1)
    log_part = (jnp.log(jnp.maximum(n, 1).astype(F32) / max_exact) / math.log(REL_MAX_DIST / max_exact)
                * (half - max_exact)).astype(jnp.int32)
    bucket = jnp.where(n < max_exact, n, jnp.minimum(max_exact + log_part, half - 1))
    bucket = jnp.arange(half, dtype=jnp.int32)[bucket]
    ids = jnp.arange(half, dtype=jnp.int32)
    return jnp.sum((bucket[None, :] < ids[:, None]).astype(jnp.int32), axis=1)


BIAS_ROWS = 64


def _bias_kernel(lb_ref, tab_ref, lo_ref, hi_ref, off_ref, o_ref, *, limit):
    h = pl.program_id(0)
    half = REL_BUCKETS // 2
    ncol = o_ref.shape[2]

    def body(r, carry):
        rows = pl.ds(pl.multiple_of(r * BIAS_ROWS, BIAS_ROWS), BIAS_ROWS)

        def general():
            rel = lax.broadcasted_iota(jnp.int32, (BIAS_ROWS, ncol), 1) + off_ref[rows, :]
            n = jnp.abs(rel)
            neg = jnp.full(rel.shape, tab_ref[0, h], F32)
            pos = jnp.full(rel.shape, tab_ref[half, h], F32)
            for b in range(1, half):
                reached = n >= lb_ref[b]
                neg = jnp.where(reached, tab_ref[b, h], neg)
                pos = jnp.where(reached, tab_ref[half + b, h], pos)
            out = jnp.where(rel > 0, pos, neg)
            if limit is not None:
                out = jnp.where(n > limit, NEG, out)
            o_ref[0, rows, :] = out

        if limit is not None:
            general()
            return carry
        far = lb_ref[half - 1]
        all_after = lo_ref[r] >= far
        all_before = -(hi_ref[r] + (ncol - 1)) >= far

        @pl.when(all_after)
        def _():
            o_ref[0, rows, :] = jnp.full((BIAS_ROWS, ncol), tab_ref[2 * half - 1, h], F32)

        @pl.when(all_before)
        def _():
            o_ref[0, rows, :] = jnp.full((BIAS_ROWS, ncol), tab_ref[half - 1, h], F32)

        pl.when(jnp.logical_not(jnp.logical_or(all_after, all_before)))(general)
        return carry
    lax.fori_loop(0, o_ref.shape[1] // BIAS_ROWS, body, 0)


def _bias_tiles(table, row_offset, ncol, limit=None):
    nh = table.shape[1]
    nrow = row_offset.shape[0]
    off = row_offset.astype(jnp.int32)
    chunks = off.reshape(nrow // BIAS_ROWS, BIAS_ROWS)
    smem = pl.BlockSpec(memory_space=pltpu.SMEM)
    return pl.pallas_call(
        functools.partial(_bias_kernel, limit=limit),
        grid=(nh,),
        in_specs=[smem, smem, smem, smem, pl.BlockSpec((nrow, 1), lambda h: (0, 0))],
        out_specs=pl.BlockSpec((1, nrow, ncol), lambda h: (h, 0, 0)),
        out_shape=jax.ShapeDtypeStruct((nh, nrow, ncol), F32),
        compiler_params=_params("parallel"),
        name="bias_tiles",
    )(_bucket_lower_bounds(), table.astype(F32), jnp.min(chunks, axis=1), jnp.max(chunks, axis=1),
      off.reshape(nrow, 1))


def _diff_bias_tiles(table, tq):
    r = jnp.arange(5 * tq)
    return _bias_tiles(table, (r // tq - 2) * tq - r % tq, tq).reshape(table.shape[1], 5, tq, tq)


def _win_bias_tiles(table, tq):
    return _bias_tiles(table, -WINDOW - jnp.arange(tq), tq + 2 * WINDOW, limit=WINDOW)


def _diff_kernel(q_ref, k_ref, v_ref, band_ref, lq1_ref, lk1_ref, lq2_ref, lk2_ref, g_ref, o_ref,
                 s_scr, mx_scr, *, tq, rc, n_tiles, tiles_per_head):
    t = pl.program_id(0)
    hd = v_ref.shape[2]

    @pl.when(t == 0)
    def _():
        s_scr[...] = jnp.zeros_like(s_scr)
        mx_scr[...] = jnp.zeros_like(mx_scr)

    def tick(slot):
        other = 1 - slot
        i = jnp.minimum(t, n_tiles - 1) % tiles_per_head
        nk = k_ref.shape[1] // tq
        q = (q_ref[0].astype(F32) * (DIFF_QK ** -0.5 * LOG2E)).astype(BF16)
        lane = lax.broadcasted_iota(jnp.int32, q.shape, 1)
        zero = jnp.zeros_like(q)
        qs = jnp.concatenate([jnp.where(lane < DIFF_QK, q, zero), jnp.where(lane >= DIFF_QK, q, zero)], axis=0)
        lanes = mx_scr.shape[-1]
        tb = band_ref.shape[2]
        nb = tq // tb
        for j in range(nk):
            s = _dot_nt(qs, k_ref[0, j * tq:(j + 1) * tq, :])
            for r0 in range(0, 2 * tq, tb):
                rows = slice(r0, r0 + tb)
                fold = None
                for cb in range(nb):
                    d = (j - i) * nb + cb - (r0 // tb) % nb
                    sb = s[rows, cb * tb:(cb + 1) * tb] + band_ref[0, jnp.clip(d, -2, 2) + 2]
                    s_scr[slot, rows, j * tq + cb * tb:j * tq + (cb + 1) * tb] = sb
                    fold = functools.reduce(jnp.maximum, [sb[:, c:c + lanes] for c in range(0, tb, lanes)]
                                            + ([] if fold is None else [fold]))
                mx_scr[slot, rows, :] = fold if j == 0 else jnp.maximum(mx_scr[slot, rows, :], fold)

        nchunk = 2 * tq // rc
        chunk = lambda r: slice(r * rc, (r + 1) * rc)
        p = jnp.concatenate(
            [jnp.exp2(s_scr[other, chunk(r), :] - jnp.max(mx_scr[other, chunk(r), :], axis=-1, keepdims=True))
             .astype(BF16) for r in range(nchunk)], axis=0)
        v_ones = jnp.concatenate([v_ref[0], jnp.ones(v_ref.shape[1:], BF16)], axis=1)
        pv = _dot(p, v_ones)
        lam = (jnp.exp(jnp.sum(lq1_ref[...] * lk1_ref[...], axis=-1, keepdims=True))
               - jnp.exp(jnp.sum(lq2_ref[...] * lk2_ref[...], axis=-1, keepdims=True)) + LAMBDA_INIT)
        o = pv[0:tq, 0:hd] * (1.0 / pv[0:tq, hd:]) - pv[tq:, 0:hd] * (lam / pv[tq:, hd:])
        ms = jnp.mean(o * o, axis=-1, keepdims=True)
        o_ref[0] = (o * lax.rsqrt(ms + LN_EPS) * g_ref[...] * (1.0 - LAMBDA_INIT)).astype(BF16)

    pl.when(t % 2 == 0)(functools.partial(tick, 0))
    pl.when(t % 2 == 1)(functools.partial(tick, 1))


def _diff_attn(proj, band, lq1, lk1, lq2, lk2, g, tq=256, rc=32):
    b, s, _ = proj.shape
    cb = HEAD_DIM
    ni = s // tq
    n_tiles = b * DIFF_HEADS * ni
    depth = 1

    def tile(t, lag):
        tc = jnp.clip(t - lag, 0, n_tiles - 1)
        return tc // (DIFF_HEADS * ni), (tc // ni) % DIFF_HEADS, tc % ni

    def q_map(t):
        bi, h, i = tile(t, 0)
        return bi, i, OFF_DQ // cb + h

    def k_map(t):
        bi, h, _ = tile(t, 0)
        return bi, 0, OFF_DK // cb + h

    def v_map(t):
        bi, h, _ = tile(t, depth)
        return bi, 0, OFF_DV // cb + h

    def o_map(t):
        bi, h, i = tile(t, depth)
        return bi, i, h

    vec = lambda n: pl.BlockSpec((1, n), lambda t: (0, 0))
    return pl.pallas_call(
        functools.partial(_diff_kernel, tq=tq, rc=rc, n_tiles=n_tiles, tiles_per_head=ni),
        grid=(n_tiles + depth,),
        in_specs=[pl.BlockSpec((1, tq, cb), q_map),
                  pl.BlockSpec((1, s, cb), k_map),
                  pl.BlockSpec((1, s, cb), v_map),
                  pl.BlockSpec((1,) + band.shape[1:], lambda t: (tile(t, 0)[1], 0, 0, 0)),
                  vec(DIFF_QK), vec(DIFF_QK), vec(DIFF_QK), vec(DIFF_QK), vec(HEAD_DIM)],
        out_specs=pl.BlockSpec((1, tq, cb), o_map),
        out_shape=jax.ShapeDtypeStruct((b, s, BRANCH_W), BF16),
        scratch_shapes=[pltpu.VMEM((2, 2 * tq, s), F32), pltpu.VMEM((2, 2 * tq, cb), F32)],
        compiler_params=_params("arbitrary"),
        name="diff_attn",
    )(proj, proj, proj, band, lq1, lk1, lq2, lk2, g)


def _win_kernel(sink_ref, q_ref, kp_ref, km_ref, kn_ref, vp_ref, vm_ref, vn_ref, bias_ref, o_ref, *, tq):
    g = pl.program_id(1)
    i = pl.program_id(2)
    first = i == 0
    last = i == pl.num_programs(2) - 1
    w = WINDOW
    hd = HEAD_DIM
    tb = bias_ref.shape[1]
    nkeys = tb + 2 * w
    keys = jnp.concatenate([kp_ref[0], km_ref[0], kn_ref[0]], axis=0)
    vals = jnp.concatenate([vp_ref[0], vm_ref[0], vn_ref[0]], axis=0)
    v_ones = jnp.concatenate([vals, jnp.ones((tq + 2 * w, hd), BF16)], axis=1)
    col = lax.broadcasted_iota(jnp.int32, (1, nkeys), 1)
    scale = jnp.asarray(hd ** -0.5 * LOG2E, F32)
    for r0 in range(0, tq, tb):
        rows = slice(r0, r0 + tb)
        band = slice(r0, r0 + nkeys)
        outside = None
        if r0 == 0:
            outside = jnp.logical_and(first, col < w)
        if r0 + tb == tq:
            past_end = jnp.logical_and(last, col >= tb + w)
            outside = past_end if outside is None else jnp.logical_or(outside, past_end)
        for j in range(WIN_GROUP):
            cols = slice(j * hd, (j + 1) * hd)
            s = _dot_nt(q_ref[0, rows, cols], keys[band]) * scale + bias_ref[j]
            if outside is not None:
                s = jnp.where(outside, NEG, s)
            sink = sink_ref[g * WIN_GROUP + j]
            fold = functools.reduce(jnp.maximum, [s[:, c:c + w] for c in range(0, nkeys, w)])
            m = jnp.maximum(jnp.max(fold, axis=-1, keepdims=True), sink)
            pv = _dot(jnp.exp2(s - m).astype(BF16), v_ones[band])
            o_ref[0, rows, cols] = (pv[:, 0:hd] / (pv[:, hd:] + jnp.exp2(sink - m))).astype(BF16)


def _win_attn(proj, bias, sink, tq=256):
    b, s, _ = proj.shape
    hd = HEAD_DIM
    nb = s // WINDOW
    r = tq // WINDOW
    prev = lambda bi, g, i: jnp.maximum(i * r - 1, 0)
    nxt = lambda bi, g, i: jnp.minimum((i + 1) * r, nb - 1)
    kcol, vcol = OFF_WK // hd, OFF_WV // hd
    return pl.pallas_call(
        functools.partial(_win_kernel, tq=tq),
        grid=(b, WIN_KV_HEADS, s // tq),
        in_specs=[pl.BlockSpec(memory_space=pltpu.SMEM),
                  pl.BlockSpec((1, tq, WIN_GROUP * hd), lambda bi, g, i: (bi, i, OFF_WQ // (WIN_GROUP * hd) + g)),
                  pl.BlockSpec((1, WINDOW, hd), lambda bi, g, i: (bi, prev(bi, g, i), kcol + g)),
                  pl.BlockSpec((1, tq, hd), lambda bi, g, i: (bi, i, kcol + g)),
                  pl.BlockSpec((1, WINDOW, hd), lambda bi, g, i: (bi, nxt(bi, g, i), kcol + g)),
                  pl.BlockSpec((1, WINDOW, hd), lambda bi, g, i: (bi, prev(bi, g, i), vcol + g)),
                  pl.BlockSpec((1, tq, hd), lambda bi, g, i: (bi, i, vcol + g)),
                  pl.BlockSpec((1, WINDOW, hd), lambda bi, g, i: (bi, nxt(bi, g, i), vcol + g)),
                  pl.BlockSpec((WIN_GROUP,) + bias.shape[1:], lambda bi, g, i: (g, 0, 0))],
        out_specs=pl.BlockSpec((1, tq, WIN_GROUP * hd), lambda bi, g, i: (bi, i, g)),
        out_shape=jax.ShapeDtypeStruct((b, s, BRANCH_W), BF16),
        compiler_params=_params("parallel", "parallel", "arbitrary"),
        name="win_attn",
    )(sink, proj, proj, proj, proj, proj, proj, proj, bias)


def _mem_attn_kernel(q_ref, k_ref, v_ref, o_ref):
    s = _dot_nt(q_ref[0], k_ref[0]) * jnp.asarray(MEM_DIM ** -0.5, F32)
    p = jnp.exp(s - jnp.max(s, axis=-1, keepdims=True))
    p = p * (1.0 / jnp.sum(p, axis=-1, keepdims=True))
    o_ref[0] = _dot(p.astype(BF16), v_ref[0]).astype(BF16)


def _mem_attn(proj, memkv, tq=2048):
    b, s, _ = proj.shape
    md = MEM_DIM
    return pl.pallas_call(
        _mem_attn_kernel,
        grid=(b, MEM_HEADS, s // tq),
        in_specs=[pl.BlockSpec((1, tq, md), lambda bi, h, i: (bi, i, OFF_MQ // md + h)),
                  pl.BlockSpec((1, N_MEM, md), lambda bi, h, i: (bi, 0, h)),
                  pl.BlockSpec((1, N_MEM, md), lambda bi, h, i: (bi, 0, MEM_HEADS + h))],
        out_specs=pl.BlockSpec((1, tq, md), lambda bi, h, i: (bi, i, h)),
        out_shape=jax.ShapeDtypeStruct((b, s, BRANCH_W), BF16),
        compiler_params=_params("parallel", "parallel", "arbitrary"),
        name="mem_attn",
    )(proj, memkv, memkv)


def _mix_kernel(x_ref, lg_ref, lb_ref, a_ref, b_ref, c_ref, g0_ref, g1_ref, g2_ref, wb_ref, wo_ref,
                ln_g_ref, ln_b_ref, o_ref, acc_scr):
    n = pl.program_id(1)
    nrow = x_ref.shape[0] // LN_ROWS

    @pl.when(n == 0)
    def _():
        acc_scr[...] = jnp.zeros_like(acc_scr)

    mixed = None
    for k, (br_ref, gate_ref) in enumerate(((a_ref, g0_ref), (b_ref, g1_ref), (c_ref, g2_ref))):
        term = gate_ref[...].astype(F32) * _dot(br_ref[...], wb_ref[k])
        mixed = term if mixed is None else mixed + term
    acc_scr[...] += _dot(mixed.astype(BF16), wo_ref[...])

    @pl.when(n == pl.num_programs(1) - 1)
    def _():
        def body(r, c):
            rows = pl.ds(pl.multiple_of(r * LN_ROWS, LN_ROWS), LN_ROWS)
            hh = _ln_rows(x_ref[rows, :], lg_ref[...], lb_ref[...])
            o_ref[rows, :] = _ln_rows(ALPHA * hh + acc_scr[rows, :], ln_g_ref[...], ln_b_ref[...])
            return c
        lax.fori_loop(0, nrow, body, 0)


def _mix(x2, lg, lb, a, b, c, gates, wb, wo, g1, b1, tm=512, tn=512):
    m, d = x2.shape
    bw = a.shape[1]
    nn = d // tn
    row = lambda w: pl.BlockSpec((tm, w), lambda i, n: (i, 0))
    vec = pl.BlockSpec((1, d), lambda i, n: (0, 0))
    gts = [pl.BlockSpec((tm, tn), functools.partial(lambda i, n, k: (i, k * nn + n), k=k)) for k in range(3)]
    return pl.pallas_call(
        _mix_kernel,
        grid=(m // tm, nn),
        in_specs=[row(d), vec, vec, row(bw), row(bw), row(bw), *gts,
                  pl.BlockSpec((3, bw, tn), lambda i, n: (0, 0, n)),
                  pl.BlockSpec((tn, d), lambda i, n: (n, 0)), vec, vec],
        out_specs=pl.BlockSpec((tm, d), lambda i, n: (i, 0)),
        out_shape=jax.ShapeDtypeStruct((m, d), F32),
        scratch_shapes=[pltpu.VMEM((tm, d), F32)],
        compiler_params=_params("parallel", "arbitrary"),
        name="mix",
    )(x2, lg, lb, a, b, c, gates, gates, gates, wb, wo, g1, b1)


HALO = 16


def _gelu_tanh(x):
    return 0.5 * x * (1.0 + jnp.tanh(math.sqrt(2.0 / math.pi) * (x + 0.044715 * (x * x * x))))


def _ffn_kernel(h_ref, hp_ref, hn_ref, wv_ref, wg_ref, cwv_ref, cwg_ref, cbv_ref, cbg_ref, wd_ref,
                g2_ref, b2_ref, o_ref, hx_scr, acc_scr, *, tiles_per_seq):
    i = pl.program_id(0)
    f = pl.program_id(1)
    tm = h_ref.shape[0]
    nrow = tm // LN_ROWS

    @pl.when(f == 0)
    def _():
        seq_first = (i % tiles_per_seq) == 0
        seq_last = (i % tiles_per_seq) == tiles_per_seq - 1
        hx_scr[0:HALO, :] = jnp.where(seq_first, 0.0, hp_ref[...]).astype(BF16)
        hx_scr[HALO:HALO + tm, :] = h_ref[...].astype(BF16)
        hx_scr[HALO + tm:, :] = jnp.where(seq_last, 0.0, hn_ref[...]).astype(BF16)
        acc_scr[...] = jnp.zeros_like(acc_scr)

    hx = hx_scr[...]
    ext = tm + 2 * HALO

    def conv(u, cw_ref, cb_ref):
        prev = pltpu.roll(u, 1, 0)[HALO:HALO + tm]
        nxt = pltpu.roll(u, ext - 1, 0)[HALO:HALO + tm]
        return prev * cw_ref[0:1, :] + u[HALO:HALO + tm] * cw_ref[1:2, :] + nxt * cw_ref[2:3, :] + cb_ref[...]

    gelu_gate = _gelu_tanh(conv(_dot(hx, wg_ref[...]), cwg_ref, cbg_ref))
    val = conv(_dot(hx, wv_ref[...]), cwv_ref, cbv_ref)
    acc_scr[...] += _dot((gelu_gate * val).astype(BF16), wd_ref[...])

    @pl.when(f == pl.num_programs(1) - 1)
    def _():
        def body(r, c):
            rows = pl.ds(pl.multiple_of(r * LN_ROWS, LN_ROWS), LN_ROWS)
            o_ref[rows, :] = _ln_rows(ALPHA * h_ref[rows, :] + acc_scr[rows, :], g2_ref[...], b2_ref[...])
            return c
        lax.fori_loop(0, nrow, body, 0)


def _ffn(h1, w_up, cw, cb, w_down, g2, b2, seq, tm=512, tf=512):
    m, d = h1.shape
    nf = D_FF_PAD // tf
    hb = tm // HALO
    nhb = m // HALO
    vec = pl.BlockSpec((1, d), lambda i, f: (0, 0))
    return pl.pallas_call(
        functools.partial(_ffn_kernel, tiles_per_seq=seq // tm),
        grid=(m // tm, nf),
        in_specs=[pl.BlockSpec((tm, d), lambda i, f: (i, 0)),
                  pl.BlockSpec((HALO, d), lambda i, f: (jnp.maximum(i * hb - 1, 0), 0)),
                  pl.BlockSpec((HALO, d), lambda i, f: (jnp.minimum((i + 1) * hb, nhb - 1), 0)),
                  pl.BlockSpec((d, tf), lambda i, f: (0, f)),
                  pl.BlockSpec((d, tf), lambda i, f: (0, nf + f)),
                  pl.BlockSpec((3, tf), lambda i, f: (0, f)),
                  pl.BlockSpec((3, tf), lambda i, f: (0, nf + f)),
                  pl.BlockSpec((1, tf), lambda i, f: (0, f)),
                  pl.BlockSpec((1, tf), lambda i, f: (0, nf + f)),
                  pl.BlockSpec((tf, d), lambda i, f: (f, 0)), vec, vec],
        out_specs=pl.BlockSpec((tm, d), lambda i, f: (i, 0)),
        out_shape=jax.ShapeDtypeStruct((m, d), F32),
        scratch_shapes=[pltpu.VMEM((tm + 2 * HALO, d), BF16), pltpu.VMEM((tm, d), F32)],
        compiler_params=_params("parallel", "arbitrary"),
        name="ffn",
    )(h1, h1, h1, w_up, w_up, cw, cw, cb, cb, w_down, g2, b2)


def _w_up_prep_kernel(w_ref, o_ref):
    o_ref[:, 0:D_FF] = w_ref[...].astype(BF16)
    o_ref[:, D_FF:] = jnp.zeros((o_ref.shape[0], D_FF_PAD - D_FF), BF16)


def _w_up_prep(w_up, tr=256):
    d = w_up.shape[0]
    return pl.pallas_call(
        _w_up_prep_kernel,
        grid=(2, d // tr),
        in_specs=[pl.BlockSpec((tr, D_FF), lambda half, r: (r, half))],
        out_specs=pl.BlockSpec((tr, D_FF_PAD), lambda half, r: (r, half)),
        out_shape=jax.ShapeDtypeStruct((d, 2 * D_FF_PAD), BF16),
        compiler_params=_params("parallel", "parallel"),
        name="w_up_prep",
    )(w_up)


def _w_down_prep_kernel(w_ref, o_ref):
    o_ref[0:D_FF, :] = w_ref[...].astype(BF16)
    o_ref[D_FF:, :] = jnp.zeros((D_FF_PAD - D_FF, o_ref.shape[1]), BF16)


def _w_down_prep(w_down, tc=256):
    d = w_down.shape[1]
    return pl.pallas_call(
        _w_down_prep_kernel,
        grid=(d // tc,),
        in_specs=[pl.BlockSpec((D_FF, tc), lambda c: (0, c))],
        out_specs=pl.BlockSpec((D_FF_PAD, tc), lambda c: (0, c)),
        out_shape=jax.ShapeDtypeStruct((D_FF_PAD, d), BF16),
        compiler_params=_params("parallel"),
        name="w_down_prep",
    )(w_down)


def _pad_ff(t, axis):
    val, gate = jnp.split(t, 2, axis=axis)
    pad = [(0, 0)] * t.ndim
    pad[axis] = (0, D_FF_PAD - D_FF)
    return jnp.concatenate([jnp.pad(val, pad), jnp.pad(gate, pad)], axis=axis)


def kernel(x, mem, ln_in_g, ln_in_b, rel_table, w_in, w_mem_kv, diff_lq1, diff_lk1, diff_lq2, diff_lk2,
           diff_subln_g, win_sink, w_gate, b_gate, w_branch, w_o, ln1_g, ln1_b, w_up, conv_w, conv_b,
           w_down, ln2_g, ln2_b):
    assert w_in.shape[0] == DEPTH == 1
    bsz, seq, d = x.shape
    x2 = x.reshape(bsz * seq, d)
    row = lambda v: v.reshape(1, -1).astype(F32)
    l = 0
    hb = _ln(x2, row(ln_in_g), row(ln_in_b))
    proj = _proj(hb, w_in[l])
    proj = proj.reshape(bsz, seq, IN_W)
    memkv = _mem_kv(mem.reshape(bsz * N_MEM, d), w_mem_kv[l]).reshape(bsz, N_MEM, 2 * BRANCH_W)

    tq_d, tb_d, tq_w, tb_w = 512, 256, 512, 256
    a = _diff_attn(proj, _diff_bias_tiles(rel_table[:, :DIFF_HEADS] * LOG2E, tb_d), row(diff_lq1[l]), row(diff_lk1[l]),
                   row(diff_lq2[l]), row(diff_lk2[l]), row(diff_subln_g[l]), tq=tq_d)
    b = _win_attn(proj, _win_bias_tiles(rel_table[:, DIFF_HEADS:] * LOG2E, tb_w), win_sink[l].astype(F32) * LOG2E,
                  tq=tq_w)
    c = _mem_attn(proj, memkv)

    m = bsz * seq
    gates = _gates(hb, w_gate[l], row(b_gate[l]))
    h1 = _mix(x2, row(ln_in_g), row(ln_in_b), a.reshape(m, -1), b.reshape(m, -1), c.reshape(m, -1), gates,
              w_branch[l].astype(BF16), w_o[l].astype(BF16), row(ln1_g[l]), row(ln1_b[l]))

    w_up_p = _w_up_prep(w_up[l].astype(F32))
    cw_p = _pad_ff(conv_w[l].astype(F32), 1)
    cb_p = _pad_ff(row(conv_b[l]), 1)
    w_down_p = _w_down_prep(w_down[l].astype(F32))
    out = _ffn(h1, w_up_p, cw_p, cb_p, w_down_p, row(ln2_g[l]), row(ln2_b[l]), seq)
    return out.reshape(bsz, seq, d)
```

```python
import functools
import math

import jax
import jax.numpy as jnp
from jax import lax
from jax.experimental import pallas as pl
from jax.experimental.pallas import tpu as pltpu

F32 = jnp.float32
BF16 = jnp.bfloat16

D_MODEL = 2048
SEQ = 2048
N_MEM = 256
HEAD_DIM = 128
BRANCH_W = 1024
DIFF_HEADS = 8
DIFF_QK = 64
WIN_HEADS = 8
WIN_KV_HEADS = 2
WIN_GROUP = WIN_HEADS // WIN_KV_HEADS
WINDOW = 128
MEM_HEADS = 4
MEM_DIM = 256
OFF_DQ, OFF_DK, OFF_DV, OFF_WQ, OFF_WK, OFF_WV, OFF_MQ = 0, 1024, 2048, 3072, 4096, 4352, 4608
IN_W = 5632
D_FF = 5504
D_FF_PAD = 5632
REL_BUCKETS = 32
REL_MAX_DIST = 128
DEPTH = 1
ALPHA = (2 * DEPTH) ** 0.25
LN_EPS = 1e-5
NEG = -1e30
LOG2E = math.log2(math.e)
LAMBDA_INIT = 0.8 - 0.6 * math.exp(-0.3 * 0)

VMEM_LIMIT = 56 * 1024 * 1024
LN_ROWS = 128


def _ln_rows(x, g, b):
    mu = jnp.mean(x, axis=-1, keepdims=True)
    xc = x - mu
    var = jnp.mean(xc * xc, axis=-1, keepdims=True)
    return xc * lax.rsqrt(var + LN_EPS) * g + b


def _dot(a, b):
    return jnp.dot(a, b, preferred_element_type=F32)


def _dot_nt(a, b):
    return lax.dot_general(a, b, (((1,), (1,)), ((), ())), preferred_element_type=F32)


def _params(*sem):
    return pltpu.CompilerParams(dimension_semantics=sem, vmem_limit_bytes=VMEM_LIMIT)


def _ln_kernel(x_ref, g_ref, b_ref, h_ref):
    def body(r, c):
        rows = pl.ds(pl.multiple_of(r * LN_ROWS, LN_ROWS), LN_ROWS)
        h_ref[rows, :] = _ln_rows(x_ref[rows, :], g_ref[...], b_ref[...]).astype(BF16)
        return c
    lax.fori_loop(0, x_ref.shape[0] // LN_ROWS, body, 0)


def _ln(x2, g, b, tm=512):
    m, d = x2.shape
    vec = pl.BlockSpec((1, d), lambda i: (0, 0))
    return pl.pallas_call(
        _ln_kernel,
        grid=(m // tm,),
        in_specs=[pl.BlockSpec((tm, d), lambda i: (i, 0)), vec, vec],
        out_specs=pl.BlockSpec((tm, d), lambda i: (i, 0)),
        out_shape=jax.ShapeDtypeStruct((m, d), BF16),
        compiler_params=_params("parallel"),
        name="ln_in",
    )(x2, g, b)


def _proj_kernel(h_ref, w_ref, o_ref):
    o_ref[...] = _dot(h_ref[...], w_ref[...].astype(BF16)).astype(BF16)


def _proj(hb, w, tm=2048, tn=512):
    m, d = hb.shape
    n = w.shape[1]
    return pl.pallas_call(
        _proj_kernel,
        grid=(m // tm, n // tn),
        in_specs=[pl.BlockSpec((tm, d), lambda i, j: (i, 0)),
                  pl.BlockSpec((d, tn), lambda i, j: (0, j))],
        out_specs=pl.BlockSpec((tm, tn), lambda i, j: (i, j)),
        out_shape=jax.ShapeDtypeStruct((m, n), BF16),
        compiler_params=_params("parallel", "arbitrary"),
        name="proj",
    )(hb, w)


def _gates_kernel(h_ref, w_ref, b_ref, o_ref):
    z = _dot(h_ref[...], w_ref[...].astype(BF16)) + b_ref[...]
    o_ref[...] = (0.5 * jnp.tanh(0.5 * z) + 0.5).astype(BF16)


def _gates(hb, w, b, tm=2048, tn=512):
    m, d = hb.shape
    n = w.shape[1]
    return pl.pallas_call(
        _gates_kernel,
        grid=(m // tm, n // tn),
        in_specs=[pl.BlockSpec((tm, d), lambda i, j: (i, 0)),
                  pl.BlockSpec((d, tn), lambda i, j: (0, j)),
                  pl.BlockSpec((1, tn), lambda i, j: (0, j))],
        out_specs=pl.BlockSpec((tm, tn), lambda i, j: (i, j)),
        out_shape=jax.ShapeDtypeStruct((m, n), BF16),
        compiler_params=_params("parallel", "arbitrary"),
        name="gates",
    )(hb, w, b)


def _mem_kv_kernel(m_ref, w_ref, o_ref):
    o_ref[...] = _dot(m_ref[...].astype(BF16), w_ref[...].astype(BF16)).astype(BF16)


def _mem_kv(mem2, w, tn=512):
    m, d = mem2.shape
    n = w.shape[1]
    return pl.pallas_call(
        _mem_kv_kernel,
        grid=(n // tn,),
        in_specs=[pl.BlockSpec((m, d), lambda j: (0, 0)),
                  pl.BlockSpec((d, tn), lambda j: (0, j))],
        out_specs=pl.BlockSpec((m, tn), lambda j: (0, j)),
        out_shape=jax.ShapeDtypeStruct((m, n), BF16),
        compiler_params=_params("parallel"),
        name="mem_kv",
    )(mem2, w)


def _bucket_lower_bounds():
    half = REL_BUCKETS // 2
    max_exact = half // 2
    n = jnp.arange(REL_MAX_DIST + 1)
    log_part = (jnp.log(jnp.maximum(n, 1).astype(F32) / max_exact) / math.log(REL_MAX_DIST / max_exact)
                * (half - max_exact)).astype(jnp.int32)
    bucket = jnp.where(n < max_exact, n, jnp.minimum(max_exact + log_part, half - 1))
    bucket = jnp.arange(half, dtype=jnp.int32)[bucket]
    ids = jnp.arange(half, dtype=jnp.int32)
    return jnp.sum((bucket[None, :] < ids[:, None]).astype(jnp.int32), axis=1)


BIAS_ROWS = 64


def _bias_kernel(lb_ref, tab_ref, lo_ref, hi_ref, off_ref, o_ref, *, limit):
    h = pl.program_id(0)
    half = REL_BUCKETS // 2
    ncol = o_ref.shape[2]

    def body(r, carry):
        rows = pl.ds(pl.multiple_of(r * BIAS_ROWS, BIAS_ROWS), BIAS_ROWS)

        def general():
            rel = lax.broadcasted_iota(jnp.int32, (BIAS_ROWS, ncol), 1) + off_ref[rows, :]
            n = jnp.abs(rel)
            neg = jnp.full(rel.shape, tab_ref[0, h], F32)
            pos = jnp.full(rel.shape, tab_ref[half, h], F32)
            for b in range(1, half):
                reached = n >= lb_ref[b]
                neg = jnp.where(reached, tab_ref[b, h], neg)
                pos = jnp.where(reached, tab_ref[half + b, h], pos)
            out = jnp.where(rel > 0, pos, neg)
            if limit is not None:
                out = jnp.where(n > limit, NEG, out)
            o_ref[0, rows, :] = out

        if limit is not None:
            general()
            return carry
        far = lb_ref[half - 1]
        all_after = lo_ref[r] >= far
        all_before = -(hi_ref[r] + (ncol - 1)) >= far

        @pl.when(all_after)
        def _():
            o_ref[0, rows, :] = jnp.full((BIAS_ROWS, ncol), tab_ref[2 * half - 1, h], F32)

        @pl.when(all_before)
        def _():
            o_ref[0, rows, :] = jnp.full((BIAS_ROWS, ncol), tab_ref[half - 1, h], F32)

        pl.when(jnp.logical_not(jnp.logical_or(all_after, all_before)))(general)
        return carry
    lax.fori_loop(0, o_ref.shape[1] // BIAS_ROWS, body, 0)


def _bias_tiles(table, row_offset, ncol, limit=None):
    nh = table.shape[1]
    nrow = row_offset.shape[0]
    off = row_offset.astype(jnp.int32)
    chunks = off.reshape(nrow // BIAS_ROWS, BIAS_ROWS)
    smem = pl.BlockSpec(memory_space=pltpu.SMEM)
    return pl.pallas_call(
        functools.partial(_bias_kernel, limit=limit),
        grid=(nh,),
        in_specs=[smem, smem, smem, smem, pl.BlockSpec((nrow, 1), lambda h: (0, 0))],
        out_specs=pl.BlockSpec((1, nrow, ncol), lambda h: (h, 0, 0)),
        out_shape=jax.ShapeDtypeStruct((nh, nrow, ncol), F32),
        compiler_params=_params("parallel"),
        name="bias_tiles",
    )(_bucket_lower_bounds(), table.astype(F32), jnp.min(chunks, axis=1), jnp.max(chunks, axis=1),
      off.reshape(nrow, 1))


def _diff_bias_tiles(table, tq):
    r = jnp.arange(5 * tq)
    return _bias_tiles(table, (r // tq - 2) * tq - r % tq, tq).reshape(table.shape[1], 5, tq, tq)


def _win_bias_tiles(table, tq):
    return _bias_tiles(table, -WINDOW - jnp.arange(tq), tq + 2 * WINDOW, limit=WINDOW)


def _diff_kernel(q_ref, k_ref, v_ref, band_ref, lq1_ref, lk1_ref, lq2_ref, lk2_ref, g_ref, o_ref,
                 s_scr, mx_scr, *, tq, rc, n_tiles, tiles_per_head):
    t = pl.program_id(0)
    hd = v_ref.shape[2]

    @pl.when(t == 0)
    def _():
        s_scr[...] = jnp.zeros_like(s_scr)
        mx_scr[...] = jnp.zeros_like(mx_scr)

    def tick(slot):
        other = 1 - slot
        i = jnp.minimum(t, n_tiles - 1) % tiles_per_head
        nk = k_ref.shape[1] // tq
        q = (q_ref[0].astype(F32) * (DIFF_QK ** -0.5 * LOG2E)).astype(BF16)
        lane = lax.broadcasted_iota(jnp.int32, q.shape, 1)
        zero = jnp.zeros_like(q)
        qs = jnp.concatenate([jnp.where(lane < DIFF_QK, q, zero), jnp.where(lane >= DIFF_QK, q, zero)], axis=0)
        lanes = mx_scr.shape[-1]
        tb = band_ref.shape[2]
        nb = tq // tb
        for j in range(nk):
            s = _dot_nt(qs, k_ref[0, j * tq:(j + 1) * tq, :])
            for r0 in range(0, 2 * tq, tb):
                rows = slice(r0, r0 + tb)
                fold = None
                for cb in range(nb):
                    d = (j - i) * nb + cb - (r0 // tb) % nb
                    sb = s[rows, cb * tb:(cb + 1) * tb] + band_ref[0, jnp.clip(d, -2, 2) + 2]
                    s_scr[slot, rows, j * tq + cb * tb:j * tq + (cb + 1) * tb] = sb
                    fold = functools.reduce(jnp.maximum, [sb[:, c:c + lanes] for c in range(0, tb, lanes)]
                                            + ([] if fold is None else [fold]))
                mx_scr[slot, rows, :] = fold if j == 0 else jnp.maximum(mx_scr[slot, rows, :], fold)

        nchunk = 2 * tq // rc
        chunk = lambda r: slice(r * rc, (r + 1) * rc)
        p = jnp.concatenate(
            [jnp.exp2(s_scr[other, chunk(r), :] - jnp.max(mx_scr[other, chunk(r), :], axis=-1, keepdims=True))
             .astype(BF16) for r in range(nchunk)], axis=0)
        v_ones = jnp.concatenate([v_ref[0], jnp.ones(v_ref.shape[1:], BF16)], axis=1)
        pv = _dot(p, v_ones)
        lam = (jnp.exp(jnp.sum(lq1_ref[...] * lk1_ref[...], axis=-1, keepdims=True))
               - jnp.exp(jnp.sum(lq2_ref[...] * lk2_ref[...], axis=-1, keepdims=True)) + LAMBDA_INIT)
        o = pv[0:tq, 0:hd] * (1.0 / pv[0:tq, hd:]) - pv[tq:, 0:hd] * (lam / pv[tq:, hd:])
        ms = jnp.mean(o * o, axis=-1, keepdims=True)
        o_ref[0] = (o * lax.rsqrt(ms + LN_EPS) * g_ref[...] * (1.0 - LAMBDA_INIT)).astype(BF16)

    pl.when(t % 2 == 0)(functools.partial(tick, 0))
    pl.when(t % 2 == 1)(functools.partial(tick, 1))


def _diff_attn(proj, band, lq1, lk1, lq2, lk2, g, tq=256, rc=32):
    b, s, _ = proj.shape
    cb = HEAD_DIM
    ni = s // tq
    n_tiles = b * DIFF_HEADS * ni
    depth = 1

    def tile(t, lag):
        tc = jnp.clip(t - lag, 0, n_tiles - 1)
        return tc // (DIFF_HEADS * ni), (tc // ni) % DIFF_HEADS, tc % ni

    def q_map(t):
        bi, h, i = tile(t, 0)
        return bi, i, OFF_DQ // cb + h

    def k_map(t):
        bi, h, _ = tile(t, 0)
        return bi, 0, OFF_DK // cb + h

    def v_map(t):
        bi, h, _ = tile(t, depth)
        return bi, 0, OFF_DV // cb + h

    def o_map(t):
        bi, h, i = tile(t, depth)
        return bi, i, h

    vec = lambda n: pl.BlockSpec((1, n), lambda t: (0, 0))
    return pl.pallas_call(
        functools.partial(_diff_kernel, tq=tq, rc=rc, n_tiles=n_tiles, tiles_per_head=ni),
        grid=(n_tiles + depth,),
        in_specs=[pl.BlockSpec((1, tq, cb), q_map),
                  pl.BlockSpec((1, s, cb), k_map),
                  pl.BlockSpec((1, s, cb), v_map),
                  pl.BlockSpec((1,) + band.shape[1:], lambda t: (tile(t, 0)[1], 0, 0, 0)),
                  vec(DIFF_QK), vec(DIFF_QK), vec(DIFF_QK), vec(DIFF_QK), vec(HEAD_DIM)],
        out_specs=pl.BlockSpec((1, tq, cb), o_map),
        out_shape=jax.ShapeDtypeStruct((b, s, BRANCH_W), BF16),
        scratch_shapes=[pltpu.VMEM((2, 2 * tq, s), F32), pltpu.VMEM((2, 2 * tq, cb), F32)],
        compiler_params=_params("arbitrary"),
        name="diff_attn",
    )(proj, proj, proj, band, lq1, lk1, lq2, lk2, g)


def _win_kernel(sink_ref, q_ref, kp_ref, km_ref, kn_ref, vp_ref, vm_ref, vn_ref, bias_ref, o_ref, *, tq):
    g = pl.program_id(1)
    i = pl.program_id(2)
    first = i == 0
    last = i == pl.num_programs(2) - 1
    w = WINDOW
    hd = HEAD_DIM
    tb = bias_ref.shape[1]
    nkeys = tb + 2 * w
    keys = jnp.concatenate([kp_ref[0], km_ref[0], kn_ref[0]], axis=0)
    vals = jnp.concatenate([vp_ref[0], vm_ref[0], vn_ref[0]], axis=0)
    v_ones = jnp.concatenate([vals, jnp.ones((tq + 2 * w, hd), BF16)], axis=1)
    col = lax.broadcasted_iota(jnp.int32, (1, nkeys), 1)
    scale = jnp.asarray(hd ** -0.5 * LOG2E, F32)
    for r0 in range(0, tq, tb):
        rows = slice(r0, r0 + tb)
        band = slice(r0, r0 + nkeys)
        outside = None
        if r0 == 0:
            outside = jnp.logical_and(first, col < w)
        if r0 + tb == tq:
            past_end = jnp.logical_and(last, col >= tb + w)
            outside = past_end if outside is None else jnp.logical_or(outside, past_end)
        for j in range(WIN_GROUP):
            cols = slice(j * hd, (j + 1) * hd)
            s = _dot_nt(q_ref[0, rows, cols], keys[band]) * scale + bias_ref[j]
            if outside is not None:
                s = jnp.where(outside, NEG, s)
            sink = sink_ref[g * WIN_GROUP + j]
            fold = functools.reduce(jnp.maximum, [s[:, c:c + w] for c in range(0, nkeys, w)])
            m = jnp.maximum(jnp.max(fold, axis=-1, keepdims=True), sink)
            pv = _dot(jnp.exp2(s - m).astype(BF16), v_ones[band])
            o_ref[0, rows, cols] = (pv[:, 0:hd] / (pv[:, hd:] + jnp.exp2(sink - m))).astype(BF16)


def _win_attn(proj, bias, sink, tq=256):
    b, s, _ = proj.shape
    hd = HEAD_DIM
    nb = s // WINDOW
    r = tq // WINDOW
    prev = lambda bi, g, i: jnp.maximum(i * r - 1, 0)
    nxt = lambda bi, g, i: jnp.minimum((i + 1) * r, nb - 1)
    kcol, vcol = OFF_WK // hd, OFF_WV // hd
    return pl.pallas_call(
        functools.partial(_win_kernel, tq=tq),
        grid=(b, WIN_KV_HEADS, s // tq),
        in_specs=[pl.BlockSpec(memory_space=pltpu.SMEM),
                  pl.BlockSpec((1, tq, WIN_GROUP * hd), lambda bi, g, i: (bi, i, OFF_WQ // (WIN_GROUP * hd) + g)),
                  pl.BlockSpec((1, WINDOW, hd), lambda bi, g, i: (bi, prev(bi, g, i), kcol + g)),
                  pl.BlockSpec((1, tq, hd), lambda bi, g, i: (bi, i, kcol + g)),
                  pl.BlockSpec((1, WINDOW, hd), lambda bi, g, i: (bi, nxt(bi, g, i), kcol + g)),
                  pl.BlockSpec((1, WINDOW, hd), lambda bi, g, i: (bi, prev(bi, g, i), vcol + g)),
                  pl.BlockSpec((1, tq, hd), lambda bi, g, i: (bi, i, vcol + g)),
                  pl.BlockSpec((1, WINDOW, hd), lambda bi, g, i: (bi, nxt(bi, g, i), vcol + g)),
                  pl.BlockSpec((WIN_GROUP,) + bias.shape[1:], lambda bi, g, i: (g, 0, 0))],
        out_specs=pl.BlockSpec((1, tq, WIN_GROUP * hd), lambda bi, g, i: (bi, i, g)),
        out_shape=jax.ShapeDtypeStruct((b, s, BRANCH_W), BF16),
        compiler_params=_params("parallel", "parallel", "arbitrary"),
        name="win_attn",
    )(sink, proj, proj, proj, proj, proj, proj, proj, bias)


def _mem_attn_kernel(q_ref, k_ref, v_ref, o_ref):
    s = _dot_nt(q_ref[0], k_ref[0]) * jnp.asarray(MEM_DIM ** -0.5, F32)
    p = jnp.exp(s - jnp.max(s, axis=-1, keepdims=True))
    p = p * (1.0 / jnp.sum(p, axis=-1, keepdims=True))
    o_ref[0] = _dot(p.astype(BF16), v_ref[0]).astype(BF16)


def _mem_attn(proj, memkv, tq=2048):
    b, s, _ = proj.shape
    md = MEM_DIM
    return pl.pallas_call(
        _mem_attn_kernel,
        grid=(b, MEM_HEADS, s // tq),
        in_specs=[pl.BlockSpec((1, tq, md), lambda bi, h, i: (bi, i, OFF_MQ // md + h)),
                  pl.BlockSpec((1, N_MEM, md), lambda bi, h, i: (bi, 0, h)),
                  pl.BlockSpec((1, N_MEM, md), lambda bi, h, i: (bi, 0, MEM_HEADS + h))],
        out_specs=pl.BlockSpec((1, tq, md), lambda bi, h, i: (bi, i, h)),
        out_shape=jax.ShapeDtypeStruct((b, s, BRANCH_W), BF16),
        compiler_params=_params("parallel", "parallel", "arbitrary"),
        name="mem_attn",
    )(proj, memkv, memkv)


def _mix_kernel(x_ref, lg_ref, lb_ref, a_ref, b_ref, c_ref, g0_ref, g1_ref, g2_ref, wb_ref, wo_ref,
                ln_g_ref, ln_b_ref, o_ref, acc_scr):
    n = pl.program_id(1)
    nrow = x_ref.shape[0] // LN_ROWS

    @pl.when(n == 0)
    def _():
        acc_scr[...] = jnp.zeros_like(acc_scr)

    mixed = None
    for k, (br_ref, gate_ref) in enumerate(((a_ref, g0_ref), (b_ref, g1_ref), (c_ref, g2_ref))):
        term = gate_ref[...].astype(F32) * _dot(br_ref[...], wb_ref[k])
        mixed = term if mixed is None else mixed + term
    acc_scr[...] += _dot(mixed.astype(BF16), wo_ref[...])

    @pl.when(n == pl.num_programs(1) - 1)
    def _():
        def body(r, c):
            rows = pl.ds(pl.multiple_of(r * LN_ROWS, LN_ROWS), LN_ROWS)
            hh = _ln_rows(x_ref[rows, :], lg_ref[...], lb_ref[...])
            o_ref[rows, :] = _ln_rows(ALPHA * hh + acc_scr[rows, :], ln_g_ref[...], ln_b_ref[...])
            return c
        lax.fori_loop(0, nrow, body, 0)


def _mix(x2, lg, lb, a, b, c, gates, wb, wo, g1, b1, tm=512, tn=512):
    m, d = x2.shape
    bw = a.shape[1]
    nn = d // tn
    row = lambda w: pl.BlockSpec((tm, w), lambda i, n: (i, 0))
    vec = pl.BlockSpec((1, d), lambda i, n: (0, 0))
    gts = [pl.BlockSpec((tm, tn), functools.partial(lambda i, n, k: (i, k * nn + n), k=k)) for k in range(3)]
    return pl.pallas_call(
        _mix_kernel,
        grid=(m // tm, nn),
        in_specs=[row(d), vec, vec, row(bw), row(bw), row(bw), *gts,
                  pl.BlockSpec((3, bw, tn), lambda i, n: (0, 0, n)),
                  pl.BlockSpec((tn, d), lambda i, n: (n, 0)), vec, vec],
        out_specs=pl.BlockSpec((tm, d), lambda i, n: (i, 0)),
        out_shape=jax.ShapeDtypeStruct((m, d), F32),
        scratch_shapes=[pltpu.VMEM((tm, d), F32)],
        compiler_params=_params("parallel", "arbitrary"),
        name="mix",
    )(x2, lg, lb, a, b, c, gates, gates, gates, wb, wo, g1, b1)


HALO = 16


def _gelu_tanh(x):
    return 0.5 * x * (1.0 + jnp.tanh(math.sqrt(2.0 / math.pi) * (x + 0.044715 * (x * x * x))))


def _ffn_kernel(h_ref, hp_ref, hn_ref, wv_ref, wg_ref, cwv_ref, cwg_ref, cbv_ref, cbg_ref, wd_ref,
                g2_ref, b2_ref, o_ref, hx_scr, acc_scr, *, tiles_per_seq):
    i = pl.program_id(0)
    f = pl.program_id(1)
    tm = h_ref.shape[0]
    nrow = tm // LN_ROWS

    @pl.when(f == 0)
    def _():
        seq_first = (i % tiles_per_seq) == 0
        seq_last = (i % tiles_per_seq) == tiles_per_seq - 1
        hx_scr[0:HALO, :] = jnp.where(seq_first, 0.0, hp_ref[...]).astype(BF16)
        hx_scr[HALO:HALO + tm, :] = h_ref[...].astype(BF16)
        hx_scr[HALO + tm:, :] = jnp.where(seq_last, 0.0, hn_ref[...]).astype(BF16)
        acc_scr[...] = jnp.zeros_like(acc_scr)

    hx = hx_scr[...]
    ext = tm + 2 * HALO

    def conv(u, cw_ref, cb_ref):
        prev = pltpu.roll(u, 1, 0)[HALO:HALO + tm]
        nxt = pltpu.roll(u, ext - 1, 0)[HALO:HALO + tm]
        return prev * cw_ref[0:1, :] + u[HALO:HALO + tm] * cw_ref[1:2, :] + nxt * cw_ref[2:3, :] + cb_ref[...]

    gelu_gate = _gelu_tanh(conv(_dot(hx, wg_ref[...]), cwg_ref, cbg_ref))
    val = conv(_dot(hx, wv_ref[...]), cwv_ref, cbv_ref)
    acc_scr[...] += _dot((gelu_gate * val).astype(BF16), wd_ref[...])

    @pl.when(f == pl.num_programs(1) - 1)
    def _():
        def body(r, c):
            rows = pl.ds(pl.multiple_of(r * LN_ROWS, LN_ROWS), LN_ROWS)
            o_ref[rows, :] = _ln_rows(ALPHA * h_ref[rows, :] + acc_scr[rows, :], g2_ref[...], b2_ref[...])
            return c
        lax.fori_loop(0, nrow, body, 0)


def _ffn(h1, w_up, cw, cb, w_down, g2, b2, seq, tm=512, tf=512):
    m, d = h1.shape
    nf = D_FF_PAD // tf
    hb = tm // HALO
    nhb = m // HALO
    vec = pl.BlockSpec((1, d), lambda i, f: (0, 0))
    return pl.pallas_call(
        functools.partial(_ffn_kernel, tiles_per_seq=seq // tm),
        grid=(m // tm, nf),
        in_specs=[pl.BlockSpec((tm, d), lambda i, f: (i, 0)),
                  pl.BlockSpec((HALO, d), lambda i, f: (jnp.maximum(i * hb - 1, 0), 0)),
                  pl.BlockSpec((HALO, d), lambda i, f: (jnp.minimum((i + 1) * hb, nhb - 1), 0)),
                  pl.BlockSpec((d, tf), lambda i, f: (0, f)),
                  pl.BlockSpec((d, tf), lambda i, f: (0, nf + f)),
                  pl.BlockSpec((3, tf), lambda i, f: (0, f)),
                  pl.BlockSpec((3, tf), lambda i, f: (0, nf + f)),
                  pl.BlockSpec((1, tf), lambda i, f: (0, f)),
                  pl.BlockSpec((1, tf), lambda i, f: (0, nf + f)),
                  pl.BlockSpec((tf, d), lambda i, f: (f, 0)), vec, vec],
        out_specs=pl.BlockSpec((tm, d), lambda i, f: (i, 0)),
        out_shape=jax.ShapeDtypeStruct((m, d), F32),
        scratch_shapes=[pltpu.VMEM((tm + 2 * HALO, d), BF16), pltpu.VMEM((tm, d), F32)],
        compiler_params=_params("parallel", "arbitrary"),
        name="ffn",
    )(h1, h1, h1, w_up, w_up, cw, cw, cb, cb, w_down, g2, b2)


def _w_up_prep_kernel(w_ref, o_ref):
    o_ref[:, 0:D_FF] = w_ref[...].astype(BF16)
    o_ref[:, D_FF:] = jnp.zeros((o_ref.shape[0], D_FF_PAD - D_FF), BF16)


def _w_up_prep(w_up, tr=256):
    d = w_up.shape[0]
    return pl.pallas_call(
        _w_up_prep_kernel,
        grid=(2, d // tr),
        in_specs=[pl.BlockSpec((tr, D_FF), lambda half, r: (r, half))],
        out_specs=pl.BlockSpec((tr, D_FF_PAD), lambda half, r: (r, half)),
        out_shape=jax.ShapeDtypeStruct((d, 2 * D_FF_PAD), BF16),
        compiler_params=_params("parallel", "parallel"),
        name="w_up_prep",
    )(w_up)


def _w_down_prep_kernel(w_ref, o_ref):
    o_ref[0:D_FF, :] = w_ref[...].astype(BF16)
    o_ref[D_FF:, :] = jnp.zeros((D_FF_PAD - D_FF, o_ref.shape[1]), BF16)


def _w_down_prep(w_down, tc=256):
    d = w_down.shape[1]
    return pl.pallas_call(
        _w_down_prep_kernel,
        grid=(d // tc,),
        in_specs=[pl.BlockSpec((D_FF, tc), lambda c: (0, c))],
        out_specs=pl.BlockSpec((D_FF_PAD, tc), lambda c: (0, c)),
        out_shape=jax.ShapeDtypeStruct((D_FF_PAD, d), BF16),
        compiler_params=_params("parallel"),
        name="w_down_prep",
    )(w_down)


def _pad_ff(t, axis):
    val, gate = jnp.split(t, 2, axis=axis)
    pad = [(0, 0)] * t.ndim
    pad[axis] = (0, D_FF_PAD - D_FF)
    return jnp.concatenate([jnp.pad(val, pad), jnp.pad(gate, pad)], axis=axis)


def kernel(x, mem, ln_in_g, ln_in_b, rel_table, w_in, w_mem_kv, diff_lq1, diff_lk1, diff_lq2, diff_lk2,
           diff_subln_g, win_sink, w_gate, b_gate, w_branch, w_o, ln1_g, ln1_b, w_up, conv_w, conv_b,
           w_down, ln2_g, ln2_b):
    assert w_in.shape[0] == DEPTH == 1
    bsz, seq, d = x.shape
    x2 = x.reshape(bsz * seq, d)
    row = lambda v: v.reshape(1, -1).astype(F32)
    l = 0
    hb = _ln(x2, row(ln_in_g), row(ln_in_b))
    proj = _proj(hb, w_in[l])
    proj = proj.reshape(bsz, seq, IN_W)
    memkv = _mem_kv(mem.reshape(bsz * N_MEM, d), w_mem_kv[l]).reshape(bsz, N_MEM, 2 * BRANCH_W)

    tq_d, tb_d, tq_w, tb_w = 512, 256, 512, 256
    a = _diff_attn(proj, _diff_bias_tiles(rel_table[:, :DIFF_HEADS] * LOG2E, tb_d), row(diff_lq1[l]), row(diff_lk1[l]),
                   row(diff_lq2[l]), row(diff_lk2[l]), row(diff_subln_g[l]), tq=tq_d)
    b = _win_attn(proj, _win_bias_tiles(rel_table[:, DIFF_HEADS:] * LOG2E, tb_w), win_sink[l].astype(F32) * LOG2E,
                  tq=tq_w)
    c = _mem_attn(proj, memkv)

    m = bsz * seq
    gates = _gates(hb, w_gate[l], row(b_gate[l]))
    h1 = _mix(x2, row(ln_in_g), row(ln_in_b), a.reshape(m, -1), b.reshape(m, -1), c.reshape(m, -1), gates,
              w_branch[l].astype(BF16), w_o[l].astype(BF16), row(ln1_g[l]), row(ln1_b[l]))

    w_up_p = _w_up_prep(w_up[l].astype(F32))
    cw_p = _pad_ff(conv_w[l].astype(F32), 1)
    cb_p = _pad_ff(row(conv_b[l]), 1)
    w_down_p = _w_down_prep(w_down[l].astype(F32))
    out = _ffn(h1, w_up_p, cw_p, cb_p, w_down_p, row(ln2_g[l]), row(ln2_b[l]), seq)
    return out.reshape(bsz, seq, d)
```

```python
import functools
import math

import jax
import jax.numpy as jnp
from jax import lax
from jax.experimental import pallas as pl
from jax.experimental.pallas import tpu as pltpu

F32 = jnp.float32
BF16 = jnp.bfloat16

D_MODEL = 2048
SEQ = 2048
N_MEM = 256
HEAD_DIM = 128
BRANCH_W = 1024
DIFF_HEADS = 8
DIFF_QK = 64
WIN_HEADS = 8
WIN_KV_HEADS = 2
WIN_GROUP = WIN_HEADS // WIN_KV_HEADS
WINDOW = 128
MEM_HEADS = 4
MEM_DIM = 256
OFF_DQ, OFF_DK, OFF_DV, OFF_WQ, OFF_WK, OFF_WV, OFF_MQ = 0, 1024, 2048, 3072, 4096, 4352, 4608
IN_W = 5632
D_FF = 5504
D_FF_PAD = 5632
REL_BUCKETS = 32
REL_MAX_DIST = 128
DEPTH = 1
ALPHA = (2 * DEPTH) ** 0.25
LN_EPS = 1e-5
NEG = -1e30
LOG2E = math.log2(math.e)
LAMBDA_INIT = 0.8 - 0.6 * math.exp(-0.3 * 0)

VMEM_LIMIT = 56 * 1024 * 1024
LN_ROWS = 128


def _ln_rows(x, g, b):
    mu = jnp.mean(x, axis=-1, keepdims=True)
    xc = x - mu
    var = jnp.mean(xc * xc, axis=-1, keepdims=True)
    return xc * lax.rsqrt(var + LN_EPS) * g + b


def _dot(a, b):
    return jnp.dot(a, b, preferred_element_type=F32)


def _dot_nt(a, b):
    return lax.dot_general(a, b, (((1,), (1,)), ((), ())), preferred_element_type=F32)


def _params(*sem):
    return pltpu.CompilerParams(dimension_semantics=sem, vmem_limit_bytes=VMEM_LIMIT)


def _ln_kernel(x_ref, g_ref, b_ref, h_ref):
    def body(r, c):
        rows = pl.ds(pl.multiple_of(r * LN_ROWS, LN_ROWS), LN_ROWS)
        h_ref[rows, :] = _ln_rows(x_ref[rows, :], g_ref[...], b_ref[...]).astype(BF16)
        return c
    lax.fori_loop(0, x_ref.shape[0] // LN_ROWS, body, 0)


def _ln(x2, g, b, tm=512):
    m, d = x2.shape
    vec = pl.BlockSpec((1, d), lambda i: (0, 0))
    return pl.pallas_call(
        _ln_kernel,
        grid=(m // tm,),
        in_specs=[pl.BlockSpec((tm, d), lambda i: (i, 0)), vec, vec],
        out_specs=pl.BlockSpec((tm, d), lambda i: (i, 0)),
        out_shape=jax.ShapeDtypeStruct((m, d), BF16),
        compiler_params=_params("parallel"),
        name="ln_in",
    )(x2, g, b)


MM_ROWS = 2048


def _proj_kernel(h_ref, w_ref, o_ref):
    def body(r, c):
        rows = pl.ds(pl.multiple_of(r * MM_ROWS, MM_ROWS), MM_ROWS)
        o_ref[rows, :] = _dot(h_ref[rows, :], w_ref[...].astype(BF16)).astype(BF16)
        return c
    lax.fori_loop(0, h_ref.shape[0] // MM_ROWS, body, 0)


def _proj(hb, w, tm=4096, tn=512):
    m, d = hb.shape
    n = w.shape[1]
    return pl.pallas_call(
        _proj_kernel,
        grid=(m // tm, n // tn),
        in_specs=[pl.BlockSpec((tm, d), lambda i, j: (i, 0)),
                  pl.BlockSpec((d, tn), lambda i, j: (0, j))],
        out_specs=pl.BlockSpec((tm, tn), lambda i, j: (i, j)),
        out_shape=jax.ShapeDtypeStruct((m, n), BF16),
        compiler_params=_params("parallel", "arbitrary"),
        name="proj",
    )(hb, w)


def _gates_kernel(h_ref, w_ref, b_ref, o_ref):
    def body(r, c):
        rows = pl.ds(pl.multiple_of(r * MM_ROWS, MM_ROWS), MM_ROWS)
        z = _dot(h_ref[rows, :], w_ref[...].astype(BF16)) + b_ref[...]
        o_ref[rows, :] = (0.5 * jnp.tanh(0.5 * z) + 0.5).astype(BF16)
        return c
    lax.fori_loop(0, h_ref.shape[0] // MM_ROWS, body, 0)


def _gates(hb, w, b, tm=4096, tn=512):
    m, d = hb.shape
    n = w.shape[1]
    return pl.pallas_call(
        _gates_kernel,
        grid=(m // tm, n // tn),
        in_specs=[pl.BlockSpec((tm, d), lambda i, j: (i, 0)),
                  pl.BlockSpec((d, tn), lambda i, j: (0, j)),
                  pl.BlockSpec((1, tn), lambda i, j: (0, j))],
        out_specs=pl.BlockSpec((tm, tn), lambda i, j: (i, j)),
        out_shape=jax.ShapeDtypeStruct((m, n), BF16),
        compiler_params=_params("parallel", "arbitrary"),
        name="gates",
    )(hb, w, b)


def _mem_kv_kernel(m_ref, w_ref, o_ref):
    o_ref[...] = _dot(m_ref[...].astype(BF16), w_ref[...].astype(BF16)).astype(BF16)


def _mem_kv(mem2, w, tn=512):
    m, d = mem2.shape
    n = w.shape[1]
    return pl.pallas_call(
        _mem_kv_kernel,
        grid=(n // tn,),
        in_specs=[pl.BlockSpec((m, d), lambda j: (0, 0)),
                  pl.BlockSpec((d, tn), lambda j: (0, j))],
        out_specs=pl.BlockSpec((m, tn), lambda j: (0, j)),
        out_shape=jax.ShapeDtypeStruct((m, n), BF16),
        compiler_params=_params("parallel"),
        name="mem_kv",
    )(mem2, w)


def _bucket_lower_bounds():
    half = REL_BUCKETS // 2
    max_exact = half // 2
    n = jnp.arange(REL_MAX_DIST + 1)
    log_part = (jnp.log(jnp.maximum(n, 1).astype(F32) / max_exact) / math.log(REL_MAX_DIST / max_exact)
                * (half - max_exact)).astype(jnp.int32)
    bucket = jnp.where(n < max_exact, n, jnp.minimum(max_exact + log_part, half - 1))
    bucket = jnp.arange(half, dtype=jnp.int32)[bucket]
    ids = jnp.arange(half, dtype=jnp.int32)
    return jnp.sum((bucket[None, :] < ids[:, None]).astype(jnp.int32), axis=1)


BIAS_ROWS = 64


def _bias_kernel(lb_ref, tab_ref, lo_ref, hi_ref, off_ref, o_ref, *, limit):
    h = pl.program_id(0)
    half = REL_BUCKETS // 2
    ncol = o_ref.shape[2]

    def body(r, carry):
        rows = pl.ds(pl.multiple_of(r * BIAS_ROWS, BIAS_ROWS), BIAS_ROWS)

        def general():
            rel = lax.broadcasted_iota(jnp.int32, (BIAS_ROWS, ncol), 1) + off_ref[rows, :]
            n = jnp.abs(rel)
            neg = jnp.full(rel.shape, tab_ref[0, h], F32)
            pos = jnp.full(rel.shape, tab_ref[half, h], F32)
            for b in range(1, half):
                reached = n >= lb_ref[b]
                neg = jnp.where(reached, tab_ref[b, h], neg)
                pos = jnp.where(reached, tab_ref[half + b, h], pos)
            out = jnp.where(rel > 0, pos, neg)
            if limit is not None:
                out = jnp.where(n > limit, NEG, out)
            o_ref[0, rows, :] = out

        if limit is not None:
            general()
            return carry
        far = lb_ref[half - 1]
        all_after = lo_ref[r] >= far
        all_before = -(hi_ref[r] + (ncol - 1)) >= far

        @pl.when(all_after)
        def _():
            o_ref[0, rows, :] = jnp.full((BIAS_ROWS, ncol), tab_ref[2 * half - 1, h], F32)

        @pl.when(all_before)
        def _():
            o_ref[0, rows, :] = jnp.full((BIAS_ROWS, ncol), tab_ref[half - 1, h], F32)

        pl.when(jnp.logical_not(jnp.logical_or(all_after, all_before)))(general)
        return carry
    lax.fori_loop(0, o_ref.shape[1] // BIAS_ROWS, body, 0)


def _bias_tiles(table, row_offset, ncol, limit=None):
    nh = table.shape[1]
    nrow = row_offset.shape[0]
    off = row_offset.astype(jnp.int32)
    chunks = off.reshape(nrow // BIAS_ROWS, BIAS_ROWS)
    smem = pl.BlockSpec(memory_space=pltpu.SMEM)
    return pl.pallas_call(
        functools.partial(_bias_kernel, limit=limit),
        grid=(nh,),
        in_specs=[smem, smem, smem, smem, pl.BlockSpec((nrow, 1), lambda h: (0, 0))],
        out_specs=pl.BlockSpec((1, nrow, ncol), lambda h: (h, 0, 0)),
        out_shape=jax.ShapeDtypeStruct((nh, nrow, ncol), F32),
        compiler_params=_params("parallel"),
        name="bias_tiles",
    )(_bucket_lower_bounds(), table.astype(F32), jnp.min(chunks, axis=1), jnp.max(chunks, axis=1),
      off.reshape(nrow, 1))


def _diff_bias_tiles(table, tq):
    r = jnp.arange(5 * tq)
    return _bias_tiles(table, (r // tq - 2) * tq - r % tq, tq).reshape(table.shape[1], 5, tq, tq)


def _win_bias_tiles(table, tq):
    return _bias_tiles(table, -WINDOW - jnp.arange(tq), tq + 2 * WINDOW, limit=WINDOW)


def _diff_kernel(q_ref, k_ref, v_ref, band_ref, lq1_ref, lk1_ref, lq2_ref, lk2_ref, g_ref, o_ref,
                 s_scr, mx_scr, *, tq, rc, n_tiles, tiles_per_head):
    t = pl.program_id(0)
    hd = v_ref.shape[2]

    @pl.when(t == 0)
    def _():
        s_scr[...] = jnp.zeros_like(s_scr)
        mx_scr[...] = jnp.zeros_like(mx_scr)

    def tick(slot):
        other = 1 - slot
        i = jnp.minimum(t, n_tiles - 1) % tiles_per_head
        nk = k_ref.shape[1] // tq
        q = (q_ref[0].astype(F32) * (DIFF_QK ** -0.5 * LOG2E)).astype(BF16)
        lane = lax.broadcasted_iota(jnp.int32, q.shape, 1)
        zero = jnp.zeros_like(q)
        qs = jnp.concatenate([jnp.where(lane < DIFF_QK, q, zero), jnp.where(lane >= DIFF_QK, q, zero)], axis=0)
        lanes = mx_scr.shape[-1]
        tb = band_ref.shape[2]
        nb = tq // tb
        for j in range(nk):
            s = _dot_nt(qs, k_ref[0, j * tq:(j + 1) * tq, :])
            for r0 in range(0, 2 * tq, tb):
                rows = slice(r0, r0 + tb)
                fold = None
                for cb in range(nb):
                    d = (j - i) * nb + cb - (r0 // tb) % nb
                    sb = s[rows, cb * tb:(cb + 1) * tb] + band_ref[0, jnp.clip(d, -2, 2) + 2]
                    s_scr[slot, rows, j * tq + cb * tb:j * tq + (cb + 1) * tb] = sb
                    fold = functools.reduce(jnp.maximum, [sb[:, c:c + lanes] for c in range(0, tb, lanes)]
                                            + ([] if fold is None else [fold]))
                mx_scr[slot, rows, :] = fold if j == 0 else jnp.maximum(mx_scr[slot, rows, :], fold)

        nchunk = 2 * tq // rc
        chunk = lambda r: slice(r * rc, (r + 1) * rc)
        p = jnp.concatenate(
            [jnp.exp2(s_scr[other, chunk(r), :] - jnp.max(mx_scr[other, chunk(r), :], axis=-1, keepdims=True))
             .astype(BF16) for r in range(nchunk)], axis=0)
        v_ones = jnp.concatenate([v_ref[0], jnp.ones(v_ref.shape[1:], BF16)], axis=1)
        pv = _dot(p, v_ones)
        lam = (jnp.exp(jnp.sum(lq1_ref[...] * lk1_ref[...], axis=-1, keepdims=True))
               - jnp.exp(jnp.sum(lq2_ref[...] * lk2_ref[...], axis=-1, keepdims=True)) + LAMBDA_INIT)
        o = pv[0:tq, 0:hd] * (1.0 / pv[0:tq, hd:]) - pv[tq:, 0:hd] * (lam / pv[tq:, hd:])
        ms = jnp.mean(o * o, axis=-1, keepdims=True)
        o_ref[0] = (o * lax.rsqrt(ms + LN_EPS) * g_ref[...] * (1.0 - LAMBDA_INIT)).astype(BF16)

    pl.when(t % 2 == 0)(functools.partial(tick, 0))
    pl.when(t % 2 == 1)(functools.partial(tick, 1))


def _diff_attn(proj, band, lq1, lk1, lq2, lk2, g, tq=256, rc=32):
    b, s, _ = proj.shape
    cb = HEAD_DIM
    ni = s // tq
    n_tiles = b * DIFF_HEADS * ni
    depth = 1

    def tile(t, lag):
        tc = jnp.clip(t - lag, 0, n_tiles - 1)
        return tc // (DIFF_HEADS * ni), (tc // ni) % DIFF_HEADS, tc % ni

    def q_map(t):
        bi, h, i = tile(t, 0)
        return bi, i, OFF_DQ // cb + h

    def k_map(t):
        bi, h, _ = tile(t, 0)
        return bi, 0, OFF_DK // cb + h

    def v_map(t):
        bi, h, _ = tile(t, depth)
        return bi, 0, OFF_DV // cb + h

    def o_map(t):
        bi, h, i = tile(t, depth)
        return bi, i, h

    vec = lambda n: pl.BlockSpec((1, n), lambda t: (0, 0))
    return pl.pallas_call(
        functools.partial(_diff_kernel, tq=tq, rc=rc, n_tiles=n_tiles, tiles_per_head=ni),
        grid=(n_tiles + depth,),
        in_specs=[pl.BlockSpec((1, tq, cb), q_map),
                  pl.BlockSpec((1, s, cb), k_map),
                  pl.BlockSpec((1, s, cb), v_map),
                  pl.BlockSpec((1,) + band.shape[1:], lambda t: (tile(t, 0)[1], 0, 0, 0)),
                  vec(DIFF_QK), vec(DIFF_QK), vec(DIFF_QK), vec(DIFF_QK), vec(HEAD_DIM)],
        out_specs=pl.BlockSpec((1, tq, cb), o_map),
        out_shape=jax.ShapeDtypeStruct((b, s, BRANCH_W), BF16),
        scratch_shapes=[pltpu.VMEM((2, 2 * tq, s), F32), pltpu.VMEM((2, 2 * tq, cb), F32)],
        compiler_params=_params("arbitrary"),
        name="diff_attn",
    )(proj, proj, proj, band, lq1, lk1, lq2, lk2, g)


def _win_kernel(sink_ref, q_ref, kp_ref, km_ref, kn_ref, vp_ref, vm_ref, vn_ref, bias_ref, o_ref, *, tq):
    g = pl.program_id(1)
    i = pl.program_id(2)
    first = i == 0
    last = i == pl.num_programs(2) - 1
    w = WINDOW
    hd = HEAD_DIM
    tb = bias_ref.shape[1]
    nkeys = tb + 2 * w
    keys = jnp.concatenate([kp_ref[0], km_ref[0], kn_ref[0]], axis=0)
    vals = jnp.concatenate([vp_ref[0], vm_ref[0], vn_ref[0]], axis=0)
    v_ones = jnp.concatenate([vals, jnp.ones((tq + 2 * w, hd), BF16)], axis=1)
    col = lax.broadcasted_iota(jnp.int32, (1, nkeys), 1)
    scale = jnp.asarray(hd ** -0.5 * LOG2E, F32)
    for r0 in range(0, tq, tb):
        rows = slice(r0, r0 + tb)
        band = slice(r0, r0 + nkeys)
        outside = None
        if r0 == 0:
            outside = jnp.logical_and(first, col < w)
        if r0 + tb == tq:
            past_end = jnp.logical_and(last, col >= tb + w)
            outside = past_end if outside is None else jnp.logical_or(outside, past_end)
        for j in range(WIN_GROUP):
            cols = slice(j * hd, (j + 1) * hd)
            s = _dot_nt(q_ref[0, rows, cols], keys[band]) * scale + bias_ref[j]
            if outside is not None:
                s = jnp.where(outside, NEG, s)
            sink = sink_ref[g * WIN_GROUP + j]
            fold = functools.reduce(jnp.maximum, [s[:, c:c + w] for c in range(0, nkeys, w)])
            m = jnp.maximum(jnp.max(fold, axis=-1, keepdims=True), sink)
            pv = _dot(jnp.exp2(s - m).astype(BF16), v_ones[band])
            o_ref[0, rows, cols] = (pv[:, 0:hd] / (pv[:, hd:] + jnp.exp2(sink - m))).astype(BF16)


def _win_attn(proj, bias, sink, tq=256):
    b, s, _ = proj.shape
    hd = HEAD_DIM
    nb = s // WINDOW
    r = tq // WINDOW
    prev = lambda bi, g, i: jnp.maximum(i * r - 1, 0)
    nxt = lambda bi, g, i: jnp.minimum((i + 1) * r, nb - 1)
    kcol, vcol = OFF_WK // hd, OFF_WV // hd
    return pl.pallas_call(
        functools.partial(_win_kernel, tq=tq),
        grid=(b, WIN_KV_HEADS, s // tq),
        in_specs=[pl.BlockSpec(memory_space=pltpu.SMEM),
                  pl.BlockSpec((1, tq, WIN_GROUP * hd), lambda bi, g, i: (bi, i, OFF_WQ // (WIN_GROUP * hd) + g)),
                  pl.BlockSpec((1, WINDOW, hd), lambda bi, g, i: (bi, prev(bi, g, i), kcol + g)),
                  pl.BlockSpec((1, tq, hd), lambda bi, g, i: (bi, i, kcol + g)),
                  pl.BlockSpec((1, WINDOW, hd), lambda bi, g, i: (bi, nxt(bi, g, i), kcol + g)),
                  pl.BlockSpec((1, WINDOW, hd), lambda bi, g, i: (bi, prev(bi, g, i), vcol + g)),
                  pl.BlockSpec((1, tq, hd), lambda bi, g, i: (bi, i, vcol + g)),
                  pl.BlockSpec((1, WINDOW, hd), lambda bi, g, i: (bi, nxt(bi, g, i), vcol + g)),
                  pl.BlockSpec((WIN_GROUP,) + bias.shape[1:], lambda bi, g, i: (g, 0, 0))],
        out_specs=pl.BlockSpec((1, tq, WIN_GROUP * hd), lambda bi, g, i: (bi, i, g)),
        out_shape=jax.ShapeDtypeStruct((b, s, BRANCH_W), BF16),
        compiler_params=_params("parallel", "parallel", "arbitrary"),
        name="win_attn",
    )(sink, proj, proj, proj, proj, proj, proj, proj, bias)


def _mem_attn_kernel(q_ref, k_ref, v_ref, o_ref):
    s = _dot_nt(q_ref[0], k_ref[0]) * jnp.asarray(MEM_DIM ** -0.5, F32)
    p = jnp.exp(s - jnp.max(s, axis=-1, keepdims=True))
    p = p * (1.0 / jnp.sum(p, axis=-1, keepdims=True))
    o_ref[0] = _dot(p.astype(BF16), v_ref[0]).astype(BF16)


def _mem_attn(proj, memkv, tq=2048):
    b, s, _ = proj.shape
    md = MEM_DIM
    return pl.pallas_call(
        _mem_attn_kernel,
        grid=(b, MEM_HEADS, s // tq),
        in_specs=[pl.BlockSpec((1, tq, md), lambda bi, h, i: (bi, i, OFF_MQ // md + h)),
                  pl.BlockSpec((1, N_MEM, md), lambda bi, h, i: (bi, 0, h)),
                  pl.BlockSpec((1, N_MEM, md), lambda bi, h, i: (bi, 0, MEM_HEADS + h))],
        out_specs=pl.BlockSpec((1, tq, md), lambda bi, h, i: (bi, i, h)),
        out_shape=jax.ShapeDtypeStruct((b, s, BRANCH_W), BF16),
        compiler_params=_params("parallel", "parallel", "arbitrary"),
        name="mem_attn",
    )(proj, memkv, memkv)


def _mix_kernel(x_ref, lg_ref, lb_ref, a_ref, b_ref, c_ref, g0_ref, g1_ref, g2_ref, wb_ref, wo_ref,
                ln_g_ref, ln_b_ref, o_ref, acc_scr):
    n = pl.program_id(1)
    nrow = x_ref.shape[0] // LN_ROWS

    @pl.when(n == 0)
    def _():
        acc_scr[...] = jnp.zeros_like(acc_scr)

    mixed = None
    for k, (br_ref, gate_ref) in enumerate(((a_ref, g0_ref), (b_ref, g1_ref), (c_ref, g2_ref))):
        term = gate_ref[...].astype(F32) * _dot(br_ref[...], wb_ref[k])
        mixed = term if mixed is None else mixed + term
    acc_scr[...] += _dot(mixed.astype(BF16), wo_ref[...])

    @pl.when(n == pl.num_programs(1) - 1)
    def _():
        def body(r, c):
            rows = pl.ds(pl.multiple_of(r * LN_ROWS, LN_ROWS), LN_ROWS)
            hh = _ln_rows(x_ref[rows, :], lg_ref[...], lb_ref[...])
            o_ref[rows, :] = _ln_rows(ALPHA * hh + acc_scr[rows, :], ln_g_ref[...], ln_b_ref[...])
            return c
        lax.fori_loop(0, nrow, body, 0)


def _mix(x2, lg, lb, a, b, c, gates, wb, wo, g1, b1, tm=512, tn=512):
    m, d = x2.shape
    bw = a.shape[1]
    nn = d // tn
    row = lambda w: pl.BlockSpec((tm, w), lambda i, n: (i, 0))
    vec = pl.BlockSpec((1, d), lambda i, n: (0, 0))
    gts = [pl.BlockSpec((tm, tn), functools.partial(lambda i, n, k: (i, k * nn + n), k=k)) for k in range(3)]
    return pl.pallas_call(
        _mix_kernel,
        grid=(m // tm, nn),
        in_specs=[row(d), vec, vec, row(bw), row(bw), row(bw), *gts,
                  pl.BlockSpec((3, bw, tn), lambda i, n: (0, 0, n)),
                  pl.BlockSpec((tn, d), lambda i, n: (n, 0)), vec, vec],
        out_specs=pl.BlockSpec((tm, d), lambda i, n: (i, 0)),
        out_shape=jax.ShapeDtypeStruct((m, d), F32),
        scratch_shapes=[pltpu.VMEM((tm, d), F32)],
        compiler_params=_params("parallel", "arbitrary"),
        name="mix",
    )(x2, lg, lb, a, b, c, gates, gates, gates, wb, wo, g1, b1)


HALO = 16


def _gelu_tanh(x):
    return 0.5 * x * (1.0 + jnp.tanh(math.sqrt(2.0 / math.pi) * (x + 0.044715 * (x * x * x))))


def _ffn_kernel(h_ref, hp_ref, hn_ref, wv_ref, wg_ref, cwv_ref, cwg_ref, cbv_ref, cbg_ref, wd_ref,
                g2_ref, b2_ref, o_ref, hx_scr, acc_scr, *, tiles_per_seq):
    i = pl.program_id(0)
    f = pl.program_id(1)
    tm = h_ref.shape[0]
    nrow = tm // LN_ROWS

    @pl.when(f == 0)
    def _():
        seq_first = (i % tiles_per_seq) == 0
        seq_last = (i % tiles_per_seq) == tiles_per_seq - 1
        hx_scr[0:HALO, :] = jnp.where(seq_first, 0.0, hp_ref[...]).astype(BF16)
        hx_scr[HALO:HALO + tm, :] = h_ref[...].astype(BF16)
        hx_scr[HALO + tm:, :] = jnp.where(seq_last, 0.0, hn_ref[...]).astype(BF16)
        acc_scr[...] = jnp.zeros_like(acc_scr)

    hx = hx_scr[...]
    ext = tm + 2 * HALO

    def conv(u, cw_ref, cb_ref):
        prev = pltpu.roll(u, 1, 0)[HALO:HALO + tm]
        nxt = pltpu.roll(u, ext - 1, 0)[HALO:HALO + tm]
        return prev * cw_ref[0:1, :] + u[HALO:HALO + tm] * cw_ref[1:2, :] + nxt * cw_ref[2:3, :] + cb_ref[...]

    gelu_gate = _gelu_tanh(conv(_dot(hx, wg_ref[...]), cwg_ref, cbg_ref))
    val = conv(_dot(hx, wv_ref[...]), cwv_ref, cbv_ref)
    acc_scr[...] += _dot((gelu_gate * val).astype(BF16), wd_ref[...])

    @pl.when(f == pl.num_programs(1) - 1)
    def _():
        def body(r, c):
            rows = pl.ds(pl.multiple_of(r * LN_ROWS, LN_ROWS), LN_ROWS)
            o_ref[rows, :] = _ln_rows(ALPHA * h_ref[rows, :] + acc_scr[rows, :], g2_ref[...], b2_ref[...])
            return c
        lax.fori_loop(0, nrow, body, 0)


def _ffn(h1, w_up, cw, cb, w_down, g2, b2, seq, tm=512, tf=512):
    m, d = h1.shape
    nf = D_FF_PAD // tf
    hb = tm // HALO
    nhb = m // HALO
    vec = pl.BlockSpec((1, d), lambda i, f: (0, 0))
    return pl.pallas_call(
        functools.partial(_ffn_kernel, tiles_per_seq=seq // tm),
        grid=(m // tm, nf),
        in_specs=[pl.BlockSpec((tm, d), lambda i, f: (i, 0)),
                  pl.BlockSpec((HALO, d), lambda i, f: (jnp.maximum(i * hb - 1, 0), 0)),
                  pl.BlockSpec((HALO, d), lambda i, f: (jnp.minimum((i + 1) * hb, nhb - 1), 0)),
                  pl.BlockSpec((d, tf), lambda i, f: (0, f)),
                  pl.BlockSpec((d, tf), lambda i, f: (0, nf + f)),
                  pl.BlockSpec((3, tf), lambda i, f: (0, f)),
                  pl.BlockSpec((3, tf), lambda i, f: (0, nf + f)),
                  pl.BlockSpec((1, tf), lambda i, f: (0, f)),
                  pl.BlockSpec((1, tf), lambda i, f: (0, nf + f)),
                  pl.BlockSpec((tf, d), lambda i, f: (f, 0)), vec, vec],
        out_specs=pl.BlockSpec((tm, d), lambda i, f: (i, 0)),
        out_shape=jax.ShapeDtypeStruct((m, d), F32),
        scratch_shapes=[pltpu.VMEM((tm + 2 * HALO, d), BF16), pltpu.VMEM((tm, d), F32)],
        compiler_params=_params("parallel", "arbitrary"),
        name="ffn",
    )(h1, h1, h1, w_up, w_up, cw, cw, cb, cb, w_down, g2, b2)


def _w_up_prep_kernel(w_ref, o_ref):
    o_ref[:, 0:D_FF] = w_ref[...].astype(BF16)
    o_ref[:, D_FF:] = jnp.zeros((o_ref.shape[0], D_FF_PAD - D_FF), BF16)


def _w_up_prep(w_up, tr=256):
    d = w_up.shape[0]
    return pl.pallas_call(
        _w_up_prep_kernel,
        grid=(2, d // tr),
        in_specs=[pl.BlockSpec((tr, D_FF), lambda half, r: (r, half))],
        out_specs=pl.BlockSpec((tr, D_FF_PAD), lambda half, r: (r, half)),
        out_shape=jax.ShapeDtypeStruct((d, 2 * D_FF_PAD), BF16),
        compiler_params=_params("parallel", "parallel"),
        name="w_up_prep",
    )(w_up)


def _w_down_prep_kernel(w_ref, o_ref):
    o_ref[0:D_FF, :] = w_ref[...].astype(BF16)
    o_ref[D_FF:, :] = jnp.zeros((D_FF_PAD - D_FF, o_ref.shape[1]), BF16)


def _w_down_prep(w_down, tc=256):
    d = w_down.shape[1]
    return pl.pallas_call(
        _w_down_prep_kernel,
        grid=(d // tc,),
        in_specs=[pl.BlockSpec((D_FF, tc), lambda c: (0, c))],
        out_specs=pl.BlockSpec((D_FF_PAD, tc), lambda c: (0, c)),
        out_shape=jax.ShapeDtypeStruct((D_FF_PAD, d), BF16),
        compiler_params=_params("parallel"),
        name="w_down_prep",
    )(w_down)


def _pad_ff(t, axis):
    val, gate = jnp.split(t, 2, axis=axis)
    pad = [(0, 0)] * t.ndim
    pad[axis] = (0, D_FF_PAD - D_FF)
    return jnp.concatenate([jnp.pad(val, pad), jnp.pad(gate, pad)], axis=axis)


def kernel(x, mem, ln_in_g, ln_in_b, rel_table, w_in, w_mem_kv, diff_lq1, diff_lk1, diff_lq2, diff_lk2,
           diff_subln_g, win_sink, w_gate, b_gate, w_branch, w_o, ln1_g, ln1_b, w_up, conv_w, conv_b,
           w_down, ln2_g, ln2_b):
    assert w_in.shape[0] == DEPTH == 1
    bsz, seq, d = x.shape
    x2 = x.reshape(bsz * seq, d)
    row = lambda v: v.reshape(1, -1).astype(F32)
    l = 0
    hb = _ln(x2, row(ln_in_g), row(ln_in_b))
    proj = _proj(hb, w_in[l])
    proj = proj.reshape(bsz, seq, IN_W)
    memkv = _mem_kv(mem.reshape(bsz * N_MEM, d), w_mem_kv[l]).reshape(bsz, N_MEM, 2 * BRANCH_W)

    tq_d, tb_d, tq_w, tb_w = 512, 256, 512, 256
    a = _diff_attn(proj, _diff_bias_tiles(rel_table[:, :DIFF_HEADS] * LOG2E, tb_d), row(diff_lq1[l]), row(diff_lk1[l]),
                   row(diff_lq2[l]), row(diff_lk2[l]), row(diff_subln_g[l]), tq=tq_d)
    b = _win_attn(proj, _win_bias_tiles(rel_table[:, DIFF_HEADS:] * LOG2E, tb_w), win_sink[l].astype(F32) * LOG2E,
                  tq=tq_w)
    c = _mem_attn(proj, memkv)

    m = bsz * seq
    gates = _gates(hb, w_gate[l], row(b_gate[l]))
    h1 = _mix(x2, row(ln_in_g), row(ln_in_b), a.reshape(m, -1), b.reshape(m, -1), c.reshape(m, -1), gates,
              w_branch[l].astype(BF16), w_o[l].astype(BF16), row(ln1_g[l]), row(ln1_b[l]))

    w_up_p = _w_up_prep(w_up[l].astype(F32))
    cw_p = _pad_ff(conv_w[l].astype(F32), 1)
    cb_p = _pad_ff(row(conv_b[l]), 1)
    w_down_p = _w_down_prep(w_down[l].astype(F32))
    out = _ffn(h1, w_up_p, cw_p, cb_p, w_down_p, row(ln2_g[l]), row(ln2_b[l]), seq)
    return out.reshape(bsz, seq, d)
```

```python
import functools
import math

import jax
import jax.numpy as jnp
from jax import lax
from jax.experimental import pallas as pl
from jax.experimental.pallas import tpu as pltpu

F32 = jnp.float32
BF16 = jnp.bfloat16

D_MODEL = 2048
SEQ = 2048
N_MEM = 256
HEAD_DIM = 128
BRANCH_W = 1024
DIFF_HEADS = 8
DIFF_QK = 64
WIN_HEADS = 8
WIN_KV_HEADS = 2
WIN_GROUP = WIN_HEADS // WIN_KV_HEADS
WINDOW = 128
MEM_HEADS = 4
MEM_DIM = 256
OFF_DQ, OFF_DK, OFF_DV, OFF_WQ, OFF_WK, OFF_WV, OFF_MQ = 0, 1024, 2048, 3072, 4096, 4352, 4608
IN_W = 5632
D_FF = 5504
D_FF_PAD = 5632
REL_BUCKETS = 32
REL_MAX_DIST = 128
DEPTH = 1
ALPHA = (2 * DEPTH) ** 0.25
LN_EPS = 1e-5
NEG = -1e30
LOG2E = math.log2(math.e)
LAMBDA_INIT = 0.8 - 0.6 * math.exp(-0.3 * 0)

VMEM_LIMIT = 56 * 1024 * 1024
LN_ROWS = 128


def _ln_rows(x, g, b):
    mu = jnp.mean(x, axis=-1, keepdims=True)
    xc = x - mu
    var = jnp.mean(xc * xc, axis=-1, keepdims=True)
    return xc * lax.rsqrt(var + LN_EPS) * g + b


def _dot(a, b):
    return jnp.dot(a, b, preferred_element_type=F32)


def _dot_nt(a, b):
    return lax.dot_general(a, b, (((1,), (1,)), ((), ())), preferred_element_type=F32)


def _params(*sem):
    return pltpu.CompilerParams(dimension_semantics=sem, vmem_limit_bytes=VMEM_LIMIT)


def _ln_kernel(x_ref, g_ref, b_ref, h_ref):
    def body(r, c):
        rows = pl.ds(pl.multiple_of(r * LN_ROWS, LN_ROWS), LN_ROWS)
        h_ref[rows, :] = _ln_rows(x_ref[rows, :], g_ref[...], b_ref[...]).astype(BF16)
        return c
    lax.fori_loop(0, x_ref.shape[0] // LN_ROWS, body, 0)


def _ln(x2, g, b, tm=512):
    m, d = x2.shape
    vec = pl.BlockSpec((1, d), lambda i: (0, 0))
    return pl.pallas_call(
        _ln_kernel,
        grid=(m // tm,),
        in_specs=[pl.BlockSpec((tm, d), lambda i: (i, 0)), vec, vec],
        out_specs=pl.BlockSpec((tm, d), lambda i: (i, 0)),
        out_shape=jax.ShapeDtypeStruct((m, d), BF16),
        compiler_params=_params("parallel"),
        name="ln_in",
    )(x2, g, b)


MM_ROWS = 2048


def _proj_kernel(h_ref, w_ref, o_ref):
    def body(r, c):
        rows = pl.ds(pl.multiple_of(r * MM_ROWS, MM_ROWS), MM_ROWS)
        o_ref[rows, :] = _dot(h_ref[rows, :], w_ref[...].astype(BF16)).astype(BF16)
        return c
    lax.fori_loop(0, h_ref.shape[0] // MM_ROWS, body, 0)


def _proj(hb, w, tm=4096, tn=512):
    m, d = hb.shape
    n = w.shape[1]
    return pl.pallas_call(
        _proj_kernel,
        grid=(m // tm, n // tn),
        in_specs=[pl.BlockSpec((tm, d), lambda i, j: (i, 0)),
                  pl.BlockSpec((d, tn), lambda i, j: (0, j))],
        out_specs=pl.BlockSpec((tm, tn), lambda i, j: (i, j)),
        out_shape=jax.ShapeDtypeStruct((m, n), BF16),
        compiler_params=_params("parallel", "arbitrary"),
        name="proj",
    )(hb, w)


def _gates_kernel(h_ref, w_ref, b_ref, o_ref):
    def body(r, c):
        rows = pl.ds(pl.multiple_of(r * MM_ROWS, MM_ROWS), MM_ROWS)
        z = _dot(h_ref[rows, :], w_ref[...].astype(BF16)) + b_ref[...]
        o_ref[rows, :] = (0.5 * jnp.tanh(0.5 * z) + 0.5).astype(BF16)
        return c
    lax.fori_loop(0, h_ref.shape[0] // MM_ROWS, body, 0)


def _gates(hb, w, b, tm=4096, tn=512):
    m, d = hb.shape
    n = w.shape[1]
    return pl.pallas_call(
        _gates_kernel,
        grid=(m // tm, n // tn),
        in_specs=[pl.BlockSpec((tm, d), lambda i, j: (i, 0)),
                  pl.BlockSpec((d, tn), lambda i, j: (0, j)),
                  pl.BlockSpec((1, tn), lambda i, j: (0, j))],
        out_specs=pl.BlockSpec((tm, tn), lambda i, j: (i, j)),
        out_shape=jax.ShapeDtypeStruct((m, n), BF16),
        compiler_params=_params("parallel", "arbitrary"),
        name="gates",
    )(hb, w, b)


def _mem_kv_kernel(m_ref, w_ref, o_ref):
    o_ref[...] = _dot(m_ref[...].astype(BF16), w_ref[...].astype(BF16)).astype(BF16)


def _mem_kv(mem2, w, tn=512):
    m, d = mem2.shape
    n = w.shape[1]
    return pl.pallas_call(
        _mem_kv_kernel,
        grid=(n // tn,),
        in_specs=[pl.BlockSpec((m, d), lambda j: (0, 0)),
                  pl.BlockSpec((d, tn), lambda j: (0, j))],
        out_specs=pl.BlockSpec((m, tn), lambda j: (0, j)),
        out_shape=jax.ShapeDtypeStruct((m, n), BF16),
        compiler_params=_params("parallel"),
        name="mem_kv",
    )(mem2, w)


def _bucket_lower_bounds():
    half = REL_BUCKETS // 2
    max_exact = half // 2
    n = jnp.arange(REL_MAX_DIST + 1)
    log_part = (jnp.log(jnp.maximum(n, 1).astype(F32) / max_exact) / math.log(REL_MAX_DIST / max_exact)
                * (half - max_exact)).astype(jnp.int32)
    bucket = jnp.where(n < max_exact, n, jnp.minimum(max_exact + log_part, half - 1))
    bucket = jnp.arange(half, dtype=jnp.int32)[bucket]
    ids = jnp.arange(half, dtype=jnp.int32)
    return jnp.sum((bucket[None, :] < ids[:, None]).astype(jnp.int32), axis=1)


BIAS_ROWS = 64


def _bias_kernel(lb_ref, tab_ref, lo_ref, hi_ref, off_ref, o_ref, *, limit):
    h = pl.program_id(0)
    half = REL_BUCKETS // 2
    ncol = o_ref.shape[2]

    def body(r, carry):
        rows = pl.ds(pl.multiple_of(r * BIAS_ROWS, BIAS_ROWS), BIAS_ROWS)

        def general():
            rel = lax.broadcasted_iota(jnp.int32, (BIAS_ROWS, ncol), 1) + off_ref[rows, :]
            n = jnp.abs(rel)
            neg = jnp.full(rel.shape, tab_ref[0, h], F32)
            pos = jnp.full(rel.shape, tab_ref[half, h], F32)
            for b in range(1, half):
                reached = n >= lb_ref[b]
                neg = jnp.where(reached, tab_ref[b, h], neg)
                pos = jnp.where(reached, tab_ref[half + b, h], pos)
            out = jnp.where(rel > 0, pos, neg)
            if limit is not None:
                out = jnp.where(n > limit, NEG, out)
            o_ref[0, rows, :] = out

        if limit is not None:
            general()
            return carry
        far = lb_ref[half - 1]
        all_after = lo_ref[r] >= far
        all_before = -(hi_ref[r] + (ncol - 1)) >= far

        @pl.when(all_after)
        def _():
            o_ref[0, rows, :] = jnp.full((BIAS_ROWS, ncol), tab_ref[2 * half - 1, h], F32)

        @pl.when(all_before)
        def _():
            o_ref[0, rows, :] = jnp.full((BIAS_ROWS, ncol), tab_ref[half - 1, h], F32)

        pl.when(jnp.logical_not(jnp.logical_or(all_after, all_before)))(general)
        return carry
    lax.fori_loop(0, o_ref.shape[1] // BIAS_ROWS, body, 0)


def _bias_tiles(table, row_offset, ncol, limit=None):
    nh = table.shape[1]
    nrow = row_offset.shape[0]
    off = row_offset.astype(jnp.int32)
    chunks = off.reshape(nrow // BIAS_ROWS, BIAS_ROWS)
    smem = pl.BlockSpec(memory_space=pltpu.SMEM)
    return pl.pallas_call(
        functools.partial(_bias_kernel, limit=limit),
        grid=(nh,),
        in_specs=[smem, smem, smem, smem, pl.BlockSpec((nrow, 1), lambda h: (0, 0))],
        out_specs=pl.BlockSpec((1, nrow, ncol), lambda h: (h, 0, 0)),
        out_shape=jax.ShapeDtypeStruct((nh, nrow, ncol), F32),
        compiler_params=_params("parallel"),
        name="bias_tiles",
    )(_bucket_lower_bounds(), table.astype(F32), jnp.min(chunks, axis=1), jnp.max(chunks, axis=1),
      off.reshape(nrow, 1))


def _diff_bias_tiles(table, tq):
    r = jnp.arange(5 * tq)
    return _bias_tiles(table, (r // tq - 2) * tq - r % tq, tq).reshape(table.shape[1], 5, tq, tq)


def _win_bias_tiles(table, tq):
    return _bias_tiles(table, -WINDOW - jnp.arange(tq), tq + 2 * WINDOW, limit=WINDOW)


def _diff_kernel(q_ref, k_ref, v_ref, band_ref, lq1_ref, lk1_ref, lq2_ref, lk2_ref, g_ref, o_ref,
                 s_scr, mx_scr, *, tq, rc, n_tiles, tiles_per_head):
    t = pl.program_id(0)
    hd = v_ref.shape[2]

    @pl.when(t == 0)
    def _():
        s_scr[...] = jnp.zeros_like(s_scr)
        mx_scr[...] = jnp.zeros_like(mx_scr)

    def tick(slot):
        other = 1 - slot
        i = jnp.minimum(t, n_tiles - 1) % tiles_per_head
        nk = k_ref.shape[1] // tq
        q = (q_ref[0].astype(F32) * (DIFF_QK ** -0.5 * LOG2E)).astype(BF16)
        lane = lax.broadcasted_iota(jnp.int32, q.shape, 1)
        zero = jnp.zeros_like(q)
        qs = jnp.concatenate([jnp.where(lane < DIFF_QK, q, zero), jnp.where(lane >= DIFF_QK, q, zero)], axis=0)
        lanes = mx_scr.shape[-1]
        tb = band_ref.shape[2]
        nb = tq // tb
        for j in range(nk):
            s = _dot_nt(qs, k_ref[0, j * tq:(j + 1) * tq, :])
            for r0 in range(0, 2 * tq, tb):
                rows = slice(r0, r0 + tb)
                fold = None
                for cb in range(nb):
                    d = (j - i) * nb + cb - (r0 // tb) % nb
                    sb = s[rows, cb * tb:(cb + 1) * tb] + band_ref[0, jnp.clip(d, -2, 2) + 2]
                    s_scr[slot, rows, j * tq + cb * tb:j * tq + (cb + 1) * tb] = sb
                    fold = functools.reduce(jnp.maximum, [sb[:, c:c + lanes] for c in range(0, tb, lanes)]
                                            + ([] if fold is None else [fold]))
                mx_scr[slot, rows, :] = fold if j == 0 else jnp.maximum(mx_scr[slot, rows, :], fold)

        nchunk = 2 * tq // rc
        chunk = lambda r: slice(r * rc, (r + 1) * rc)
        p = jnp.concatenate(
            [jnp.exp2(s_scr[other, chunk(r), :] - jnp.max(mx_scr[other, chunk(r), :], axis=-1, keepdims=True))
             .astype(BF16) for r in range(nchunk)], axis=0)
        v_ones = jnp.concatenate([v_ref[0], jnp.ones(v_ref.shape[1:], BF16)], axis=1)
        pv = _dot(p, v_ones)
        lam = (jnp.exp(jnp.sum(lq1_ref[...] * lk1_ref[...], axis=-1, keepdims=True))
               - jnp.exp(jnp.sum(lq2_ref[...] * lk2_ref[...], axis=-1, keepdims=True)) + LAMBDA_INIT)
        o = pv[0:tq, 0:hd] * (1.0 / pv[0:tq, hd:]) - pv[tq:, 0:hd] * (lam / pv[tq:, hd:])
        ms = jnp.mean(o * o, axis=-1, keepdims=True)
        o_ref[0] = (o * lax.rsqrt(ms + LN_EPS) * g_ref[...] * (1.0 - LAMBDA_INIT)).astype(BF16)

    pl.when(t % 2 == 0)(functools.partial(tick, 0))
    pl.when(t % 2 == 1)(functools.partial(tick, 1))


def _diff_attn(proj, band, lq1, lk1, lq2, lk2, g, tq=256, rc=32):
    b, s, _ = proj.shape
    cb = HEAD_DIM
    ni = s // tq
    n_tiles = b * DIFF_HEADS * ni
    depth = 1

    def tile(t, lag):
        tc = jnp.clip(t - lag, 0, n_tiles - 1)
        return tc // (DIFF_HEADS * ni), (tc // ni) % DIFF_HEADS, tc % ni

    def q_map(t):
        bi, h, i = tile(t, 0)
        return bi, i, OFF_DQ // cb + h

    def k_map(t):
        bi, h, _ = tile(t, 0)
        return bi, 0, OFF_DK // cb + h

    def v_map(t):
        bi, h, _ = tile(t, depth)
        return bi, 0, OFF_DV // cb + h

    def o_map(t):
        bi, h, i = tile(t, depth)
        return bi, i, h

    vec = lambda n: pl.BlockSpec((1, n), lambda t: (0, 0))
    return pl.pallas_call(
        functools.partial(_diff_kernel, tq=tq, rc=rc, n_tiles=n_tiles, tiles_per_head=ni),
        grid=(n_tiles + depth,),
        in_specs=[pl.BlockSpec((1, tq, cb), q_map),
                  pl.BlockSpec((1, s, cb), k_map),
                  pl.BlockSpec((1, s, cb), v_map),
                  pl.BlockSpec((1,) + band.shape[1:], lambda t: (tile(t, 0)[1], 0, 0, 0)),
                  vec(DIFF_QK), vec(DIFF_QK), vec(DIFF_QK), vec(DIFF_QK), vec(HEAD_DIM)],
        out_specs=pl.BlockSpec((1, tq, cb), o_map),
        out_shape=jax.ShapeDtypeStruct((b, s, BRANCH_W), BF16),
        scratch_shapes=[pltpu.VMEM((2, 2 * tq, s), F32), pltpu.VMEM((2, 2 * tq, cb), F32)],
        compiler_params=_params("arbitrary"),
        name="diff_attn",
    )(proj, proj, proj, band, lq1, lk1, lq2, lk2, g)


def _win_kernel(sink_ref, q_ref, kp_ref, km_ref, kn_ref, vp_ref, vm_ref, vn_ref, bias_ref, o_ref, *, tq):
    g = pl.program_id(1)
    i = pl.program_id(2)
    first = i == 0
    last = i == pl.num_programs(2) - 1
    w = WINDOW
    hd = HEAD_DIM
    tb = bias_ref.shape[1]
    nkeys = tb + 2 * w
    keys = jnp.concatenate([kp_ref[0], km_ref[0], kn_ref[0]], axis=0)
    vals = jnp.concatenate([vp_ref[0], vm_ref[0], vn_ref[0]], axis=0)
    v_ones = jnp.concatenate([vals, jnp.ones((tq + 2 * w, hd), BF16)], axis=1)
    col = lax.broadcasted_iota(jnp.int32, (1, nkeys), 1)
    scale = jnp.asarray(hd ** -0.5 * LOG2E, F32)
    for r0 in range(0, tq, tb):
        rows = slice(r0, r0 + tb)
        band = slice(r0, r0 + nkeys)
        outside = None
        if r0 == 0:
            outside = jnp.logical_and(first, col < w)
        if r0 + tb == tq:
            past_end = jnp.logical_and(last, col >= tb + w)
            outside = past_end if outside is None else jnp.logical_or(outside, past_end)
        for j in range(WIN_GROUP):
            cols = slice(j * hd, (j + 1) * hd)
            s = _dot_nt(q_ref[0, rows, cols], keys[band]) * scale + bias_ref[j]
            if outside is not None:
                s = jnp.where(outside, NEG, s)
            sink = sink_ref[g * WIN_GROUP + j]
            fold = functools.reduce(jnp.maximum, [s[:, c:c + w] for c in range(0, nkeys, w)])
            m = jnp.maximum(jnp.max(fold, axis=-1, keepdims=True), sink)
            pv = _dot(jnp.exp2(s - m).astype(BF16), v_ones[band])
            o_ref[0, rows, cols] = (pv[:, 0:hd] / (pv[:, hd:] + jnp.exp2(sink - m))).astype(BF16)


def _win_attn(proj, bias, sink, tq=256):
    b, s, _ = proj.shape
    hd = HEAD_DIM
    nb = s // WINDOW
    r = tq // WINDOW
    prev = lambda bi, g, i: jnp.maximum(i * r - 1, 0)
    nxt = lambda bi, g, i: jnp.minimum((i + 1) * r, nb - 1)
    kcol, vcol = OFF_WK // hd, OFF_WV // hd
    return pl.pallas_call(
        functools.partial(_win_kernel, tq=tq),
        grid=(b, WIN_KV_HEADS, s // tq),
        in_specs=[pl.BlockSpec(memory_space=pltpu.SMEM),
                  pl.BlockSpec((1, tq, WIN_GROUP * hd), lambda bi, g, i: (bi, i, OFF_WQ // (WIN_GROUP * hd) + g)),
                  pl.BlockSpec((1, WINDOW, hd), lambda bi, g, i: (bi, prev(bi, g, i), kcol + g)),
                  pl.BlockSpec((1, tq, hd), lambda bi, g, i: (bi, i, kcol + g)),
                  pl.BlockSpec((1, WINDOW, hd), lambda bi, g, i: (bi, nxt(bi, g, i), kcol + g)),
                  pl.BlockSpec((1, WINDOW, hd), lambda bi, g, i: (bi, prev(bi, g, i), vcol + g)),
                  pl.BlockSpec((1, tq, hd), lambda bi, g, i: (bi, i, vcol + g)),
                  pl.BlockSpec((1, WINDOW, hd), lambda bi, g, i: (bi, nxt(bi, g, i), vcol + g)),
                  pl.BlockSpec((WIN_GROUP,) + bias.shape[1:], lambda bi, g, i: (g, 0, 0))],
        out_specs=pl.BlockSpec((1, tq, WIN_GROUP * hd), lambda bi, g, i: (bi, i, g)),
        out_shape=jax.ShapeDtypeStruct((b, s, BRANCH_W), BF16),
        compiler_params=_params("parallel", "parallel", "arbitrary"),
        name="win_attn",
    )(sink, proj, proj, proj, proj, proj, proj, proj, bias)


def _mem_attn_kernel(q_ref, k_ref, v_ref, o_ref):
    s = _dot_nt(q_ref[0], k_ref[0]) * jnp.asarray(MEM_DIM ** -0.5, F32)
    p = jnp.exp(s - jnp.max(s, axis=-1, keepdims=True))
    pv = _dot(p.astype(BF16), v_ref[0])
    o_ref[0] = (pv * (1.0 / jnp.sum(p, axis=-1, keepdims=True))).astype(BF16)


def _mem_attn(proj, memkv, tq=2048):
    b, s, _ = proj.shape
    md = MEM_DIM
    return pl.pallas_call(
        _mem_attn_kernel,
        grid=(b, MEM_HEADS, s // tq),
        in_specs=[pl.BlockSpec((1, tq, md), lambda bi, h, i: (bi, i, OFF_MQ // md + h)),
                  pl.BlockSpec((1, N_MEM, md), lambda bi, h, i: (bi, 0, h)),
                  pl.BlockSpec((1, N_MEM, md), lambda bi, h, i: (bi, 0, MEM_HEADS + h))],
        out_specs=pl.BlockSpec((1, tq, md), lambda bi, h, i: (bi, i, h)),
        out_shape=jax.ShapeDtypeStruct((b, s, BRANCH_W), BF16),
        compiler_params=_params("parallel", "parallel", "arbitrary"),
        name="mem_attn",
    )(proj, memkv, memkv)


def _mix_kernel(x_ref, lg_ref, lb_ref, a_ref, b_ref, c_ref, g0_ref, g1_ref, g2_ref, wb_ref, wo_ref,
                ln_g_ref, ln_b_ref, o_ref, acc_scr):
    n = pl.program_id(1)
    nrow = x_ref.shape[0] // LN_ROWS

    @pl.when(n == 0)
    def _():
        acc_scr[...] = jnp.zeros_like(acc_scr)

    mixed = None
    for k, (br_ref, gate_ref) in enumerate(((a_ref, g0_ref), (b_ref, g1_ref), (c_ref, g2_ref))):
        term = gate_ref[...].astype(F32) * _dot(br_ref[...], wb_ref[k])
        mixed = term if mixed is None else mixed + term
    acc_scr[...] += _dot(mixed.astype(BF16), wo_ref[...])

    @pl.when(n == pl.num_programs(1) - 1)
    def _():
        def body(r, c):
            rows = pl.ds(pl.multiple_of(r * LN_ROWS, LN_ROWS), LN_ROWS)
            hh = _ln_rows(x_ref[rows, :], lg_ref[...], lb_ref[...])
            o_ref[rows, :] = _ln_rows(ALPHA * hh + acc_scr[rows, :], ln_g_ref[...], ln_b_ref[...])
            return c
        lax.fori_loop(0, nrow, body, 0)


def _mix(x2, lg, lb, a, b, c, gates, wb, wo, g1, b1, tm=512, tn=512):
    m, d = x2.shape
    bw = a.shape[1]
    nn = d // tn
    row = lambda w: pl.BlockSpec((tm, w), lambda i, n: (i, 0))
    vec = pl.BlockSpec((1, d), lambda i, n: (0, 0))
    gts = [pl.BlockSpec((tm, tn), functools.partial(lambda i, n, k: (i, k * nn + n), k=k)) for k in range(3)]
    return pl.pallas_call(
        _mix_kernel,
        grid=(m // tm, nn),
        in_specs=[row(d), vec, vec, row(bw), row(bw), row(bw), *gts,
                  pl.BlockSpec((3, bw, tn), lambda i, n: (0, 0, n)),
                  pl.BlockSpec((tn, d), lambda i, n: (n, 0)), vec, vec],
        out_specs=pl.BlockSpec((tm, d), lambda i, n: (i, 0)),
        out_shape=jax.ShapeDtypeStruct((m, d), F32),
        scratch_shapes=[pltpu.VMEM((tm, d), F32)],
        compiler_params=_params("parallel", "arbitrary"),
        name="mix",
    )(x2, lg, lb, a, b, c, gates, gates, gates, wb, wo, g1, b1)


HALO = 16


def _gelu_tanh(x):
    return 0.5 * x * (1.0 + jnp.tanh(math.sqrt(2.0 / math.pi) * (x + 0.044715 * (x * x * x))))


def _ffn_kernel(h_ref, hp_ref, hn_ref, wv_ref, wg_ref, cwv_ref, cwg_ref, cbv_ref, cbg_ref, wd_ref,
                g2_ref, b2_ref, o_ref, hx_scr, acc_scr, *, tiles_per_seq):
    i = pl.program_id(0)
    f = pl.program_id(1)
    tm = h_ref.shape[0]
    nrow = tm // LN_ROWS

    @pl.when(f == 0)
    def _():
        seq_first = (i % tiles_per_seq) == 0
        seq_last = (i % tiles_per_seq) == tiles_per_seq - 1
        hx_scr[0:HALO, :] = jnp.where(seq_first, 0.0, hp_ref[...]).astype(BF16)
        hx_scr[HALO:HALO + tm, :] = h_ref[...].astype(BF16)
        hx_scr[HALO + tm:, :] = jnp.where(seq_last, 0.0, hn_ref[...]).astype(BF16)
        acc_scr[...] = jnp.zeros_like(acc_scr)

    hx = hx_scr[...]
    ext = tm + 2 * HALO

    def conv(u, cw_ref, cb_ref):
        prev = pltpu.roll(u, 1, 0)[HALO:HALO + tm]
        nxt = pltpu.roll(u, ext - 1, 0)[HALO:HALO + tm]
        return prev * cw_ref[0:1, :] + u[HALO:HALO + tm] * cw_ref[1:2, :] + nxt * cw_ref[2:3, :] + cb_ref[...]

    gelu_gate = _gelu_tanh(conv(_dot(hx, wg_ref[...]), cwg_ref, cbg_ref))
    val = conv(_dot(hx, wv_ref[...]), cwv_ref, cbv_ref)
    acc_scr[...] += _dot((gelu_gate * val).astype(BF16), wd_ref[...])

    @pl.when(f == pl.num_programs(1) - 1)
    def _():
        def body(r, c):
            rows = pl.ds(pl.multiple_of(r * LN_ROWS, LN_ROWS), LN_ROWS)
            o_ref[rows, :] = _ln_rows(ALPHA * h_ref[rows, :] + acc_scr[rows, :], g2_ref[...], b2_ref[...])
            return c
        lax.fori_loop(0, nrow, body, 0)


def _ffn(h1, w_up, cw, cb, w_down, g2, b2, seq, tm=512, tf=512):
    m, d = h1.shape
    nf = D_FF_PAD // tf
    hb = tm // HALO
    nhb = m // HALO
    vec = pl.BlockSpec((1, d), lambda i, f: (0, 0))
    return pl.pallas_call(
        functools.partial(_ffn_kernel, tiles_per_seq=seq // tm),
        grid=(m // tm, nf),
        in_specs=[pl.BlockSpec((tm, d), lambda i, f: (i, 0)),
                  pl.BlockSpec((HALO, d), lambda i, f: (jnp.maximum(i * hb - 1, 0), 0)),
                  pl.BlockSpec((HALO, d), lambda i, f: (jnp.minimum((i + 1) * hb, nhb - 1), 0)),
                  pl.BlockSpec((d, tf), lambda i, f: (0, f)),
                  pl.BlockSpec((d, tf), lambda i, f: (0, nf + f)),
                  pl.BlockSpec((3, tf), lambda i, f: (0, f)),
                  pl.BlockSpec((3, tf), lambda i, f: (0, nf + f)),
                  pl.BlockSpec((1, tf), lambda i, f: (0, f)),
                  pl.BlockSpec((1, tf), lambda i, f: (0, nf + f)),
                  pl.BlockSpec((tf, d), lambda i, f: (f, 0)), vec, vec],
        out_specs=pl.BlockSpec((tm, d), lambda i, f: (i, 0)),
        out_shape=jax.ShapeDtypeStruct((m, d), F32),
        scratch_shapes=[pltpu.VMEM((tm + 2 * HALO, d), BF16), pltpu.VMEM((tm, d), F32)],
        compiler_params=_params("parallel", "arbitrary"),
        name="ffn",
    )(h1, h1, h1, w_up, w_up, cw, cw, cb, cb, w_down, g2, b2)


def _w_up_prep_kernel(w_ref, o_ref):
    o_ref[:, 0:D_FF] = w_ref[...].astype(BF16)
    o_ref[:, D_FF:] = jnp.zeros((o_ref.shape[0], D_FF_PAD - D_FF), BF16)


def _w_up_prep(w_up, tr=256):
    d = w_up.shape[0]
    return pl.pallas_call(
        _w_up_prep_kernel,
        grid=(2, d // tr),
        in_specs=[pl.BlockSpec((tr, D_FF), lambda half, r: (r, half))],
        out_specs=pl.BlockSpec((tr, D_FF_PAD), lambda half, r: (r, half)),
        out_shape=jax.ShapeDtypeStruct((d, 2 * D_FF_PAD), BF16),
        compiler_params=_params("parallel", "parallel"),
        name="w_up_prep",
    )(w_up)


def _w_down_prep_kernel(w_ref, o_ref):
    o_ref[0:D_FF, :] = w_ref[...].astype(BF16)
    o_ref[D_FF:, :] = jnp.zeros((D_FF_PAD - D_FF, o_ref.shape[1]), BF16)


def _w_down_prep(w_down, tc=256):
    d = w_down.shape[1]
    return pl.pallas_call(
        _w_down_prep_kernel,
        grid=(d // tc,),
        in_specs=[pl.BlockSpec((D_FF, tc), lambda c: (0, c))],
        out_specs=pl.BlockSpec((D_FF_PAD, tc), lambda c: (0, c)),
        out_shape=jax.ShapeDtypeStruct((D_FF_PAD, d), BF16),
        compiler_params=_params("parallel"),
        name="w_down_prep",
    )(w_down)


def _pad_ff(t, axis):
    val, gate = jnp.split(t, 2, axis=axis)
    pad = [(0, 0)] * t.ndim
    pad[axis] = (0, D_FF_PAD - D_FF)
    return jnp.concatenate([jnp.pad(val, pad), jnp.pad(gate, pad)], axis=axis)


def kernel(x, mem, ln_in_g, ln_in_b, rel_table, w_in, w_mem_kv, diff_lq1, diff_lk1, diff_lq2, diff_lk2,
           diff_subln_g, win_sink, w_gate, b_gate, w_branch, w_o, ln1_g, ln1_b, w_up, conv_w, conv_b,
           w_down, ln2_g, ln2_b):
    assert w_in.shape[0] == DEPTH == 1
    bsz, seq, d = x.shape
    x2 = x.reshape(bsz * seq, d)
    row = lambda v: v.reshape(1, -1).astype(F32)
    l = 0
    hb = _ln(x2, row(ln_in_g), row(ln_in_b))
    proj = _proj(hb, w_in[l])
    proj = proj.reshape(bsz, seq, IN_W)
    memkv = _mem_kv(mem.reshape(bsz * N_MEM, d), w_mem_kv[l]).reshape(bsz, N_MEM, 2 * BRANCH_W)

    tq_d, tb_d, tq_w, tb_w = 512, 256, 512, 256
    a = _diff_attn(proj, _diff_bias_tiles(rel_table[:, :DIFF_HEADS] * LOG2E, tb_d), row(diff_lq1[l]), row(diff_lk1[l]),
                   row(diff_lq2[l]), row(diff_lk2[l]), row(diff_subln_g[l]), tq=tq_d)
    b = _win_attn(proj, _win_bias_tiles(rel_table[:, DIFF_HEADS:] * LOG2E, tb_w), win_sink[l].astype(F32) * LOG2E,
                  tq=tq_w)
    c = _mem_attn(proj, memkv)

    m = bsz * seq
    gates = _gates(hb, w_gate[l], row(b_gate[l]))
    h1 = _mix(x2, row(ln_in_g), row(ln_in_b), a.reshape(m, -1), b.reshape(m, -1), c.reshape(m, -1), gates,
              w_branch[l].astype(BF16), w_o[l].astype(BF16), row(ln1_g[l]), row(ln1_b[l]))

    w_up_p = _w_up_prep(w_up[l].astype(F32))
    cw_p = _pad_ff(conv_w[l].astype(F32), 1)
    cb_p = _pad_ff(row(conv_b[l]), 1)
    w_down_p = _w_down_prep(w_down[l].astype(F32))
    out = _ffn(h1, w_up_p, cw_p, cb_p, w_down_p, row(ln2_g[l]), row(ln2_b[l]), seq)
    return out.reshape(bsz, seq, d)
```
